```python
import jax, jax.numpy as jnp
from jax import lax
import numpy as np

D_MODEL = 2048
BATCH = 4
SEQ = 2048
DEPTH = 1
DEC_BATCH = 128
DEC_SEQ = 8
PAST_LEN = 2048
PAGE_SIZE = 128

RET_HEADS = 4
RET_DK = 256
RET_DV = 256
RET_WIDTH = RET_HEADS * RET_DV
RET_CHUNK = 128
NSA_HEADS = 16
NSA_KV = 4
NSA_REP = NSA_HEADS // NSA_KV
HEAD_DIM = 64
NSA_WIDTH = NSA_HEADS * HEAD_DIM
CMP_LEN = 32
CMP_STRIDE = 16
CMP_HIDDEN = 256
SEL_BLK = 64
SEL_TOPK = 16
WINDOW = 512
WIN_QBLK = 128
SEL_QBLK = 32
SCALE = HEAD_DIM ** -0.5
D_MIX = RET_WIDTH + NSA_WIDTH
SPLITS = (RET_HEADS * RET_DK, RET_HEADS * RET_DK, RET_WIDTH, RET_WIDTH, NSA_WIDTH, 6 * NSA_KV * HEAD_DIM, 3 * NSA_HEADS)
D_IN = sum(SPLITS)
D_FF = 5632
CONV_W = 3
EPS = 1e-6

kernel_name = 'retnet_nsa_hymba_convffn_step'


def _rms(x, g):
    xf = x.astype(jnp.float32)
    y = xf * lax.rsqrt(jnp.mean(xf * xf, axis=-1, keepdims=True) + EPS)
    return (y * g.astype(jnp.float32)).astype(x.dtype)


def _masked_softmax(s, valid):
    s = jnp.where(valid, s, -jnp.inf)
    m = jnp.max(s, axis=-1, keepdims=True)
    m = jnp.where(jnp.isfinite(m), m, 0.0)
    e = jnp.where(valid, jnp.exp(s - m), 0.0)
    return e / jnp.maximum(jnp.sum(e, axis=-1, keepdims=True), 1e-30)


def _alibi_slopes():
    h = jnp.arange(1, NSA_HEADS + 1, dtype=jnp.float32)
    return jnp.exp2(-8.0 * h / NSA_HEADS).reshape(NSA_KV, NSA_REP)


def _rotate(x, pos):
    half = x.shape[-1] // 2
    inv = 1.0 / (10000.0 ** jnp.linspace(0.0, 1.0, half, dtype=jnp.float32))
    ang = pos.astype(jnp.float32)[:, None] * inv[None, :]
    cos = jnp.cos(ang)[:, None, :]
    sin = jnp.sin(ang)[:, None, :]
    xf = x.astype(jnp.float32)
    x1, x2 = xf[..., :half], xf[..., half:]
    return jnp.concatenate([x1 * cos - x2 * sin, x2 * cos + x1 * sin], axis=-1).astype(x.dtype)


def _retention(q, k, v, state0, chunk):
    B, T, H, DK = q.shape
    DV = v.shape[-1]
    n = T // chunk
    lg = jnp.log(1.0 - jnp.exp2(-5.0 - jnp.arange(H, dtype=jnp.float32)))
    i = jnp.arange(chunk, dtype=jnp.float32)
    diff = i[:, None] - i[None, :]
    dmask = jnp.where(diff >= 0, jnp.exp(lg[:, None, None] * jnp.maximum(diff, 0.0)), 0.0)
    xi = jnp.exp(lg[None, :] * (i[:, None] + 1.0))
    zeta = jnp.exp(lg[None, :] * (chunk - 1.0 - i[:, None]))
    g_chunk = jnp.exp(lg * chunk)

    def to_chunks(a):
        return a.astype(jnp.float32).reshape(B, n, chunk, H, a.shape[-1]).swapaxes(0, 1)

    def step(S, inp):
        qi, ki, vi = inp
        s = jnp.einsum('bihd,bjhd->bhij', qi, ki) * dmask
        o = jnp.einsum('bhij,bjhe->bihe', s, vi) + jnp.einsum('bihd,bhde->bihe', qi, S) * xi[:, :, None]
        S = S * g_chunk[:, None, None] + jnp.einsum('bjhd,bjhe->bhde', ki * zeta[:, :, None], vi)
        return S, o

    S, o = lax.scan(step, state0.astype(jnp.float32), (to_chunks(q), to_chunks(k), to_chunks(v)))
    return o.swapaxes(0, 1).reshape(B, T, H, DV).astype(q.dtype), S


def _block_cover(n_cmp, n_sel):
    cs = np.arange(n_cmp)[:, None] * CMP_STRIDE
    js = np.arange(n_sel)[None, :] * SEL_BLK
    cov = np.clip(np.minimum(cs + CMP_LEN, js + SEL_BLK) - np.maximum(cs, js), 0, None)
    return jnp.asarray(cov / CMP_LEN, dtype=jnp.float32)


def _compress(kv_raw, pe, w1, b1, w2, b2):
    B, Tk = kv_raw.shape[:2]
    n_half = Tk // CMP_STRIDE
    hb = kv_raw[:, :n_half * CMP_STRIDE].reshape(B, n_half, CMP_STRIDE, 2, NSA_KV, HEAD_DIM)
    pe = pe.reshape(2, 2, CMP_STRIDE, HEAD_DIM)
    w1 = w1.reshape(2, 2, CMP_STRIDE, HEAD_DIM, CMP_HIDDEN)

    def half(j):
        pej = pe[:, j].transpose(1, 0, 2)[:, :, None, :]
        return jnp.einsum('bnlcgd,cldh->bncgh', hb + pej, w1[:, j])

    hid = jax.nn.silu(half(0)[:, :-1] + half(1)[:, 1:] + b1[:, None, :])
    return jnp.einsum('bncgh,chd->bncgd', hid, w2) + b2[:, None, :]


def _band_attend(q, qpos, k, v, kpos):
    slopes = _alibi_slopes()
    dist = qpos[:, None] - kpos[None, :]
    valid = (kpos[None, :] >= 0) & (dist >= 0) & (dist < WINDOW)
    s = jnp.einsum('bqgrd,bsgd->bgrqs', q, k).astype(jnp.float32) * SCALE \
        - slopes[None, :, :, None, None] * dist.astype(jnp.float32)
    p = _masked_softmax(s, valid)
    return jnp.einsum('bgrqs,bsgd->bqgrd', p.astype(v.dtype), v)


def _window_prompt(q, win):
    B, T = q.shape[:2]
    nb = T // WIN_QBLK
    kp = jnp.pad(win, ((0, 0), (WINDOW, 0), (0, 0), (0, 0), (0, 0)))
    qb = q.reshape((B, nb, WIN_QBLK) + q.shape[2:]).swapaxes(0, 1)
    starts = jnp.arange(nb) * WIN_QBLK

    def blk(args):
        qi, st = args
        ctx = lax.dynamic_slice_in_dim(kp, st, WINDOW + WIN_QBLK, axis=1)
        return _band_attend(qi, st + jnp.arange(WIN_QBLK), ctx[:, :, 0], ctx[:, :, 1],
                            st - WINDOW + jnp.arange(WINDOW + WIN_QBLK))

    o = lax.map(blk, (qb, starts))
    return o.swapaxes(0, 1).reshape(q.shape)


def _nsa_cmp_sel(q, q0, rows, sel_qblk, p):
    B, Tq = q.shape[:2]
    Tk = rows.shape[1]
    slopes = _alibi_slopes()
    qpos = q0 + jnp.arange(Tq)
    cmp = _compress(rows[:, :, 0:2], p['cmp_pe'], p['cmp_w1'], p['cmp_b1'], p['cmp_w2'], p['cmp_b2'])
    kc = _rms(cmp[:, :, 0], p['k_norm_cmp'])
    vc = cmp[:, :, 1]
    n_cmp = kc.shape[1]
    cend = jnp.arange(n_cmp) * CMP_STRIDE + (CMP_LEN - 1)
    dist = qpos[:, None] - cend[None, :]
    s = jnp.einsum('btgrd,bngd->bgrtn', q, kc).astype(jnp.float32) * SCALE \
        - slopes[None, :, :, None, None] * dist.astype(jnp.float32)
    pc = _masked_softmax(s, dist >= 0)
    o_cmp = jnp.einsum('bgrtn,bngd->btgrd', pc.astype(vc.dtype), vc)
    n_sel = -(-Tk // SEL_BLK)
    k_sel = min(SEL_TOPK, n_sel)
    imp = jnp.einsum('bgrtn,nj->btgj', pc, _block_cover(n_cmp, n_sel))
    j = jnp.arange(n_sel)[None, :]
    cur = (qpos // SEL_BLK)[:, None]
    valid_j = (j <= cur)[None, :, None, :]
    forced = ((j == 0) | (j == cur) | (j == cur - 1))[None, :, None, :]
    score = jnp.where(valid_j, jnp.where(forced, jnp.inf, imp), -jnp.inf)
    top_val, top_idx = lax.top_k(score, k_sel)
    top_ok = top_val > -jnp.inf
    pad = n_sel * SEL_BLK - Tk

    def blocks(a):
        a = jnp.pad(a, ((0, 0), (0, pad), (0, 0), (0, 0)))
        return a.reshape(B, n_sel, SEL_BLK, NSA_KV, HEAD_DIM).transpose(0, 3, 1, 2, 4)

    kb = blocks(rows[:, :, 2])
    vb = blocks(rows[:, :, 3])
    nqb = Tq // sel_qblk

    def split_q(a):
        return a.reshape((B, nqb, sel_qblk) + a.shape[2:]).swapaxes(0, 1)

    bi = jnp.arange(B)[:, None, None, None]
    gi = jnp.arange(NSA_KV)[None, None, :, None]

    def sel_block(args):
        qi, ii, oki, pi = args
        kg = kb[bi, gi, ii]
        vg = vb[bi, gi, ii]
        kpos = ii[..., None] * SEL_BLK + jnp.arange(SEL_BLK)
        d = pi[None, :, None, None, None] - kpos
        ok = (oki[..., None] & (d >= 0))[:, :, :, None]
        sc = jnp.einsum('bqgrd,bqgkld->bqgrkl', qi, kg).astype(jnp.float32) * SCALE \
            - slopes[None, None, :, :, None, None] * d[:, :, :, None].astype(jnp.float32)
        shp = sc.shape
        flat = shp[:4] + (shp[4] * shp[5],)
        pr = _masked_softmax(sc.reshape(flat), jnp.broadcast_to(ok, shp).reshape(flat)).reshape(shp)
        return jnp.einsum('bqgrkl,bqgkld->bqgrd', pr.astype(vg.dtype), vg)

    o_sel = lax.map(sel_block, (split_q(q), split_q(top_idx), split_q(top_ok), qpos.reshape(nqb, sel_qblk)))
    return o_cmp, o_sel.swapaxes(0, 1).reshape(q.shape)


def _project(x, pos, p):
    B, T = x.shape[:2]
    z = _rms(x, p['g_attn']) @ p['w_in']
    points = [int(c) for c in np.cumsum(SPLITS)[:-1]]
    rq, rk, rv, rg, nq, nkv, ng = jnp.split(z, points, axis=-1)
    rq = _rotate(rq.reshape(B, T, RET_HEADS, RET_DK), pos)
    rk = _rotate(rk.reshape(B, T, RET_HEADS, RET_DK), pos) * (RET_DK ** -0.5)
    rv = rv.reshape(B, T, RET_HEADS, RET_DV)
    nq = _rms(nq.reshape(B, T, NSA_KV, NSA_REP, HEAD_DIM), p['q_norm'])
    nkv = nkv.reshape(B, T, 6, NSA_KV, HEAD_DIM)
    k_slc = _rms(nkv[:, :, 2], p['k_norm_slc'])
    k_win = _rms(nkv[:, :, 4], p['k_norm_win'])
    rows = jnp.stack([nkv[:, :, 0], nkv[:, :, 1], k_slc, nkv[:, :, 3]], axis=2)
    win = jnp.stack([k_win, nkv[:, :, 5]], axis=2)
    gates = jax.nn.sigmoid(ng.reshape(B, T, NSA_KV, NSA_REP, 3))
    return rq, rk, rv, rg, nq, rows, win, gates


def _mixer_out(x, ret_o, rg, o_cmp, o_sel, o_win, gates, p):
    B, T = x.shape[:2]
    ret = _rms(ret_o, p['ret_gn']) * jax.nn.silu(rg.reshape(ret_o.shape))
    nsa = gates[..., 0:1] * o_cmp + gates[..., 1:2] * o_sel + gates[..., 2:3] * o_win
    mix = jnp.concatenate([ret.reshape(B, T, RET_WIDTH), nsa.reshape(B, T, NSA_WIDTH)], axis=-1)
    return x + mix @ p['w_out']


def _conv_ffn(x, prev, p):
    T = x.shape[1]
    u = _rms(x, p['g_ffn']) @ p['w_up']
    up = jnp.concatenate([prev.astype(u.dtype), u], axis=1)
    c = p['conv_b'] + p['conv_w'][CONV_W - 1] * up[:, CONV_W - 1:]
    for j in range(CONV_W - 1):
        c = c + p['conv_w'][j] * up[:, j:j + T]
    a, b = jnp.split(c, 2, axis=-1)
    return x + (jax.nn.silu(a) * b) @ p['w_down'], up[:, T:]


def _layer_prompt(x, p):
    B, T = x.shape[:2]
    pos = jnp.arange(T)
    rq, rk, rv, rg, nq, rows, win, gates = _project(x, pos, p)
    s0 = jnp.zeros((B, RET_HEADS, RET_DK, RET_DV), jnp.float32)
    ret_o, ret_s = _retention(rq, rk, rv, s0, min(RET_CHUNK, T))
    o_cmp, o_sel = _nsa_cmp_sel(nq, 0, rows, SEL_QBLK, p)
    o_win = _window_prompt(nq, win)
    h = _mixer_out(x, ret_o, rg, o_cmp, o_sel, o_win, gates, p)
    y, conv_s = _conv_ffn(h, jnp.zeros((B, CONV_W - 1, 2 * D_FF), h.dtype), p)
    return y, rows, win[:, -min(WINDOW, T):], ret_s, conv_s


def _layer_sample(x, cache_kv, cache_win, state_ret, state_conv, page_table, p):
    B, T = x.shape[:2]
    past = page_table.shape[1] * cache_kv.shape[1]
    pos = past + jnp.arange(T)
    rq, rk, rv, rg, nq, rows, win, gates = _project(x, pos, p)
    hist = cache_kv[page_table].reshape((B, past) + cache_kv.shape[2:])
    full = jnp.concatenate([hist.astype(rows.dtype), rows], axis=1)
    ret_o, ret_s = _retention(rq, rk, rv, state_ret, T)
    o_cmp, o_sel = _nsa_cmp_sel(nq, past, full, 1, p)
    wb = cache_win.shape[1]
    ctx = jnp.concatenate([cache_win.astype(win.dtype), win], axis=1)
    o_win = _band_attend(nq, pos, ctx[:, :, 0], ctx[:, :, 1], past - wb + jnp.arange(wb + T))
    h = _mixer_out(x, ret_o, rg, o_cmp, o_sel, o_win, gates, p)
    y, conv_s = _conv_ffn(h, state_conv, p)
    return y, rows, ctx[:, T:], ret_s, conv_s


def setup_inputs(seed: int = 0) -> dict:
    key = jax.random.key(seed)
    ks = iter(jax.random.split(key, 32))

    def nrm(shape, scale):
        return jax.random.normal(next(ks), shape, jnp.float32) * scale

    n_pages = PAST_LEN // PAGE_SIZE
    n_used = DEC_BATCH * n_pages
    n_pool = n_used + max(1, n_used // 4)
    win_buf = min(WINDOW, PAST_LEN)
    L = DEPTH
    return {
        'x_prompt': nrm((BATCH, SEQ, D_MODEL), 1.0),
        'x_sample': nrm((DEC_BATCH, DEC_SEQ, D_MODEL), 1.0),
        'cache_kv': nrm((L, n_pool, PAGE_SIZE, 4, NSA_KV, HEAD_DIM), 1.0),
        'cache_win': nrm((L, DEC_BATCH, win_buf, 2, NSA_KV, HEAD_DIM), 1.0),
        'state_ret': nrm((L, DEC_BATCH, RET_HEADS, RET_DK, RET_DV), 0.5),
        'state_conv': nrm((L, DEC_BATCH, CONV_W - 1, 2 * D_FF), 1.0),
        'page_table': jax.random.permutation(next(ks), n_pool)[:n_used].reshape(DEC_BATCH, n_pages).astype(jnp.int32),
        'g_attn': 1.0 + nrm((L, D_MODEL), 0.1),
        'w_in': nrm((L, D_MODEL, D_IN), D_MODEL ** -0.5),
        'q_norm': 1.0 + nrm((L, HEAD_DIM), 0.1),
        'k_norm_cmp': 1.0 + nrm((L, HEAD_DIM), 0.1),
        'k_norm_slc': 1.0 + nrm((L, HEAD_DIM), 0.1),
        'k_norm_win': 1.0 + nrm((L, HEAD_DIM), 0.1),
        'cmp_pe': nrm((L, 2, CMP_LEN, HEAD_DIM), 0.5),
        'cmp_w1': nrm((L, 2, CMP_LEN, HEAD_DIM, CMP_HIDDEN), (CMP_LEN * HEAD_DIM) ** -0.5),
        'cmp_b1': nrm((L, 2, CMP_HIDDEN), 0.01),
        'cmp_w2': nrm((L, 2, CMP_HIDDEN, HEAD_DIM), CMP_HIDDEN ** -0.5),
        'cmp_b2': nrm((L, 2, HEAD_DIM), 0.01),
        'ret_gn': 1.0 + nrm((L, RET_HEADS, RET_DV), 0.1),
        'w_out': nrm((L, D_MIX, D_MODEL), D_MIX ** -0.5),
        'g_ffn': 1.0 + nrm((L, D_MODEL), 0.1),
        'w_up': nrm((L, D_MODEL, 2 * D_FF), D_MODEL ** -0.5),
        'conv_w': nrm((L, CONV_W, 2 * D_FF), CONV_W ** -0.5),
        'conv_b': nrm((L, 2 * D_FF), 0.01),
        'w_down': nrm((L, D_FF, D_MODEL), D_FF ** -0.5),
    }


def reference(x_prompt, x_sample, cache_kv, cache_win, state_ret, state_conv, page_table,
              g_attn, w_in, q_norm, k_norm_cmp, k_norm_slc, k_norm_win, cmp_pe, cmp_w1, cmp_b1,
              cmp_w2, cmp_b2, ret_gn, w_out, g_ffn, w_up, conv_w, conv_b, w_down):
    xp, xs = x_prompt, x_sample
    kv_p, kv_s, win_p, win_s, ret_p, ret_s, conv_p, conv_s = [], [], [], [], [], [], [], []
    for l in range(DEPTH):
        p = dict(g_attn=g_attn[l], w_in=w_in[l], q_norm=q_norm[l], k_norm_cmp=k_norm_cmp[l],
                 k_norm_slc=k_norm_slc[l], k_norm_win=k_norm_win[l], cmp_pe=cmp_pe[l], cmp_w1=cmp_w1[l],
                 cmp_b1=cmp_b1[l], cmp_w2=cmp_w2[l], cmp_b2=cmp_b2[l], ret_gn=ret_gn[l], w_out=w_out[l],
                 g_ffn=g_ffn[l], w_up=w_up[l], conv_w=conv_w[l], conv_b=conv_b[l], w_down=w_down[l])
        xp, a, b, c, d = _layer_prompt(xp, p)
        kv_p.append(a); win_p.append(b); ret_p.append(c); conv_p.append(d)
        xs, a, b, c, d = _layer_sample(xs, cache_kv[l], cache_win[l], state_ret[l], state_conv[l], page_table, p)
        kv_s.append(a); win_s.append(b); ret_s.append(c); conv_s.append(d)
    return (xp, xs, jnp.stack(kv_p), jnp.stack(kv_s), jnp.stack(win_p), jnp.stack(win_s),
            jnp.stack(ret_p), jnp.stack(ret_s), jnp.stack(conv_p), jnp.stack(conv_s))
```

```python
import functools

import numpy as np
import jax
import jax.numpy as jnp
from jax import lax
from jax.experimental import pallas as pl
from jax.experimental.pallas import tpu as pltpu

D_MODEL = 2048
BATCH = 4
SEQ = 2048
DEC_BATCH = 128
DEC_SEQ = 8
PAST_LEN = 2048
PAGE_SIZE = 128
N_PAGES = PAST_LEN // PAGE_SIZE
RET_HEADS = 4
RET_DK = 256
RET_DV = 256
RET_WIDTH = RET_HEADS * RET_DV
RET_CHUNK = 128
NSA_HEADS = 16
NSA_KV = 4
NSA_REP = NSA_HEADS // NSA_KV
HEAD_DIM = 64
NSA_WIDTH = NSA_HEADS * HEAD_DIM
KV_WIDTH = NSA_KV * HEAD_DIM
CMP_LEN = 32
CMP_STRIDE = 16
CMP_HIDDEN = 256
SEL_BLK = 64
SEL_TOPK = 16
WINDOW = 512
SCALE = HEAD_DIM ** -0.5
D_FF = 5632
CONV_W = 3
EPS = 1e-6

N_HALF = PAST_LEN // CMP_STRIDE
NEG = -1e30
FORCED = 1e30
VMEM_LIMIT = 56 * 1024 * 1024

BF16 = jnp.bfloat16
F32 = jnp.float32


def _params(semantics, vmem=VMEM_LIMIT):
    return pltpu.CompilerParams(dimension_semantics=semantics, vmem_limit_bytes=vmem)


def _dot(a, b):
    return jnp.dot(a, b, preferred_element_type=F32)


def _dot_nt(a, b):
    return lax.dot_general(a, b, (((1,), (1,)), ((), ())), preferred_element_type=F32)


def _split_bf16(x):
    hi = x.astype(BF16)
    lo = (x - hi.astype(F32)).astype(BF16)
    return hi, lo


def _group_mask(width, g):
    lane = lax.broadcasted_iota(jnp.int32, (1, width), 1)
    return (lane >= g * HEAD_DIM) & (lane < (g + 1) * HEAD_DIM)


def _rms_groups64(z, gain):
    width = z.shape[-1]
    outs = []
    for k in range(width // 128):
        zk = z[:, k * 128:(k + 1) * 128]
        zz = zk * zk
        lane = lax.broadcasted_iota(jnp.int32, (1, 128), 1)
        lo_half = lane < HEAD_DIM
        s_lo = jnp.sum(jnp.where(lo_half, zz, 0.0), axis=-1, keepdims=True)
        s_hi = jnp.sum(jnp.where(lo_half, 0.0, zz), axis=-1, keepdims=True)
        ms = jnp.where(lo_half, s_lo, s_hi) * (1.0 / HEAD_DIM)
        outs.append(zk * lax.rsqrt(ms + EPS))
    y = outs[0] if len(outs) == 1 else jnp.concatenate(outs, axis=-1)
    return y * gain


def _rmsnorm_kernel(x_ref, g_ref, o_ref):
    x = x_ref[...]
    ms = jnp.mean(x * x, axis=-1, keepdims=True)
    o_ref[...] = (x * lax.rsqrt(ms + EPS) * g_ref[...]).astype(o_ref.dtype)


def _rmsnorm(x, g, tm=512):
    n, d = x.shape
    return pl.pallas_call(
        _rmsnorm_kernel,
        grid=(n // tm,),
        in_specs=[pl.BlockSpec((tm, d), lambda i: (i, 0)), pl.BlockSpec((1, d), lambda i: (0, 0))],
        out_specs=pl.BlockSpec((tm, d), lambda i: (i, 0)),
        out_shape=jax.ShapeDtypeStruct((n, d), BF16),
        compiler_params=_params(("parallel",)),
        name="rmsnorm",
    )(x, g.reshape(1, d))


def _proj_plain_kernel(x_ref, w_ref, o_ref):
    o_ref[...] = _dot(x_ref[...], w_ref[...]).astype(o_ref.dtype)


def _proj_sigmoid_kernel(x_ref, w_ref, o_ref):
    o_ref[...] = jax.nn.sigmoid(_dot(x_ref[...], w_ref[...]))


def _proj_norm_kernel(x_ref, w_ref, gain_ref, nmask_ref, o_ref):
    z = _dot(x_ref[...], w_ref[...])
    o_ref[...] = jnp.where(nmask_ref[...] > 0.5, _rms_groups64(z, gain_ref[...]), z)


def _proj_rot_kernel(x_ref, w_ref, cos_ref, sin_ref, scale_ref, o_ref):
    z = _dot(x_ref[...], w_ref[...])
    c = cos_ref[...]
    s = sin_ref[...]
    sc = scale_ref[...]
    for hh in range(z.shape[-1] // RET_DK):
        x1 = z[:, hh * RET_DK:hh * RET_DK + 128]
        x2 = z[:, hh * RET_DK + 128:(hh + 1) * RET_DK]
        o_ref[:, hh * RET_DK:hh * RET_DK + 128] = (x1 * c - x2 * s) * sc[:, hh * RET_DK:hh * RET_DK + 128]
        o_ref[:, hh * RET_DK + 128:(hh + 1) * RET_DK] = (x2 * c + x1 * s) * sc[:, hh * RET_DK + 128:(hh + 1) * RET_DK]


def _proj(xn, w, kernel, extras=(), extra_specs=(), out_dtype=F32, tm=1024, tn=512, name="proj"):
    n, d = xn.shape
    c = w.shape[1]
    tn = min(tn, c)
    return pl.pallas_call(
        kernel,
        grid=(n // tm, c // tn),
        in_specs=[pl.BlockSpec((tm, d), lambda i, j: (i, 0)), pl.BlockSpec((d, tn), lambda i, j: (0, j))]
        + list(extra_specs),
        out_specs=pl.BlockSpec((tm, tn), lambda i, j: (i, j)),
        out_shape=jax.ShapeDtypeStruct((n, c), out_dtype),
        compiler_params=_params(("parallel", "arbitrary")),
        name=name,
    )(xn, w, *extras)


def _project_all(x2d, pos, wts, tm=1024):
    n = x2d.shape[0]
    xn = _rmsnorm(x2d, wts["g_attn"])
    half = RET_DK // 2
    inv = 1.0 / (10000.0 ** jnp.linspace(0.0, 1.0, half, dtype=F32))
    ang = pos.astype(F32)[:, None] * inv[None, :]
    cos = jnp.cos(ang)
    sin = jnp.sin(ang)
    tn = 512
    row_spec = pl.BlockSpec((tm, 128), lambda i, j: (i, 0))
    col_spec = pl.BlockSpec((1, tn), lambda i, j: (0, j))
    qk = _proj(xn, wts["w_rqk"], _proj_rot_kernel, (cos, sin, wts["rot_scale"]),
               (row_spec, row_spec, col_spec), tm=tm, name="proj_rot")
    vg = _proj(xn, wts["w_rvg"], _proj_plain_kernel, tm=tm, name="proj_vg")
    nq = _proj(xn, wts["w_nq"], _proj_norm_kernel, (wts["nq_gain"], wts["nq_mask"]),
               (col_spec, col_spec), tm=tm, name="proj_nq")
    kv = _proj(xn, wts["w_kv"], _proj_norm_kernel, (wts["kv_gain"], wts["kv_mask"]),
               (col_spec, col_spec), tm=tm, name="proj_kv")
    win = _proj(xn, wts["w_win"], _proj_norm_kernel, (wts["win_gain"], wts["win_mask"]),
                (col_spec, col_spec), tm=tm, name="proj_win")
    gates = _proj(xn, wts["w_ng"], _proj_sigmoid_kernel, tm=tm, name="proj_gate")
    return qk, vg, nq, kv, win, gates


def _retention_step(q, k, v, state, dmask, xi, zeta, gch):
    qb = q.astype(BF16)
    kb = k.astype(BF16)
    vb = v.astype(BF16)
    s = _dot_nt(qb, kb) * dmask
    o = _dot(s.astype(BF16), vb) + _dot(qb, state.astype(BF16)) * xi
    kz = (k * zeta).astype(BF16)
    new_state = state * gch + _dot(kz.T, vb)
    return o, new_state


def _ret_mix(o, gn, gate):
    ms = jnp.mean(o * o, axis=-1, keepdims=True)
    return o * lax.rsqrt(ms + EPS) * gn * jax.nn.silu(gate)


def _ret_prompt_kernel(q_ref, k_ref, v_ref, g_ref, dmask_ref, xi_ref, zeta_ref, gch_ref, gn_ref,
                       mix_ref, state_ref):
    @pl.when(pl.program_id(2) == 0)
    def _():
        state_ref[...] = jnp.zeros_like(state_ref)

    o, new_state = _retention_step(q_ref[0], k_ref[0], v_ref[0], state_ref[0, 0], dmask_ref[0],
                                   xi_ref[0], zeta_ref[0], gch_ref[0])
    state_ref[0, 0] = new_state
    mix_ref[0] = _ret_mix(o, gn_ref[0], g_ref[0]).astype(mix_ref.dtype)


def _ret_tables(chunk):
    h = jnp.arange(RET_HEADS, dtype=F32)
    lg = jnp.log(1.0 - jnp.exp2(-5.0 - h))
    i = jnp.arange(chunk, dtype=F32)
    diff = i[:, None] - i[None, :]
    dmask = jnp.where(diff >= 0, jnp.exp(lg[:, None, None] * jnp.maximum(diff, 0.0)), 0.0)
    xi = jnp.exp(lg[:, None] * (i[None, :] + 1.0))
    zeta = jnp.exp(lg[:, None] * (chunk - 1.0 - i[None, :]))
    gch = jnp.exp(lg * chunk)
    bc = lambda a: jnp.broadcast_to(a[:, :, None], (RET_HEADS, chunk, RET_DV))
    return dmask, bc(xi), bc(zeta), jnp.broadcast_to(gch[:, None, None], (RET_HEADS, 1, RET_DV))


def _retention_prompt(qk, vg, gn):
    b, t = BATCH, SEQ
    c = RET_CHUNK
    qk3 = qk.reshape(b, t, 2 * RET_WIDTH)
    vg3 = vg.reshape(b, t, 2 * RET_WIDTH)
    dmask, xi, zeta, gch = _ret_tables(c)
    tile = lambda off: pl.BlockSpec((1, c, RET_DK), lambda bi, hi, ci: (bi, ci, hi + off))
    head = lambda rows: pl.BlockSpec((1, rows, RET_DV), lambda bi, hi, ci: (hi, 0, 0))
    return pl.pallas_call(
        _ret_prompt_kernel,
        grid=(b, RET_HEADS, t // c),
        in_specs=[tile(0), tile(RET_HEADS), tile(0), tile(RET_HEADS),
                  pl.BlockSpec((1, c, c), lambda bi, hi, ci: (hi, 0, 0)), head(c), head(c), head(1), head(1)],
        out_specs=[pl.BlockSpec((1, c, RET_DV), lambda bi, hi, ci: (bi, ci, hi)),
                   pl.BlockSpec((1, 1, RET_DK, RET_DV), lambda bi, hi, ci: (bi, hi, 0, 0))],
        out_shape=[jax.ShapeDtypeStruct((b, t, RET_WIDTH), BF16),
                   jax.ShapeDtypeStruct((b, RET_HEADS, RET_DK, RET_DV), F32)],
        compiler_params=_params(("parallel", "parallel", "arbitrary")),
        name="retention_prompt",
    )(qk3, qk3, vg3, vg3, dmask, xi, zeta, gch, gn.reshape(RET_HEADS, 1, RET_DV))


def _ret_sample_kernel(qk_ref, vg_ref, s0_ref, dmask_ref, xi_ref, zeta_ref, gch_ref, gn_ref, mix_ref, state_ref):
    t = qk_ref.shape[1]
    pad = lambda a: jnp.concatenate([a, jnp.zeros((RET_PAD - t, a.shape[-1]), F32)], axis=0)
    for h in range(RET_HEADS):
        lo, hi = h * RET_DK, (h + 1) * RET_DK
        o, new_state = _retention_step(pad(qk_ref[0, :, lo:hi]), pad(qk_ref[0, :, RET_WIDTH + lo:RET_WIDTH + hi]),
                                       pad(vg_ref[0, :, lo:hi]), s0_ref[0, h], dmask_ref[h], xi_ref[h],
                                       zeta_ref[h], gch_ref[h])
        state_ref[0, h] = new_state
        mix_ref[0, :, lo:hi] = _ret_mix(o[0:t], gn_ref[h], vg_ref[0, :, RET_WIDTH + lo:RET_WIDTH + hi]).astype(mix_ref.dtype)


RET_PAD = 128


def _retention_sample(qk, vg, state0, gn):
    b, t = DEC_BATCH, DEC_SEQ
    dmask, xi, zeta, gch = _ret_tables(t)
    dmask = jnp.pad(dmask, ((0, 0), (0, RET_PAD - t), (0, RET_PAD - t)))
    xi = jnp.pad(xi, ((0, 0), (0, RET_PAD - t), (0, 0)))
    zeta = jnp.pad(zeta, ((0, 0), (0, RET_PAD - t), (0, 0)))
    full = lambda shape: pl.BlockSpec(shape, lambda bi: (0,) * len(shape))
    return pl.pallas_call(
        _ret_sample_kernel,
        grid=(b,),
        in_specs=[pl.BlockSpec((1, t, 2 * RET_WIDTH), lambda bi: (bi, 0, 0)),
                  pl.BlockSpec((1, t, 2 * RET_WIDTH), lambda bi: (bi, 0, 0)),
                  pl.BlockSpec((1, RET_HEADS, RET_DK, RET_DV), lambda bi: (bi, 0, 0, 0)),
                  full((RET_HEADS, RET_PAD, RET_PAD)), full((RET_HEADS, RET_PAD, RET_DV)), full((RET_HEADS, RET_PAD, RET_DV)),
                  full((RET_HEADS, 1, RET_DV)), full((RET_HEADS, 1, RET_DV))],
        out_specs=[pl.BlockSpec((1, t, RET_WIDTH), lambda bi: (bi, 0, 0)),
                   pl.BlockSpec((1, RET_HEADS, RET_DK, RET_DV), lambda bi: (bi, 0, 0, 0))],
        out_shape=[jax.ShapeDtypeStruct((b, t, RET_WIDTH), BF16),
                   jax.ShapeDtypeStruct((b, RET_HEADS, RET_DK, RET_DV), F32)],
        compiler_params=_params(("parallel",)),
        name="retention_sample",
    )(qk.reshape(b, t, -1), vg.reshape(b, t, -1), state0, dmask, xi, zeta, gch, gn.reshape(RET_HEADS, 1, RET_DV))


def _cmp_bias_kernel(pe_ref, w_ref, b1_ref, o_ref):
    acc = b1_ref[0]
    for j in range(2):
        acc = acc + _dot(pe_ref[0, j].astype(BF16), w_ref[0, j])[0:1]
    o_ref[0] = acc


def _cmp_bias(pe8, w1flat, b1):
    return pl.pallas_call(
        _cmp_bias_kernel,
        grid=(2,),
        in_specs=[pl.BlockSpec((1, 2, 16, CMP_STRIDE * HEAD_DIM), lambda c: (c, 0, 0, 0)),
                  pl.BlockSpec((1, 2, CMP_STRIDE * HEAD_DIM, CMP_HIDDEN), lambda c: (c, 0, 0, 0)),
                  pl.BlockSpec((1, 1, CMP_HIDDEN), lambda c: (c, 0, 0))],
        out_specs=pl.BlockSpec((1, 1, CMP_HIDDEN), lambda c: (c, 0, 0)),
        out_shape=jax.ShapeDtypeStruct((2, 1, CMP_HIDDEN), F32),
        compiler_params=_params(("parallel",)),
        name="cmp_bias",
    )(pe8, w1flat, b1)


def _compress_body(load_rows, w1_ref, bias_ref, w2_ref, b2_ref, gain_ref, o_ref, xm_ref):
    for l in range(CMP_STRIDE):
        xl = load_rows(l)
        for g in range(NSA_KV):
            xm_ref[g * N_HALF:(g + 1) * N_HALF, l * KV_WIDTH:(l + 1) * KV_WIDTH] = jnp.where(
                _group_mask(KV_WIDTH, g), xl, 0.0).astype(BF16)
    acc = _dot(xm_ref[...], w1_ref[0])
    h0 = acc[:, :CMP_HIDDEN]
    h1 = pltpu.roll(acc[:, CMP_HIDDEN:], 4 * N_HALF - 1, 0)
    hid = jax.nn.silu(h0 + h1 + bias_ref[0])
    out = _dot(hid.astype(BF16), w2_ref[0]) + b2_ref[0]
    res = jnp.zeros((N_HALF, KV_WIDTH), F32)
    for g in range(NSA_KV):
        res = res + jnp.where(_group_mask(KV_WIDTH, g), out[g * N_HALF:(g + 1) * N_HALF], 0.0)
    normed = _rms_groups64(res, gain_ref[0])
    o_ref[0, 0] = jnp.where(pl.program_id(0) == 0, normed, res)


def _compress_prompt_kernel(lo_ref, hi_ref, w1_ref, bias_ref, w2_ref, b2_ref, gain_ref, o_ref, xm_ref):
    load = lambda l: jnp.concatenate(
        [r[0, pl.ds(l, N_HALF, stride=CMP_STRIDE), :] for r in (lo_ref, hi_ref)], axis=-1)
    _compress_body(load, w1_ref, bias_ref, w2_ref, b2_ref, gain_ref, o_ref, xm_ref)


def _compress_sample_kernel(pt_ref, *refs):
    pages_lo = refs[:N_PAGES]
    pages_hi = refs[N_PAGES:2 * N_PAGES]
    w1_ref, bias_ref, w2_ref, b2_ref, gain_ref, o_ref, xm_ref = refs[2 * N_PAGES:]
    per_page = PAGE_SIZE // CMP_STRIDE

    def load(l):
        halves = [jnp.concatenate([pg[0, pl.ds(l, per_page, stride=CMP_STRIDE), :] for pg in pages], axis=0)
                  for pages in (pages_lo, pages_hi)]
        return jnp.concatenate(halves, axis=-1)

    _compress_body(load, w1_ref, bias_ref, w2_ref, b2_ref, gain_ref, o_ref, xm_ref)


def _cmp_weight_specs(nargs):
    cmap = (lambda c, b: (c, 0, 0)) if nargs == 2 else (lambda c, b, pt: (c, 0, 0))
    return [pl.BlockSpec((1, CMP_STRIDE * KV_WIDTH, 2 * CMP_HIDDEN), cmap),
            pl.BlockSpec((1, 1, CMP_HIDDEN), cmap),
            pl.BlockSpec((1, CMP_HIDDEN, KV_WIDTH), cmap),
            pl.BlockSpec((1, 1, KV_WIDTH), cmap),
            pl.BlockSpec((1, 1, KV_WIDTH), cmap)]


def _compress_prompt(kv3, cw):
    b = kv3.shape[0]
    return pl.pallas_call(
        _compress_prompt_kernel,
        grid=(2, b),
        in_specs=[pl.BlockSpec((1, SEQ, 128), lambda c, bi: (bi, 0, 2 * c)),
                  pl.BlockSpec((1, SEQ, 128), lambda c, bi: (bi, 0, 2 * c + 1))] + _cmp_weight_specs(2),
        out_specs=pl.BlockSpec((1, 1, N_HALF, KV_WIDTH), lambda c, bi: (c, bi, 0, 0)),
        out_shape=jax.ShapeDtypeStruct((2, b, N_HALF, KV_WIDTH), F32),
        scratch_shapes=[pltpu.VMEM((NSA_KV * N_HALF, CMP_STRIDE * KV_WIDTH), BF16)],
        compiler_params=_params(("arbitrary", "arbitrary")),
        name="compress_prompt",
    )(kv3, kv3, cw["w1rep"], cw["bias"], cw["w2rep"], cw["b2rep"], cw["gain"])


def _compress_sample(cache3, page_flat, cw):
    b = DEC_BATCH
    page_specs = [pl.BlockSpec((1, PAGE_SIZE, 128),
                               functools.partial(lambda c, bi, pt, p, h: (pt[bi * N_PAGES + p], 0, 2 * c + h), p=p, h=h))
                  for h in range(2) for p in range(N_PAGES)]
    grid_spec = pltpu.PrefetchScalarGridSpec(
        num_scalar_prefetch=1,
        grid=(2, b),
        in_specs=page_specs + _cmp_weight_specs(3),
        out_specs=pl.BlockSpec((1, 1, N_HALF, KV_WIDTH), lambda c, bi, pt: (c, bi, 0, 0)),
        scratch_shapes=[pltpu.VMEM((NSA_KV * N_HALF, CMP_STRIDE * KV_WIDTH), BF16)],
    )
    return pl.pallas_call(
        _compress_sample_kernel,
        grid_spec=grid_spec,
        out_shape=jax.ShapeDtypeStruct((2, b, N_HALF, KV_WIDTH), F32),
        compiler_params=_params(("arbitrary", "arbitrary")),
        name="compress_sample",
    )(page_flat, *([cache3] * (2 * N_PAGES)), cw["w1rep"], cw["bias"], cw["w2rep"], cw["b2rep"], cw["gain"])


def _select_blocks(imp, cur, n_blocks):
    rows = imp.shape[0]
    j = lax.broadcasted_iota(jnp.int32, (rows, 1), 0)
    valid = j <= cur
    forced = (j == 0) | (j == cur) | (j == cur - 1)
    score = jnp.where(valid, jnp.where(forced, FORCED, imp), NEG)
    rank = jnp.zeros(score.shape, F32)
    for i in range(n_blocks):
        row = score[i:i + 1, :]
        ahead = (row > score) | ((row == score) & (j > i))
        rank = rank + jnp.where(ahead, 1.0, 0.0)
    return jnp.where(valid & (rank < float(SEL_TOPK)) & (j < n_blocks), 1.0, 0.0)


def _softmax_rows(s, valid):
    s = jnp.where(valid, s, NEG)
    m = jnp.max(s, axis=-1, keepdims=True)
    e = jnp.where(valid, jnp.exp(s - m), 0.0)
    return e, jnp.maximum(jnp.sum(e, axis=-1, keepdims=True), 1e-30)


def _softmax_cols(s, valid):
    s = jnp.where(valid, s, NEG)
    m = jnp.max(s, axis=0, keepdims=True)
    e = jnp.where(valid, jnp.exp(s - m), 0.0)
    return e, jnp.maximum(jnp.sum(e, axis=0, keepdims=True), 1e-30)


def _slope(g, r):
    return float(2.0 ** (-8.0 * (g * NSA_REP + r + 1) / NSA_HEADS))


NSA_TQ = 128
SEL_CHUNK = 512
WIN_KEYS = WINDOW + NSA_TQ


def _nsa_prompt_kernel(q_ref, gate_ref, kc_ref, vc_ref, ks_ref, vs_ref, kw_ref, vw_ref, cover_ref, expand_ref,
                       o_ref, ksb, vsb, kwb, vwb):
    i = pl.program_id(1)
    tq = NSA_TQ
    rows = NSA_REP * tq

    @pl.when(i == 0)
    def _():
        ksb[...] = ks_ref[0].astype(BF16)
        vsb[...] = vs_ref[0].astype(BF16)
        kwb[...] = kw_ref[0].astype(BF16)
        vwb[...] = vw_ref[0].astype(BF16)

    q = q_ref[0] * SCALE
    gates = gate_ref[0]
    kcb = kc_ref[0, 0].astype(BF16)
    vcb = vc_ref[0, 0].astype(BF16)
    q0 = i * tq
    qpos_col = q0 + lax.broadcasted_iota(jnp.int32, (tq, 1), 0)
    qpos4 = jnp.concatenate([qpos_col] * NSA_REP, axis=0)
    qpos_row = q0 + lax.broadcasted_iota(jnp.int32, (1, tq), 1)
    cur_row = lax.shift_right_logical(qpos_row, 6)
    win_start = pl.multiple_of(jnp.maximum(q0 - WINDOW, 0), NSA_TQ)
    n_chunks = lax.shift_right_logical(q0 + tq + SEL_CHUNK - 1, 9)

    out = [jnp.zeros((tq, KV_WIDTH), F32) for _ in range(NSA_REP)]
    for g in range(NSA_KV):
        gm = _group_mask(KV_WIDTH, g)
        qg = jnp.concatenate(
            [jnp.where(gm, q[:, r * KV_WIDTH:(r + 1) * KV_WIDTH], 0.0) for r in range(NSA_REP)], axis=0
        ).astype(BF16)
        slope = jnp.concatenate([jnp.full((tq, 1), _slope(g, r), F32) for r in range(NSA_REP)], axis=0)

        def gate_col(branch):
            cols = [gates[:, g * 12 + r * 3 + branch:g * 12 + r * 3 + branch + 1] for r in range(NSA_REP)]
            return jnp.concatenate(cols, axis=0)

        n_idx = lax.broadcasted_iota(jnp.int32, (1, N_HALF), 1)
        dist = (qpos4 - (n_idx * CMP_STRIDE + (CMP_LEN - 1))).astype(F32)
        valid = dist >= 0.0
        e, den = _softmax_rows(_dot_nt(qg, kcb) - slope * dist, valid)
        pc = e / den
        o_cmp = _dot(pc.astype(BF16), vcb)
        pcsum = pc[0:tq]
        for r in range(1, NSA_REP):
            pcsum = pcsum + pc[r * tq:(r + 1) * tq]
        hi, lo = _split_bf16(pcsum)
        imp = _dot_nt(cover_ref[...], hi) + _dot_nt(cover_ref[...], lo)
        sel = _select_blocks(imp, cur_row, SEQ // SEL_BLK)
        sel = jnp.concatenate([sel, jnp.zeros((128 - sel.shape[0], tq), F32)], axis=0)
        sel_q = sel.T.astype(BF16)
        sel4 = jnp.concatenate([sel_q] * NSA_REP, axis=0)

        def sel_step(kk, carry):
            m, l, acc = carry
            ks = pl.multiple_of(kk * SEL_CHUNK, SEL_CHUNK)
            kpos = ks + lax.broadcasted_iota(jnp.int32, (1, SEL_CHUNK), 1)
            d = (qpos4 - kpos).astype(F32)
            chosen = _dot(sel4, expand_ref[kk])
            ok = (chosen > 0.5) & (d >= 0.0)
            s = jnp.where(ok, _dot_nt(qg, ksb[pl.ds(ks, SEL_CHUNK), :]) - slope * d, NEG)
            m_new = jnp.maximum(m, jnp.max(s, axis=-1, keepdims=True))
            p = jnp.where(ok, jnp.exp(s - m_new), 0.0)
            alpha = jnp.exp(m - m_new)
            l = alpha * l + jnp.sum(p, axis=-1, keepdims=True)
            acc = alpha * acc + _dot(p.astype(BF16), vsb[pl.ds(ks, SEL_CHUNK), :])
            return m_new, l, acc

        m0 = jnp.full((rows, 1), NEG, F32)
        l0 = jnp.zeros((rows, 1), F32)
        a0 = jnp.zeros((rows, KV_WIDTH), F32)
        _, l_sel, acc_sel = lax.fori_loop(0, n_chunks, sel_step, (m0, l0, a0))
        o_sel = acc_sel / jnp.maximum(l_sel, 1e-30)

        kpos = win_start + lax.broadcasted_iota(jnp.int32, (1, WIN_KEYS), 1)
        d = (qpos4 - kpos).astype(F32)
        ok = (d >= 0.0) & (d < float(WINDOW))
        e, den = _softmax_rows(_dot_nt(qg, kwb[pl.ds(win_start, WIN_KEYS), :]) - slope * d, ok)
        o_win = _dot(e.astype(BF16), vwb[pl.ds(win_start, WIN_KEYS), :]) / den

        mixed = gate_col(0) * o_cmp + gate_col(1) * o_sel + gate_col(2) * o_win
        for r in range(NSA_REP):
            out[r] = out[r] + jnp.where(gm, mixed[r * tq:(r + 1) * tq], 0.0)

    for r in range(NSA_REP):
        o_ref[0, :, r * KV_WIDTH:(r + 1) * KV_WIDTH] = out[r].astype(o_ref.dtype)


def _nsa_prompt(nq, gates, cmp_tok, kv, win, cover_t, expand):
    b, t, tq = BATCH, SEQ, NSA_TQ
    nq3 = nq.reshape(b, t, NSA_WIDTH)
    g3 = gates.reshape(b, t, 128)
    kv3 = kv.reshape(b, t, 4 * KV_WIDTH)
    win3 = win.reshape(b, t, 2 * KV_WIDTH)
    col = lambda c: pl.BlockSpec((1, t, KV_WIDTH), lambda bi, i: (bi, 0, c))
    tok = lambda c: pl.BlockSpec((1, 1, N_HALF, KV_WIDTH), lambda bi, i: (c, bi, 0, 0))
    return pl.pallas_call(
        _nsa_prompt_kernel,
        grid=(b, t // tq),
        in_specs=[pl.BlockSpec((1, tq, NSA_WIDTH), lambda bi, i: (bi, i, 0)),
                  pl.BlockSpec((1, tq, 128), lambda bi, i: (bi, i, 0)),
                  tok(0), tok(1), col(2), col(3), col(0), col(1),
                  pl.BlockSpec(cover_t.shape, lambda bi, i: (0, 0)),
                  pl.BlockSpec(expand.shape, lambda bi, i: (0, 0, 0))],
        out_specs=pl.BlockSpec((1, tq, NSA_WIDTH), lambda bi, i: (bi, i, 0)),
        out_shape=jax.ShapeDtypeStruct((b, t, NSA_WIDTH), BF16),
        scratch_shapes=[pltpu.VMEM((t, KV_WIDTH), BF16) for _ in range(4)],
        compiler_params=_params(("parallel", "arbitrary")),
        name="nsa_prompt",
    )(nq3, g3, cmp_tok, cmp_tok, kv3, kv3, win3, win3, cover_t, expand)


SAMPLE_SEL_BLOCKS = -(-(PAST_LEN + DEC_SEQ) // SEL_BLK)
SEL_ROWS = 48


def _nsa_sample_kernel(pt_ref, *refs):
    pages = refs[:N_PAGES]
    (q_ref, gate_ref, kc_ref, vc_ref, new_ref, cw_ref, wnew_ref, cover_ref, perm_ref, o_ref) = refs[N_PAGES:]
    t = DEC_SEQ
    lanes = NSA_REP * NSA_KV * t

    q = q_ref[0] * SCALE
    pieces = []
    for r in range(NSA_REP):
        qr = q[:, r * KV_WIDTH:(r + 1) * KV_WIDTH]
        for g in range(NSA_KV):
            pieces.append(jnp.where(_group_mask(KV_WIDTH, g), qr, 0.0))
    qm = jnp.concatenate(pieces, axis=0).astype(BF16)

    lane = lax.broadcasted_iota(jnp.int32, (1, lanes), 1)
    t_row = lane & (t - 1)
    g_row = lax.shift_right_logical(lane, 3) & (NSA_KV - 1)
    r_row = lax.shift_right_logical(lane, 5)
    head = (g_row * NSA_REP + r_row + 1).astype(F32)
    slope = jnp.exp2(-8.0 * head / NSA_HEADS)
    qpos = PAST_LEN + t_row

    def attend(keys_b, kpos_col, extra_ok=None, window=False):
        s = _dot_nt(keys_b, qm)
        d = (qpos - kpos_col).astype(F32)
        ok = d >= 0.0
        if window:
            ok = ok & (d < float(WINDOW))
        if extra_ok is not None:
            ok = ok & extra_ok
        return s - slope * d, ok

    kcb = kc_ref[0, 0].astype(BF16)
    vcb = vc_ref[0, 0].astype(BF16)
    cend = lax.broadcasted_iota(jnp.int32, (N_HALF, 1), 0) * CMP_STRIDE + (CMP_LEN - 1)
    s, ok = attend(kcb, cend)
    e, den = _softmax_cols(s, ok)
    pc = e / den
    o_cmp = _dot(pc.T.astype(BF16), vcb)
    hi, lo = _split_bf16(pc)
    imp = _dot(cover_ref[...], hi) + _dot(cover_ref[...], lo)
    imp = imp + pltpu.roll(imp, 32, 1) + pltpu.roll(imp, 64, 1) + pltpu.roll(imp, 96, 1)
    sel = _select_blocks(imp, lax.shift_right_logical(qpos, 6), SAMPLE_SEL_BLOCKS)

    scores, oks, vals = [], [], []
    for p in range(N_PAGES):
        kpg = pages[p][0, :, 0:KV_WIDTH].astype(BF16)
        vals.append(pages[p][0, :, KV_WIDTH:2 * KV_WIDTH].astype(BF16))
        kpos = p * PAGE_SIZE + lax.broadcasted_iota(jnp.int32, (PAGE_SIZE, 1), 0)
        chosen = jnp.concatenate(
            [jnp.broadcast_to(sel[2 * p + h:2 * p + h + 1, :], (SEL_BLK, lanes)) for h in range(2)], axis=0)
        s, ok = attend(kpg, kpos, chosen > 0.5)
        scores.append(s)
        oks.append(ok)
    pad_rows = lambda a: jnp.concatenate([a, jnp.zeros((128 - t, a.shape[-1]), F32)], axis=0)
    knew = pad_rows(new_ref[0, :, 0:KV_WIDTH]).astype(BF16)
    vals.append(pad_rows(new_ref[0, :, KV_WIDTH:2 * KV_WIDTH]).astype(BF16))
    row = lax.broadcasted_iota(jnp.int32, (128, 1), 0)
    last = SAMPLE_SEL_BLOCKS - 1
    s, ok = attend(knew, PAST_LEN + row, (jnp.broadcast_to(sel[last:last + 1, :], (128, lanes)) > 0.5) & (row < t))
    scores.append(s)
    oks.append(ok)
    e, den = _softmax_cols(jnp.concatenate(scores, axis=0), jnp.concatenate(oks, axis=0))
    prob = e / den
    o_sel = jnp.zeros((lanes, KV_WIDTH), F32)
    for p in range(N_PAGES + 1):
        o_sel = o_sel + _dot(prob[p * PAGE_SIZE:(p + 1) * PAGE_SIZE].T.astype(BF16), vals[p])

    wb = cw_ref.shape[1]
    kw = jnp.concatenate([cw_ref[0, :, 0:KV_WIDTH], wnew_ref[0, :, 0:KV_WIDTH],
                          jnp.zeros((128 - t, KV_WIDTH), F32)], axis=0).astype(BF16)
    vw = jnp.concatenate([cw_ref[0, :, KV_WIDTH:2 * KV_WIDTH], wnew_ref[0, :, KV_WIDTH:2 * KV_WIDTH],
                          jnp.zeros((128 - t, KV_WIDTH), F32)], axis=0).astype(BF16)
    row = lax.broadcasted_iota(jnp.int32, (wb + 128, 1), 0)
    kpos = PAST_LEN - wb + row
    s, ok = attend(kw, kpos, row < wb + t, window=True)
    e, den = _softmax_cols(s, ok)
    prob = e / den
    o_win = jnp.zeros((lanes, KV_WIDTH), F32)
    for c in range((wb + 128) // 128):
        o_win = o_win + _dot(prob[c * 128:(c + 1) * 128].T.astype(BF16), vw[c * 128:(c + 1) * 128])

    ghi, glo = _split_bf16(pad_rows(gate_ref[0]))
    grow = _dot(perm_ref[...], ghi) + _dot(perm_ref[...], glo)
    rowi = lax.broadcasted_iota(jnp.int32, (lanes, 1), 0)
    base = (lax.shift_right_logical(rowi, 3) & (NSA_KV - 1)) * 12 + lax.shift_right_logical(rowi, 5) * 3
    lane_i = lax.broadcasted_iota(jnp.int32, (1, 128), 1)

    def gate_col(branch):
        return jnp.sum(jnp.where(lane_i == base + branch, grow, 0.0), axis=-1, keepdims=True)

    mixed = gate_col(0) * o_cmp + gate_col(1) * o_sel + gate_col(2) * o_win
    for r in range(NSA_REP):
        acc = jnp.zeros((t, KV_WIDTH), F32)
        for g in range(NSA_KV):
            lo_row = (r * NSA_KV + g) * t
            acc = acc + jnp.where(_group_mask(KV_WIDTH, g), mixed[lo_row:lo_row + t], 0.0)
        o_ref[0, :, r * KV_WIDTH:(r + 1) * KV_WIDTH] = acc.astype(o_ref.dtype)


def _nsa_sample(nq, gates, cmp_tok, cache3, page_flat, kv_new, cache_win, win_new, cover_t, perm):
    b, t = DEC_BATCH, DEC_SEQ
    wb = cache_win.shape[1]
    per_b = lambda shape: pl.BlockSpec(shape, lambda bi, pt: (bi, 0, 0))
    page_specs = [pl.BlockSpec((1, PAGE_SIZE, 2 * KV_WIDTH),
                               functools.partial(lambda bi, pt, p: (pt[bi * N_PAGES + p], 0, 1), p=p))
                  for p in range(N_PAGES)]
    tok = lambda c: pl.BlockSpec((1, 1, N_HALF, KV_WIDTH), lambda bi, pt: (c, bi, 0, 0))
    grid_spec = pltpu.PrefetchScalarGridSpec(
        num_scalar_prefetch=1,
        grid=(b,),
        in_specs=page_specs + [per_b((1, t, NSA_WIDTH)), per_b((1, t, 128)), tok(0), tok(1),
                               pl.BlockSpec((1, t, 2 * KV_WIDTH), lambda bi, pt: (bi, 0, 1)),
                               per_b((1, wb, 2 * KV_WIDTH)), per_b((1, t, 2 * KV_WIDTH)),
                               pl.BlockSpec(cover_t.shape, lambda bi, pt: (0, 0)),
                               pl.BlockSpec(perm.shape, lambda bi, pt: (0, 0))],
        out_specs=per_b((1, t, NSA_WIDTH)),
    )
    return pl.pallas_call(
        _nsa_sample_kernel,
        grid_spec=grid_spec,
        out_shape=jax.ShapeDtypeStruct((b, t, NSA_WIDTH), BF16),
        compiler_params=_params(("arbitrary",)),
        name="nsa_sample",
    )(page_flat, *([cache3] * N_PAGES), nq.reshape(b, t, NSA_WIDTH), gates.reshape(b, t, 128), cmp_tok, cmp_tok,
      kv_new.reshape(b, t, 4 * KV_WIDTH), cache_win, win_new.reshape(b, t, 2 * KV_WIDTH), cover_t, perm)


def _out_kernel(x_ref, ret_ref, nsa_ref, wr_ref, wn_ref, o_ref):
    o_ref[...] = x_ref[...] + _dot(ret_ref[...], wr_ref[...]) + _dot(nsa_ref[...], wn_ref[...])


def _mixer_out(x2d, ret, nsa, w_ret, w_nsa, tm=1024, tn=512):
    n, d = x2d.shape
    return pl.pallas_call(
        _out_kernel,
        grid=(n // tm, d // tn),
        in_specs=[pl.BlockSpec((tm, tn), lambda i, j: (i, j)),
                  pl.BlockSpec((tm, RET_WIDTH), lambda i, j: (i, 0)),
                  pl.BlockSpec((tm, NSA_WIDTH), lambda i, j: (i, 0)),
                  pl.BlockSpec((RET_WIDTH, tn), lambda i, j: (0, j)),
                  pl.BlockSpec((NSA_WIDTH, tn), lambda i, j: (0, j))],
        out_specs=pl.BlockSpec((tm, tn), lambda i, j: (i, j)),
        out_shape=jax.ShapeDtypeStruct((n, d), F32),
        compiler_params=_params(("parallel", "arbitrary")),
        name="mixer_out",
    )(x2d, ret, nsa, w_ret, w_nsa)


FFN_TN = 512
FFN_COLS = D_FF // FFN_TN


def _ffn_up_kernel(*refs, hist, step, tiles_per_seq, has_prev):
    if has_prev:
        x_ref, wa_ref, wb_ref, cwa_ref, cwb_ref, cba_ref, cbb_ref, pa_ref, pb_ref = refs[:9]
        act_ref, sta_ref, stb_ref, exta, extb = refs[9:]
    else:
        x_ref, wa_ref, wb_ref, cwa_ref, cwb_ref, cba_ref, cbb_ref = refs[:7]
        act_ref, sta_ref, stb_ref, exta, extb = refs[7:]
        pa_ref = pb_ref = None
    tm = x_ref.shape[0]
    first = (pl.program_id(1) % tiles_per_seq) == 0
    x = x_ref[...]
    outs = []
    for w_ref, cw_ref, cb_ref, p_ref, st_ref, ext in ((wa_ref, cwa_ref, cba_ref, pa_ref, sta_ref, exta),
                                                     (wb_ref, cwb_ref, cbb_ref, pb_ref, stb_ref, extb)):
        u = _dot(x, w_ref[...])
        if has_prev:
            ext[hist - 2 * step:hist, :] = p_ref[...]
        else:
            @pl.when(first)
            def _():
                ext[hist - 2 * step:hist, :] = jnp.zeros((2 * step, u.shape[-1]), F32)
        ext[hist:hist + tm, :] = u
        cw = cw_ref[...]
        c = (cb_ref[...] + cw[2:3] * u + cw[1:2] * ext[hist - step:hist - step + tm, :]
             + cw[0:1] * ext[hist - 2 * step:hist - 2 * step + tm, :])
        outs.append(c)
        tail = ext[hist + tm - 2 * step:hist + tm, :]
        if has_prev:
            st_ref[...] = tail
        else:
            st_ref[0] = tail
            ext[hist - 2 * step:hist, :] = tail
    act_ref[...] = (jax.nn.silu(outs[0]) * outs[1]).astype(act_ref.dtype)


def _ffn_up(hn, w_up, conv_w, conv_b, prev, *, tm, step, seqs, tiles_per_seq):
    n, d = hn.shape
    tn = FFN_TN
    hist = 8 if step == 1 else 2 * step
    has_prev = prev is not None
    half = lambda off: (lambda j, i: (0, j + off))
    in_specs = [pl.BlockSpec((tm, d), lambda j, i: (i, 0)),
                pl.BlockSpec((d, tn), half(0)), pl.BlockSpec((d, tn), half(FFN_COLS)),
                pl.BlockSpec((CONV_W, tn), half(0)), pl.BlockSpec((CONV_W, tn), half(FFN_COLS)),
                pl.BlockSpec((1, tn), half(0)), pl.BlockSpec((1, tn), half(FFN_COLS))]
    args = [hn, w_up, w_up, conv_w, conv_w, conv_b, conv_b]
    if has_prev:
        in_specs += [pl.BlockSpec((2 * step, tn), half(0)), pl.BlockSpec((2 * step, tn), half(FFN_COLS))]
        args += [prev, prev]
        st_spec = pl.BlockSpec((2 * step, tn), lambda j, i: (0, j))
        st_shape = jax.ShapeDtypeStruct((2 * step, D_FF), F32)
    else:
        st_spec = pl.BlockSpec((1, 2, tn), lambda j, i: (i // tiles_per_seq, 0, j))
        st_shape = jax.ShapeDtypeStruct((seqs, 2, D_FF), F32)
    kernel = functools.partial(_ffn_up_kernel, hist=hist, step=step, tiles_per_seq=tiles_per_seq, has_prev=has_prev)
    return pl.pallas_call(
        kernel,
        grid=(FFN_COLS, n // tm),
        in_specs=in_specs,
        out_specs=[pl.BlockSpec((tm, tn), lambda j, i: (i, j)), st_spec, st_spec],
        out_shape=[jax.ShapeDtypeStruct((n, D_FF), BF16), st_shape, st_shape],
        scratch_shapes=[pltpu.VMEM((hist + tm, tn), F32), pltpu.VMEM((hist + tm, tn), F32)],
        compiler_params=_params(("arbitrary", "arbitrary")),
        name="ffn_up",
    )(*args)


def _ffn_down_kernel(h_ref, a_ref, w_ref, o_ref):
    o_ref[...] = h_ref[...] + _dot(a_ref[...], w_ref[...])


def _ffn_down(h, act, w_down, tm=512, tn=512):
    n, d = h.shape
    return pl.pallas_call(
        _ffn_down_kernel,
        grid=(n // tm, d // tn),
        in_specs=[pl.BlockSpec((tm, tn), lambda i, j: (i, j)),
                  pl.BlockSpec((tm, D_FF), lambda i, j: (i, 0)),
                  pl.BlockSpec((D_FF, tn), lambda i, j: (0, j))],
        out_specs=pl.BlockSpec((tm, tn), lambda i, j: (i, j)),
        out_shape=jax.ShapeDtypeStruct((n, d), F32),
        compiler_params=_params(("parallel", "arbitrary")),
        name="ffn_down",
    )(h, act, w_down)


def _block_cover_t(n_sel, rows):
    cs = np.arange(N_HALF - 1)[None, :] * CMP_STRIDE
    js = np.arange(n_sel)[:, None] * SEL_BLK
    cov = np.clip(np.minimum(cs + CMP_LEN, js + SEL_BLK) - np.maximum(cs, js), 0, None) / CMP_LEN
    out = np.zeros((rows, N_HALF), np.float32)
    out[:n_sel, :N_HALF - 1] = cov
    return jnp.asarray(out, dtype=BF16)


def _prepare_weights(g_attn, w_in, q_norm, k_norm_cmp, k_norm_slc, k_norm_win, cmp_pe, cmp_w1, cmp_b1, cmp_w2,
                     cmp_b2, w_out):
    d = D_MODEL
    o = 0
    w_rqk = w_in[:, o:o + 2 * RET_WIDTH]; o += 2 * RET_WIDTH
    w_rvg = w_in[:, o:o + 2 * RET_WIDTH]; o += 2 * RET_WIDTH
    w_nq = w_in[:, o:o + NSA_WIDTH]; o += NSA_WIDTH
    w_kv = w_in[:, o:o + 4 * KV_WIDTH]; o += 4 * KV_WIDTH
    w_win = w_in[:, o:o + 2 * KV_WIDTH]; o += 2 * KV_WIDTH
    w_ng = w_in[:, o:]
    w_nq = w_nq.reshape(d, NSA_KV, NSA_REP, HEAD_DIM).transpose(0, 2, 1, 3).reshape(d, NSA_WIDTH)
    w_ng = jnp.pad(w_ng, ((0, 0), (0, 128 - w_ng.shape[1])))
    tile4 = lambda v: jnp.tile(v, NSA_KV)
    zeros = jnp.zeros((KV_WIDTH,), F32)
    ones = jnp.ones((KV_WIDTH,), F32)
    wts = dict(
        g_attn=g_attn,
        w_rqk=w_rqk.astype(BF16), w_rvg=w_rvg.astype(BF16), w_nq=w_nq.astype(BF16),
        w_kv=w_kv.astype(BF16), w_win=w_win.astype(BF16), w_ng=w_ng.astype(BF16),
        rot_scale=jnp.concatenate([jnp.ones((RET_WIDTH,), F32),
                                   jnp.full((RET_WIDTH,), RET_DK ** -0.5, F32)]).reshape(1, -1),
        nq_gain=jnp.tile(q_norm, NSA_HEADS).reshape(1, -1), nq_mask=jnp.ones((1, NSA_WIDTH), F32),
        kv_gain=jnp.concatenate([zeros, zeros, tile4(k_norm_slc), zeros]).reshape(1, -1),
        kv_mask=jnp.concatenate([zeros, zeros, ones, zeros]).reshape(1, -1),
        win_gain=jnp.concatenate([tile4(k_norm_win), zeros]).reshape(1, -1),
        win_mask=jnp.concatenate([ones, zeros]).reshape(1, -1),
    )
    w1 = cmp_w1.reshape(2, 2, CMP_STRIDE, HEAD_DIM, CMP_HIDDEN)
    w1rep = jnp.broadcast_to(w1.transpose(0, 2, 3, 1, 4)[:, :, None],
                             (2, CMP_STRIDE, NSA_KV, HEAD_DIM, 2, CMP_HIDDEN))
    w1rep = w1rep.reshape(2, CMP_STRIDE * KV_WIDTH, 2 * CMP_HIDDEN).astype(BF16)
    w1flat = w1.reshape(2, 2, CMP_STRIDE * HEAD_DIM, CMP_HIDDEN).astype(BF16)
    pe8 = jnp.pad(cmp_pe.reshape(2, 2, 1, CMP_STRIDE * HEAD_DIM), ((0, 0), (0, 0), (0, 15), (0, 0)))
    bias = _cmp_bias(pe8, w1flat, cmp_b1.reshape(2, 1, CMP_HIDDEN))
    cw = dict(
        w1rep=w1rep, bias=bias,
        w2rep=jnp.tile(cmp_w2, (1, 1, NSA_KV)).astype(BF16),
        b2rep=jnp.tile(cmp_b2, (1, NSA_KV)).reshape(2, 1, KV_WIDTH),
        gain=jnp.stack([tile4(k_norm_cmp), ones]).reshape(2, 1, KV_WIDTH),
    )
    w_ret = w_out[:RET_WIDTH].astype(BF16)
    w_nsa = w_out[RET_WIDTH:].reshape(NSA_KV, NSA_REP, HEAD_DIM, d).transpose(1, 0, 2, 3).reshape(NSA_WIDTH, d)
    return wts, cw, w_ret, w_nsa.astype(BF16)


def kernel(x_prompt, x_sample, cache_kv, cache_win, state_ret, state_conv, page_table, g_attn, w_in, q_norm,
           k_norm_cmp, k_norm_slc, k_norm_win, cmp_pe, cmp_w1, cmp_b1, cmp_w2, cmp_b2, ret_gn, w_out, g_ffn, w_up,
           conv_w, conv_b, w_down):
    assert x_prompt.shape == (BATCH, SEQ, D_MODEL) and x_sample.shape == (DEC_BATCH, DEC_SEQ, D_MODEL)
    assert g_attn.shape[0] == 1, "single layer"
    wts, cw, w_ret, w_nsa = _prepare_weights(g_attn[0], w_in[0], q_norm[0], k_norm_cmp[0], k_norm_slc[0],
                                              k_norm_win[0], cmp_pe[0], cmp_w1[0], cmp_b1[0], cmp_w2[0],
                                              cmp_b2[0], w_out[0])
    w_up_b = w_up[0].astype(BF16)
    w_down_b = w_down[0].astype(BF16)
    cb = conv_b[0].reshape(1, -1)
    n_p, n_s = BATCH * SEQ, DEC_BATCH * DEC_SEQ
    xp = x_prompt.reshape(n_p, D_MODEL)
    xs = x_sample.reshape(n_s, D_MODEL)

    pos_p = jnp.tile(jnp.arange(SEQ, dtype=jnp.int32), BATCH)
    qk, vg, nq, kv_p, win_p, gates = _project_all(xp, pos_p, wts)
    ret_mix, ret_state_p = _retention_prompt(qk, vg, ret_gn[0])
    kv3 = kv_p.reshape(BATCH, SEQ, 4 * KV_WIDTH)
    cmp_tok = _compress_prompt(kv3, cw)
    cover_p = _block_cover_t(SEQ // SEL_BLK, SEQ // SEL_BLK)
    key = np.arange(SEQ).reshape(SEQ // SEL_CHUNK, 1, SEL_CHUNK)
    expand = jnp.asarray((key // SEL_BLK) == np.arange(128)[None, :, None], dtype=BF16)
    nsa = _nsa_prompt(nq, gates, cmp_tok, kv_p, win_p, cover_p, expand)
    h_p = _mixer_out(xp, ret_mix.reshape(n_p, RET_WIDTH), nsa.reshape(n_p, NSA_WIDTH), w_ret, w_nsa)
    hn_p = _rmsnorm(h_p, g_ffn[0])
    act_p, st_a, st_b = _ffn_up(hn_p, w_up_b, conv_w[0], cb, None, tm=1024, step=1, seqs=BATCH,
                                tiles_per_seq=SEQ // 1024)
    y_p = _ffn_down(h_p, act_p, w_down_b)
    conv_p = jnp.concatenate([st_a, st_b], axis=-1)

    pos_s = jnp.tile(PAST_LEN + jnp.arange(DEC_SEQ, dtype=jnp.int32), DEC_BATCH)
    qk, vg, nq, kv_s, win_s, gates = _project_all(xs, pos_s, wts)
    ret_mix, ret_state_s = _retention_sample(qk, vg, state_ret[0], ret_gn[0])
    cache3 = cache_kv[0].reshape(cache_kv.shape[1], PAGE_SIZE, 4 * KV_WIDTH)
    page_flat = page_table.reshape(-1).astype(jnp.int32)
    cmp_tok = _compress_sample(cache3, page_flat, cw)
    wb = cache_win.shape[2]
    cwin = cache_win[0].reshape(DEC_BATCH, wb, 2 * KV_WIDTH)
    cover_s = _block_cover_t(SAMPLE_SEL_BLOCKS, SEL_ROWS)
    lanes = np.arange(128)
    perm = jnp.asarray((lanes[:, None] % DEC_SEQ) == lanes[None, :], dtype=BF16)
    nsa = _nsa_sample(nq, gates, cmp_tok, cache3, page_flat, kv_s, cwin, win_s, cover_s, perm)
    h_s = _mixer_out(xs, ret_mix.reshape(n_s, RET_WIDTH), nsa.reshape(n_s, NSA_WIDTH), w_ret, w_nsa)
    to_tm = lambda a: a.reshape(DEC_BATCH, -1, a.shape[-1]).transpose(1, 0, 2).reshape(-1, a.shape[-1])
    from_tm = lambda a, t: a.reshape(t, DEC_BATCH, a.shape[-1]).transpose(1, 0, 2)
    h_tm = to_tm(h_s)
    hn_s = _rmsnorm(h_tm, g_ffn[0])
    act_s, st_a, st_b = _ffn_up(hn_s, w_up_b, conv_w[0], cb, to_tm(state_conv[0]), tm=n_s, step=DEC_BATCH,
                                seqs=DEC_BATCH, tiles_per_seq=1)
    y_s = from_tm(_ffn_down(h_tm, act_s, w_down_b), DEC_SEQ)
    conv_s = from_tm(jnp.concatenate([st_a, st_b], axis=-1), CONV_W - 1)

    win_sample = jnp.concatenate([cwin[:, DEC_SEQ:], win_s.reshape(DEC_BATCH, DEC_SEQ, -1)], axis=1)
    return (
        y_p.reshape(BATCH, SEQ, D_MODEL),
        y_s,
        kv_p.reshape(1, BATCH, SEQ, 4, NSA_KV, HEAD_DIM),
        kv_s.reshape(1, DEC_BATCH, DEC_SEQ, 4, NSA_KV, HEAD_DIM),
        win_p.reshape(BATCH, SEQ, 2 * KV_WIDTH)[:, SEQ - WINDOW:].reshape(1, BATCH, WINDOW, 2, NSA_KV, HEAD_DIM),
        win_sample.reshape(1, DEC_BATCH, wb, 2, NSA_KV, HEAD_DIM),
        ret_state_p[None],
        ret_state_s[None],
        conv_p[None],
        conv_s[None],
    )
```

```python
import functools

import numpy as np
import jax
import jax.numpy as jnp
from jax import lax
from jax.experimental import pallas as pl
from jax.experimental.pallas import tpu as pltpu

D_MODEL = 2048
BATCH = 4
SEQ = 2048
DEC_BATCH = 128
DEC_SEQ = 8
PAST_LEN = 2048
PAGE_SIZE = 128
N_PAGES = PAST_LEN // PAGE_SIZE
RET_HEADS = 4
RET_DK = 256
RET_DV = 256
RET_WIDTH = RET_HEADS * RET_DV
RET_CHUNK = 128
NSA_HEADS = 16
NSA_KV = 4
NSA_REP = NSA_HEADS // NSA_KV
HEAD_DIM = 64
NSA_WIDTH = NSA_HEADS * HEAD_DIM
KV_WIDTH = NSA_KV * HEAD_DIM
CMP_LEN = 32
CMP_STRIDE = 16
CMP_HIDDEN = 256
SEL_BLK = 64
SEL_TOPK = 16
WINDOW = 512
SCALE = HEAD_DIM ** -0.5
D_FF = 5632
CONV_W = 3
EPS = 1e-6

N_HALF = PAST_LEN // CMP_STRIDE
NEG = -1e30
FORCED = 1e30
VMEM_LIMIT = 56 * 1024 * 1024

BF16 = jnp.bfloat16
F32 = jnp.float32


def _params(semantics, vmem=VMEM_LIMIT):
    return pltpu.CompilerParams(dimension_semantics=semantics, vmem_limit_bytes=vmem)


def _dot(a, b):
    return jnp.dot(a, b, preferred_element_type=F32)


def _dot_nt(a, b):
    return lax.dot_general(a, b, (((1,), (1,)), ((), ())), preferred_element_type=F32)


def _split_bf16(x):
    hi = x.astype(BF16)
    lo = (x - hi.astype(F32)).astype(BF16)
    return hi, lo


def _group_mask(width, g):
    lane = lax.broadcasted_iota(jnp.int32, (1, width), 1)
    return (lane >= g * HEAD_DIM) & (lane < (g + 1) * HEAD_DIM)


def _rms_groups64(z, gain):
    width = z.shape[-1]
    outs = []
    for k in range(width // 128):
        zk = z[:, k * 128:(k + 1) * 128]
        zz = zk * zk
        lane = lax.broadcasted_iota(jnp.int32, (1, 128), 1)
        lo_half = lane < HEAD_DIM
        s_lo = jnp.sum(jnp.where(lo_half, zz, 0.0), axis=-1, keepdims=True)
        s_hi = jnp.sum(jnp.where(lo_half, 0.0, zz), axis=-1, keepdims=True)
        ms = jnp.where(lo_half, s_lo, s_hi) * (1.0 / HEAD_DIM)
        outs.append(zk * lax.rsqrt(ms + EPS))
    y = outs[0] if len(outs) == 1 else jnp.concatenate(outs, axis=-1)
    return y * gain


def _rmsnorm_kernel(x_ref, g_ref, o_ref):
    x = x_ref[...]
    ms = jnp.mean(x * x, axis=-1, keepdims=True)
    o_ref[...] = (x * lax.rsqrt(ms + EPS) * g_ref[...]).astype(o_ref.dtype)


def _rmsnorm(x, g, tm=512):
    n, d = x.shape
    return pl.pallas_call(
        _rmsnorm_kernel,
        grid=(n // tm,),
        in_specs=[pl.BlockSpec((tm, d), lambda i: (i, 0)), pl.BlockSpec((1, d), lambda i: (0, 0))],
        out_specs=pl.BlockSpec((tm, d), lambda i: (i, 0)),
        out_shape=jax.ShapeDtypeStruct((n, d), BF16),
        compiler_params=_params(("parallel",)),
        name="rmsnorm",
    )(x, g.reshape(1, d))


def _proj_plain_kernel(x_ref, w_ref, o_ref):
    o_ref[...] = _dot(x_ref[...], w_ref[...]).astype(o_ref.dtype)


def _proj_sigmoid_kernel(x_ref, w_ref, o_ref):
    o_ref[...] = jax.nn.sigmoid(_dot(x_ref[...], w_ref[...]))


def _proj_norm_kernel(x_ref, w_ref, gain_ref, nmask_ref, o_ref):
    z = _dot(x_ref[...], w_ref[...])
    o_ref[...] = jnp.where(nmask_ref[...] > 0.5, _rms_groups64(z, gain_ref[...]), z)


def _proj_rot_kernel(x_ref, w_ref, cos_ref, sin_ref, scale_ref, o_ref):
    z = _dot(x_ref[...], w_ref[...])
    c = cos_ref[...]
    s = sin_ref[...]
    sc = scale_ref[...]
    for hh in range(z.shape[-1] // RET_DK):
        x1 = z[:, hh * RET_DK:hh * RET_DK + 128]
        x2 = z[:, hh * RET_DK + 128:(hh + 1) * RET_DK]
        o_ref[:, hh * RET_DK:hh * RET_DK + 128] = (x1 * c - x2 * s) * sc[:, hh * RET_DK:hh * RET_DK + 128]
        o_ref[:, hh * RET_DK + 128:(hh + 1) * RET_DK] = (x2 * c + x1 * s) * sc[:, hh * RET_DK + 128:(hh + 1) * RET_DK]


def _proj(xn, w, kernel, extras=(), extra_specs=(), out_dtype=F32, tm=1024, tn=512, name="proj"):
    n, d = xn.shape
    c = w.shape[1]
    tn = min(tn, c)
    return pl.pallas_call(
        kernel,
        grid=(n // tm, c // tn),
        in_specs=[pl.BlockSpec((tm, d), lambda i, j: (i, 0)), pl.BlockSpec((d, tn), lambda i, j: (0, j))]
        + list(extra_specs),
        out_specs=pl.BlockSpec((tm, tn), lambda i, j: (i, j)),
        out_shape=jax.ShapeDtypeStruct((n, c), out_dtype),
        compiler_params=_params(("parallel", "arbitrary")),
        name=name,
    )(xn, w, *extras)


def _project_all(x2d, pos, wts, tm=1024):
    n = x2d.shape[0]
    xn = _rmsnorm(x2d, wts["g_attn"])
    half = RET_DK // 2
    inv = 1.0 / (10000.0 ** jnp.linspace(0.0, 1.0, half, dtype=F32))
    ang = pos.astype(F32)[:, None] * inv[None, :]
    cos = jnp.cos(ang)
    sin = jnp.sin(ang)
    tn = 512
    row_spec = pl.BlockSpec((tm, 128), lambda i, j: (i, 0))
    col_spec = pl.BlockSpec((1, tn), lambda i, j: (0, j))
    qk = _proj(xn, wts["w_rqk"], _proj_rot_kernel, (cos, sin, wts["rot_scale"]),
               (row_spec, row_spec, col_spec), tm=tm, name="proj_rot")
    vg = _proj(xn, wts["w_rvg"], _proj_plain_kernel, tm=tm, name="proj_vg")
    nq = _proj(xn, wts["w_nq"], _proj_norm_kernel, (wts["nq_gain"], wts["nq_mask"]),
               (col_spec, col_spec), tm=tm, name="proj_nq")
    kv = _proj(xn, wts["w_kv"], _proj_norm_kernel, (wts["kv_gain"], wts["kv_mask"]),
               (col_spec, col_spec), tm=tm, name="proj_kv")
    win = _proj(xn, wts["w_win"], _proj_norm_kernel, (wts["win_gain"], wts["win_mask"]),
                (col_spec, col_spec), tm=tm, name="proj_win")
    gates = _proj(xn, wts["w_ng"], _proj_sigmoid_kernel, tm=tm, name="proj_gate")
    return qk, vg, nq, kv, win, gates


def _retention_step(q, k, v, state, dmask, xi, zeta, gch):
    qb = q.astype(BF16)
    kb = k.astype(BF16)
    vb = v.astype(BF16)
    s = _dot_nt(qb, kb) * dmask
    o = _dot(s.astype(BF16), vb) + _dot(qb, state.astype(BF16)) * xi
    kz = (k * zeta).astype(BF16)
    new_state = state * gch + _dot(kz.T, vb)
    return o, new_state


def _ret_mix(o, gn, gate):
    ms = jnp.mean(o * o, axis=-1, keepdims=True)
    return o * lax.rsqrt(ms + EPS) * gn * jax.nn.silu(gate)


def _ret_prompt_kernel(q_ref, k_ref, v_ref, g_ref, dmask_ref, xi_ref, zeta_ref, gch_ref, gn_ref,
                       mix_ref, state_ref):
    @pl.when(pl.program_id(2) == 0)
    def _():
        state_ref[...] = jnp.zeros_like(state_ref)

    o, new_state = _retention_step(q_ref[0], k_ref[0], v_ref[0], state_ref[0, 0], dmask_ref[0],
                                   xi_ref[0], zeta_ref[0], gch_ref[0])
    state_ref[0, 0] = new_state
    mix_ref[0] = _ret_mix(o, gn_ref[0], g_ref[0]).astype(mix_ref.dtype)


def _ret_tables(chunk):
    h = jnp.arange(RET_HEADS, dtype=F32)
    lg = jnp.log(1.0 - jnp.exp2(-5.0 - h))
    i = jnp.arange(chunk, dtype=F32)
    diff = i[:, None] - i[None, :]
    dmask = jnp.where(diff >= 0, jnp.exp(lg[:, None, None] * jnp.maximum(diff, 0.0)), 0.0)
    xi = jnp.exp(lg[:, None] * (i[None, :] + 1.0))
    zeta = jnp.exp(lg[:, None] * (chunk - 1.0 - i[None, :]))
    gch = jnp.exp(lg * chunk)
    bc = lambda a: jnp.broadcast_to(a[:, :, None], (RET_HEADS, chunk, RET_DV))
    return dmask, bc(xi), bc(zeta), jnp.broadcast_to(gch[:, None, None], (RET_HEADS, 1, RET_DV))


def _retention_prompt(qk, vg, gn):
    b, t = BATCH, SEQ
    c = RET_CHUNK
    qk3 = qk.reshape(b, t, 2 * RET_WIDTH)
    vg3 = vg.reshape(b, t, 2 * RET_WIDTH)
    dmask, xi, zeta, gch = _ret_tables(c)
    tile = lambda off: pl.BlockSpec((1, c, RET_DK), lambda bi, hi, ci: (bi, ci, hi + off))
    head = lambda rows: pl.BlockSpec((1, rows, RET_DV), lambda bi, hi, ci: (hi, 0, 0))
    return pl.pallas_call(
        _ret_prompt_kernel,
        grid=(b, RET_HEADS, t // c),
        in_specs=[tile(0), tile(RET_HEADS), tile(0), tile(RET_HEADS),
                  pl.BlockSpec((1, c, c), lambda bi, hi, ci: (hi, 0, 0)), head(c), head(c), head(1), head(1)],
        out_specs=[pl.BlockSpec((1, c, RET_DV), lambda bi, hi, ci: (bi, ci, hi)),
                   pl.BlockSpec((1, 1, RET_DK, RET_DV), lambda bi, hi, ci: (bi, hi, 0, 0))],
        out_shape=[jax.ShapeDtypeStruct((b, t, RET_WIDTH), BF16),
                   jax.ShapeDtypeStruct((b, RET_HEADS, RET_DK, RET_DV), F32)],
        compiler_params=_params(("parallel", "parallel", "arbitrary")),
        name="retention_prompt",
    )(qk3, qk3, vg3, vg3, dmask, xi, zeta, gch, gn.reshape(RET_HEADS, 1, RET_DV))


def _ret_sample_kernel(qk_ref, vg_ref, s0_ref, dmask_ref, xi_ref, zeta_ref, gch_ref, gn_ref, mix_ref, state_ref):
    t = qk_ref.shape[1]
    pad = lambda a: jnp.concatenate([a, jnp.zeros((RET_PAD - t, a.shape[-1]), F32)], axis=0)
    for h in range(RET_HEADS):
        lo, hi = h * RET_DK, (h + 1) * RET_DK
        o, new_state = _retention_step(pad(qk_ref[0, :, lo:hi]), pad(qk_ref[0, :, RET_WIDTH + lo:RET_WIDTH + hi]),
                                       pad(vg_ref[0, :, lo:hi]), s0_ref[0, h], dmask_ref[h], xi_ref[h],
                                       zeta_ref[h], gch_ref[h])
        state_ref[0, h] = new_state
        mix_ref[0, :, lo:hi] = _ret_mix(o[0:t], gn_ref[h], vg_ref[0, :, RET_WIDTH + lo:RET_WIDTH + hi]).astype(mix_ref.dtype)


RET_PAD = 128


def _retention_sample(qk, vg, state0, gn):
    b, t = DEC_BATCH, DEC_SEQ
    dmask, xi, zeta, gch = _ret_tables(t)
    dmask = jnp.pad(dmask, ((0, 0), (0, RET_PAD - t), (0, RET_PAD - t)))
    xi = jnp.pad(xi, ((0, 0), (0, RET_PAD - t), (0, 0)))
    zeta = jnp.pad(zeta, ((0, 0), (0, RET_PAD - t), (0, 0)))
    full = lambda shape: pl.BlockSpec(shape, lambda bi: (0,) * len(shape))
    return pl.pallas_call(
        _ret_sample_kernel,
        grid=(b,),
        in_specs=[pl.BlockSpec((1, t, 2 * RET_WIDTH), lambda bi: (bi, 0, 0)),
                  pl.BlockSpec((1, t, 2 * RET_WIDTH), lambda bi: (bi, 0, 0)),
                  pl.BlockSpec((1, RET_HEADS, RET_DK, RET_DV), lambda bi: (bi, 0, 0, 0)),
                  full((RET_HEADS, RET_PAD, RET_PAD)), full((RET_HEADS, RET_PAD, RET_DV)), full((RET_HEADS, RET_PAD, RET_DV)),
                  full((RET_HEADS, 1, RET_DV)), full((RET_HEADS, 1, RET_DV))],
        out_specs=[pl.BlockSpec((1, t, RET_WIDTH), lambda bi: (bi, 0, 0)),
                   pl.BlockSpec((1, RET_HEADS, RET_DK, RET_DV), lambda bi: (bi, 0, 0, 0))],
        out_shape=[jax.ShapeDtypeStruct((b, t, RET_WIDTH), BF16),
                   jax.ShapeDtypeStruct((b, RET_HEADS, RET_DK, RET_DV), F32)],
        compiler_params=_params(("parallel",)),
        name="retention_sample",
    )(qk.reshape(b, t, -1), vg.reshape(b, t, -1), state0, dmask, xi, zeta, gch, gn.reshape(RET_HEADS, 1, RET_DV))


def _cmp_bias_kernel(pe_ref, w_ref, b1_ref, o_ref):
    acc = b1_ref[0]
    for j in range(2):
        acc = acc + _dot(pe_ref[0, j].astype(BF16), w_ref[0, j])[0:1]
    o_ref[0] = acc


def _cmp_bias(pe8, w1flat, b1):
    return pl.pallas_call(
        _cmp_bias_kernel,
        grid=(2,),
        in_specs=[pl.BlockSpec((1, 2, 16, CMP_STRIDE * HEAD_DIM), lambda c: (c, 0, 0, 0)),
                  pl.BlockSpec((1, 2, CMP_STRIDE * HEAD_DIM, CMP_HIDDEN), lambda c: (c, 0, 0, 0)),
                  pl.BlockSpec((1, 1, CMP_HIDDEN), lambda c: (c, 0, 0))],
        out_specs=pl.BlockSpec((1, 1, CMP_HIDDEN), lambda c: (c, 0, 0)),
        out_shape=jax.ShapeDtypeStruct((2, 1, CMP_HIDDEN), F32),
        compiler_params=_params(("parallel",)),
        name="cmp_bias",
    )(pe8, w1flat, b1)


def _compress_body(load_rows, w1_ref, bias_ref, w2_ref, b2_ref, gain_ref, o_ref, xm_ref):
    for l in range(CMP_STRIDE):
        xl = load_rows(l)
        for g in range(NSA_KV):
            xm_ref[g * N_HALF:(g + 1) * N_HALF, l * KV_WIDTH:(l + 1) * KV_WIDTH] = jnp.where(
                _group_mask(KV_WIDTH, g), xl, 0.0).astype(BF16)
    acc = _dot(xm_ref[...], w1_ref[0])
    h0 = acc[:, :CMP_HIDDEN]
    h1 = pltpu.roll(acc[:, CMP_HIDDEN:], 4 * N_HALF - 1, 0)
    hid = jax.nn.silu(h0 + h1 + bias_ref[0])
    out = _dot(hid.astype(BF16), w2_ref[0]) + b2_ref[0]
    res = jnp.zeros((N_HALF, KV_WIDTH), F32)
    for g in range(NSA_KV):
        res = res + jnp.where(_group_mask(KV_WIDTH, g), out[g * N_HALF:(g + 1) * N_HALF], 0.0)
    normed = _rms_groups64(res, gain_ref[0])
    o_ref[0, 0] = jnp.where(pl.program_id(0) == 0, normed, res)


def _compress_prompt_kernel(lo_ref, hi_ref, w1_ref, bias_ref, w2_ref, b2_ref, gain_ref, o_ref, xm_ref):
    load = lambda l: jnp.concatenate(
        [r[0, pl.ds(l, N_HALF, stride=CMP_STRIDE), :] for r in (lo_ref, hi_ref)], axis=-1)
    _compress_body(load, w1_ref, bias_ref, w2_ref, b2_ref, gain_ref, o_ref, xm_ref)


def _compress_sample_kernel(pt_ref, *refs):
    pages = refs[:N_PAGES]
    w1_ref, bias_ref, w2t_ref, b2_ref, gain_ref, o_ref, t_ref, xm_ref = refs[N_PAGES:]
    for p in range(N_PAGES):
        xt = pages[p][0, 0]
        for pair in range(2):
            t_ref[pair, p * PAGE_SIZE:(p + 1) * PAGE_SIZE, :] = xt[pair * 128:(pair + 1) * 128, :].T
    lane = lax.broadcasted_iota(jnp.int32, (1, 128), 1)
    for l in range(CMP_STRIDE):
        for pair in range(2):
            rows = t_ref[pair, pl.ds(l, N_HALF, stride=CMP_STRIDE), :]
            for member in range(2):
                g = 2 * pair + member
                keep = (lane >= member * HEAD_DIM) & (lane < (member + 1) * HEAD_DIM)
                xm_ref[g * N_HALF:(g + 1) * N_HALF, l * 128:(l + 1) * 128] = jnp.where(keep, rows, 0.0).astype(BF16)
    acc = _dot(xm_ref[...], w1_ref[0])
    h0 = acc[:, :CMP_HIDDEN]
    h1 = pltpu.roll(acc[:, CMP_HIDDEN:], 4 * N_HALF - 1, 0)
    hid = jax.nn.silu(h0 + h1 + bias_ref[0]).astype(BF16)
    outs = []
    for g in range(NSA_KV):
        og = _dot_nt(w2t_ref[0], hid[g * N_HALF:(g + 1) * N_HALF]) + b2_ref[0]
        ms = jnp.mean(og * og, axis=0, keepdims=True)
        normed = og * lax.rsqrt(ms + EPS) * gain_ref[0]
        outs.append(jnp.where(pl.program_id(0) == 0, normed, og))
    o_ref[0, 0] = jnp.concatenate(outs, axis=0)


def _cmp_weight_specs(nargs):
    cmap = (lambda c, b: (c, 0, 0)) if nargs == 2 else (lambda c, b, pt: (c, 0, 0))
    return [pl.BlockSpec((1, CMP_STRIDE * KV_WIDTH, 2 * CMP_HIDDEN), cmap),
            pl.BlockSpec((1, 1, CMP_HIDDEN), cmap),
            pl.BlockSpec((1, CMP_HIDDEN, KV_WIDTH), cmap),
            pl.BlockSpec((1, 1, KV_WIDTH), cmap),
            pl.BlockSpec((1, 1, KV_WIDTH), cmap)]


def _compress_prompt(kv3, cw):
    b = kv3.shape[0]
    return pl.pallas_call(
        _compress_prompt_kernel,
        grid=(2, b),
        in_specs=[pl.BlockSpec((1, SEQ, 128), lambda c, bi: (bi, 0, 2 * c)),
                  pl.BlockSpec((1, SEQ, 128), lambda c, bi: (bi, 0, 2 * c + 1))] + _cmp_weight_specs(2),
        out_specs=pl.BlockSpec((1, 1, N_HALF, KV_WIDTH), lambda c, bi: (c, bi, 0, 0)),
        out_shape=jax.ShapeDtypeStruct((2, b, N_HALF, KV_WIDTH), F32),
        scratch_shapes=[pltpu.VMEM((NSA_KV * N_HALF, CMP_STRIDE * KV_WIDTH), BF16)],
        compiler_params=_params(("arbitrary", "arbitrary")),
        name="compress_prompt",
    )(kv3, kv3, cw["w1rep"], cw["bias"], cw["w2rep"], cw["b2rep"], cw["gain"])


def _compress_sample(cache_t, page_flat, cw):
    b = DEC_BATCH
    page_specs = [pl.BlockSpec((1, 1, KV_WIDTH, PAGE_SIZE),
                               functools.partial(lambda c, bi, pt, p: (pt[bi * N_PAGES + p], c, 0, 0), p=p))
                  for p in range(N_PAGES)]
    cmap = lambda c, bi, pt: (c, 0, 0)
    grid_spec = pltpu.PrefetchScalarGridSpec(
        num_scalar_prefetch=1,
        grid=(2, b),
        in_specs=page_specs + [pl.BlockSpec((1, CMP_STRIDE * 128, 2 * CMP_HIDDEN), cmap),
                               pl.BlockSpec((1, 1, CMP_HIDDEN), cmap),
                               pl.BlockSpec((1, HEAD_DIM, CMP_HIDDEN), cmap),
                               pl.BlockSpec((1, HEAD_DIM, N_HALF), cmap),
                               pl.BlockSpec((1, HEAD_DIM, N_HALF), cmap)],
        out_specs=pl.BlockSpec((1, 1, KV_WIDTH, N_HALF), lambda c, bi, pt: (c, bi, 0, 0)),
        scratch_shapes=[pltpu.VMEM((2, PAST_LEN, 128), F32),
                        pltpu.VMEM((NSA_KV * N_HALF, CMP_STRIDE * 128), BF16)],
    )
    return pl.pallas_call(
        _compress_sample_kernel,
        grid_spec=grid_spec,
        out_shape=jax.ShapeDtypeStruct((2, b, KV_WIDTH, N_HALF), F32),
        compiler_params=_params(("arbitrary", "arbitrary")),
        name="compress_sample",
    )(page_flat, *([cache_t] * N_PAGES), cw["w1pair"], cw["bias"], cw["w2t"], cw["b2col"], cw["gaincol"])


def _select_blocks(imp, cur, n_blocks):
    rows = imp.shape[0]
    j = lax.broadcasted_iota(jnp.int32, (rows, 1), 0)
    valid = j <= cur
    forced = (j == 0) | (j == cur) | (j == cur - 1)
    score = jnp.where(valid, jnp.where(forced, FORCED, imp), NEG)
    rank = jnp.zeros(score.shape, F32)
    for i in range(n_blocks):
        row = score[i:i + 1, :]
        ahead = (row > score) | ((row == score) & (j > i))
        rank = rank + jnp.where(ahead, 1.0, 0.0)
    return jnp.where(valid & (rank < float(SEL_TOPK)) & (j < n_blocks), 1.0, 0.0)


def _softmax_rows(s, valid):
    s = jnp.where(valid, s, NEG)
    m = jnp.max(s, axis=-1, keepdims=True)
    e = jnp.where(valid, jnp.exp(s - m), 0.0)
    return e, jnp.maximum(jnp.sum(e, axis=-1, keepdims=True), 1e-30)


def _softmax_cols(s, valid):
    s = jnp.where(valid, s, NEG)
    m = jnp.max(s, axis=0, keepdims=True)
    e = jnp.where(valid, jnp.exp(s - m), 0.0)
    return e, jnp.maximum(jnp.sum(e, axis=0, keepdims=True), 1e-30)


def _slope(g, r):
    return float(2.0 ** (-8.0 * (g * NSA_REP + r + 1) / NSA_HEADS))


NSA_TQ = 128
SEL_CHUNK = 512
WIN_KEYS = WINDOW + NSA_TQ


def _nsa_prompt_kernel(q_ref, gate_ref, kc_ref, vc_ref, ks_ref, vs_ref, kw_ref, vw_ref, cover_ref, expand_ref,
                       o_ref, ksb, vsb, kwb, vwb):
    i = pl.program_id(1)
    tq = NSA_TQ
    rows = NSA_REP * tq

    @pl.when(i == 0)
    def _():
        ksb[...] = ks_ref[0].astype(BF16)
        vsb[...] = vs_ref[0].astype(BF16)
        kwb[...] = kw_ref[0].astype(BF16)
        vwb[...] = vw_ref[0].astype(BF16)

    q = q_ref[0] * SCALE
    gates = gate_ref[0]
    kcb = kc_ref[0, 0].astype(BF16)
    vcb = vc_ref[0, 0].astype(BF16)
    q0 = i * tq
    qpos_col = q0 + lax.broadcasted_iota(jnp.int32, (tq, 1), 0)
    qpos4 = jnp.concatenate([qpos_col] * NSA_REP, axis=0)
    qpos_row = q0 + lax.broadcasted_iota(jnp.int32, (1, tq), 1)
    cur_row = lax.shift_right_logical(qpos_row, 6)
    win_start = pl.multiple_of(jnp.maximum(q0 - WINDOW, 0), NSA_TQ)
    n_chunks = lax.shift_right_logical(q0 + tq + SEL_CHUNK - 1, 9)

    out = [jnp.zeros((tq, KV_WIDTH), F32) for _ in range(NSA_REP)]
    for g in range(NSA_KV):
        gm = _group_mask(KV_WIDTH, g)
        qg = jnp.concatenate(
            [jnp.where(gm, q[:, r * KV_WIDTH:(r + 1) * KV_WIDTH], 0.0) for r in range(NSA_REP)], axis=0
        ).astype(BF16)
        slope = jnp.concatenate([jnp.full((tq, 1), _slope(g, r), F32) for r in range(NSA_REP)], axis=0)

        def gate_col(branch):
            cols = [gates[:, g * 12 + r * 3 + branch:g * 12 + r * 3 + branch + 1] for r in range(NSA_REP)]
            return jnp.concatenate(cols, axis=0)

        n_idx = lax.broadcasted_iota(jnp.int32, (1, N_HALF), 1)
        dist = (qpos4 - (n_idx * CMP_STRIDE + (CMP_LEN - 1))).astype(F32)
        valid = dist >= 0.0
        e, den = _softmax_rows(_dot_nt(qg, kcb) - slope * dist, valid)
        pc = e / den
        o_cmp = _dot(pc.astype(BF16), vcb)
        pcsum = pc[0:tq]
        for r in range(1, NSA_REP):
            pcsum = pcsum + pc[r * tq:(r + 1) * tq]
        hi, lo = _split_bf16(pcsum)
        imp = _dot_nt(cover_ref[...], hi) + _dot_nt(cover_ref[...], lo)
        sel = _select_blocks(imp, cur_row, SEQ // SEL_BLK)
        sel = jnp.concatenate([sel, jnp.zeros((128 - sel.shape[0], tq), F32)], axis=0)
        sel_q = sel.T.astype(BF16)
        sel4 = jnp.concatenate([sel_q] * NSA_REP, axis=0)

        def sel_step(kk, carry):
            m, l, acc = carry
            ks = pl.multiple_of(kk * SEL_CHUNK, SEL_CHUNK)
            kpos = ks + lax.broadcasted_iota(jnp.int32, (1, SEL_CHUNK), 1)
            d = (qpos4 - kpos).astype(F32)
            chosen = _dot(sel4, expand_ref[kk])
            ok = (chosen > 0.5) & (d >= 0.0)
            s = jnp.where(ok, _dot_nt(qg, ksb[pl.ds(ks, SEL_CHUNK), :]) - slope * d, NEG)
            m_new = jnp.maximum(m, jnp.max(s, axis=-1, keepdims=True))
            p = jnp.where(ok, jnp.exp(s - m_new), 0.0)
            alpha = jnp.exp(m - m_new)
            l = alpha * l + jnp.sum(p, axis=-1, keepdims=True)
            acc = alpha * acc + _dot(p.astype(BF16), vsb[pl.ds(ks, SEL_CHUNK), :])
            return m_new, l, acc

        m0 = jnp.full((rows, 1), NEG, F32)
        l0 = jnp.zeros((rows, 1), F32)
        a0 = jnp.zeros((rows, KV_WIDTH), F32)
        _, l_sel, acc_sel = lax.fori_loop(0, n_chunks, sel_step, (m0, l0, a0))
        o_sel = acc_sel / jnp.maximum(l_sel, 1e-30)

        kpos = win_start + lax.broadcasted_iota(jnp.int32, (1, WIN_KEYS), 1)
        d = (qpos4 - kpos).astype(F32)
        ok = (d >= 0.0) & (d < float(WINDOW))
        e, den = _softmax_rows(_dot_nt(qg, kwb[pl.ds(win_start, WIN_KEYS), :]) - slope * d, ok)
        o_win = _dot(e.astype(BF16), vwb[pl.ds(win_start, WIN_KEYS), :]) / den

        mixed = gate_col(0) * o_cmp + gate_col(1) * o_sel + gate_col(2) * o_win
        for r in range(NSA_REP):
            out[r] = out[r] + jnp.where(gm, mixed[r * tq:(r + 1) * tq], 0.0)

    for r in range(NSA_REP):
        o_ref[0, :, r * KV_WIDTH:(r + 1) * KV_WIDTH] = out[r].astype(o_ref.dtype)


def _nsa_prompt(nq, gates, cmp_tok, kv, win, cover_t, expand):
    b, t, tq = BATCH, SEQ, NSA_TQ
    nq3 = nq.reshape(b, t, NSA_WIDTH)
    g3 = gates.reshape(b, t, 128)
    kv3 = kv.reshape(b, t, 4 * KV_WIDTH)
    win3 = win.reshape(b, t, 2 * KV_WIDTH)
    col = lambda c: pl.BlockSpec((1, t, KV_WIDTH), lambda bi, i: (bi, 0, c))
    tok = lambda c: pl.BlockSpec((1, 1, N_HALF, KV_WIDTH), lambda bi, i: (c, bi, 0, 0))
    return pl.pallas_call(
        _nsa_prompt_kernel,
        grid=(b, t // tq),
        in_specs=[pl.BlockSpec((1, tq, NSA_WIDTH), lambda bi, i: (bi, i, 0)),
                  pl.BlockSpec((1, tq, 128), lambda bi, i: (bi, i, 0)),
                  tok(0), tok(1), col(2), col(3), col(0), col(1),
                  pl.BlockSpec(cover_t.shape, lambda bi, i: (0, 0)),
                  pl.BlockSpec(expand.shape, lambda bi, i: (0, 0, 0))],
        out_specs=pl.BlockSpec((1, tq, NSA_WIDTH), lambda bi, i: (bi, i, 0)),
        out_shape=jax.ShapeDtypeStruct((b, t, NSA_WIDTH), BF16),
        scratch_shapes=[pltpu.VMEM((t, KV_WIDTH), BF16) for _ in range(4)],
        compiler_params=_params(("parallel", "arbitrary")),
        name="nsa_prompt",
    )(nq3, g3, cmp_tok, cmp_tok, kv3, kv3, win3, win3, cover_t, expand)


SAMPLE_SEL_BLOCKS = -(-(PAST_LEN + DEC_SEQ) // SEL_BLK)
SEL_ROWS = 48


def _nsa_sample_kernel(pt_ref, *refs):
    pages = refs[:N_PAGES]
    (q_ref, gate_ref, kc_ref, vc_ref, new_ref, cw_ref, wnew_ref, cover_ref, perm_ref, o_ref, wout_ref) = refs[N_PAGES:]
    t = DEC_SEQ
    nrow = NSA_REP * NSA_KV * t
    wb = cw_ref.shape[-1]

    q = q_ref[0] * SCALE
    pieces = []
    for r in range(NSA_REP):
        qr = q[:, r * KV_WIDTH:(r + 1) * KV_WIDTH]
        for g in range(NSA_KV):
            pieces.append(jnp.where(_group_mask(KV_WIDTH, g), qr, 0.0))
    qm = jnp.concatenate(pieces, axis=0).astype(BF16)

    rowi = lax.broadcasted_iota(jnp.int32, (nrow, 1), 0)
    g_col = lax.shift_right_logical(rowi, 3) & (NSA_KV - 1)
    r_col = lax.shift_right_logical(rowi, 5)
    slope = jnp.exp2(-8.0 * (g_col * NSA_REP + r_col + 1).astype(F32) / NSA_HEADS)
    qpos = PAST_LEN + (rowi & (t - 1))
    lane = lax.broadcasted_iota(jnp.int32, (1, 128), 1)
    pad_rows = lambda a: jnp.concatenate([a, jnp.zeros((128 - t, a.shape[-1]), F32)], axis=0)

    def bias_and_mask(s, kpos_row, extra_ok=None, window=False):
        d = (qpos - kpos_row).astype(F32)
        ok = d >= 0.0
        if window:
            ok = ok & (d < float(WINDOW))
        if extra_ok is not None:
            ok = ok & extra_ok
        return s - slope * d, ok

    s, ok = bias_and_mask(_dot(qm, kc_ref[0, 0].astype(BF16)), lane * CMP_STRIDE + (CMP_LEN - 1))
    e, den = _softmax_rows(s, ok)
    pc = e / den
    o_cmp = _dot_nt(pc.astype(BF16), vc_ref[0, 0].astype(BF16))
    hi, lo = _split_bf16(pc)
    imp = _dot(hi, cover_ref[...]) + _dot(lo, cover_ref[...])
    per_rep = NSA_KV * t
    imp = imp[0:per_rep] + imp[per_rep:2 * per_rep] + imp[2 * per_rep:3 * per_rep] + imp[3 * per_rep:]
    imp_t = jnp.concatenate([imp, jnp.zeros((nrow - per_rep, 128), F32)], axis=0).T
    cur = lax.shift_right_logical(PAST_LEN + (lane & (t - 1)), 6)
    sel_t = _select_blocks(imp_t[0:SEL_ROWS], cur, SAMPLE_SEL_BLOCKS)
    sel = jnp.concatenate([sel_t, jnp.zeros((128 - SEL_ROWS, 128), F32)], axis=0).T[0:per_rep]
    sel = jnp.concatenate([sel] * NSA_REP, axis=0)

    scores, oks = [], []
    for p in range(N_PAGES):
        chosen = jnp.where(lane < SEL_BLK, sel[:, 2 * p:2 * p + 1], sel[:, 2 * p + 1:2 * p + 2])
        s, ok = bias_and_mask(_dot(qm, pages[p][0, 0].astype(BF16)), p * PAGE_SIZE + lane, chosen > 0.5)
        scores.append(s)
        oks.append(ok)
    knew = pad_rows(new_ref[0, :, 0:KV_WIDTH]).astype(BF16)
    vnew = pad_rows(new_ref[0, :, KV_WIDTH:2 * KV_WIDTH]).astype(BF16)
    last = SAMPLE_SEL_BLOCKS - 1
    s, ok = bias_and_mask(_dot_nt(qm, knew), PAST_LEN + lane, (sel[:, last:last + 1] > 0.5) & (lane < t))
    scores.append(s)
    oks.append(ok)
    e, den = _softmax_rows(jnp.concatenate(scores, axis=-1), jnp.concatenate(oks, axis=-1))
    prob = (e / den).astype(BF16)
    o_sel = _dot(prob[:, PAST_LEN:], vnew)
    for p in range(N_PAGES):
        o_sel = o_sel + _dot_nt(prob[:, p * PAGE_SIZE:(p + 1) * PAGE_SIZE], pages[p][0, 1].astype(BF16))

    lane_w = lax.broadcasted_iota(jnp.int32, (1, wb), 1)
    s_old, ok_old = bias_and_mask(_dot(qm, cw_ref[0, 0].astype(BF16)), PAST_LEN - wb + lane_w, window=True)
    kwn = pad_rows(wnew_ref[0, :, 0:KV_WIDTH]).astype(BF16)
    vwn = pad_rows(wnew_ref[0, :, KV_WIDTH:2 * KV_WIDTH]).astype(BF16)
    s_new, ok_new = bias_and_mask(_dot_nt(qm, kwn), PAST_LEN + lane, lane < t, window=True)
    e, den = _softmax_rows(jnp.concatenate([s_old, s_new], axis=-1), jnp.concatenate([ok_old, ok_new], axis=-1))
    prob = (e / den).astype(BF16)
    o_win = _dot_nt(prob[:, 0:wb], cw_ref[0, 1].astype(BF16)) + _dot(prob[:, wb:], vwn)

    ghi, glo = _split_bf16(pad_rows(gate_ref[0]))
    grow = _dot(perm_ref[...], ghi) + _dot(perm_ref[...], glo)
    base = g_col * 12 + r_col * 3

    def gate_col(branch):
        return jnp.sum(jnp.where(lane == base + branch, grow, 0.0), axis=-1, keepdims=True)

    mixed = gate_col(0) * o_cmp + gate_col(1) * o_sel + gate_col(2) * o_win
    for r in range(NSA_REP):
        acc = jnp.zeros((t, KV_WIDTH), F32)
        for g in range(NSA_KV):
            lo_row = (r * NSA_KV + g) * t
            acc = acc + jnp.where(_group_mask(KV_WIDTH, g), mixed[lo_row:lo_row + t], 0.0)
        o_ref[0, :, r * KV_WIDTH:(r + 1) * KV_WIDTH] = acc.astype(o_ref.dtype)

    for c in range(2):
        new_t = pad_rows(wnew_ref[0, :, c * KV_WIDTH:(c + 1) * KV_WIDTH]).T
        tail = jnp.concatenate([jnp.zeros((KV_WIDTH, wb - 128), F32), pltpu.roll(new_t, 128 - t, 1)], axis=-1)
        wout_ref[0, c] = jnp.where(lane_w >= wb - t, tail, pltpu.roll(cw_ref[0, c], wb - t, 1))


def _nsa_sample(nq, gates, cmp_tok, cache_t, page_flat, kv_new, win_t, win_new, cover, perm):
    b, t = DEC_BATCH, DEC_SEQ
    wb = win_t.shape[-1]
    per_b = lambda shape: pl.BlockSpec(shape, lambda bi, pt: (bi,) + (0,) * (len(shape) - 1))
    page_specs = [pl.BlockSpec((1, 2, KV_WIDTH, PAGE_SIZE),
                               functools.partial(lambda bi, pt, p: (pt[bi * N_PAGES + p], 1, 0, 0), p=p))
                  for p in range(N_PAGES)]
    tok = lambda c: pl.BlockSpec((1, 1, KV_WIDTH, N_HALF), lambda bi, pt: (c, bi, 0, 0))
    grid_spec = pltpu.PrefetchScalarGridSpec(
        num_scalar_prefetch=1,
        grid=(b,),
        in_specs=page_specs + [per_b((1, t, NSA_WIDTH)), per_b((1, t, 128)), tok(0), tok(1),
                               pl.BlockSpec((1, t, 2 * KV_WIDTH), lambda bi, pt: (bi, 0, 1)),
                               per_b((1, 2, KV_WIDTH, wb)), per_b((1, t, 2 * KV_WIDTH)),
                               pl.BlockSpec(cover.shape, lambda bi, pt: (0, 0)),
                               pl.BlockSpec(perm.shape, lambda bi, pt: (0, 0))],
        out_specs=[per_b((1, t, NSA_WIDTH)), per_b((1, 2, KV_WIDTH, wb))],
    )
    return pl.pallas_call(
        _nsa_sample_kernel,
        grid_spec=grid_spec,
        out_shape=[jax.ShapeDtypeStruct((b, t, NSA_WIDTH), BF16), jax.ShapeDtypeStruct(win_t.shape, F32)],
        compiler_params=_params(("arbitrary",)),
        name="nsa_sample",
    )(page_flat, *([cache_t] * N_PAGES), nq.reshape(b, t, NSA_WIDTH), gates.reshape(b, t, 128), cmp_tok, cmp_tok,
      kv_new.reshape(b, t, 4 * KV_WIDTH), win_t, win_new.reshape(b, t, 2 * KV_WIDTH), cover, perm)


def _out_kernel(x_ref, ret_ref, nsa_ref, wr_ref, wn_ref, o_ref):
    o_ref[...] = x_ref[...] + _dot(ret_ref[...], wr_ref[...]) + _dot(nsa_ref[...], wn_ref[...])


def _mixer_out(x2d, ret, nsa, w_ret, w_nsa, tm=1024, tn=512):
    n, d = x2d.shape
    return pl.pallas_call(
        _out_kernel,
        grid=(n // tm, d // tn),
        in_specs=[pl.BlockSpec((tm, tn), lambda i, j: (i, j)),
                  pl.BlockSpec((tm, RET_WIDTH), lambda i, j: (i, 0)),
                  pl.BlockSpec((tm, NSA_WIDTH), lambda i, j: (i, 0)),
                  pl.BlockSpec((RET_WIDTH, tn), lambda i, j: (0, j)),
                  pl.BlockSpec((NSA_WIDTH, tn), lambda i, j: (0, j))],
        out_specs=pl.BlockSpec((tm, tn), lambda i, j: (i, j)),
        out_shape=jax.ShapeDtypeStruct((n, d), F32),
        compiler_params=_params(("parallel", "arbitrary")),
        name="mixer_out",
    )(x2d, ret, nsa, w_ret, w_nsa)


FFN_TN = 512
FFN_COLS = D_FF // FFN_TN


def _ffn_up_kernel(*refs, hist, step, tiles_per_seq, has_prev):
    if has_prev:
        x_ref, wa_ref, wb_ref, cwa_ref, cwb_ref, cba_ref, cbb_ref, pa_ref, pb_ref = refs[:9]
        act_ref, sta_ref, stb_ref, exta, extb = refs[9:]
    else:
        x_ref, wa_ref, wb_ref, cwa_ref, cwb_ref, cba_ref, cbb_ref = refs[:7]
        act_ref, sta_ref, stb_ref, exta, extb = refs[7:]
        pa_ref = pb_ref = None
    tm = x_ref.shape[0]
    first = (pl.program_id(1) % tiles_per_seq) == 0
    x = x_ref[...]
    outs = []
    for w_ref, cw_ref, cb_ref, p_ref, st_ref, ext in ((wa_ref, cwa_ref, cba_ref, pa_ref, sta_ref, exta),
                                                     (wb_ref, cwb_ref, cbb_ref, pb_ref, stb_ref, extb)):
        u = _dot(x, w_ref[...])
        if has_prev:
            ext[hist - 2 * step:hist, :] = p_ref[...]
        else:
            @pl.when(first)
            def _():
                ext[hist - 2 * step:hist, :] = jnp.zeros((2 * step, u.shape[-1]), F32)
        ext[hist:hist + tm, :] = u
        cw = cw_ref[...]
        c = (cb_ref[...] + cw[2:3] * u + cw[1:2] * ext[hist - step:hist - step + tm, :]
             + cw[0:1] * ext[hist - 2 * step:hist - 2 * step + tm, :])
        outs.append(c)
        tail = ext[hist + tm - 2 * step:hist + tm, :]
        if has_prev:
            st_ref[...] = tail
        else:
            st_ref[0] = tail
            ext[hist - 2 * step:hist, :] = tail
    act_ref[...] = (jax.nn.silu(outs[0]) * outs[1]).astype(act_ref.dtype)


def _ffn_up(hn, w_up, conv_w, conv_b, prev, *, tm, step, seqs, tiles_per_seq):
    n, d = hn.shape
    tn = FFN_TN
    hist = 8 if step == 1 else 2 * step
    has_prev = prev is not None
    half = lambda off: (lambda j, i: (0, j + off))
    in_specs = [pl.BlockSpec((tm, d), lambda j, i: (i, 0)),
                pl.BlockSpec((d, tn), half(0)), pl.BlockSpec((d, tn), half(FFN_COLS)),
                pl.BlockSpec((CONV_W, tn), half(0)), pl.BlockSpec((CONV_W, tn), half(FFN_COLS)),
                pl.BlockSpec((1, tn), half(0)), pl.BlockSpec((1, tn), half(FFN_COLS))]
    args = [hn, w_up, w_up, conv_w, conv_w, conv_b, conv_b]
    if has_prev:
        in_specs += [pl.BlockSpec((2 * step, tn), half(0)), pl.BlockSpec((2 * step, tn), half(FFN_COLS))]
        args += [prev, prev]
        st_spec = pl.BlockSpec((2 * step, tn), lambda j, i: (0, j))
        st_shape = jax.ShapeDtypeStruct((2 * step, D_FF), F32)
    else:
        st_spec = pl.BlockSpec((1, 2, tn), lambda j, i: (i // tiles_per_seq, 0, j))
        st_shape = jax.ShapeDtypeStruct((seqs, 2, D_FF), F32)
    kernel = functools.partial(_ffn_up_kernel, hist=hist, step=step, tiles_per_seq=tiles_per_seq, has_prev=has_prev)
    return pl.pallas_call(
        kernel,
        grid=(FFN_COLS, n // tm),
        in_specs=in_specs,
        out_specs=[pl.BlockSpec((tm, tn), lambda j, i: (i, j)), st_spec, st_spec],
        out_shape=[jax.ShapeDtypeStruct((n, D_FF), BF16), st_shape, st_shape],
        scratch_shapes=[pltpu.VMEM((hist + tm, tn), F32), pltpu.VMEM((hist + tm, tn), F32)],
        compiler_params=_params(("arbitrary", "arbitrary")),
        name="ffn_up",
    )(*args)


def _ffn_down_kernel(h_ref, a_ref, w_ref, o_ref):
    o_ref[...] = h_ref[...] + _dot(a_ref[...], w_ref[...])


def _ffn_down(h, act, w_down, tm=512, tn=512):
    n, d = h.shape
    return pl.pallas_call(
        _ffn_down_kernel,
        grid=(n // tm, d // tn),
        in_specs=[pl.BlockSpec((tm, tn), lambda i, j: (i, j)),
                  pl.BlockSpec((tm, D_FF), lambda i, j: (i, 0)),
                  pl.BlockSpec((D_FF, tn), lambda i, j: (0, j))],
        out_specs=pl.BlockSpec((tm, tn), lambda i, j: (i, j)),
        out_shape=jax.ShapeDtypeStruct((n, d), F32),
        compiler_params=_params(("parallel", "arbitrary")),
        name="ffn_down",
    )(h, act, w_down)


def _block_cover_t(n_sel, rows):
    cs = np.arange(N_HALF - 1)[None, :] * CMP_STRIDE
    js = np.arange(n_sel)[:, None] * SEL_BLK
    cov = np.clip(np.minimum(cs + CMP_LEN, js + SEL_BLK) - np.maximum(cs, js), 0, None) / CMP_LEN
    out = np.zeros((rows, N_HALF), np.float32)
    out[:n_sel, :N_HALF - 1] = cov
    return jnp.asarray(out, dtype=BF16)


def _prepare_weights(g_attn, w_in, q_norm, k_norm_cmp, k_norm_slc, k_norm_win, cmp_pe, cmp_w1, cmp_b1, cmp_w2,
                     cmp_b2, w_out):
    d = D_MODEL
    o = 0
    w_rqk = w_in[:, o:o + 2 * RET_WIDTH]; o += 2 * RET_WIDTH
    w_rvg = w_in[:, o:o + 2 * RET_WIDTH]; o += 2 * RET_WIDTH
    w_nq = w_in[:, o:o + NSA_WIDTH]; o += NSA_WIDTH
    w_kv = w_in[:, o:o + 4 * KV_WIDTH]; o += 4 * KV_WIDTH
    w_win = w_in[:, o:o + 2 * KV_WIDTH]; o += 2 * KV_WIDTH
    w_ng = w_in[:, o:]
    w_nq = w_nq.reshape(d, NSA_KV, NSA_REP, HEAD_DIM).transpose(0, 2, 1, 3).reshape(d, NSA_WIDTH)
    w_ng = jnp.pad(w_ng, ((0, 0), (0, 128 - w_ng.shape[1])))
    tile4 = lambda v: jnp.tile(v, NSA_KV)
    zeros = jnp.zeros((KV_WIDTH,), F32)
    ones = jnp.ones((KV_WIDTH,), F32)
    wts = dict(
        g_attn=g_attn,
        w_rqk=w_rqk.astype(BF16), w_rvg=w_rvg.astype(BF16), w_nq=w_nq.astype(BF16),
        w_kv=w_kv.astype(BF16), w_win=w_win.astype(BF16), w_ng=w_ng.astype(BF16),
        rot_scale=jnp.concatenate([jnp.ones((RET_WIDTH,), F32),
                                   jnp.full((RET_WIDTH,), RET_DK ** -0.5, F32)]).reshape(1, -1),
        nq_gain=jnp.tile(q_norm, NSA_HEADS).reshape(1, -1), nq_mask=jnp.ones((1, NSA_WIDTH), F32),
        kv_gain=jnp.concatenate([zeros, zeros, tile4(k_norm_slc), zeros]).reshape(1, -1),
        kv_mask=jnp.concatenate([zeros, zeros, ones, zeros]).reshape(1, -1),
        win_gain=jnp.concatenate([tile4(k_norm_win), zeros]).reshape(1, -1),
        win_mask=jnp.concatenate([ones, zeros]).reshape(1, -1),
    )
    w1 = cmp_w1.reshape(2, 2, CMP_STRIDE, HEAD_DIM, CMP_HIDDEN)
    w1rep = jnp.broadcast_to(w1.transpose(0, 2, 3, 1, 4)[:, :, None],
                             (2, CMP_STRIDE, NSA_KV, HEAD_DIM, 2, CMP_HIDDEN))
    w1rep = w1rep.reshape(2, CMP_STRIDE * KV_WIDTH, 2 * CMP_HIDDEN).astype(BF16)
    w1flat = w1.reshape(2, 2, CMP_STRIDE * HEAD_DIM, CMP_HIDDEN).astype(BF16)
    pe8 = jnp.pad(cmp_pe.reshape(2, 2, 1, CMP_STRIDE * HEAD_DIM), ((0, 0), (0, 0), (0, 15), (0, 0)))
    bias = _cmp_bias(pe8, w1flat, cmp_b1.reshape(2, 1, CMP_HIDDEN))
    w1pair = jnp.broadcast_to(w1.transpose(0, 2, 3, 1, 4)[:, :, None], (2, CMP_STRIDE, 2, HEAD_DIM, 2, CMP_HIDDEN))
    col = lambda v: jnp.broadcast_to(v[:, :, None], (2, HEAD_DIM, N_HALF))
    cw = dict(
        w1pair=w1pair.reshape(2, CMP_STRIDE * 128, 2 * CMP_HIDDEN).astype(BF16),
        w2t=cmp_w2.transpose(0, 2, 1).astype(BF16), b2col=col(cmp_b2),
        gaincol=col(jnp.stack([k_norm_cmp, jnp.ones((HEAD_DIM,), F32)])),
        w1rep=w1rep, bias=bias,
        w2rep=jnp.tile(cmp_w2, (1, 1, NSA_KV)).astype(BF16),
        b2rep=jnp.tile(cmp_b2, (1, NSA_KV)).reshape(2, 1, KV_WIDTH),
        gain=jnp.stack([tile4(k_norm_cmp), ones]).reshape(2, 1, KV_WIDTH),
    )
    w_ret = w_out[:RET_WIDTH].astype(BF16)
    w_nsa = w_out[RET_WIDTH:].reshape(NSA_KV, NSA_REP, HEAD_DIM, d).transpose(1, 0, 2, 3).reshape(NSA_WIDTH, d)
    return wts, cw, w_ret, w_nsa.astype(BF16)


def kernel(x_prompt, x_sample, cache_kv, cache_win, state_ret, state_conv, page_table, g_attn, w_in, q_norm,
           k_norm_cmp, k_norm_slc, k_norm_win, cmp_pe, cmp_w1, cmp_b1, cmp_w2, cmp_b2, ret_gn, w_out, g_ffn, w_up,
           conv_w, conv_b, w_down):
    assert x_prompt.shape == (BATCH, SEQ, D_MODEL) and x_sample.shape == (DEC_BATCH, DEC_SEQ, D_MODEL)
    assert g_attn.shape[0] == 1, "single layer"
    wts, cw, w_ret, w_nsa = _prepare_weights(g_attn[0], w_in[0], q_norm[0], k_norm_cmp[0], k_norm_slc[0],
                                              k_norm_win[0], cmp_pe[0], cmp_w1[0], cmp_b1[0], cmp_w2[0],
                                              cmp_b2[0], w_out[0])
    w_up_b = w_up[0].astype(BF16)
    w_down_b = w_down[0].astype(BF16)
    cb = conv_b[0].reshape(1, -1)
    n_p, n_s = BATCH * SEQ, DEC_BATCH * DEC_SEQ
    xp = x_prompt.reshape(n_p, D_MODEL)
    xs = x_sample.reshape(n_s, D_MODEL)

    pos_p = jnp.tile(jnp.arange(SEQ, dtype=jnp.int32), BATCH)
    qk, vg, nq, kv_p, win_p, gates = _project_all(xp, pos_p, wts)
    ret_mix, ret_state_p = _retention_prompt(qk, vg, ret_gn[0])
    kv3 = kv_p.reshape(BATCH, SEQ, 4 * KV_WIDTH)
    cmp_tok = _compress_prompt(kv3, cw)
    cover_p = _block_cover_t(SEQ // SEL_BLK, SEQ // SEL_BLK)
    key = np.arange(SEQ).reshape(SEQ // SEL_CHUNK, 1, SEL_CHUNK)
    expand = jnp.asarray((key // SEL_BLK) == np.arange(128)[None, :, None], dtype=BF16)
    nsa = _nsa_prompt(nq, gates, cmp_tok, kv_p, win_p, cover_p, expand)
    h_p = _mixer_out(xp, ret_mix.reshape(n_p, RET_WIDTH), nsa.reshape(n_p, NSA_WIDTH), w_ret, w_nsa)
    hn_p = _rmsnorm(h_p, g_ffn[0])
    act_p, st_a, st_b = _ffn_up(hn_p, w_up_b, conv_w[0], cb, None, tm=1024, step=1, seqs=BATCH,
                                tiles_per_seq=SEQ // 1024)
    y_p = _ffn_down(h_p, act_p, w_down_b)
    conv_p = jnp.concatenate([st_a, st_b], axis=-1)

    pos_s = jnp.tile(PAST_LEN + jnp.arange(DEC_SEQ, dtype=jnp.int32), DEC_BATCH)
    qk, vg, nq, kv_s, win_s, gates = _project_all(xs, pos_s, wts)
    ret_mix, ret_state_s = _retention_sample(qk, vg, state_ret[0], ret_gn[0])
    cache_t = cache_kv[0].transpose(0, 2, 3, 4, 1).reshape(cache_kv.shape[1], 4, KV_WIDTH, PAGE_SIZE)
    page_flat = page_table.reshape(-1).astype(jnp.int32)
    cmp_tok = _compress_sample(cache_t, page_flat, cw)
    wb = cache_win.shape[2]
    win_t = cache_win[0].transpose(0, 2, 3, 4, 1).reshape(DEC_BATCH, 2, KV_WIDTH, wb)
    cover_s = _block_cover_t(SAMPLE_SEL_BLOCKS, 128).T
    lanes = np.arange(128)
    perm = jnp.asarray((lanes[:, None] % DEC_SEQ) == lanes[None, :], dtype=BF16)
    nsa, win_out_t = _nsa_sample(nq, gates, cmp_tok, cache_t, page_flat, kv_s, win_t, win_s, cover_s, perm)
    h_s = _mixer_out(xs, ret_mix.reshape(n_s, RET_WIDTH), nsa.reshape(n_s, NSA_WIDTH), w_ret, w_nsa)
    to_tm = lambda a: a.reshape(DEC_BATCH, -1, a.shape[-1]).transpose(1, 0, 2).reshape(-1, a.shape[-1])
    from_tm = lambda a, t: a.reshape(t, DEC_BATCH, a.shape[-1]).transpose(1, 0, 2)
    h_tm = to_tm(h_s)
    hn_s = _rmsnorm(h_tm, g_ffn[0])
    act_s, st_a, st_b = _ffn_up(hn_s, w_up_b, conv_w[0], cb, to_tm(state_conv[0]), tm=n_s, step=DEC_BATCH,
                                seqs=DEC_BATCH, tiles_per_seq=1)
    y_s = from_tm(_ffn_down(h_tm, act_s, w_down_b), DEC_SEQ)
    conv_s = from_tm(jnp.concatenate([st_a, st_b], axis=-1), CONV_W - 1)

    win_sample = win_out_t.reshape(DEC_BATCH, 2, NSA_KV, HEAD_DIM, wb).transpose(0, 4, 1, 2, 3)
    return (
        y_p.reshape(BATCH, SEQ, D_MODEL),
        y_s,
        kv_p.reshape(1, BATCH, SEQ, 4, NSA_KV, HEAD_DIM),
        kv_s.reshape(1, DEC_BATCH, DEC_SEQ, 4, NSA_KV, HEAD_DIM),
        win_p.reshape(BATCH, SEQ, 2 * KV_WIDTH)[:, SEQ - WINDOW:].reshape(1, BATCH, WINDOW, 2, NSA_KV, HEAD_DIM),
        win_sample[None],
        ret_state_p[None],
        ret_state_s[None],
        conv_p[None],
        conv_s[None],
    )
```

```python
import functools

import numpy as np
import jax
import jax.numpy as jnp
from jax import lax
from jax.experimental import pallas as pl
from jax.experimental.pallas import tpu as pltpu

D_MODEL = 2048
BATCH = 4
SEQ = 2048
DEC_BATCH = 128
DEC_SEQ = 8
PAST_LEN = 2048
PAGE_SIZE = 128
N_PAGES = PAST_LEN // PAGE_SIZE
RET_HEADS = 4
RET_DK = 256
RET_DV = 256
RET_WIDTH = RET_HEADS * RET_DV
RET_CHUNK = 128
NSA_HEADS = 16
NSA_KV = 4
NSA_REP = NSA_HEADS // NSA_KV
HEAD_DIM = 64
NSA_WIDTH = NSA_HEADS * HEAD_DIM
KV_WIDTH = NSA_KV * HEAD_DIM
CMP_LEN = 32
CMP_STRIDE = 16
CMP_HIDDEN = 256
SEL_BLK = 64
SEL_TOPK = 16
WINDOW = 512
SCALE = HEAD_DIM ** -0.5
D_FF = 5632
CONV_W = 3
EPS = 1e-6

N_HALF = PAST_LEN // CMP_STRIDE
NEG = -1e30
FORCED = 1e30
VMEM_LIMIT = 56 * 1024 * 1024

BF16 = jnp.bfloat16
F32 = jnp.float32


def _params(semantics, vmem=VMEM_LIMIT):
    return pltpu.CompilerParams(dimension_semantics=semantics, vmem_limit_bytes=vmem)


def _dot(a, b):
    return jnp.dot(a, b, preferred_element_type=F32)


def _dot_nt(a, b):
    return lax.dot_general(a, b, (((1,), (1,)), ((), ())), preferred_element_type=F32)


def _split_bf16(x):
    hi = x.astype(BF16)
    lo = (x - hi.astype(F32)).astype(BF16)
    return hi, lo


def _group_mask(width, g):
    lane = lax.broadcasted_iota(jnp.int32, (1, width), 1)
    return (lane >= g * HEAD_DIM) & (lane < (g + 1) * HEAD_DIM)


def _rms_groups64(z, gain):
    width = z.shape[-1]
    outs = []
    for k in range(width // 128):
        zk = z[:, k * 128:(k + 1) * 128]
        zz = zk * zk
        lane = lax.broadcasted_iota(jnp.int32, (1, 128), 1)
        lo_half = lane < HEAD_DIM
        s_lo = jnp.sum(jnp.where(lo_half, zz, 0.0), axis=-1, keepdims=True)
        s_hi = jnp.sum(jnp.where(lo_half, 0.0, zz), axis=-1, keepdims=True)
        ms = jnp.where(lo_half, s_lo, s_hi) * (1.0 / HEAD_DIM)
        outs.append(zk * lax.rsqrt(ms + EPS))
    y = outs[0] if len(outs) == 1 else jnp.concatenate(outs, axis=-1)
    return y * gain


def _rmsnorm_kernel(x_ref, g_ref, o_ref):
    x = x_ref[...]
    ms = jnp.mean(x * x, axis=-1, keepdims=True)
    o_ref[...] = (x * lax.rsqrt(ms + EPS) * g_ref[...]).astype(o_ref.dtype)


def _rmsnorm(x, g, tm=512):
    n, d = x.shape
    return pl.pallas_call(
        _rmsnorm_kernel,
        grid=(n // tm,),
        in_specs=[pl.BlockSpec((tm, d), lambda i: (i, 0)), pl.BlockSpec((1, d), lambda i: (0, 0))],
        out_specs=pl.BlockSpec((tm, d), lambda i: (i, 0)),
        out_shape=jax.ShapeDtypeStruct((n, d), BF16),
        compiler_params=_params(("parallel",)),
        name="rmsnorm",
    )(x, g.reshape(1, d))


def _proj_plain_kernel(x_ref, w_ref, o_ref):
    o_ref[...] = _dot(x_ref[...], w_ref[...]).astype(o_ref.dtype)


def _proj_sigmoid_kernel(x_ref, w_ref, o_ref):
    o_ref[...] = jax.nn.sigmoid(_dot(x_ref[...], w_ref[...]))


def _proj_norm_kernel(x_ref, w_ref, gain_ref, nmask_ref, o_ref):
    z = _dot(x_ref[...], w_ref[...])
    o_ref[...] = jnp.where(nmask_ref[...] > 0.5, _rms_groups64(z, gain_ref[...]), z)


def _proj_rot_kernel(x_ref, w_ref, cos_ref, sin_ref, scale_ref, o_ref):
    z = _dot(x_ref[...], w_ref[...])
    c = cos_ref[...]
    s = sin_ref[...]
    sc = scale_ref[...]
    for hh in range(z.shape[-1] // RET_DK):
        x1 = z[:, hh * RET_DK:hh * RET_DK + 128]
        x2 = z[:, hh * RET_DK + 128:(hh + 1) * RET_DK]
        o_ref[:, hh * RET_DK:hh * RET_DK + 128] = (x1 * c - x2 * s) * sc[:, hh * RET_DK:hh * RET_DK + 128]
        o_ref[:, hh * RET_DK + 128:(hh + 1) * RET_DK] = (x2 * c + x1 * s) * sc[:, hh * RET_DK + 128:(hh + 1) * RET_DK]


def _proj(xn, w, kernel, extras=(), extra_specs=(), out_dtype=F32, tm=1024, tn=512, name="proj"):
    n, d = xn.shape
    c = w.shape[1]
    tn = min(tn, c)
    return pl.pallas_call(
        kernel,
        grid=(n // tm, c // tn),
        in_specs=[pl.BlockSpec((tm, d), lambda i, j: (i, 0)), pl.BlockSpec((d, tn), lambda i, j: (0, j))]
        + list(extra_specs),
        out_specs=pl.BlockSpec((tm, tn), lambda i, j: (i, j)),
        out_shape=jax.ShapeDtypeStruct((n, c), out_dtype),
        compiler_params=_params(("parallel", "arbitrary")),
        name=name,
    )(xn, w, *extras)


def _project_all(x2d, pos, wts, tm=1024):
    n = x2d.shape[0]
    xn = _rmsnorm(x2d, wts["g_attn"])
    half = RET_DK // 2
    inv = 1.0 / (10000.0 ** jnp.linspace(0.0, 1.0, half, dtype=F32))
    ang = pos.astype(F32)[:, None] * inv[None, :]
    cos = jnp.cos(ang)
    sin = jnp.sin(ang)
    tn = 512
    row_spec = pl.BlockSpec((tm, 128), lambda i, j: (i, 0))
    col_spec = pl.BlockSpec((1, tn), lambda i, j: (0, j))
    qk = _proj(xn, wts["w_rqk"], _proj_rot_kernel, (cos, sin, wts["rot_scale"]),
               (row_spec, row_spec, col_spec), tm=tm, name="proj_rot")
    vg = _proj(xn, wts["w_rvg"], _proj_plain_kernel, tm=tm, name="proj_vg")
    nq = _proj(xn, wts["w_nq"], _proj_norm_kernel, (wts["nq_gain"], wts["nq_mask"]),
               (col_spec, col_spec), tm=tm, name="proj_nq")
    kv = _proj(xn, wts["w_kv"], _proj_norm_kernel, (wts["kv_gain"], wts["kv_mask"]),
               (col_spec, col_spec), tm=tm, name="proj_kv")
    win = _proj(xn, wts["w_win"], _proj_norm_kernel, (wts["win_gain"], wts["win_mask"]),
                (col_spec, col_spec), tm=tm, name="proj_win")
    gates = _proj(xn, wts["w_ng"], _proj_sigmoid_kernel, tm=tm, name="proj_gate")
    return qk, vg, nq, kv, win, gates


def _retention_step(q, k, v, state, dmask, xi, zeta, gch):
    qb = q.astype(BF16)
    kb = k.astype(BF16)
    vb = v.astype(BF16)
    s = _dot_nt(qb, kb) * dmask
    o = _dot(s.astype(BF16), vb) + _dot(qb, state.astype(BF16)) * xi
    kz = (k * zeta).astype(BF16)
    new_state = state * gch + _dot(kz.T, vb)
    return o, new_state


def _ret_mix(o, gn, gate):
    ms = jnp.mean(o * o, axis=-1, keepdims=True)
    return o * lax.rsqrt(ms + EPS) * gn * jax.nn.silu(gate)


def _ret_prompt_kernel(qk_ref, vg_ref, dmask_ref, xi_ref, zeta_ref, gch_ref, gn_ref, mix_ref, state_ref):
    @pl.when(pl.program_id(1) == 0)
    def _():
        state_ref[...] = jnp.zeros_like(state_ref)

    for h in range(RET_HEADS):
        lo, hi = h * RET_DK, (h + 1) * RET_DK
        o, new_state = _retention_step(qk_ref[0, :, lo:hi], qk_ref[0, :, RET_WIDTH + lo:RET_WIDTH + hi],
                                       vg_ref[0, :, lo:hi], state_ref[0, h], dmask_ref[h], xi_ref[h],
                                       zeta_ref[h], gch_ref[h])
        state_ref[0, h] = new_state
        mix_ref[0, :, lo:hi] = _ret_mix(o, gn_ref[h], vg_ref[0, :, RET_WIDTH + lo:RET_WIDTH + hi]).astype(mix_ref.dtype)


def _ret_tables(chunk):
    h = jnp.arange(RET_HEADS, dtype=F32)
    lg = jnp.log(1.0 - jnp.exp2(-5.0 - h))
    i = jnp.arange(chunk, dtype=F32)
    diff = i[:, None] - i[None, :]
    dmask = jnp.where(diff >= 0, jnp.exp(lg[:, None, None] * jnp.maximum(diff, 0.0)), 0.0)
    xi = jnp.exp(lg[:, None] * (i[None, :] + 1.0))
    zeta = jnp.exp(lg[:, None] * (chunk - 1.0 - i[None, :]))
    gch = jnp.exp(lg * chunk)
    bc = lambda a: jnp.broadcast_to(a[:, :, None], (RET_HEADS, chunk, RET_DV))
    return dmask, bc(xi), bc(zeta), jnp.broadcast_to(gch[:, None, None], (RET_HEADS, 1, RET_DV))


def _retention_prompt(qk, vg, gn):
    b, t = BATCH, SEQ
    c = RET_CHUNK
    qk3 = qk.reshape(b, t, 2 * RET_WIDTH)
    vg3 = vg.reshape(b, t, 2 * RET_WIDTH)
    dmask, xi, zeta, gch = _ret_tables(c)
    rows = pl.BlockSpec((1, c, 2 * RET_WIDTH), lambda bi, ci: (bi, ci, 0))
    full = lambda shape: pl.BlockSpec(shape, lambda bi, ci: (0,) * len(shape))
    return pl.pallas_call(
        _ret_prompt_kernel,
        grid=(b, t // c),
        in_specs=[rows, rows, full((RET_HEADS, c, c)), full((RET_HEADS, c, RET_DV)), full((RET_HEADS, c, RET_DV)),
                  full((RET_HEADS, 1, RET_DV)), full((RET_HEADS, 1, RET_DV))],
        out_specs=[pl.BlockSpec((1, c, RET_WIDTH), lambda bi, ci: (bi, ci, 0)),
                   pl.BlockSpec((1, RET_HEADS, RET_DK, RET_DV), lambda bi, ci: (bi, 0, 0, 0))],
        out_shape=[jax.ShapeDtypeStruct((b, t, RET_WIDTH), BF16),
                   jax.ShapeDtypeStruct((b, RET_HEADS, RET_DK, RET_DV), F32)],
        compiler_params=_params(("parallel", "arbitrary")),
        name="retention_prompt",
    )(qk3, vg3, dmask, xi, zeta, gch, gn.reshape(RET_HEADS, 1, RET_DV))


def _ret_sample_kernel(qk_ref, vg_ref, s0_ref, dmask_ref, xi_ref, zeta_ref, gch_ref, gn_ref, mix_ref, state_ref):
    t = qk_ref.shape[1]
    pad = lambda a: jnp.concatenate([a, jnp.zeros((RET_PAD - t, a.shape[-1]), F32)], axis=0)
    for h in range(RET_HEADS):
        lo, hi = h * RET_DK, (h + 1) * RET_DK
        o, new_state = _retention_step(pad(qk_ref[0, :, lo:hi]), pad(qk_ref[0, :, RET_WIDTH + lo:RET_WIDTH + hi]),
                                       pad(vg_ref[0, :, lo:hi]), s0_ref[0, h], dmask_ref[h], xi_ref[h],
                                       zeta_ref[h], gch_ref[h])
        state_ref[0, h] = new_state
        mix_ref[0, :, lo:hi] = _ret_mix(o[0:t], gn_ref[h], vg_ref[0, :, RET_WIDTH + lo:RET_WIDTH + hi]).astype(mix_ref.dtype)


RET_PAD = 128


def _retention_sample(qk, vg, state0, gn):
    b, t = DEC_BATCH, DEC_SEQ
    dmask, xi, zeta, gch = _ret_tables(t)
    dmask = jnp.pad(dmask, ((0, 0), (0, RET_PAD - t), (0, RET_PAD - t)))
    xi = jnp.pad(xi, ((0, 0), (0, RET_PAD - t), (0, 0)))
    zeta = jnp.pad(zeta, ((0, 0), (0, RET_PAD - t), (0, 0)))
    full = lambda shape: pl.BlockSpec(shape, lambda bi: (0,) * len(shape))
    return pl.pallas_call(
        _ret_sample_kernel,
        grid=(b,),
        in_specs=[pl.BlockSpec((1, t, 2 * RET_WIDTH), lambda bi: (bi, 0, 0)),
                  pl.BlockSpec((1, t, 2 * RET_WIDTH), lambda bi: (bi, 0, 0)),
                  pl.BlockSpec((1, RET_HEADS, RET_DK, RET_DV), lambda bi: (bi, 0, 0, 0)),
                  full((RET_HEADS, RET_PAD, RET_PAD)), full((RET_HEADS, RET_PAD, RET_DV)), full((RET_HEADS, RET_PAD, RET_DV)),
                  full((RET_HEADS, 1, RET_DV)), full((RET_HEADS, 1, RET_DV))],
        out_specs=[pl.BlockSpec((1, t, RET_WIDTH), lambda bi: (bi, 0, 0)),
                   pl.BlockSpec((1, RET_HEADS, RET_DK, RET_DV), lambda bi: (bi, 0, 0, 0))],
        out_shape=[jax.ShapeDtypeStruct((b, t, RET_WIDTH), BF16),
                   jax.ShapeDtypeStruct((b, RET_HEADS, RET_DK, RET_DV), F32)],
        compiler_params=_params(("parallel",)),
        name="retention_sample",
    )(qk.reshape(b, t, -1), vg.reshape(b, t, -1), state0, dmask, xi, zeta, gch, gn.reshape(RET_HEADS, 1, RET_DV))


def _cmp_bias_kernel(pe_ref, w_ref, b1_ref, o_ref):
    acc = b1_ref[0]
    for j in range(2):
        acc = acc + _dot(pe_ref[0, j].astype(BF16), w_ref[0, j])[0:1]
    o_ref[0] = acc


def _cmp_bias(pe8, w1flat, b1):
    return pl.pallas_call(
        _cmp_bias_kernel,
        grid=(2,),
        in_specs=[pl.BlockSpec((1, 2, 16, CMP_STRIDE * HEAD_DIM), lambda c: (c, 0, 0, 0)),
                  pl.BlockSpec((1, 2, CMP_STRIDE * HEAD_DIM, CMP_HIDDEN), lambda c: (c, 0, 0, 0)),
                  pl.BlockSpec((1, 1, CMP_HIDDEN), lambda c: (c, 0, 0))],
        out_specs=pl.BlockSpec((1, 1, CMP_HIDDEN), lambda c: (c, 0, 0)),
        out_shape=jax.ShapeDtypeStruct((2, 1, CMP_HIDDEN), F32),
        compiler_params=_params(("parallel",)),
        name="cmp_bias",
    )(pe8, w1flat, b1)


def _compress_body(load_rows, w1_ref, bias_ref, w2_ref, b2_ref, gain_ref, o_ref, xm_ref):
    for l in range(CMP_STRIDE):
        xl = load_rows(l)
        for g in range(NSA_KV):
            xm_ref[g * N_HALF:(g + 1) * N_HALF, l * KV_WIDTH:(l + 1) * KV_WIDTH] = jnp.where(
                _group_mask(KV_WIDTH, g), xl, 0.0).astype(BF16)
    acc = _dot(xm_ref[...], w1_ref[0])
    h0 = acc[:, :CMP_HIDDEN]
    h1 = pltpu.roll(acc[:, CMP_HIDDEN:], 4 * N_HALF - 1, 0)
    hid = jax.nn.silu(h0 + h1 + bias_ref[0])
    out = _dot(hid.astype(BF16), w2_ref[0]) + b2_ref[0]
    res = jnp.zeros((N_HALF, KV_WIDTH), F32)
    for g in range(NSA_KV):
        res = res + jnp.where(_group_mask(KV_WIDTH, g), out[g * N_HALF:(g + 1) * N_HALF], 0.0)
    normed = _rms_groups64(res, gain_ref[0])
    o_ref[0, 0] = jnp.where(pl.program_id(0) == 0, normed, res)


def _compress_prompt_kernel(lo_ref, hi_ref, w1_ref, bias_ref, w2_ref, b2_ref, gain_ref, o_ref, xm_ref):
    load = lambda l: jnp.concatenate(
        [r[0, pl.ds(l, N_HALF, stride=CMP_STRIDE), :] for r in (lo_ref, hi_ref)], axis=-1)
    _compress_body(load, w1_ref, bias_ref, w2_ref, b2_ref, gain_ref, o_ref, xm_ref)


def _compress_sample_kernel(pt_ref, *refs):
    pages = refs[:N_PAGES]
    w1_ref, bias_ref, w2t_ref, b2_ref, gain_ref, o_ref, t_ref, xm_ref = refs[N_PAGES:]
    for p in range(N_PAGES):
        xt = pages[p][0, 0]
        for pair in range(2):
            t_ref[pair, p * PAGE_SIZE:(p + 1) * PAGE_SIZE, :] = xt[pair * 128:(pair + 1) * 128, :].T
    lane = lax.broadcasted_iota(jnp.int32, (1, 128), 1)
    for l in range(CMP_STRIDE):
        for pair in range(2):
            rows = t_ref[pair, pl.ds(l, N_HALF, stride=CMP_STRIDE), :]
            for member in range(2):
                g = 2 * pair + member
                keep = (lane >= member * HEAD_DIM) & (lane < (member + 1) * HEAD_DIM)
                xm_ref[g * N_HALF:(g + 1) * N_HALF, l * 128:(l + 1) * 128] = jnp.where(keep, rows, 0.0).astype(BF16)
    acc = _dot(xm_ref[...], w1_ref[0])
    h0 = acc[:, :CMP_HIDDEN]
    h1 = pltpu.roll(acc[:, CMP_HIDDEN:], 4 * N_HALF - 1, 0)
    hid = jax.nn.silu(h0 + h1 + bias_ref[0]).astype(BF16)
    outs = []
    for g in range(NSA_KV):
        og = _dot_nt(w2t_ref[0], hid[g * N_HALF:(g + 1) * N_HALF]) + b2_ref[0]
        ms = jnp.mean(og * og, axis=0, keepdims=True)
        normed = og * lax.rsqrt(ms + EPS) * gain_ref[0]
        outs.append(jnp.where(pl.program_id(0) == 0, normed, og))
    o_ref[0, 0] = jnp.concatenate(outs, axis=0)


def _cmp_weight_specs(nargs):
    cmap = (lambda c, b: (c, 0, 0)) if nargs == 2 else (lambda c, b, pt: (c, 0, 0))
    return [pl.BlockSpec((1, CMP_STRIDE * KV_WIDTH, 2 * CMP_HIDDEN), cmap),
            pl.BlockSpec((1, 1, CMP_HIDDEN), cmap),
            pl.BlockSpec((1, CMP_HIDDEN, KV_WIDTH), cmap),
            pl.BlockSpec((1, 1, KV_WIDTH), cmap),
            pl.BlockSpec((1, 1, KV_WIDTH), cmap)]


def _compress_prompt(kv3, cw):
    b = kv3.shape[0]
    return pl.pallas_call(
        _compress_prompt_kernel,
        grid=(2, b),
        in_specs=[pl.BlockSpec((1, SEQ, 128), lambda c, bi: (bi, 0, 2 * c)),
                  pl.BlockSpec((1, SEQ, 128), lambda c, bi: (bi, 0, 2 * c + 1))] + _cmp_weight_specs(2),
        out_specs=pl.BlockSpec((1, 1, N_HALF, KV_WIDTH), lambda c, bi: (c, bi, 0, 0)),
        out_shape=jax.ShapeDtypeStruct((2, b, N_HALF, KV_WIDTH), F32),
        scratch_shapes=[pltpu.VMEM((NSA_KV * N_HALF, CMP_STRIDE * KV_WIDTH), BF16)],
        compiler_params=_params(("arbitrary", "arbitrary")),
        name="compress_prompt",
    )(kv3, kv3, cw["w1rep"], cw["bias"], cw["w2rep"], cw["b2rep"], cw["gain"])


def _compress_sample(cache_t, page_flat, cw):
    b = DEC_BATCH
    page_specs = [pl.BlockSpec((1, 1, KV_WIDTH, PAGE_SIZE),
                               functools.partial(lambda c, bi, pt, p: (pt[bi * N_PAGES + p], c, 0, 0), p=p))
                  for p in range(N_PAGES)]
    cmap = lambda c, bi, pt: (c, 0, 0)
    grid_spec = pltpu.PrefetchScalarGridSpec(
        num_scalar_prefetch=1,
        grid=(2, b),
        in_specs=page_specs + [pl.BlockSpec((1, CMP_STRIDE * 128, 2 * CMP_HIDDEN), cmap),
                               pl.BlockSpec((1, 1, CMP_HIDDEN), cmap),
                               pl.BlockSpec((1, HEAD_DIM, CMP_HIDDEN), cmap),
                               pl.BlockSpec((1, HEAD_DIM, N_HALF), cmap),
                               pl.BlockSpec((1, HEAD_DIM, N_HALF), cmap)],
        out_specs=pl.BlockSpec((1, 1, KV_WIDTH, N_HALF), lambda c, bi, pt: (c, bi, 0, 0)),
        scratch_shapes=[pltpu.VMEM((2, PAST_LEN, 128), F32),
                        pltpu.VMEM((NSA_KV * N_HALF, CMP_STRIDE * 128), BF16)],
    )
    return pl.pallas_call(
        _compress_sample_kernel,
        grid_spec=grid_spec,
        out_shape=jax.ShapeDtypeStruct((2, b, KV_WIDTH, N_HALF), F32),
        compiler_params=_params(("arbitrary", "arbitrary")),
        name="compress_sample",
    )(page_flat, *([cache_t] * N_PAGES), cw["w1pair"], cw["bias"], cw["w2t"], cw["b2col"], cw["gaincol"])


def _select_blocks(imp, cur, n_blocks):
    rows = imp.shape[0]
    j = lax.broadcasted_iota(jnp.int32, (rows, 1), 0)
    valid = j <= cur
    forced = (j == 0) | (j == cur) | (j == cur - 1)
    score = jnp.where(valid, jnp.where(forced, FORCED, imp), NEG)
    rank = jnp.zeros(score.shape, F32)
    for i in range(n_blocks):
        row = score[i:i + 1, :]
        ahead = (row > score) | ((row == score) & (j > i))
        rank = rank + jnp.where(ahead, 1.0, 0.0)
    return jnp.where(valid & (rank < float(SEL_TOPK)) & (j < n_blocks), 1.0, 0.0)


def _softmax_rows(s, valid):
    s = jnp.where(valid, s, NEG)
    m = jnp.max(s, axis=-1, keepdims=True)
    e = jnp.where(valid, jnp.exp(s - m), 0.0)
    return e, jnp.maximum(jnp.sum(e, axis=-1, keepdims=True), 1e-30)


def _softmax_cols(s, valid):
    s = jnp.where(valid, s, NEG)
    m = jnp.max(s, axis=0, keepdims=True)
    e = jnp.where(valid, jnp.exp(s - m), 0.0)
    return e, jnp.maximum(jnp.sum(e, axis=0, keepdims=True), 1e-30)


def _slope(g, r):
    return float(2.0 ** (-8.0 * (g * NSA_REP + r + 1) / NSA_HEADS))


NSA_TQ = 128
SEL_CHUNK_LOG2 = 9
SEL_CHUNK = 1 << SEL_CHUNK_LOG2
WIN_KEYS = WINDOW + NSA_TQ


def _nsa_prompt_kernel(q_ref, gate_ref, kc_ref, vc_ref, ks_ref, vs_ref, kw_ref, vw_ref, cover_ref, expand_ref,
                       o_ref, ksb, vsb, kwb, vwb):
    i = pl.program_id(1)
    tq = NSA_TQ
    rows = NSA_REP * tq

    @pl.when(i == 0)
    def _():
        ksb[...] = ks_ref[0].astype(BF16)
        vsb[...] = vs_ref[0].astype(BF16)
        kwb[...] = kw_ref[0].astype(BF16)
        vwb[...] = vw_ref[0].astype(BF16)

    q = q_ref[0] * SCALE
    gates = gate_ref[0]
    kcb = kc_ref[0, 0].astype(BF16)
    vcb = vc_ref[0, 0].astype(BF16)
    q0 = i * tq
    qpos_col = q0 + lax.broadcasted_iota(jnp.int32, (tq, 1), 0)
    qpos4 = jnp.concatenate([qpos_col] * NSA_REP, axis=0)
    qpos_row = q0 + lax.broadcasted_iota(jnp.int32, (1, tq), 1)
    cur_row = lax.shift_right_logical(qpos_row, 6)
    win_start = pl.multiple_of(jnp.maximum(q0 - WINDOW, 0), NSA_TQ)
    n_chunks = lax.shift_right_logical(q0 + tq + SEL_CHUNK - 1, SEL_CHUNK_LOG2)

    groups = range(NSA_KV)
    qgs = [jnp.concatenate(
        [jnp.where(_group_mask(KV_WIDTH, g), q[:, r * KV_WIDTH:(r + 1) * KV_WIDTH], 0.0) for r in range(NSA_REP)],
        axis=0).astype(BF16) for g in groups]

    def gate_col(g, branch):
        cols = [gates[:, g * 12 + r * 3 + branch:g * 12 + r * 3 + branch + 1] for r in range(NSA_REP)]
        return jnp.concatenate(cols, axis=0)

    def biased(scores, d, mask_bias, g):
        slabs = [mask_bias - _slope(g, r) * d for r in range(NSA_REP)]
        return scores + jnp.concatenate(slabs, axis=0)

    n_idx = lax.broadcasted_iota(jnp.int32, (1, N_HALF), 1)
    dist = (qpos4 - (n_idx * CMP_STRIDE + (CMP_LEN - 1))).astype(F32)
    valid = dist >= 0.0
    o_cmp, sel_qs = [], []
    for g in groups:
        slope = jnp.concatenate([jnp.full((tq, 1), _slope(g, r), F32) for r in range(NSA_REP)], axis=0)
        e, den = _softmax_rows(_dot_nt(qgs[g], kcb) - slope * dist, valid)
        pc = e / den
        o_cmp.append(_dot(pc.astype(BF16), vcb))
        pcsum = pc[0:tq]
        for r in range(1, NSA_REP):
            pcsum = pcsum + pc[r * tq:(r + 1) * tq]
        hi, lo = _split_bf16(pcsum)
        imp = _dot_nt(cover_ref[...], hi) + _dot_nt(cover_ref[...], lo)
        sel = _select_blocks(imp, cur_row, SEQ // SEL_BLK)
        sel = jnp.concatenate([sel, jnp.zeros((128 - sel.shape[0], tq), F32)], axis=0)
        sel_qs.append(sel.T.astype(BF16))

    def sel_step(kk, carry):
        ks = pl.multiple_of(kk * SEL_CHUNK, SEL_CHUNK)
        kpos = ks + lax.broadcasted_iota(jnp.int32, (1, SEL_CHUNK), 1)
        di = qpos_col - kpos
        d = di.astype(F32)
        kblk = ksb[pl.ds(ks, SEL_CHUNK), :]
        vblk = vsb[pl.ds(ks, SEL_CHUNK), :]
        new = []
        for g in groups:
            m, l, acc = carry[g]
            chosen = _dot(sel_qs[g], expand_ref[kk])
            mask_bias = jnp.where((chosen > 0.5) & (di >= 0), 0.0, NEG)
            s = biased(_dot_nt(qgs[g], kblk), d, mask_bias, g)
            m_new = jnp.maximum(m, jnp.max(s, axis=-1, keepdims=True))
            p = jnp.exp(s - m_new)
            alpha = jnp.exp(m - m_new)
            l = alpha * l + jnp.sum(p, axis=-1, keepdims=True)
            acc = alpha * acc + _dot(p.astype(BF16), vblk)
            new.append((m_new, l, acc))
        return tuple(new)

    init = tuple((jnp.full((rows, 1), NEG, F32), jnp.zeros((rows, 1), F32), jnp.zeros((rows, KV_WIDTH), F32))
                 for _ in groups)
    sel_state = lax.fori_loop(0, n_chunks, sel_step, init)

    kpos = win_start + lax.broadcasted_iota(jnp.int32, (1, WIN_KEYS), 1)
    di = qpos_col - kpos
    d = di.astype(F32)
    win_bias = jnp.where((di >= 0) & (di < WINDOW), 0.0, NEG)
    kwin = kwb[pl.ds(win_start, WIN_KEYS), :]
    vwin = vwb[pl.ds(win_start, WIN_KEYS), :]

    out = [jnp.zeros((tq, KV_WIDTH), F32) for _ in range(NSA_REP)]
    for g in groups:
        s = biased(_dot_nt(qgs[g], kwin), d, win_bias, g)
        e = jnp.exp(s - jnp.max(s, axis=-1, keepdims=True))
        o_win = _dot(e.astype(BF16), vwin) / jnp.sum(e, axis=-1, keepdims=True)
        _, l_sel, acc_sel = sel_state[g]
        o_sel = acc_sel / jnp.maximum(l_sel, 1e-30)
        mixed = gate_col(g, 0) * o_cmp[g] + gate_col(g, 1) * o_sel + gate_col(g, 2) * o_win
        gm = _group_mask(KV_WIDTH, g)
        for r in range(NSA_REP):
            out[r] = out[r] + jnp.where(gm, mixed[r * tq:(r + 1) * tq], 0.0)

    for r in range(NSA_REP):
        o_ref[0, :, r * KV_WIDTH:(r + 1) * KV_WIDTH] = out[r].astype(o_ref.dtype)


def _nsa_prompt(nq, gates, cmp_tok, kv, win, cover_t, expand):
    b, t, tq = BATCH, SEQ, NSA_TQ
    nq3 = nq.reshape(b, t, NSA_WIDTH)
    g3 = gates.reshape(b, t, 128)
    kv3 = kv.reshape(b, t, 4 * KV_WIDTH)
    win3 = win.reshape(b, t, 2 * KV_WIDTH)
    col = lambda c: pl.BlockSpec((1, t, KV_WIDTH), lambda bi, i: (bi, 0, c))
    tok = lambda c: pl.BlockSpec((1, 1, N_HALF, KV_WIDTH), lambda bi, i: (c, bi, 0, 0))
    return pl.pallas_call(
        _nsa_prompt_kernel,
        grid=(b, t // tq),
        in_specs=[pl.BlockSpec((1, tq, NSA_WIDTH), lambda bi, i: (bi, i, 0)),
                  pl.BlockSpec((1, tq, 128), lambda bi, i: (bi, i, 0)),
                  tok(0), tok(1), col(2), col(3), col(0), col(1),
                  pl.BlockSpec(cover_t.shape, lambda bi, i: (0, 0)),
                  pl.BlockSpec(expand.shape, lambda bi, i: (0, 0, 0))],
        out_specs=pl.BlockSpec((1, tq, NSA_WIDTH), lambda bi, i: (bi, i, 0)),
        out_shape=jax.ShapeDtypeStruct((b, t, NSA_WIDTH), BF16),
        scratch_shapes=[pltpu.VMEM((t, KV_WIDTH), BF16) for _ in range(4)],
        compiler_params=_params(("parallel", "arbitrary")),
        name="nsa_prompt",
    )(nq3, g3, cmp_tok, cmp_tok, kv3, kv3, win3, win3, cover_t, expand)


SAMPLE_SEL_BLOCKS = -(-(PAST_LEN + DEC_SEQ) // SEL_BLK)
SEL_ROWS = 48


def _nsa_sample_kernel(pt_ref, *refs):
    pages = refs[:N_PAGES]
    (q_ref, gate_ref, kc_ref, vc_ref, new_ref, cw_ref, wnew_ref, cover_ref, perm_ref, o_ref, wout_ref) = refs[N_PAGES:]
    t = DEC_SEQ
    nrow = NSA_REP * NSA_KV * t
    wb = cw_ref.shape[-1]

    q = q_ref[0] * SCALE
    pieces = []
    for r in range(NSA_REP):
        qr = q[:, r * KV_WIDTH:(r + 1) * KV_WIDTH]
        for g in range(NSA_KV):
            pieces.append(jnp.where(_group_mask(KV_WIDTH, g), qr, 0.0))
    qm = jnp.concatenate(pieces, axis=0).astype(BF16)

    rowi = lax.broadcasted_iota(jnp.int32, (nrow, 1), 0)
    g_col = lax.shift_right_logical(rowi, 3) & (NSA_KV - 1)
    r_col = lax.shift_right_logical(rowi, 5)
    slope = jnp.exp2(-8.0 * (g_col * NSA_REP + r_col + 1).astype(F32) / NSA_HEADS)
    qpos = PAST_LEN + (rowi & (t - 1))
    lane = lax.broadcasted_iota(jnp.int32, (1, 128), 1)
    pad_rows = lambda a: jnp.concatenate([a, jnp.zeros((128 - t, a.shape[-1]), F32)], axis=0)

    def bias_and_mask(s, kpos_row, extra_ok=None, window=False):
        d = (qpos - kpos_row).astype(F32)
        ok = d >= 0.0
        if window:
            ok = ok & (d < float(WINDOW))
        if extra_ok is not None:
            ok = ok & extra_ok
        return s - slope * d, ok

    s, ok = bias_and_mask(_dot(qm, kc_ref[0, 0].astype(BF16)), lane * CMP_STRIDE + (CMP_LEN - 1))
    e, den = _softmax_rows(s, ok)
    pc = e / den
    o_cmp = _dot_nt(pc.astype(BF16), vc_ref[0, 0].astype(BF16))
    hi, lo = _split_bf16(pc)
    imp = _dot(hi, cover_ref[...]) + _dot(lo, cover_ref[...])
    per_rep = NSA_KV * t
    imp = imp[0:per_rep] + imp[per_rep:2 * per_rep] + imp[2 * per_rep:3 * per_rep] + imp[3 * per_rep:]
    imp_t = jnp.concatenate([imp, jnp.zeros((nrow - per_rep, 128), F32)], axis=0).T
    cur = lax.shift_right_logical(PAST_LEN + (lane & (t - 1)), 6)
    sel_t = _select_blocks(imp_t[0:SEL_ROWS], cur, SAMPLE_SEL_BLOCKS)
    sel = jnp.concatenate([sel_t, jnp.zeros((128 - SEL_ROWS, 128), F32)], axis=0).T[0:per_rep]
    sel = jnp.concatenate([sel] * NSA_REP, axis=0)

    scores, oks = [], []
    for p in range(N_PAGES):
        chosen = jnp.where(lane < SEL_BLK, sel[:, 2 * p:2 * p + 1], sel[:, 2 * p + 1:2 * p + 2])
        s, ok = bias_and_mask(_dot(qm, pages[p][0, 0].astype(BF16)), p * PAGE_SIZE + lane, chosen > 0.5)
        scores.append(s)
        oks.append(ok)
    knew = pad_rows(new_ref[0, :, 0:KV_WIDTH]).astype(BF16)
    vnew = pad_rows(new_ref[0, :, KV_WIDTH:2 * KV_WIDTH]).astype(BF16)
    last = SAMPLE_SEL_BLOCKS - 1
    s, ok = bias_and_mask(_dot_nt(qm, knew), PAST_LEN + lane, (sel[:, last:last + 1] > 0.5) & (lane < t))
    scores.append(s)
    oks.append(ok)
    e, den = _softmax_rows(jnp.concatenate(scores, axis=-1), jnp.concatenate(oks, axis=-1))
    prob = (e / den).astype(BF16)
    o_sel = _dot(prob[:, PAST_LEN:], vnew)
    for p in range(N_PAGES):
        o_sel = o_sel + _dot_nt(prob[:, p * PAGE_SIZE:(p + 1) * PAGE_SIZE], pages[p][0, 1].astype(BF16))

    lane_w = lax.broadcasted_iota(jnp.int32, (1, wb), 1)
    s_old, ok_old = bias_and_mask(_dot(qm, cw_ref[0, 0].astype(BF16)), PAST_LEN - wb + lane_w, window=True)
    kwn = pad_rows(wnew_ref[0, :, 0:KV_WIDTH]).astype(BF16)
    vwn = pad_rows(wnew_ref[0, :, KV_WIDTH:2 * KV_WIDTH]).astype(BF16)
    s_new, ok_new = bias_and_mask(_dot_nt(qm, kwn), PAST_LEN + lane, lane < t, window=True)
    e, den = _softmax_rows(jnp.concatenate([s_old, s_new], axis=-1), jnp.concatenate([ok_old, ok_new], axis=-1))
    prob = (e / den).astype(BF16)
    o_win = _dot_nt(prob[:, 0:wb], cw_ref[0, 1].astype(BF16)) + _dot(prob[:, wb:], vwn)

    ghi, glo = _split_bf16(pad_rows(gate_ref[0]))
    grow = _dot(perm_ref[...], ghi) + _dot(perm_ref[...], glo)
    base = g_col * 12 + r_col * 3

    def gate_col(branch):
        return jnp.sum(jnp.where(lane == base + branch, grow, 0.0), axis=-1, keepdims=True)

    mixed = gate_col(0) * o_cmp + gate_col(1) * o_sel + gate_col(2) * o_win
    for r in range(NSA_REP):
        acc = jnp.zeros((t, KV_WIDTH), F32)
        for g in range(NSA_KV):
            lo_row = (r * NSA_KV + g) * t
            acc = acc + jnp.where(_group_mask(KV_WIDTH, g), mixed[lo_row:lo_row + t], 0.0)
        o_ref[0, :, r * KV_WIDTH:(r + 1) * KV_WIDTH] = acc.astype(o_ref.dtype)

    for c in range(2):
        new_t = pad_rows(wnew_ref[0, :, c * KV_WIDTH:(c + 1) * KV_WIDTH]).T
        tail = jnp.concatenate([jnp.zeros((KV_WIDTH, wb - 128), F32), pltpu.roll(new_t, 128 - t, 1)], axis=-1)
        wout_ref[0, c] = jnp.where(lane_w >= wb - t, tail, pltpu.roll(cw_ref[0, c], wb - t, 1))


def _nsa_sample(nq, gates, cmp_tok, cache_t, page_flat, kv_new, win_t, win_new, cover, perm):
    b, t = DEC_BATCH, DEC_SEQ
    wb = win_t.shape[-1]
    per_b = lambda shape: pl.BlockSpec(shape, lambda bi, pt: (bi,) + (0,) * (len(shape) - 1))
    page_specs = [pl.BlockSpec((1, 2, KV_WIDTH, PAGE_SIZE),
                               functools.partial(lambda bi, pt, p: (pt[bi * N_PAGES + p], 1, 0, 0), p=p))
                  for p in range(N_PAGES)]
    tok = lambda c: pl.BlockSpec((1, 1, KV_WIDTH, N_HALF), lambda bi, pt: (c, bi, 0, 0))
    grid_spec = pltpu.PrefetchScalarGridSpec(
        num_scalar_prefetch=1,
        grid=(b,),
        in_specs=page_specs + [per_b((1, t, NSA_WIDTH)), per_b((1, t, 128)), tok(0), tok(1),
                               pl.BlockSpec((1, t, 2 * KV_WIDTH), lambda bi, pt: (bi, 0, 1)),
                               per_b((1, 2, KV_WIDTH, wb)), per_b((1, t, 2 * KV_WIDTH)),
                               pl.BlockSpec(cover.shape, lambda bi, pt: (0, 0)),
                               pl.BlockSpec(perm.shape, lambda bi, pt: (0, 0))],
        out_specs=[per_b((1, t, NSA_WIDTH)), per_b((1, 2, KV_WIDTH, wb))],
    )
    return pl.pallas_call(
        _nsa_sample_kernel,
        grid_spec=grid_spec,
        out_shape=[jax.ShapeDtypeStruct((b, t, NSA_WIDTH), BF16), jax.ShapeDtypeStruct(win_t.shape, F32)],
        compiler_params=_params(("arbitrary",)),
        name="nsa_sample",
    )(page_flat, *([cache_t] * N_PAGES), nq.reshape(b, t, NSA_WIDTH), gates.reshape(b, t, 128), cmp_tok, cmp_tok,
      kv_new.reshape(b, t, 4 * KV_WIDTH), win_t, win_new.reshape(b, t, 2 * KV_WIDTH), cover, perm)


def _out_kernel(x_ref, ret_ref, nsa_ref, wr_ref, wn_ref, o_ref):
    o_ref[...] = x_ref[...] + _dot(ret_ref[...], wr_ref[...]) + _dot(nsa_ref[...], wn_ref[...])


def _mixer_out(x2d, ret, nsa, w_ret, w_nsa, tm=1024, tn=512):
    n, d = x2d.shape
    return pl.pallas_call(
        _out_kernel,
        grid=(n // tm, d // tn),
        in_specs=[pl.BlockSpec((tm, tn), lambda i, j: (i, j)),
                  pl.BlockSpec((tm, RET_WIDTH), lambda i, j: (i, 0)),
                  pl.BlockSpec((tm, NSA_WIDTH), lambda i, j: (i, 0)),
                  pl.BlockSpec((RET_WIDTH, tn), lambda i, j: (0, j)),
                  pl.BlockSpec((NSA_WIDTH, tn), lambda i, j: (0, j))],
        out_specs=pl.BlockSpec((tm, tn), lambda i, j: (i, j)),
        out_shape=jax.ShapeDtypeStruct((n, d), F32),
        compiler_params=_params(("parallel", "arbitrary")),
        name="mixer_out",
    )(x2d, ret, nsa, w_ret, w_nsa)


FFN_TN = 512
FFN_COLS = D_FF // FFN_TN


def _ffn_up_kernel(*refs, hist, step, tiles_per_seq, has_prev):
    if has_prev:
        x_ref, wa_ref, wb_ref, cwa_ref, cwb_ref, cba_ref, cbb_ref, pa_ref, pb_ref = refs[:9]
        act_ref, sta_ref, stb_ref, exta, extb = refs[9:]
    else:
        x_ref, wa_ref, wb_ref, cwa_ref, cwb_ref, cba_ref, cbb_ref = refs[:7]
        act_ref, sta_ref, stb_ref, exta, extb = refs[7:]
        pa_ref = pb_ref = None
    tm = x_ref.shape[0]
    first = (pl.program_id(1) % tiles_per_seq) == 0
    x = x_ref[...]
    outs = []
    for w_ref, cw_ref, cb_ref, p_ref, st_ref, ext in ((wa_ref, cwa_ref, cba_ref, pa_ref, sta_ref, exta),
                                                     (wb_ref, cwb_ref, cbb_ref, pb_ref, stb_ref, extb)):
        u = _dot(x, w_ref[...])
        if has_prev:
            ext[hist - 2 * step:hist, :] = p_ref[...]
        else:
            @pl.when(first)
            def _():
                ext[hist - 2 * step:hist, :] = jnp.zeros((2 * step, u.shape[-1]), F32)
        ext[hist:hist + tm, :] = u
        cw = cw_ref[...]
        c = (cb_ref[...] + cw[2:3] * u + cw[1:2] * ext[hist - step:hist - step + tm, :]
             + cw[0:1] * ext[hist - 2 * step:hist - 2 * step + tm, :])
        outs.append(c)
        tail = ext[hist + tm - 2 * step:hist + tm, :]
        if has_prev:
            st_ref[...] = tail
        else:
            st_ref[0] = tail
            ext[hist - 2 * step:hist, :] = tail
    act_ref[...] = (jax.nn.silu(outs[0]) * outs[1]).astype(act_ref.dtype)


def _ffn_up(hn, w_up, conv_w, conv_b, prev, *, tm, step, seqs, tiles_per_seq):
    n, d = hn.shape
    tn = FFN_TN
    hist = 8 if step == 1 else 2 * step
    has_prev = prev is not None
    half = lambda off: (lambda j, i: (0, j + off))
    in_specs = [pl.BlockSpec((tm, d), lambda j, i: (i, 0)),
                pl.BlockSpec((d, tn), half(0)), pl.BlockSpec((d, tn), half(FFN_COLS)),
                pl.BlockSpec((CONV_W, tn), half(0)), pl.BlockSpec((CONV_W, tn), half(FFN_COLS)),
                pl.BlockSpec((1, tn), half(0)), pl.BlockSpec((1, tn), half(FFN_COLS))]
    args = [hn, w_up, w_up, conv_w, conv_w, conv_b, conv_b]
    if has_prev:
        in_specs += [pl.BlockSpec((2 * step, tn), half(0)), pl.BlockSpec((2 * step, tn), half(FFN_COLS))]
        args += [prev, prev]
        st_spec = pl.BlockSpec((2 * step, tn), lambda j, i: (0, j))
        st_shape = jax.ShapeDtypeStruct((2 * step, D_FF), F32)
    else:
        st_spec = pl.BlockSpec((1, 2, tn), lambda j, i: (i // tiles_per_seq, 0, j))
        st_shape = jax.ShapeDtypeStruct((seqs, 2, D_FF), F32)
    kernel = functools.partial(_ffn_up_kernel, hist=hist, step=step, tiles_per_seq=tiles_per_seq, has_prev=has_prev)
    return pl.pallas_call(
        kernel,
        grid=(FFN_COLS, n // tm),
        in_specs=in_specs,
        out_specs=[pl.BlockSpec((tm, tn), lambda j, i: (i, j)), st_spec, st_spec],
        out_shape=[jax.ShapeDtypeStruct((n, D_FF), BF16), st_shape, st_shape],
        scratch_shapes=[pltpu.VMEM((hist + tm, tn), F32), pltpu.VMEM((hist + tm, tn), F32)],
        compiler_params=_params(("arbitrary", "arbitrary")),
        name="ffn_up",
    )(*args)


def _ffn_down_kernel(h_ref, a_ref, w_ref, o_ref):
    o_ref[...] = h_ref[...] + _dot(a_ref[...], w_ref[...])


def _ffn_down(h, act, w_down, tm=512, tn=512):
    n, d = h.shape
    return pl.pallas_call(
        _ffn_down_kernel,
        grid=(n // tm, d // tn),
        in_specs=[pl.BlockSpec((tm, tn), lambda i, j: (i, j)),
                  pl.BlockSpec((tm, D_FF), lambda i, j: (i, 0)),
                  pl.BlockSpec((D_FF, tn), lambda i, j: (0, j))],
        out_specs=pl.BlockSpec((tm, tn), lambda i, j: (i, j)),
        out_shape=jax.ShapeDtypeStruct((n, d), F32),
        compiler_params=_params(("parallel", "arbitrary")),
        name="ffn_down",
    )(h, act, w_down)


def _block_cover_t(n_sel, rows):
    cs = np.arange(N_HALF - 1)[None, :] * CMP_STRIDE
    js = np.arange(n_sel)[:, None] * SEL_BLK
    cov = np.clip(np.minimum(cs + CMP_LEN, js + SEL_BLK) - np.maximum(cs, js), 0, None) / CMP_LEN
    out = np.zeros((rows, N_HALF), np.float32)
    out[:n_sel, :N_HALF - 1] = cov
    return jnp.asarray(out, dtype=BF16)


def _prepare_weights(g_attn, w_in, q_norm, k_norm_cmp, k_norm_slc, k_norm_win, cmp_pe, cmp_w1, cmp_b1, cmp_w2,
                     cmp_b2, w_out):
    d = D_MODEL
    o = 0
    w_rqk = w_in[:, o:o + 2 * RET_WIDTH]; o += 2 * RET_WIDTH
    w_rvg = w_in[:, o:o + 2 * RET_WIDTH]; o += 2 * RET_WIDTH
    w_nq = w_in[:, o:o + NSA_WIDTH]; o += NSA_WIDTH
    w_kv = w_in[:, o:o + 4 * KV_WIDTH]; o += 4 * KV_WIDTH
    w_win = w_in[:, o:o + 2 * KV_WIDTH]; o += 2 * KV_WIDTH
    w_ng = w_in[:, o:]
    w_nq = w_nq.reshape(d, NSA_KV, NSA_REP, HEAD_DIM).transpose(0, 2, 1, 3).reshape(d, NSA_WIDTH)
    w_ng = jnp.pad(w_ng, ((0, 0), (0, 128 - w_ng.shape[1])))
    tile4 = lambda v: jnp.tile(v, NSA_KV)
    zeros = jnp.zeros((KV_WIDTH,), F32)
    ones = jnp.ones((KV_WIDTH,), F32)
    wts = dict(
        g_attn=g_attn,
        w_rqk=w_rqk.astype(BF16), w_rvg=w_rvg.astype(BF16), w_nq=w_nq.astype(BF16),
        w_kv=w_kv.astype(BF16), w_win=w_win.astype(BF16), w_ng=w_ng.astype(BF16),
        rot_scale=jnp.concatenate([jnp.ones((RET_WIDTH,), F32),
                                   jnp.full((RET_WIDTH,), RET_DK ** -0.5, F32)]).reshape(1, -1),
        nq_gain=jnp.tile(q_norm, NSA_HEADS).reshape(1, -1), nq_mask=jnp.ones((1, NSA_WIDTH), F32),
        kv_gain=jnp.concatenate([zeros, zeros, tile4(k_norm_slc), zeros]).reshape(1, -1),
        kv_mask=jnp.concatenate([zeros, zeros, ones, zeros]).reshape(1, -1),
        win_gain=jnp.concatenate([tile4(k_norm_win), zeros]).reshape(1, -1),
        win_mask=jnp.concatenate([ones, zeros]).reshape(1, -1),
    )
    w1 = cmp_w1.reshape(2, 2, CMP_STRIDE, HEAD_DIM, CMP_HIDDEN)
    w1rep = jnp.broadcast_to(w1.transpose(0, 2, 3, 1, 4)[:, :, None],
                             (2, CMP_STRIDE, NSA_KV, HEAD_DIM, 2, CMP_HIDDEN))
    w1rep = w1rep.reshape(2, CMP_STRIDE * KV_WIDTH, 2 * CMP_HIDDEN).astype(BF16)
    w1flat = w1.reshape(2, 2, CMP_STRIDE * HEAD_DIM, CMP_HIDDEN).astype(BF16)
    pe8 = jnp.pad(cmp_pe.reshape(2, 2, 1, CMP_STRIDE * HEAD_DIM), ((0, 0), (0, 0), (0, 15), (0, 0)))
    bias = _cmp_bias(pe8, w1flat, cmp_b1.reshape(2, 1, CMP_HIDDEN))
    w1pair = jnp.broadcast_to(w1.transpose(0, 2, 3, 1, 4)[:, :, None], (2, CMP_STRIDE, 2, HEAD_DIM, 2, CMP_HIDDEN))
    col = lambda v: jnp.broadcast_to(v[:, :, None], (2, HEAD_DIM, N_HALF))
    cw = dict(
        w1pair=w1pair.reshape(2, CMP_STRIDE * 128, 2 * CMP_HIDDEN).astype(BF16),
        w2t=cmp_w2.transpose(0, 2, 1).astype(BF16), b2col=col(cmp_b2),
        gaincol=col(jnp.stack([k_norm_cmp, jnp.ones((HEAD_DIM,), F32)])),
        w1rep=w1rep, bias=bias,
        w2rep=jnp.tile(cmp_w2, (1, 1, NSA_KV)).astype(BF16),
        b2rep=jnp.tile(cmp_b2, (1, NSA_KV)).reshape(2, 1, KV_WIDTH),
        gain=jnp.stack([tile4(k_norm_cmp), ones]).reshape(2, 1, KV_WIDTH),
    )
    w_ret = w_out[:RET_WIDTH].astype(BF16)
    w_nsa = w_out[RET_WIDTH:].reshape(NSA_KV, NSA_REP, HEAD_DIM, d).transpose(1, 0, 2, 3).reshape(NSA_WIDTH, d)
    return wts, cw, w_ret, w_nsa.astype(BF16)


def kernel(x_prompt, x_sample, cache_kv, cache_win, state_ret, state_conv, page_table, g_attn, w_in, q_norm,
           k_norm_cmp, k_norm_slc, k_norm_win, cmp_pe, cmp_w1, cmp_b1, cmp_w2, cmp_b2, ret_gn, w_out, g_ffn, w_up,
           conv_w, conv_b, w_down):
    assert x_prompt.shape == (BATCH, SEQ, D_MODEL) and x_sample.shape == (DEC_BATCH, DEC_SEQ, D_MODEL)
    assert g_attn.shape[0] == 1, "single layer"
    wts, cw, w_ret, w_nsa = _prepare_weights(g_attn[0], w_in[0], q_norm[0], k_norm_cmp[0], k_norm_slc[0],
                                              k_norm_win[0], cmp_pe[0], cmp_w1[0], cmp_b1[0], cmp_w2[0],
                                              cmp_b2[0], w_out[0])
    w_up_b = w_up[0].astype(BF16)
    w_down_b = w_down[0].astype(BF16)
    cb = conv_b[0].reshape(1, -1)
    n_p, n_s = BATCH * SEQ, DEC_BATCH * DEC_SEQ
    xp = x_prompt.reshape(n_p, D_MODEL)
    xs = x_sample.reshape(n_s, D_MODEL)

    pos_p = jnp.tile(jnp.arange(SEQ, dtype=jnp.int32), BATCH)
    qk, vg, nq, kv_p, win_p, gates = _project_all(xp, pos_p, wts)
    ret_mix, ret_state_p = _retention_prompt(qk, vg, ret_gn[0])
    kv3 = kv_p.reshape(BATCH, SEQ, 4 * KV_WIDTH)
    cmp_tok = _compress_prompt(kv3, cw)
    cover_p = _block_cover_t(SEQ // SEL_BLK, SEQ // SEL_BLK)
    key = np.arange(SEQ).reshape(SEQ // SEL_CHUNK, 1, SEL_CHUNK)
    expand = jnp.asarray((key // SEL_BLK) == np.arange(128)[None, :, None], dtype=BF16)
    nsa = _nsa_prompt(nq, gates, cmp_tok, kv_p, win_p, cover_p, expand)
    h_p = _mixer_out(xp, ret_mix.reshape(n_p, RET_WIDTH), nsa.reshape(n_p, NSA_WIDTH), w_ret, w_nsa)
    hn_p = _rmsnorm(h_p, g_ffn[0])
    act_p, st_a, st_b = _ffn_up(hn_p, w_up_b, conv_w[0], cb, None, tm=1024, step=1, seqs=BATCH,
                                tiles_per_seq=SEQ // 1024)
    y_p = _ffn_down(h_p, act_p, w_down_b)
    conv_p = jnp.concatenate([st_a, st_b], axis=-1)

    pos_s = jnp.tile(PAST_LEN + jnp.arange(DEC_SEQ, dtype=jnp.int32), DEC_BATCH)
    qk, vg, nq, kv_s, win_s, gates = _project_all(xs, pos_s, wts)
    ret_mix, ret_state_s = _retention_sample(qk, vg, state_ret[0], ret_gn[0])
    cache_t = cache_kv[0].transpose(0, 2, 3, 4, 1).reshape(cache_kv.shape[1], 4, KV_WIDTH, PAGE_SIZE)
    page_flat = page_table.reshape(-1).astype(jnp.int32)
    cmp_tok = _compress_sample(cache_t, page_flat, cw)
    wb = cache_win.shape[2]
    win_t = cache_win[0].transpose(0, 2, 3, 4, 1).reshape(DEC_BATCH, 2, KV_WIDTH, wb)
    cover_s = _block_cover_t(SAMPLE_SEL_BLOCKS, 128).T
    lanes = np.arange(128)
    perm = jnp.asarray((lanes[:, None] % DEC_SEQ) == lanes[None, :], dtype=BF16)
    nsa, win_out_t = _nsa_sample(nq, gates, cmp_tok, cache_t, page_flat, kv_s, win_t, win_s, cover_s, perm)
    h_s = _mixer_out(xs, ret_mix.reshape(n_s, RET_WIDTH), nsa.reshape(n_s, NSA_WIDTH), w_ret, w_nsa)
    to_tm = lambda a: a.reshape(DEC_BATCH, -1, a.shape[-1]).transpose(1, 0, 2).reshape(-1, a.shape[-1])
    from_tm = lambda a, t: a.reshape(t, DEC_BATCH, a.shape[-1]).transpose(1, 0, 2)
    h_tm = to_tm(h_s)
    hn_s = _rmsnorm(h_tm, g_ffn[0])
    act_s, st_a, st_b = _ffn_up(hn_s, w_up_b, conv_w[0], cb, to_tm(state_conv[0]), tm=n_s, step=DEC_BATCH,
                                seqs=DEC_BATCH, tiles_per_seq=1)
    y_s = from_tm(_ffn_down(h_tm, act_s, w_down_b), DEC_SEQ)
    conv_s = from_tm(jnp.concatenate([st_a, st_b], axis=-1), CONV_W - 1)

    win_sample = win_out_t.reshape(DEC_BATCH, 2, NSA_KV, HEAD_DIM, wb).transpose(0, 4, 1, 2, 3)
    return (
        y_p.reshape(BATCH, SEQ, D_MODEL),
        y_s,
        kv_p.reshape(1, BATCH, SEQ, 4, NSA_KV, HEAD_DIM),
        kv_s.reshape(1, DEC_BATCH, DEC_SEQ, 4, NSA_KV, HEAD_DIM),
        win_p.reshape(BATCH, SEQ, 2 * KV_WIDTH)[:, SEQ - WINDOW:].reshape(1, BATCH, WINDOW, 2, NSA_KV, HEAD_DIM),
        win_sample[None],
        ret_state_p[None],
        ret_state_s[None],
        conv_p[None],
        conv_s[None],
    )
```

```python
import functools

import numpy as np
import jax
import jax.numpy as jnp
from jax import lax
from jax.experimental import pallas as pl
from jax.experimental.pallas import tpu as pltpu

D_MODEL = 2048
BATCH = 4
SEQ = 2048
DEC_BATCH = 128
DEC_SEQ = 8
PAST_LEN = 2048
PAGE_SIZE = 128
N_PAGES = PAST_LEN // PAGE_SIZE
RET_HEADS = 4
RET_DK = 256
RET_DV = 256
RET_WIDTH = RET_HEADS * RET_DV
RET_CHUNK = 128
NSA_HEADS = 16
NSA_KV = 4
NSA_REP = NSA_HEADS // NSA_KV
HEAD_DIM = 64
NSA_WIDTH = NSA_HEADS * HEAD_DIM
KV_WIDTH = NSA_KV * HEAD_DIM
CMP_LEN = 32
CMP_STRIDE = 16
CMP_HIDDEN = 256
SEL_BLK = 64
SEL_TOPK = 16
WINDOW = 512
SCALE = HEAD_DIM ** -0.5
D_FF = 5632
CONV_W = 3
EPS = 1e-6

N_HALF = PAST_LEN // CMP_STRIDE
NEG = -1e30
FORCED = 1e30
VMEM_LIMIT = 56 * 1024 * 1024

BF16 = jnp.bfloat16
F32 = jnp.float32


def _params(semantics, vmem=VMEM_LIMIT):
    return pltpu.CompilerParams(dimension_semantics=semantics, vmem_limit_bytes=vmem)


def _dot(a, b):
    return jnp.dot(a, b, preferred_element_type=F32)


def _dot_nt(a, b):
    return lax.dot_general(a, b, (((1,), (1,)), ((), ())), preferred_element_type=F32)


def _split_bf16(x):
    hi = x.astype(BF16)
    lo = (x - hi.astype(F32)).astype(BF16)
    return hi, lo


def _group_mask(width, g):
    lane = lax.broadcasted_iota(jnp.int32, (1, width), 1)
    return (lane >= g * HEAD_DIM) & (lane < (g + 1) * HEAD_DIM)


def _rms_groups64(z, gain):
    width = z.shape[-1]
    outs = []
    for k in range(width // 128):
        zk = z[:, k * 128:(k + 1) * 128]
        zz = zk * zk
        lane = lax.broadcasted_iota(jnp.int32, (1, 128), 1)
        lo_half = lane < HEAD_DIM
        s_lo = jnp.sum(jnp.where(lo_half, zz, 0.0), axis=-1, keepdims=True)
        s_hi = jnp.sum(jnp.where(lo_half, 0.0, zz), axis=-1, keepdims=True)
        ms = jnp.where(lo_half, s_lo, s_hi) * (1.0 / HEAD_DIM)
        outs.append(zk * lax.rsqrt(ms + EPS))
    y = outs[0] if len(outs) == 1 else jnp.concatenate(outs, axis=-1)
    return y * gain


def _rmsnorm_kernel(x_ref, g_ref, o_ref):
    x = x_ref[...]
    ms = jnp.mean(x * x, axis=-1, keepdims=True)
    o_ref[...] = (x * lax.rsqrt(ms + EPS) * g_ref[...]).astype(o_ref.dtype)


def _rmsnorm(x, g, tm=512):
    n, d = x.shape
    return pl.pallas_call(
        _rmsnorm_kernel,
        grid=(n // tm,),
        in_specs=[pl.BlockSpec((tm, d), lambda i: (i, 0)), pl.BlockSpec((1, d), lambda i: (0, 0))],
        out_specs=pl.BlockSpec((tm, d), lambda i: (i, 0)),
        out_shape=jax.ShapeDtypeStruct((n, d), BF16),
        compiler_params=_params(("parallel",)),
        name="rmsnorm",
    )(x, g.reshape(1, d))


def _proj_plain_kernel(x_ref, w_ref, o_ref):
    o_ref[...] = _dot(x_ref[...], w_ref[...]).astype(o_ref.dtype)


def _proj_sigmoid_kernel(x_ref, w_ref, o_ref):
    o_ref[...] = jax.nn.sigmoid(_dot(x_ref[...], w_ref[...]))


def _proj_norm_kernel(x_ref, w_ref, gain_ref, nmask_ref, o_ref):
    z = _dot(x_ref[...], w_ref[...])
    o_ref[...] = jnp.where(nmask_ref[...] > 0.5, _rms_groups64(z, gain_ref[...]), z)


def _proj_rot_kernel(x_ref, w_ref, cos_ref, sin_ref, scale_ref, o_ref):
    z = _dot(x_ref[...], w_ref[...])
    c = cos_ref[...]
    s = sin_ref[...]
    sc = scale_ref[...]
    for hh in range(z.shape[-1] // RET_DK):
        x1 = z[:, hh * RET_DK:hh * RET_DK + 128]
        x2 = z[:, hh * RET_DK + 128:(hh + 1) * RET_DK]
        o_ref[:, hh * RET_DK:hh * RET_DK + 128] = (x1 * c - x2 * s) * sc[:, hh * RET_DK:hh * RET_DK + 128]
        o_ref[:, hh * RET_DK + 128:(hh + 1) * RET_DK] = (x2 * c + x1 * s) * sc[:, hh * RET_DK + 128:(hh + 1) * RET_DK]


def _proj(xn, w, kernel, extras=(), extra_specs=(), out_dtype=F32, tm=1024, tn=512, name="proj"):
    n, d = xn.shape
    c = w.shape[1]
    tn = min(tn, c)
    return pl.pallas_call(
        kernel,
        grid=(n // tm, c // tn),
        in_specs=[pl.BlockSpec((tm, d), lambda i, j: (i, 0)), pl.BlockSpec((d, tn), lambda i, j: (0, j))]
        + list(extra_specs),
        out_specs=pl.BlockSpec((tm, tn), lambda i, j: (i, j)),
        out_shape=jax.ShapeDtypeStruct((n, c), out_dtype),
        compiler_params=_params(("parallel", "arbitrary")),
        name=name,
    )(xn, w, *extras)


def _project_all(x2d, pos, wts, tm=1024):
    n = x2d.shape[0]
    xn = _rmsnorm(x2d, wts["g_attn"])
    half = RET_DK // 2
    inv = 1.0 / (10000.0 ** jnp.linspace(0.0, 1.0, half, dtype=F32))
    ang = pos.astype(F32)[:, None] * inv[None, :]
    cos = jnp.cos(ang)
    sin = jnp.sin(ang)
    tn = 512
    row_spec = pl.BlockSpec((tm, 128), lambda i, j: (i, 0))
    col_spec = pl.BlockSpec((1, tn), lambda i, j: (0, j))
    qk = _proj(xn, wts["w_rqk"], _proj_rot_kernel, (cos, sin, wts["rot_scale"]),
               (row_spec, row_spec, col_spec), tm=tm, name="proj_rot")
    vg = _proj(xn, wts["w_rvg"], _proj_plain_kernel, tm=tm, name="proj_vg")
    nq = _proj(xn, wts["w_nq"], _proj_norm_kernel, (wts["nq_gain"], wts["nq_mask"]),
               (col_spec, col_spec), tm=tm, name="proj_nq")
    kv = _proj(xn, wts["w_kv"], _proj_norm_kernel, (wts["kv_gain"], wts["kv_mask"]),
               (col_spec, col_spec), tm=tm, name="proj_kv")
    win = _proj(xn, wts["w_win"], _proj_norm_kernel, (wts["win_gain"], wts["win_mask"]),
                (col_spec, col_spec), tm=tm, name="proj_win")
    gates = _proj(xn, wts["w_ng"], _proj_sigmoid_kernel, tm=tm, name="proj_gate")
    return qk, vg, nq, kv, win, gates


def _retention_step(q, k, v, state, dmask, xi, zeta, gch):
    qb = q.astype(BF16)
    kb = k.astype(BF16)
    vb = v.astype(BF16)
    s = _dot_nt(qb, kb) * dmask
    o = _dot(s.astype(BF16), vb) + _dot(qb, state.astype(BF16)) * xi
    kz = (k * zeta).astype(BF16)
    new_state = state * gch + _dot(kz.T, vb)
    return o, new_state


def _ret_mix(o, gn, gate):
    ms = jnp.mean(o * o, axis=-1, keepdims=True)
    return o * lax.rsqrt(ms + EPS) * gn * jax.nn.silu(gate)


def _ret_prompt_kernel(qk_ref, vg_ref, dmask_ref, xi_ref, zeta_ref, gch_ref, gn_ref, mix_ref, state_ref):
    @pl.when(pl.program_id(1) == 0)
    def _():
        state_ref[...] = jnp.zeros_like(state_ref)

    for h in range(RET_HEADS):
        lo, hi = h * RET_DK, (h + 1) * RET_DK
        o, new_state = _retention_step(qk_ref[0, :, lo:hi], qk_ref[0, :, RET_WIDTH + lo:RET_WIDTH + hi],
                                       vg_ref[0, :, lo:hi], state_ref[0, h], dmask_ref[h], xi_ref[h],
                                       zeta_ref[h], gch_ref[h])
        state_ref[0, h] = new_state
        mix_ref[0, :, lo:hi] = _ret_mix(o, gn_ref[h], vg_ref[0, :, RET_WIDTH + lo:RET_WIDTH + hi]).astype(mix_ref.dtype)


def _ret_tables(chunk):
    h = jnp.arange(RET_HEADS, dtype=F32)
    lg = jnp.log(1.0 - jnp.exp2(-5.0 - h))
    i = jnp.arange(chunk, dtype=F32)
    diff = i[:, None] - i[None, :]
    dmask = jnp.where(diff >= 0, jnp.exp(lg[:, None, None] * jnp.maximum(diff, 0.0)), 0.0)
    xi = jnp.exp(lg[:, None] * (i[None, :] + 1.0))
    zeta = jnp.exp(lg[:, None] * (chunk - 1.0 - i[None, :]))
    gch = jnp.exp(lg * chunk)
    bc = lambda a: jnp.broadcast_to(a[:, :, None], (RET_HEADS, chunk, RET_DV))
    return dmask, bc(xi), bc(zeta), jnp.broadcast_to(gch[:, None, None], (RET_HEADS, 1, RET_DV))


def _retention_prompt(qk, vg, gn):
    b, t = BATCH, SEQ
    c = RET_CHUNK
    qk3 = qk.reshape(b, t, 2 * RET_WIDTH)
    vg3 = vg.reshape(b, t, 2 * RET_WIDTH)
    dmask, xi, zeta, gch = _ret_tables(c)
    rows = pl.BlockSpec((1, c, 2 * RET_WIDTH), lambda bi, ci: (bi, ci, 0))
    full = lambda shape: pl.BlockSpec(shape, lambda bi, ci: (0,) * len(shape))
    return pl.pallas_call(
        _ret_prompt_kernel,
        grid=(b, t // c),
        in_specs=[rows, rows, full((RET_HEADS, c, c)), full((RET_HEADS, c, RET_DV)), full((RET_HEADS, c, RET_DV)),
                  full((RET_HEADS, 1, RET_DV)), full((RET_HEADS, 1, RET_DV))],
        out_specs=[pl.BlockSpec((1, c, RET_WIDTH), lambda bi, ci: (bi, ci, 0)),
                   pl.BlockSpec((1, RET_HEADS, RET_DK, RET_DV), lambda bi, ci: (bi, 0, 0, 0))],
        out_shape=[jax.ShapeDtypeStruct((b, t, RET_WIDTH), BF16),
                   jax.ShapeDtypeStruct((b, RET_HEADS, RET_DK, RET_DV), F32)],
        compiler_params=_params(("parallel", "arbitrary")),
        name="retention_prompt",
    )(qk3, vg3, dmask, xi, zeta, gch, gn.reshape(RET_HEADS, 1, RET_DV))


def _ret_sample_kernel(qk_ref, vg_ref, s0_ref, dmask_ref, xi_ref, zeta_ref, gch_ref, gn_ref, mix_ref, state_ref):
    t = qk_ref.shape[1]
    pad = lambda a: jnp.concatenate([a, jnp.zeros((RET_PAD - t, a.shape[-1]), F32)], axis=0)
    for h in range(RET_HEADS):
        lo, hi = h * RET_DK, (h + 1) * RET_DK
        o, new_state = _retention_step(pad(qk_ref[0, :, lo:hi]), pad(qk_ref[0, :, RET_WIDTH + lo:RET_WIDTH + hi]),
                                       pad(vg_ref[0, :, lo:hi]), s0_ref[0, h], dmask_ref[h], xi_ref[h],
                                       zeta_ref[h], gch_ref[h])
        state_ref[0, h] = new_state
        mix_ref[0, :, lo:hi] = _ret_mix(o[0:t], gn_ref[h], vg_ref[0, :, RET_WIDTH + lo:RET_WIDTH + hi]).astype(mix_ref.dtype)


RET_PAD = 128


def _retention_sample(qk, vg, state0, gn):
    b, t = DEC_BATCH, DEC_SEQ
    dmask, xi, zeta, gch = _ret_tables(t)
    dmask = jnp.pad(dmask, ((0, 0), (0, RET_PAD - t), (0, RET_PAD - t)))
    xi = jnp.pad(xi, ((0, 0), (0, RET_PAD - t), (0, 0)))
    zeta = jnp.pad(zeta, ((0, 0), (0, RET_PAD - t), (0, 0)))
    full = lambda shape: pl.BlockSpec(shape, lambda bi: (0,) * len(shape))
    return pl.pallas_call(
        _ret_sample_kernel,
        grid=(b,),
        in_specs=[pl.BlockSpec((1, t, 2 * RET_WIDTH), lambda bi: (bi, 0, 0)),
                  pl.BlockSpec((1, t, 2 * RET_WIDTH), lambda bi: (bi, 0, 0)),
                  pl.BlockSpec((1, RET_HEADS, RET_DK, RET_DV), lambda bi: (bi, 0, 0, 0)),
                  full((RET_HEADS, RET_PAD, RET_PAD)), full((RET_HEADS, RET_PAD, RET_DV)), full((RET_HEADS, RET_PAD, RET_DV)),
                  full((RET_HEADS, 1, RET_DV)), full((RET_HEADS, 1, RET_DV))],
        out_specs=[pl.BlockSpec((1, t, RET_WIDTH), lambda bi: (bi, 0, 0)),
                   pl.BlockSpec((1, RET_HEADS, RET_DK, RET_DV), lambda bi: (bi, 0, 0, 0))],
        out_shape=[jax.ShapeDtypeStruct((b, t, RET_WIDTH), BF16),
                   jax.ShapeDtypeStruct((b, RET_HEADS, RET_DK, RET_DV), F32)],
        compiler_params=_params(("parallel",)),
        name="retention_sample",
    )(qk.reshape(b, t, -1), vg.reshape(b, t, -1), state0, dmask, xi, zeta, gch, gn.reshape(RET_HEADS, 1, RET_DV))


def _cmp_bias_kernel(pe_ref, w_ref, b1_ref, o_ref):
    acc = b1_ref[0]
    for j in range(2):
        acc = acc + _dot(pe_ref[0, j].astype(BF16), w_ref[0, j])[0:1]
    o_ref[0] = acc


def _cmp_bias(pe8, w1flat, b1):
    return pl.pallas_call(
        _cmp_bias_kernel,
        grid=(2,),
        in_specs=[pl.BlockSpec((1, 2, 16, CMP_STRIDE * HEAD_DIM), lambda c: (c, 0, 0, 0)),
                  pl.BlockSpec((1, 2, CMP_STRIDE * HEAD_DIM, CMP_HIDDEN), lambda c: (c, 0, 0, 0)),
                  pl.BlockSpec((1, 1, CMP_HIDDEN), lambda c: (c, 0, 0))],
        out_specs=pl.BlockSpec((1, 1, CMP_HIDDEN), lambda c: (c, 0, 0)),
        out_shape=jax.ShapeDtypeStruct((2, 1, CMP_HIDDEN), F32),
        compiler_params=_params(("parallel",)),
        name="cmp_bias",
    )(pe8, w1flat, b1)


def _compress_body(load_rows, w1_ref, bias_ref, w2_ref, b2_ref, gain_ref, o_ref, xm_ref):
    for l in range(CMP_STRIDE):
        xl = load_rows(l)
        for g in range(NSA_KV):
            xm_ref[g * N_HALF:(g + 1) * N_HALF, l * KV_WIDTH:(l + 1) * KV_WIDTH] = jnp.where(
                _group_mask(KV_WIDTH, g), xl, 0.0).astype(BF16)
    acc = _dot(xm_ref[...], w1_ref[0])
    h0 = acc[:, :CMP_HIDDEN]
    h1 = pltpu.roll(acc[:, CMP_HIDDEN:], 4 * N_HALF - 1, 0)
    hid = jax.nn.silu(h0 + h1 + bias_ref[0])
    out = _dot(hid.astype(BF16), w2_ref[0]) + b2_ref[0]
    res = jnp.zeros((N_HALF, KV_WIDTH), F32)
    for g in range(NSA_KV):
        res = res + jnp.where(_group_mask(KV_WIDTH, g), out[g * N_HALF:(g + 1) * N_HALF], 0.0)
    normed = _rms_groups64(res, gain_ref[0])
    o_ref[0, 0] = jnp.where(pl.program_id(0) == 0, normed, res)


def _compress_prompt_kernel(lo_ref, hi_ref, w1_ref, bias_ref, w2_ref, b2_ref, gain_ref, o_ref, xm_ref):
    load = lambda l: jnp.concatenate(
        [r[0, pl.ds(l, N_HALF, stride=CMP_STRIDE), :] for r in (lo_ref, hi_ref)], axis=-1)
    _compress_body(load, w1_ref, bias_ref, w2_ref, b2_ref, gain_ref, o_ref, xm_ref)


def _compress_sample_kernel(pt_ref, *refs):
    pages = refs[:N_PAGES]
    w1_ref, bias_ref, w2t_ref, b2_ref, gain_ref, o_ref, t_ref, xm_ref = refs[N_PAGES:]
    lane = lax.broadcasted_iota(jnp.int32, (1, 128), 1)
    for c in range(2):
        for p in range(N_PAGES):
            xt = pages[p][0, c]
            for pair in range(2):
                t_ref[c, pair, p * PAGE_SIZE:(p + 1) * PAGE_SIZE, :] = xt[pair * 128:(pair + 1) * 128, :].T
        for l in range(CMP_STRIDE):
            for pair in range(2):
                rows = t_ref[c, pair, pl.ds(l, N_HALF, stride=CMP_STRIDE), :]
                for member in range(2):
                    g = 2 * pair + member
                    keep = (lane >= member * HEAD_DIM) & (lane < (member + 1) * HEAD_DIM)
                    xm_ref[c, g * N_HALF:(g + 1) * N_HALF, l * 128:(l + 1) * 128] = (
                        jnp.where(keep, rows, 0.0).astype(BF16))
        acc = _dot(xm_ref[c], w1_ref[c])
        h0 = acc[:, :CMP_HIDDEN]
        h1 = pltpu.roll(acc[:, CMP_HIDDEN:], 4 * N_HALF - 1, 0)
        hid = jax.nn.silu(h0 + h1 + bias_ref[c]).astype(BF16)
        outs = []
        for g in range(NSA_KV):
            og = _dot_nt(w2t_ref[c], hid[g * N_HALF:(g + 1) * N_HALF]) + b2_ref[c]
            if c == 0:
                og = og * lax.rsqrt(jnp.mean(og * og, axis=0, keepdims=True) + EPS) * gain_ref[...]
            outs.append(og)
        o_ref[c, 0] = jnp.concatenate(outs, axis=0)


def _cmp_weight_specs(nargs):
    cmap = (lambda c, b: (c, 0, 0)) if nargs == 2 else (lambda c, b, pt: (c, 0, 0))
    return [pl.BlockSpec((1, CMP_STRIDE * KV_WIDTH, 2 * CMP_HIDDEN), cmap),
            pl.BlockSpec((1, 1, CMP_HIDDEN), cmap),
            pl.BlockSpec((1, CMP_HIDDEN, KV_WIDTH), cmap),
            pl.BlockSpec((1, 1, KV_WIDTH), cmap),
            pl.BlockSpec((1, 1, KV_WIDTH), cmap)]


def _compress_prompt(kv3, cw):
    b = kv3.shape[0]
    return pl.pallas_call(
        _compress_prompt_kernel,
        grid=(2, b),
        in_specs=[pl.BlockSpec((1, SEQ, 128), lambda c, bi: (bi, 0, 2 * c)),
                  pl.BlockSpec((1, SEQ, 128), lambda c, bi: (bi, 0, 2 * c + 1))] + _cmp_weight_specs(2),
        out_specs=pl.BlockSpec((1, 1, N_HALF, KV_WIDTH), lambda c, bi: (c, bi, 0, 0)),
        out_shape=jax.ShapeDtypeStruct((2, b, N_HALF, KV_WIDTH), F32),
        scratch_shapes=[pltpu.VMEM((NSA_KV * N_HALF, CMP_STRIDE * KV_WIDTH), BF16)],
        compiler_params=_params(("arbitrary", "arbitrary")),
        name="compress_prompt",
    )(kv3, kv3, cw["w1rep"], cw["bias"], cw["w2rep"], cw["b2rep"], cw["gain"])


def _compress_sample(cache_t, page_flat, cw):
    b = DEC_BATCH
    page_specs = [pl.BlockSpec((1, 2, KV_WIDTH, PAGE_SIZE),
                               functools.partial(lambda bi, pt, p: (pt[bi * N_PAGES + p], 0, 0, 0), p=p))
                  for p in range(N_PAGES)]
    full = lambda shape: pl.BlockSpec(shape, lambda bi, pt: (0,) * len(shape))
    grid_spec = pltpu.PrefetchScalarGridSpec(
        num_scalar_prefetch=1,
        grid=(b,),
        in_specs=page_specs + [full((2, CMP_STRIDE * 128, 2 * CMP_HIDDEN)), full((2, 1, CMP_HIDDEN)),
                               full((2, HEAD_DIM, CMP_HIDDEN)), full((2, HEAD_DIM, N_HALF)),
                               full((HEAD_DIM, N_HALF))],
        out_specs=pl.BlockSpec((2, 1, KV_WIDTH, N_HALF), lambda bi, pt: (0, bi, 0, 0)),
        scratch_shapes=[pltpu.VMEM((2, 2, PAST_LEN, 128), F32),
                        pltpu.VMEM((2, NSA_KV * N_HALF, CMP_STRIDE * 128), BF16)],
    )
    return pl.pallas_call(
        _compress_sample_kernel,
        grid_spec=grid_spec,
        out_shape=jax.ShapeDtypeStruct((2, b, KV_WIDTH, N_HALF), F32),
        compiler_params=_params(("arbitrary",)),
        name="compress_sample",
    )(page_flat, *([cache_t] * N_PAGES), cw["w1pair"], cw["bias"], cw["w2t"], cw["b2col"], cw["gaincol"])


def _select_blocks(imp, cur, n_blocks):
    rows = imp.shape[0]
    j = lax.broadcasted_iota(jnp.int32, (rows, 1), 0)
    valid = j <= cur
    forced = (j == 0) | (j == cur) | (j == cur - 1)
    score = jnp.where(valid, jnp.where(forced, FORCED, imp), NEG)
    rank = jnp.zeros(score.shape, F32)
    for i in range(n_blocks):
        row = score[i:i + 1, :]
        ahead = (row > score) | ((row == score) & (j > i))
        rank = rank + jnp.where(ahead, 1.0, 0.0)
    return jnp.where(valid & (rank < float(SEL_TOPK)) & (j < n_blocks), 1.0, 0.0)


def _softmax_rows(s, valid):
    s = jnp.where(valid, s, NEG)
    m = jnp.max(s, axis=-1, keepdims=True)
    e = jnp.where(valid, jnp.exp(s - m), 0.0)
    return e, jnp.maximum(jnp.sum(e, axis=-1, keepdims=True), 1e-30)


def _softmax_cols(s, valid):
    s = jnp.where(valid, s, NEG)
    m = jnp.max(s, axis=0, keepdims=True)
    e = jnp.where(valid, jnp.exp(s - m), 0.0)
    return e, jnp.maximum(jnp.sum(e, axis=0, keepdims=True), 1e-30)


def _slope(g, r):
    return float(2.0 ** (-8.0 * (g * NSA_REP + r + 1) / NSA_HEADS))


NSA_TQ = 128
SEL_CHUNK_LOG2 = 9
SEL_CHUNK = 1 << SEL_CHUNK_LOG2
WIN_KEYS = WINDOW + NSA_TQ


def _nsa_prompt_kernel(q_ref, gate_ref, kc_ref, vc_ref, ks_ref, vs_ref, kw_ref, vw_ref, cover_ref, expand_ref,
                       o_ref, ksb, vsb, kwb, vwb):
    i = pl.program_id(1)
    tq = NSA_TQ
    rows = NSA_REP * tq

    @pl.when(i == 0)
    def _():
        ksb[...] = ks_ref[0].astype(BF16)
        vsb[...] = vs_ref[0].astype(BF16)
        kwb[...] = kw_ref[0].astype(BF16)
        vwb[...] = vw_ref[0].astype(BF16)

    q = q_ref[0] * SCALE
    gates = gate_ref[0]
    kcb = kc_ref[0, 0].astype(BF16)
    vcb = vc_ref[0, 0].astype(BF16)
    q0 = i * tq
    qpos_col = q0 + lax.broadcasted_iota(jnp.int32, (tq, 1), 0)
    qpos4 = jnp.concatenate([qpos_col] * NSA_REP, axis=0)
    qpos_row = q0 + lax.broadcasted_iota(jnp.int32, (1, tq), 1)
    cur_row = lax.shift_right_logical(qpos_row, 6)
    win_start = pl.multiple_of(jnp.maximum(q0 - WINDOW, 0), NSA_TQ)
    n_chunks = lax.shift_right_logical(q0 + tq + SEL_CHUNK - 1, SEL_CHUNK_LOG2)

    groups = range(NSA_KV)
    qgs = [jnp.concatenate(
        [jnp.where(_group_mask(KV_WIDTH, g), q[:, r * KV_WIDTH:(r + 1) * KV_WIDTH], 0.0) for r in range(NSA_REP)],
        axis=0).astype(BF16) for g in groups]

    def gate_col(g, branch):
        cols = [gates[:, g * 12 + r * 3 + branch:g * 12 + r * 3 + branch + 1] for r in range(NSA_REP)]
        return jnp.concatenate(cols, axis=0)

    def biased(scores, d, mask_bias, g):
        slabs = [mask_bias - _slope(g, r) * d for r in range(NSA_REP)]
        return scores + jnp.concatenate(slabs, axis=0)

    n_idx = lax.broadcasted_iota(jnp.int32, (1, N_HALF), 1)
    dist = (qpos4 - (n_idx * CMP_STRIDE + (CMP_LEN - 1))).astype(F32)
    valid = dist >= 0.0
    o_cmp, sel_qs = [], []
    for g in groups:
        slope = jnp.concatenate([jnp.full((tq, 1), _slope(g, r), F32) for r in range(NSA_REP)], axis=0)
        e, den = _softmax_rows(_dot_nt(qgs[g], kcb) - slope * dist, valid)
        pc = e / den
        o_cmp.append(_dot(pc.astype(BF16), vcb))
        pcsum = pc[0:tq]
        for r in range(1, NSA_REP):
            pcsum = pcsum + pc[r * tq:(r + 1) * tq]
        hi, lo = _split_bf16(pcsum)
        imp = _dot_nt(cover_ref[...], hi) + _dot_nt(cover_ref[...], lo)
        sel = _select_blocks(imp, cur_row, SEQ // SEL_BLK)
        sel = jnp.concatenate([sel, jnp.zeros((128 - sel.shape[0], tq), F32)], axis=0)
        sel_qs.append(sel.T.astype(BF16))

    def sel_step(kk, carry):
        ks = pl.multiple_of(kk * SEL_CHUNK, SEL_CHUNK)
        kpos = ks + lax.broadcasted_iota(jnp.int32, (1, SEL_CHUNK), 1)
        di = qpos_col - kpos
        d = di.astype(F32)
        kblk = ksb[pl.ds(ks, SEL_CHUNK), :]
        vblk = vsb[pl.ds(ks, SEL_CHUNK), :]
        new = []
        for g in groups:
            m, l, acc = carry[g]
            chosen = _dot(sel_qs[g], expand_ref[kk])
            mask_bias = jnp.where((chosen > 0.5) & (di >= 0), 0.0, NEG)
            s = biased(_dot_nt(qgs[g], kblk), d, mask_bias, g)
            m_new = jnp.maximum(m, jnp.max(s, axis=-1, keepdims=True))
            p = jnp.exp(s - m_new)
            alpha = jnp.exp(m - m_new)
            l = alpha * l + jnp.sum(p, axis=-1, keepdims=True)
            acc = alpha * acc + _dot(p.astype(BF16), vblk)
            new.append((m_new, l, acc))
        return tuple(new)

    init = tuple((jnp.full((rows, 1), NEG, F32), jnp.zeros((rows, 1), F32), jnp.zeros((rows, KV_WIDTH), F32))
                 for _ in groups)
    sel_state = lax.fori_loop(0, n_chunks, sel_step, init)

    kpos = win_start + lax.broadcasted_iota(jnp.int32, (1, WIN_KEYS), 1)
    di = qpos_col - kpos
    d = di.astype(F32)
    win_bias = jnp.where((di >= 0) & (di < WINDOW), 0.0, NEG)
    kwin = kwb[pl.ds(win_start, WIN_KEYS), :]
    vwin = vwb[pl.ds(win_start, WIN_KEYS), :]

    out = [jnp.zeros((tq, KV_WIDTH), F32) for _ in range(NSA_REP)]
    for g in groups:
        s = biased(_dot_nt(qgs[g], kwin), d, win_bias, g)
        e = jnp.exp(s - jnp.max(s, axis=-1, keepdims=True))
        o_win = _dot(e.astype(BF16), vwin) / jnp.sum(e, axis=-1, keepdims=True)
        _, l_sel, acc_sel = sel_state[g]
        o_sel = acc_sel / jnp.maximum(l_sel, 1e-30)
        mixed = gate_col(g, 0) * o_cmp[g] + gate_col(g, 1) * o_sel + gate_col(g, 2) * o_win
        gm = _group_mask(KV_WIDTH, g)
        for r in range(NSA_REP):
            out[r] = out[r] + jnp.where(gm, mixed[r * tq:(r + 1) * tq], 0.0)

    for r in range(NSA_REP):
        o_ref[0, :, r * KV_WIDTH:(r + 1) * KV_WIDTH] = out[r].astype(o_ref.dtype)


def _nsa_prompt(nq, gates, cmp_tok, kv, win, cover_t, expand):
    b, t, tq = BATCH, SEQ, NSA_TQ
    nq3 = nq.reshape(b, t, NSA_WIDTH)
    g3 = gates.reshape(b, t, 128)
    kv3 = kv.reshape(b, t, 4 * KV_WIDTH)
    win3 = win.reshape(b, t, 2 * KV_WIDTH)
    col = lambda c: pl.BlockSpec((1, t, KV_WIDTH), lambda bi, i: (bi, 0, c))
    tok = lambda c: pl.BlockSpec((1, 1, N_HALF, KV_WIDTH), lambda bi, i: (c, bi, 0, 0))
    return pl.pallas_call(
        _nsa_prompt_kernel,
        grid=(b, t // tq),
        in_specs=[pl.BlockSpec((1, tq, NSA_WIDTH), lambda bi, i: (bi, i, 0)),
                  pl.BlockSpec((1, tq, 128), lambda bi, i: (bi, i, 0)),
                  tok(0), tok(1), col(2), col(3), col(0), col(1),
                  pl.BlockSpec(cover_t.shape, lambda bi, i: (0, 0)),
                  pl.BlockSpec(expand.shape, lambda bi, i: (0, 0, 0))],
        out_specs=pl.BlockSpec((1, tq, NSA_WIDTH), lambda bi, i: (bi, i, 0)),
        out_shape=jax.ShapeDtypeStruct((b, t, NSA_WIDTH), BF16),
        scratch_shapes=[pltpu.VMEM((t, KV_WIDTH), BF16) for _ in range(4)],
        compiler_params=_params(("parallel", "arbitrary")),
        name="nsa_prompt",
    )(nq3, g3, cmp_tok, cmp_tok, kv3, kv3, win3, win3, cover_t, expand)


SAMPLE_SEL_BLOCKS = -(-(PAST_LEN + DEC_SEQ) // SEL_BLK)
SEL_ROWS = 48


def _nsa_sample_kernel(pt_ref, *refs):
    pages = refs[:N_PAGES]
    (q_ref, gate_ref, kc_ref, vc_ref, new_ref, cw_ref, wnew_ref, cover_ref, perm_ref, o_ref, wout_ref) = refs[N_PAGES:]
    t = DEC_SEQ
    nrow = NSA_REP * NSA_KV * t
    wb = cw_ref.shape[-1]

    q = q_ref[0] * SCALE
    pieces = []
    for r in range(NSA_REP):
        qr = q[:, r * KV_WIDTH:(r + 1) * KV_WIDTH]
        for g in range(NSA_KV):
            pieces.append(jnp.where(_group_mask(KV_WIDTH, g), qr, 0.0))
    qm = jnp.concatenate(pieces, axis=0).astype(BF16)

    rowi = lax.broadcasted_iota(jnp.int32, (nrow, 1), 0)
    g_col = lax.shift_right_logical(rowi, 3) & (NSA_KV - 1)
    r_col = lax.shift_right_logical(rowi, 5)
    slope = jnp.exp2(-8.0 * (g_col * NSA_REP + r_col + 1).astype(F32) / NSA_HEADS)
    qpos = PAST_LEN + (rowi & (t - 1))
    lane = lax.broadcasted_iota(jnp.int32, (1, 128), 1)
    pad_rows = lambda a: jnp.concatenate([a, jnp.zeros((128 - t, a.shape[-1]), F32)], axis=0)

    def bias_and_mask(s, kpos_row, extra_ok=None, window=False):
        d = (qpos - kpos_row).astype(F32)
        ok = d >= 0.0
        if window:
            ok = ok & (d < float(WINDOW))
        if extra_ok is not None:
            ok = ok & extra_ok
        return s - slope * d, ok

    s, ok = bias_and_mask(_dot(qm, kc_ref[0, 0].astype(BF16)), lane * CMP_STRIDE + (CMP_LEN - 1))
    e, den = _softmax_rows(s, ok)
    pc = e / den
    o_cmp = _dot_nt(pc.astype(BF16), vc_ref[0, 0].astype(BF16))
    hi, lo = _split_bf16(pc)
    imp = _dot(hi, cover_ref[...]) + _dot(lo, cover_ref[...])
    per_rep = NSA_KV * t
    imp = imp[0:per_rep] + imp[per_rep:2 * per_rep] + imp[2 * per_rep:3 * per_rep] + imp[3 * per_rep:]
    imp_t = jnp.concatenate([imp, jnp.zeros((nrow - per_rep, 128), F32)], axis=0).T
    cur = lax.shift_right_logical(PAST_LEN + (lane & (t - 1)), 6)
    sel_t = _select_blocks(imp_t[0:SEL_ROWS], cur, SAMPLE_SEL_BLOCKS)
    sel = jnp.concatenate([sel_t, jnp.zeros((128 - SEL_ROWS, 128), F32)], axis=0).T[0:per_rep]
    sel = jnp.concatenate([sel] * NSA_REP, axis=0)

    scores, oks = [], []
    for p in range(N_PAGES):
        chosen = jnp.where(lane < SEL_BLK, sel[:, 2 * p:2 * p + 1], sel[:, 2 * p + 1:2 * p + 2])
        s, ok = bias_and_mask(_dot(qm, pages[p][0, 0].astype(BF16)), p * PAGE_SIZE + lane, chosen > 0.5)
        scores.append(s)
        oks.append(ok)
    knew = pad_rows(new_ref[0, :, 0:KV_WIDTH]).astype(BF16)
    vnew = pad_rows(new_ref[0, :, KV_WIDTH:2 * KV_WIDTH]).astype(BF16)
    last = SAMPLE_SEL_BLOCKS - 1
    s, ok = bias_and_mask(_dot_nt(qm, knew), PAST_LEN + lane, (sel[:, last:last + 1] > 0.5) & (lane < t))
    scores.append(s)
    oks.append(ok)
    e, den = _softmax_rows(jnp.concatenate(scores, axis=-1), jnp.concatenate(oks, axis=-1))
    prob = (e / den).astype(BF16)
    o_sel = _dot(prob[:, PAST_LEN:], vnew)
    for p in range(N_PAGES):
        o_sel = o_sel + _dot_nt(prob[:, p * PAGE_SIZE:(p + 1) * PAGE_SIZE], pages[p][0, 1].astype(BF16))

    lane_w = lax.broadcasted_iota(jnp.int32, (1, wb), 1)
    s_old, ok_old = bias_and_mask(_dot(qm, cw_ref[0, 0].astype(BF16)), PAST_LEN - wb + lane_w, window=True)
    kwn = pad_rows(wnew_ref[0, :, 0:KV_WIDTH]).astype(BF16)
    vwn = pad_rows(wnew_ref[0, :, KV_WIDTH:2 * KV_WIDTH]).astype(BF16)
    s_new, ok_new = bias_and_mask(_dot_nt(qm, kwn), PAST_LEN + lane, lane < t, window=True)
    e, den = _softmax_rows(jnp.concatenate([s_old, s_new], axis=-1), jnp.concatenate([ok_old, ok_new], axis=-1))
    prob = (e / den).astype(BF16)
    o_win = _dot_nt(prob[:, 0:wb], cw_ref[0, 1].astype(BF16)) + _dot(prob[:, wb:], vwn)

    ghi, glo = _split_bf16(pad_rows(gate_ref[0]))
    grow = _dot(perm_ref[...], ghi) + _dot(perm_ref[...], glo)
    base = g_col * 12 + r_col * 3

    def gate_col(branch):
        return jnp.sum(jnp.where(lane == base + branch, grow, 0.0), axis=-1, keepdims=True)

    mixed = gate_col(0) * o_cmp + gate_col(1) * o_sel + gate_col(2) * o_win
    for r in range(NSA_REP):
        acc = jnp.zeros((t, KV_WIDTH), F32)
        for g in range(NSA_KV):
            lo_row = (r * NSA_KV + g) * t
            acc = acc + jnp.where(_group_mask(KV_WIDTH, g), mixed[lo_row:lo_row + t], 0.0)
        o_ref[0, :, r * KV_WIDTH:(r + 1) * KV_WIDTH] = acc.astype(o_ref.dtype)

    for c in range(2):
        new_t = pad_rows(wnew_ref[0, :, c * KV_WIDTH:(c + 1) * KV_WIDTH]).T
        tail = jnp.concatenate([jnp.zeros((KV_WIDTH, wb - 128), F32), pltpu.roll(new_t, 128 - t, 1)], axis=-1)
        wout_ref[0, c] = jnp.where(lane_w >= wb - t, tail, pltpu.roll(cw_ref[0, c], wb - t, 1))


def _nsa_sample(nq, gates, cmp_tok, cache_t, page_flat, kv_new, win_t, win_new, cover, perm):
    b, t = DEC_BATCH, DEC_SEQ
    wb = win_t.shape[-1]
    per_b = lambda shape: pl.BlockSpec(shape, lambda bi, pt: (bi,) + (0,) * (len(shape) - 1))
    page_specs = [pl.BlockSpec((1, 2, KV_WIDTH, PAGE_SIZE),
                               functools.partial(lambda bi, pt, p: (pt[bi * N_PAGES + p], 1, 0, 0), p=p))
                  for p in range(N_PAGES)]
    tok = lambda c: pl.BlockSpec((1, 1, KV_WIDTH, N_HALF), lambda bi, pt: (c, bi, 0, 0))
    grid_spec = pltpu.PrefetchScalarGridSpec(
        num_scalar_prefetch=1,
        grid=(b,),
        in_specs=page_specs + [per_b((1, t, NSA_WIDTH)), per_b((1, t, 128)), tok(0), tok(1),
                               pl.BlockSpec((1, t, 2 * KV_WIDTH), lambda bi, pt: (bi, 0, 1)),
                               per_b((1, 2, KV_WIDTH, wb)), per_b((1, t, 2 * KV_WIDTH)),
                               pl.BlockSpec(cover.shape, lambda bi, pt: (0, 0)),
                               pl.BlockSpec(perm.shape, lambda bi, pt: (0, 0))],
        out_specs=[per_b((1, t, NSA_WIDTH)), per_b((1, 2, KV_WIDTH, wb))],
    )
    return pl.pallas_call(
        _nsa_sample_kernel,
        grid_spec=grid_spec,
        out_shape=[jax.ShapeDtypeStruct((b, t, NSA_WIDTH), BF16), jax.ShapeDtypeStruct(win_t.shape, F32)],
        compiler_params=_params(("arbitrary",)),
        name="nsa_sample",
    )(page_flat, *([cache_t] * N_PAGES), nq.reshape(b, t, NSA_WIDTH), gates.reshape(b, t, 128), cmp_tok, cmp_tok,
      kv_new.reshape(b, t, 4 * KV_WIDTH), win_t, win_new.reshape(b, t, 2 * KV_WIDTH), cover, perm)


def _out_kernel(x_ref, ret_ref, nsa_ref, wr_ref, wn_ref, o_ref):
    o_ref[...] = x_ref[...] + _dot(ret_ref[...], wr_ref[...]) + _dot(nsa_ref[...], wn_ref[...])


def _mixer_out(x2d, ret, nsa, w_ret, w_nsa, tm=1024, tn=512):
    n, d = x2d.shape
    return pl.pallas_call(
        _out_kernel,
        grid=(n // tm, d // tn),
        in_specs=[pl.BlockSpec((tm, tn), lambda i, j: (i, j)),
                  pl.BlockSpec((tm, RET_WIDTH), lambda i, j: (i, 0)),
                  pl.BlockSpec((tm, NSA_WIDTH), lambda i, j: (i, 0)),
                  pl.BlockSpec((RET_WIDTH, tn), lambda i, j: (0, j)),
                  pl.BlockSpec((NSA_WIDTH, tn), lambda i, j: (0, j))],
        out_specs=pl.BlockSpec((tm, tn), lambda i, j: (i, j)),
        out_shape=jax.ShapeDtypeStruct((n, d), F32),
        compiler_params=_params(("parallel", "arbitrary")),
        name="mixer_out",
    )(x2d, ret, nsa, w_ret, w_nsa)


FFN_TN = 512
FFN_COLS = D_FF // FFN_TN


def _ffn_up_kernel(*refs, hist, step, tiles_per_seq, has_prev):
    if has_prev:
        x_ref, wa_ref, wb_ref, cwa_ref, cwb_ref, cba_ref, cbb_ref, pa_ref, pb_ref = refs[:9]
        act_ref, sta_ref, stb_ref, exta, extb = refs[9:]
    else:
        x_ref, wa_ref, wb_ref, cwa_ref, cwb_ref, cba_ref, cbb_ref = refs[:7]
        act_ref, sta_ref, stb_ref, exta, extb = refs[7:]
        pa_ref = pb_ref = None
    tm = x_ref.shape[0]
    first = (pl.program_id(1) % tiles_per_seq) == 0
    x = x_ref[...]
    halves = ((wa_ref, cwa_ref, cba_ref, pa_ref, sta_ref, exta), (wb_ref, cwb_ref, cbb_ref, pb_ref, stb_ref, extb))
    if not has_prev:
        @pl.when(first)
        def _():
            for half in halves:
                half[5][hist - 2 * step:hist, :] = jnp.zeros((2 * step, half[5].shape[-1]), F32)
    ups = [_dot(x, half[0][...]) for half in halves]
    outs = []
    for (w_ref, cw_ref, cb_ref, p_ref, st_ref, ext), u in zip(halves, ups):
        if has_prev:
            ext[hist - 2 * step:hist, :] = p_ref[...]
        ext[hist:hist + tm, :] = u
        cw = cw_ref[...]
        c = (cb_ref[...] + cw[2:3] * u + cw[1:2] * ext[hist - step:hist - step + tm, :]
             + cw[0:1] * ext[hist - 2 * step:hist - 2 * step + tm, :])
        outs.append(c)
        tail = ext[hist + tm - 2 * step:hist + tm, :]
        if has_prev:
            st_ref[...] = tail
        else:
            st_ref[0] = tail
            ext[hist - 2 * step:hist, :] = tail
    act_ref[...] = (jax.nn.silu(outs[0]) * outs[1]).astype(act_ref.dtype)


def _ffn_up(hn, w_up, conv_w, conv_b, prev, *, tm, step, seqs, tiles_per_seq):
    n, d = hn.shape
    tn = FFN_TN
    hist = 8 if step == 1 else 2 * step
    has_prev = prev is not None
    half = lambda off: (lambda j, i: (0, j + off))
    in_specs = [pl.BlockSpec((tm, d), lambda j, i: (i, 0)),
                pl.BlockSpec((d, tn), half(0)), pl.BlockSpec((d, tn), half(FFN_COLS)),
                pl.BlockSpec((CONV_W, tn), half(0)), pl.BlockSpec((CONV_W, tn), half(FFN_COLS)),
                pl.BlockSpec((1, tn), half(0)), pl.BlockSpec((1, tn), half(FFN_COLS))]
    args = [hn, w_up, w_up, conv_w, conv_w, conv_b, conv_b]
    if has_prev:
        in_specs += [pl.BlockSpec((2 * step, tn), half(0)), pl.BlockSpec((2 * step, tn), half(FFN_COLS))]
        args += [prev, prev]
        st_spec = pl.BlockSpec((2 * step, tn), lambda j, i: (0, j))
        st_shape = jax.ShapeDtypeStruct((2 * step, D_FF), F32)
    else:
        st_spec = pl.BlockSpec((1, 2, tn), lambda j, i: (i // tiles_per_seq, 0, j))
        st_shape = jax.ShapeDtypeStruct((seqs, 2, D_FF), F32)
    kernel = functools.partial(_ffn_up_kernel, hist=hist, step=step, tiles_per_seq=tiles_per_seq, has_prev=has_prev)
    return pl.pallas_call(
        kernel,
        grid=(FFN_COLS, n // tm),
        in_specs=in_specs,
        out_specs=[pl.BlockSpec((tm, tn), lambda j, i: (i, j)), st_spec, st_spec],
        out_shape=[jax.ShapeDtypeStruct((n, D_FF), BF16), st_shape, st_shape],
        scratch_shapes=[pltpu.VMEM((hist + tm, tn), F32), pltpu.VMEM((hist + tm, tn), F32)],
        compiler_params=_params(("arbitrary", "arbitrary")),
        name="ffn_up",
    )(*args)


def _ffn_down_kernel(h_ref, a_ref, w_ref, o_ref):
    o_ref[...] = h_ref[...] + _dot(a_ref[...], w_ref[...])


def _ffn_down(h, act, w_down, tm=512, tn=512):
    n, d = h.shape
    return pl.pallas_call(
        _ffn_down_kernel,
        grid=(n // tm, d // tn),
        in_specs=[pl.BlockSpec((tm, tn), lambda i, j: (i, j)),
                  pl.BlockSpec((tm, D_FF), lambda i, j: (i, 0)),
                  pl.BlockSpec((D_FF, tn), lambda i, j: (0, j))],
        out_specs=pl.BlockSpec((tm, tn), lambda i, j: (i, j)),
        out_shape=jax.ShapeDtypeStruct((n, d), F32),
        compiler_params=_params(("parallel", "arbitrary")),
        name="ffn_down",
    )(h, act, w_down)


def _block_cover_t(n_sel, rows):
    cs = np.arange(N_HALF - 1)[None, :] * CMP_STRIDE
    js = np.arange(n_sel)[:, None] * SEL_BLK
    cov = np.clip(np.minimum(cs + CMP_LEN, js + SEL_BLK) - np.maximum(cs, js), 0, None) / CMP_LEN
    out = np.zeros((rows, N_HALF), np.float32)
    out[:n_sel, :N_HALF - 1] = cov
    return jnp.asarray(out, dtype=BF16)


def _prepare_weights(g_attn, w_in, q_norm, k_norm_cmp, k_norm_slc, k_norm_win, cmp_pe, cmp_w1, cmp_b1, cmp_w2,
                     cmp_b2, w_out):
    d = D_MODEL
    o = 0
    w_rqk = w_in[:, o:o + 2 * RET_WIDTH]; o += 2 * RET_WIDTH
    w_rvg = w_in[:, o:o + 2 * RET_WIDTH]; o += 2 * RET_WIDTH
    w_nq = w_in[:, o:o + NSA_WIDTH]; o += NSA_WIDTH
    w_kv = w_in[:, o:o + 4 * KV_WIDTH]; o += 4 * KV_WIDTH
    w_win = w_in[:, o:o + 2 * KV_WIDTH]; o += 2 * KV_WIDTH
    w_ng = w_in[:, o:]
    w_nq = w_nq.reshape(d, NSA_KV, NSA_REP, HEAD_DIM).transpose(0, 2, 1, 3).reshape(d, NSA_WIDTH)
    w_ng = jnp.pad(w_ng, ((0, 0), (0, 128 - w_ng.shape[1])))
    tile4 = lambda v: jnp.tile(v, NSA_KV)
    zeros = jnp.zeros((KV_WIDTH,), F32)
    ones = jnp.ones((KV_WIDTH,), F32)
    wts = dict(
        g_attn=g_attn,
        w_rqk=w_rqk.astype(BF16), w_rvg=w_rvg.astype(BF16), w_nq=w_nq.astype(BF16),
        w_kv=w_kv.astype(BF16), w_win=w_win.astype(BF16), w_ng=w_ng.astype(BF16),
        rot_scale=jnp.concatenate([jnp.ones((RET_WIDTH,), F32),
                                   jnp.full((RET_WIDTH,), RET_DK ** -0.5, F32)]).reshape(1, -1),
        nq_gain=jnp.tile(q_norm, NSA_HEADS).reshape(1, -1), nq_mask=jnp.ones((1, NSA_WIDTH), F32),
        kv_gain=jnp.concatenate([zeros, zeros, tile4(k_norm_slc), zeros]).reshape(1, -1),
        kv_mask=jnp.concatenate([zeros, zeros, ones, zeros]).reshape(1, -1),
        win_gain=jnp.concatenate([tile4(k_norm_win), zeros]).reshape(1, -1),
        win_mask=jnp.concatenate([ones, zeros]).reshape(1, -1),
    )
    w1 = cmp_w1.reshape(2, 2, CMP_STRIDE, HEAD_DIM, CMP_HIDDEN)
    w1rep = jnp.broadcast_to(w1.transpose(0, 2, 3, 1, 4)[:, :, None],
                             (2, CMP_STRIDE, NSA_KV, HEAD_DIM, 2, CMP_HIDDEN))
    w1rep = w1rep.reshape(2, CMP_STRIDE * KV_WIDTH, 2 * CMP_HIDDEN).astype(BF16)
    w1flat = w1.reshape(2, 2, CMP_STRIDE * HEAD_DIM, CMP_HIDDEN).astype(BF16)
    pe8 = jnp.pad(cmp_pe.reshape(2, 2, 1, CMP_STRIDE * HEAD_DIM), ((0, 0), (0, 0), (0, 15), (0, 0)))
    bias = _cmp_bias(pe8, w1flat, cmp_b1.reshape(2, 1, CMP_HIDDEN))
    w1pair = jnp.broadcast_to(w1.transpose(0, 2, 3, 1, 4)[:, :, None], (2, CMP_STRIDE, 2, HEAD_DIM, 2, CMP_HIDDEN))
    col = lambda v: jnp.broadcast_to(v[:, :, None], (2, HEAD_DIM, N_HALF))
    cw = dict(
        w1pair=w1pair.reshape(2, CMP_STRIDE * 128, 2 * CMP_HIDDEN).astype(BF16),
        w2t=cmp_w2.transpose(0, 2, 1).astype(BF16), b2col=col(cmp_b2),
        gaincol=jnp.broadcast_to(k_norm_cmp[:, None], (HEAD_DIM, N_HALF)),
        w1rep=w1rep, bias=bias,
        w2rep=jnp.tile(cmp_w2, (1, 1, NSA_KV)).astype(BF16),
        b2rep=jnp.tile(cmp_b2, (1, NSA_KV)).reshape(2, 1, KV_WIDTH),
        gain=jnp.stack([tile4(k_norm_cmp), ones]).reshape(2, 1, KV_WIDTH),
    )
    w_ret = w_out[:RET_WIDTH].astype(BF16)
    w_nsa = w_out[RET_WIDTH:].reshape(NSA_KV, NSA_REP, HEAD_DIM, d).transpose(1, 0, 2, 3).reshape(NSA_WIDTH, d)
    return wts, cw, w_ret, w_nsa.astype(BF16)


def kernel(x_prompt, x_sample, cache_kv, cache_win, state_ret, state_conv, page_table, g_attn, w_in, q_norm,
           k_norm_cmp, k_norm_slc, k_norm_win, cmp_pe, cmp_w1, cmp_b1, cmp_w2, cmp_b2, ret_gn, w_out, g_ffn, w_up,
           conv_w, conv_b, w_down):
    assert x_prompt.shape == (BATCH, SEQ, D_MODEL) and x_sample.shape == (DEC_BATCH, DEC_SEQ, D_MODEL)
    assert g_attn.shape[0] == 1, "single layer"
    wts, cw, w_ret, w_nsa = _prepare_weights(g_attn[0], w_in[0], q_norm[0], k_norm_cmp[0], k_norm_slc[0],
                                              k_norm_win[0], cmp_pe[0], cmp_w1[0], cmp_b1[0], cmp_w2[0],
                                              cmp_b2[0], w_out[0])
    w_up_b = w_up[0].astype(BF16)
    w_down_b = w_down[0].astype(BF16)
    cb = conv_b[0].reshape(1, -1)
    n_p, n_s = BATCH * SEQ, DEC_BATCH * DEC_SEQ
    xp = x_prompt.reshape(n_p, D_MODEL)
    xs = x_sample.reshape(n_s, D_MODEL)

    pos_p = jnp.tile(jnp.arange(SEQ, dtype=jnp.int32), BATCH)
    qk, vg, nq, kv_p, win_p, gates = _project_all(xp, pos_p, wts)
    ret_mix, ret_state_p = _retention_prompt(qk, vg, ret_gn[0])
    kv3 = kv_p.reshape(BATCH, SEQ, 4 * KV_WIDTH)
    cmp_tok = _compress_prompt(kv3, cw)
    cover_p = _block_cover_t(SEQ // SEL_BLK, SEQ // SEL_BLK)
    key = np.arange(SEQ).reshape(SEQ // SEL_CHUNK, 1, SEL_CHUNK)
    expand = jnp.asarray((key // SEL_BLK) == np.arange(128)[None, :, None], dtype=BF16)
    nsa = _nsa_prompt(nq, gates, cmp_tok, kv_p, win_p, cover_p, expand)
    h_p = _mixer_out(xp, ret_mix.reshape(n_p, RET_WIDTH), nsa.reshape(n_p, NSA_WIDTH), w_ret, w_nsa)
    hn_p = _rmsnorm(h_p, g_ffn[0])
    act_p, st_a, st_b = _ffn_up(hn_p, w_up_b, conv_w[0], cb, None, tm=1024, step=1, seqs=BATCH,
                                tiles_per_seq=SEQ // 1024)
    y_p = _ffn_down(h_p, act_p, w_down_b)
    conv_p = jnp.concatenate([st_a, st_b], axis=-1)

    pos_s = jnp.tile(PAST_LEN + jnp.arange(DEC_SEQ, dtype=jnp.int32), DEC_BATCH)
    qk, vg, nq, kv_s, win_s, gates = _project_all(xs, pos_s, wts)
    ret_mix, ret_state_s = _retention_sample(qk, vg, state_ret[0], ret_gn[0])
    cache_t = cache_kv[0].transpose(0, 2, 3, 4, 1).reshape(cache_kv.shape[1], 4, KV_WIDTH, PAGE_SIZE)
    page_flat = page_table.reshape(-1).astype(jnp.int32)
    cmp_tok = _compress_sample(cache_t, page_flat, cw)
    wb = cache_win.shape[2]
    win_t = cache_win[0].transpose(0, 2, 3, 4, 1).reshape(DEC_BATCH, 2, KV_WIDTH, wb)
    cover_s = _block_cover_t(SAMPLE_SEL_BLOCKS, 128).T
    lanes = np.arange(128)
    perm = jnp.asarray((lanes[:, None] % DEC_SEQ) == lanes[None, :], dtype=BF16)
    nsa, win_out_t = _nsa_sample(nq, gates, cmp_tok, cache_t, page_flat, kv_s, win_t, win_s, cover_s, perm)
    h_s = _mixer_out(xs, ret_mix.reshape(n_s, RET_WIDTH), nsa.reshape(n_s, NSA_WIDTH), w_ret, w_nsa)
    to_tm = lambda a: a.reshape(DEC_BATCH, -1, a.shape[-1]).transpose(1, 0, 2).reshape(-1, a.shape[-1])
    from_tm = lambda a, t: a.reshape(t, DEC_BATCH, a.shape[-1]).transpose(1, 0, 2)
    h_tm = to_tm(h_s)
    hn_s = _rmsnorm(h_tm, g_ffn[0])
    act_s, st_a, st_b = _ffn_up(hn_s, w_up_b, conv_w[0], cb, to_tm(state_conv[0]), tm=n_s, step=DEC_BATCH,
                                seqs=DEC_BATCH, tiles_per_seq=1)
    y_s = from_tm(_ffn_down(h_tm, act_s, w_down_b), DEC_SEQ)
    conv_s = from_tm(jnp.concatenate([st_a, st_b], axis=-1), CONV_W - 1)

    win_sample = win_out_t.reshape(DEC_BATCH, 2, NSA_KV, HEAD_DIM, wb).transpose(0, 4, 1, 2, 3)
    return (
        y_p.reshape(BATCH, SEQ, D_MODEL),
        y_s,
        kv_p.reshape(1, BATCH, SEQ, 4, NSA_KV, HEAD_DIM),
        kv_s.reshape(1, DEC_BATCH, DEC_SEQ, 4, NSA_KV, HEAD_DIM),
        win_p.reshape(BATCH, SEQ, 2 * KV_WIDTH)[:, SEQ - WINDOW:].reshape(1, BATCH, WINDOW, 2, NSA_KV, HEAD_DIM),
        win_sample[None],
        ret_state_p[None],
        ret_state_s[None],
        conv_p[None],
        conv_s[None],
    )
```

```python
import functools

import numpy as np
import jax
import jax.numpy as jnp
from jax import lax
from jax.experimental import pallas as pl
from jax.experimental.pallas import tpu as pltpu

D_MODEL = 2048
BATCH = 4
SEQ = 2048
DEC_BATCH = 128
DEC_SEQ = 8
PAST_LEN = 2048
PAGE_SIZE = 128
N_PAGES = PAST_LEN // PAGE_SIZE
RET_HEADS = 4
RET_DK = 256
RET_DV = 256
RET_WIDTH = RET_HEADS * RET_DV
RET_CHUNK = 128
NSA_HEADS = 16
NSA_KV = 4
NSA_REP = NSA_HEADS // NSA_KV
HEAD_DIM = 64
NSA_WIDTH = NSA_HEADS * HEAD_DIM
KV_WIDTH = NSA_KV * HEAD_DIM
CMP_LEN = 32
CMP_STRIDE = 16
CMP_HIDDEN = 256
SEL_BLK = 64
SEL_TOPK = 16
WINDOW = 512
SCALE = HEAD_DIM ** -0.5
D_FF = 5632
CONV_W = 3
EPS = 1e-6

N_HALF = PAST_LEN // CMP_STRIDE
NEG = -1e30
FORCED = 1e30
VMEM_LIMIT = 56 * 1024 * 1024

BF16 = jnp.bfloat16
F32 = jnp.float32


def _params(semantics, vmem=VMEM_LIMIT):
    return pltpu.CompilerParams(dimension_semantics=semantics, vmem_limit_bytes=vmem)


def _dot(a, b):
    return jnp.dot(a, b, preferred_element_type=F32)


def _dot_nt(a, b):
    return lax.dot_general(a, b, (((1,), (1,)), ((), ())), preferred_element_type=F32)


def _split_bf16(x):
    hi = x.astype(BF16)
    lo = (x - hi.astype(F32)).astype(BF16)
    return hi, lo


def _group_mask(width, g):
    lane = lax.broadcasted_iota(jnp.int32, (1, width), 1)
    return (lane >= g * HEAD_DIM) & (lane < (g + 1) * HEAD_DIM)


def _rms_groups64(z, gain):
    width = z.shape[-1]
    outs = []
    for k in range(width // 128):
        zk = z[:, k * 128:(k + 1) * 128]
        zz = zk * zk
        lane = lax.broadcasted_iota(jnp.int32, (1, 128), 1)
        lo_half = lane < HEAD_DIM
        s_lo = jnp.sum(jnp.where(lo_half, zz, 0.0), axis=-1, keepdims=True)
        s_hi = jnp.sum(jnp.where(lo_half, 0.0, zz), axis=-1, keepdims=True)
        ms = jnp.where(lo_half, s_lo, s_hi) * (1.0 / HEAD_DIM)
        outs.append(zk * lax.rsqrt(ms + EPS))
    y = outs[0] if len(outs) == 1 else jnp.concatenate(outs, axis=-1)
    return y * gain


def _rmsnorm_kernel(x_ref, g_ref, o_ref):
    x = x_ref[...]
    ms = jnp.mean(x * x, axis=-1, keepdims=True)
    o_ref[...] = (x * lax.rsqrt(ms + EPS) * g_ref[...]).astype(o_ref.dtype)


def _rmsnorm(x, g, tm=512):
    n, d = x.shape
    return pl.pallas_call(
        _rmsnorm_kernel,
        grid=(n // tm,),
        in_specs=[pl.BlockSpec((tm, d), lambda i: (i, 0)), pl.BlockSpec((1, d), lambda i: (0, 0))],
        out_specs=pl.BlockSpec((tm, d), lambda i: (i, 0)),
        out_shape=jax.ShapeDtypeStruct((n, d), BF16),
        compiler_params=_params(("parallel",)),
        name="rmsnorm",
    )(x, g.reshape(1, d))


def _proj_plain_kernel(x_ref, w_ref, o_ref):
    o_ref[...] = _dot(x_ref[...], w_ref[...]).astype(o_ref.dtype)


def _proj_sigmoid_kernel(x_ref, w_ref, o_ref):
    o_ref[...] = jax.nn.sigmoid(_dot(x_ref[...], w_ref[...]))


def _sub_dots(x_ref, w_ref, width):
    x = x_ref[...]
    return [_dot(x, w_ref[:, k * width:(k + 1) * width]) for k in range(w_ref.shape[-1] // width)]


def _proj_norm_kernel(x_ref, w_ref, gain_ref, nmask_ref, o_ref):
    for k, z in enumerate(_sub_dots(x_ref, w_ref, KV_WIDTH)):
        cols = slice(k * KV_WIDTH, (k + 1) * KV_WIDTH)
        o_ref[:, cols] = jnp.where(nmask_ref[:, cols] > 0.5, _rms_groups64(z, gain_ref[:, cols]), z)


def _proj_rot_kernel(x_ref, w_ref, cos_ref, sin_ref, scale_ref, o_ref):
    c = cos_ref[...]
    s = sin_ref[...]
    for hh, z in enumerate(_sub_dots(x_ref, w_ref, RET_DK)):
        lo, mid, hi = hh * RET_DK, hh * RET_DK + 128, (hh + 1) * RET_DK
        x1 = z[:, :128]
        x2 = z[:, 128:]
        o_ref[:, lo:mid] = (x1 * c - x2 * s) * scale_ref[:, lo:mid]
        o_ref[:, mid:hi] = (x2 * c + x1 * s) * scale_ref[:, mid:hi]


PROJ_TN = 1024


def _proj(xn, w, kernel, row_extras=(), col_extras=(), out_dtype=F32, tm=1024, name="proj"):
    n, d = xn.shape
    c = w.shape[1]
    tn = min(PROJ_TN, c)
    extra_specs = ([pl.BlockSpec((tm, 128), lambda i, j: (i, 0)) for _ in row_extras]
                   + [pl.BlockSpec((1, tn), lambda i, j: (0, j)) for _ in col_extras])
    return pl.pallas_call(
        kernel,
        grid=(n // tm, c // tn),
        in_specs=[pl.BlockSpec((tm, d), lambda i, j: (i, 0)), pl.BlockSpec((d, tn), lambda i, j: (0, j))]
        + extra_specs,
        out_specs=pl.BlockSpec((tm, tn), lambda i, j: (i, j)),
        out_shape=jax.ShapeDtypeStruct((n, c), out_dtype),
        compiler_params=_params(("parallel", "arbitrary")),
        name=name,
    )(xn, w, *row_extras, *col_extras)


def _project_all(x2d, pos, wts, tm=1024):
    n = x2d.shape[0]
    xn = _rmsnorm(x2d, wts["g_attn"])
    half = RET_DK // 2
    inv = 1.0 / (10000.0 ** jnp.linspace(0.0, 1.0, half, dtype=F32))
    ang = pos.astype(F32)[:, None] * inv[None, :]
    cos = jnp.cos(ang)
    sin = jnp.sin(ang)
    qk = _proj(xn, wts["w_rqk"], _proj_rot_kernel, (cos, sin), (wts["rot_scale"],), tm=tm, name="proj_rot")
    vg = _proj(xn, wts["w_rvg"], _proj_plain_kernel, tm=tm, name="proj_vg")
    nq = _proj(xn, wts["w_nq"], _proj_norm_kernel, (), (wts["nq_gain"], wts["nq_mask"]), tm=tm, name="proj_nq")
    kv = _proj(xn, wts["w_kv"], _proj_norm_kernel, (), (wts["kv_gain"], wts["kv_mask"]), tm=tm, name="proj_kv")
    win = _proj(xn, wts["w_win"], _proj_norm_kernel, (), (wts["win_gain"], wts["win_mask"]), tm=tm, name="proj_win")
    gates = _proj(xn, wts["w_ng"], _proj_sigmoid_kernel, tm=tm, name="proj_gate")
    return qk, vg, nq, kv, win, gates


def _retention_step(q, k, v, state, dmask, xi, zeta, gch):
    qb = q.astype(BF16)
    kb = k.astype(BF16)
    vb = v.astype(BF16)
    s = _dot_nt(qb, kb) * dmask
    o = _dot(s.astype(BF16), vb) + _dot(qb, state.astype(BF16)) * xi
    kz = (k * zeta).astype(BF16)
    new_state = state * gch + _dot(kz.T, vb)
    return o, new_state


def _ret_mix(o, gn, gate):
    ms = jnp.mean(o * o, axis=-1, keepdims=True)
    return o * lax.rsqrt(ms + EPS) * gn * jax.nn.silu(gate)


def _ret_prompt_kernel(qk_ref, vg_ref, dmask_ref, xi_ref, zeta_ref, gch_ref, gn_ref, mix_ref, state_ref):
    @pl.when(pl.program_id(1) == 0)
    def _():
        state_ref[...] = jnp.zeros_like(state_ref)

    for h in range(RET_HEADS):
        lo, hi = h * RET_DK, (h + 1) * RET_DK
        o, new_state = _retention_step(qk_ref[0, :, lo:hi], qk_ref[0, :, RET_WIDTH + lo:RET_WIDTH + hi],
                                       vg_ref[0, :, lo:hi], state_ref[0, h], dmask_ref[h], xi_ref[h],
                                       zeta_ref[h], gch_ref[h])
        state_ref[0, h] = new_state
        mix_ref[0, :, lo:hi] = _ret_mix(o, gn_ref[h], vg_ref[0, :, RET_WIDTH + lo:RET_WIDTH + hi]).astype(mix_ref.dtype)


def _ret_tables(chunk):
    h = jnp.arange(RET_HEADS, dtype=F32)
    lg = jnp.log(1.0 - jnp.exp2(-5.0 - h))
    i = jnp.arange(chunk, dtype=F32)
    diff = i[:, None] - i[None, :]
    dmask = jnp.where(diff >= 0, jnp.exp(lg[:, None, None] * jnp.maximum(diff, 0.0)), 0.0)
    xi = jnp.exp(lg[:, None] * (i[None, :] + 1.0))
    zeta = jnp.exp(lg[:, None] * (chunk - 1.0 - i[None, :]))
    gch = jnp.exp(lg * chunk)
    bc = lambda a: jnp.broadcast_to(a[:, :, None], (RET_HEADS, chunk, RET_DV))
    return dmask, bc(xi), bc(zeta), jnp.broadcast_to(gch[:, None, None], (RET_HEADS, 1, RET_DV))


def _retention_prompt(qk, vg, gn):
    b, t = BATCH, SEQ
    c = RET_CHUNK
    qk3 = qk.reshape(b, t, 2 * RET_WIDTH)
    vg3 = vg.reshape(b, t, 2 * RET_WIDTH)
    dmask, xi, zeta, gch = _ret_tables(c)
    rows = pl.BlockSpec((1, c, 2 * RET_WIDTH), lambda bi, ci: (bi, ci, 0))
    full = lambda shape: pl.BlockSpec(shape, lambda bi, ci: (0,) * len(shape))
    return pl.pallas_call(
        _ret_prompt_kernel,
        grid=(b, t // c),
        in_specs=[rows, rows, full((RET_HEADS, c, c)), full((RET_HEADS, c, RET_DV)), full((RET_HEADS, c, RET_DV)),
                  full((RET_HEADS, 1, RET_DV)), full((RET_HEADS, 1, RET_DV))],
        out_specs=[pl.BlockSpec((1, c, RET_WIDTH), lambda bi, ci: (bi, ci, 0)),
                   pl.BlockSpec((1, RET_HEADS, RET_DK, RET_DV), lambda bi, ci: (bi, 0, 0, 0))],
        out_shape=[jax.ShapeDtypeStruct((b, t, RET_WIDTH), BF16),
                   jax.ShapeDtypeStruct((b, RET_HEADS, RET_DK, RET_DV), F32)],
        compiler_params=_params(("parallel", "arbitrary")),
        name="retention_prompt",
    )(qk3, vg3, dmask, xi, zeta, gch, gn.reshape(RET_HEADS, 1, RET_DV))


def _ret_sample_kernel(qk_ref, vg_ref, s0_ref, dmask_ref, xi_ref, zeta_ref, gch_ref, gn_ref, mix_ref, state_ref):
    t = qk_ref.shape[1]
    pad = lambda a: jnp.concatenate([a, jnp.zeros((RET_PAD - t, a.shape[-1]), F32)], axis=0)
    for s in range(qk_ref.shape[0]):
        for h in range(RET_HEADS):
            lo, hi = h * RET_DK, (h + 1) * RET_DK
            gate = vg_ref[s, :, RET_WIDTH + lo:RET_WIDTH + hi]
            o, new_state = _retention_step(pad(qk_ref[s, :, lo:hi]), pad(qk_ref[s, :, RET_WIDTH + lo:RET_WIDTH + hi]),
                                           pad(vg_ref[s, :, lo:hi]), s0_ref[s, h], dmask_ref[h], xi_ref[h],
                                           zeta_ref[h], gch_ref[h])
            state_ref[s, h] = new_state
            mix_ref[s, :, lo:hi] = _ret_mix(o[0:t], gn_ref[h], gate).astype(mix_ref.dtype)


RET_PAD = 128
RET_SEQS_PER_STEP = 2


def _retention_sample(qk, vg, state0, gn):
    b, t = DEC_BATCH, DEC_SEQ
    dmask, xi, zeta, gch = _ret_tables(t)
    dmask = jnp.pad(dmask, ((0, 0), (0, RET_PAD - t), (0, RET_PAD - t)))
    xi = jnp.pad(xi, ((0, 0), (0, RET_PAD - t), (0, 0)))
    zeta = jnp.pad(zeta, ((0, 0), (0, RET_PAD - t), (0, 0)))
    full = lambda shape: pl.BlockSpec(shape, lambda bi: (0,) * len(shape))
    return pl.pallas_call(
        _ret_sample_kernel,
        grid=(b // RET_SEQS_PER_STEP,),
        in_specs=[pl.BlockSpec((RET_SEQS_PER_STEP, t, 2 * RET_WIDTH), lambda bi: (bi, 0, 0)),
                  pl.BlockSpec((RET_SEQS_PER_STEP, t, 2 * RET_WIDTH), lambda bi: (bi, 0, 0)),
                  pl.BlockSpec((RET_SEQS_PER_STEP, RET_HEADS, RET_DK, RET_DV), lambda bi: (bi, 0, 0, 0)),
                  full((RET_HEADS, RET_PAD, RET_PAD)), full((RET_HEADS, RET_PAD, RET_DV)), full((RET_HEADS, RET_PAD, RET_DV)),
                  full((RET_HEADS, 1, RET_DV)), full((RET_HEADS, 1, RET_DV))],
        out_specs=[pl.BlockSpec((RET_SEQS_PER_STEP, t, RET_WIDTH), lambda bi: (bi, 0, 0)),
                   pl.BlockSpec((RET_SEQS_PER_STEP, RET_HEADS, RET_DK, RET_DV), lambda bi: (bi, 0, 0, 0))],
        out_shape=[jax.ShapeDtypeStruct((b, t, RET_WIDTH), BF16),
                   jax.ShapeDtypeStruct((b, RET_HEADS, RET_DK, RET_DV), F32)],
        compiler_params=_params(("parallel",)),
        name="retention_sample",
    )(qk.reshape(b, t, -1), vg.reshape(b, t, -1), state0, dmask, xi, zeta, gch, gn.reshape(RET_HEADS, 1, RET_DV))


def _cmp_bias_kernel(pe_ref, w_ref, b1_ref, o_ref):
    acc = b1_ref[0]
    for j in range(2):
        acc = acc + _dot(pe_ref[0, j].astype(BF16), w_ref[0, j])[0:1]
    o_ref[0] = acc


def _cmp_bias(pe8, w1flat, b1):
    return pl.pallas_call(
        _cmp_bias_kernel,
        grid=(2,),
        in_specs=[pl.BlockSpec((1, 2, 16, CMP_STRIDE * HEAD_DIM), lambda c: (c, 0, 0, 0)),
                  pl.BlockSpec((1, 2, CMP_STRIDE * HEAD_DIM, CMP_HIDDEN), lambda c: (c, 0, 0, 0)),
                  pl.BlockSpec((1, 1, CMP_HIDDEN), lambda c: (c, 0, 0))],
        out_specs=pl.BlockSpec((1, 1, CMP_HIDDEN), lambda c: (c, 0, 0)),
        out_shape=jax.ShapeDtypeStruct((2, 1, CMP_HIDDEN), F32),
        compiler_params=_params(("parallel",)),
        name="cmp_bias",
    )(pe8, w1flat, b1)


def _compress_body(load_rows, w1_ref, bias_ref, w2_ref, b2_ref, gain_ref, o_ref, xm_ref):
    for l in range(CMP_STRIDE):
        xl = load_rows(l)
        for g in range(NSA_KV):
            xm_ref[g * N_HALF:(g + 1) * N_HALF, l * KV_WIDTH:(l + 1) * KV_WIDTH] = jnp.where(
                _group_mask(KV_WIDTH, g), xl, 0.0).astype(BF16)
    acc = _dot(xm_ref[...], w1_ref[0])
    h0 = acc[:, :CMP_HIDDEN]
    h1 = pltpu.roll(acc[:, CMP_HIDDEN:], 4 * N_HALF - 1, 0)
    hid = jax.nn.silu(h0 + h1 + bias_ref[0])
    out = _dot(hid.astype(BF16), w2_ref[0]) + b2_ref[0]
    res = jnp.zeros((N_HALF, KV_WIDTH), F32)
    for g in range(NSA_KV):
        res = res + jnp.where(_group_mask(KV_WIDTH, g), out[g * N_HALF:(g + 1) * N_HALF], 0.0)
    normed = _rms_groups64(res, gain_ref[0])
    o_ref[0, 0] = jnp.where(pl.program_id(0) == 0, normed, res)


def _compress_prompt_kernel(lo_ref, hi_ref, w1_ref, bias_ref, w2_ref, b2_ref, gain_ref, o_ref, xm_ref):
    load = lambda l: jnp.concatenate(
        [r[0, pl.ds(l, N_HALF, stride=CMP_STRIDE), :] for r in (lo_ref, hi_ref)], axis=-1)
    _compress_body(load, w1_ref, bias_ref, w2_ref, b2_ref, gain_ref, o_ref, xm_ref)


def _compress_sample_kernel(pt_ref, *refs):
    pages = refs[:N_PAGES]
    w1_ref, bias_ref, w2t_ref, b2_ref, gain_ref, o_ref, t_ref, xm_ref = refs[N_PAGES:]
    lane = lax.broadcasted_iota(jnp.int32, (1, 128), 1)
    for c in range(2):
        for p in range(N_PAGES):
            xt = pages[p][0, c]
            for pair in range(2):
                t_ref[c, pair, p * PAGE_SIZE:(p + 1) * PAGE_SIZE, :] = xt[pair * 128:(pair + 1) * 128, :].T
        for l in range(CMP_STRIDE):
            for pair in range(2):
                rows = t_ref[c, pair, pl.ds(l, N_HALF, stride=CMP_STRIDE), :]
                for member in range(2):
                    g = 2 * pair + member
                    keep = (lane >= member * HEAD_DIM) & (lane < (member + 1) * HEAD_DIM)
                    xm_ref[c, g * N_HALF:(g + 1) * N_HALF, l * 128:(l + 1) * 128] = (
                        jnp.where(keep, rows, 0.0).astype(BF16))
        acc = _dot(xm_ref[c], w1_ref[c])
        h0 = acc[:, :CMP_HIDDEN]
        h1 = pltpu.roll(acc[:, CMP_HIDDEN:], 4 * N_HALF - 1, 0)
        hid = jax.nn.silu(h0 + h1 + bias_ref[c]).astype(BF16)
        outs = []
        for g in range(NSA_KV):
            og = _dot_nt(w2t_ref[c], hid[g * N_HALF:(g + 1) * N_HALF]) + b2_ref[c]
            if c == 0:
                og = og * lax.rsqrt(jnp.mean(og * og, axis=0, keepdims=True) + EPS) * gain_ref[...]
            outs.append(og)
        o_ref[c, 0] = jnp.concatenate(outs, axis=0)


def _cmp_weight_specs(nargs):
    cmap = (lambda c, b: (c, 0, 0)) if nargs == 2 else (lambda c, b, pt: (c, 0, 0))
    return [pl.BlockSpec((1, CMP_STRIDE * KV_WIDTH, 2 * CMP_HIDDEN), cmap),
            pl.BlockSpec((1, 1, CMP_HIDDEN), cmap),
            pl.BlockSpec((1, CMP_HIDDEN, KV_WIDTH), cmap),
            pl.BlockSpec((1, 1, KV_WIDTH), cmap),
            pl.BlockSpec((1, 1, KV_WIDTH), cmap)]


def _compress_prompt(kv3, cw):
    b = kv3.shape[0]
    return pl.pallas_call(
        _compress_prompt_kernel,
        grid=(2, b),
        in_specs=[pl.BlockSpec((1, SEQ, 128), lambda c, bi: (bi, 0, 2 * c)),
                  pl.BlockSpec((1, SEQ, 128), lambda c, bi: (bi, 0, 2 * c + 1))] + _cmp_weight_specs(2),
        out_specs=pl.BlockSpec((1, 1, N_HALF, KV_WIDTH), lambda c, bi: (c, bi, 0, 0)),
        out_shape=jax.ShapeDtypeStruct((2, b, N_HALF, KV_WIDTH), F32),
        scratch_shapes=[pltpu.VMEM((NSA_KV * N_HALF, CMP_STRIDE * KV_WIDTH), BF16)],
        compiler_params=_params(("arbitrary", "arbitrary")),
        name="compress_prompt",
    )(kv3, kv3, cw["w1rep"], cw["bias"], cw["w2rep"], cw["b2rep"], cw["gain"])


def _compress_sample(cache_t, page_flat, cw):
    b = DEC_BATCH
    page_specs = [pl.BlockSpec((1, 2, KV_WIDTH, PAGE_SIZE),
                               functools.partial(lambda bi, pt, p: (pt[bi * N_PAGES + p], 0, 0, 0), p=p))
                  for p in range(N_PAGES)]
    full = lambda shape: pl.BlockSpec(shape, lambda bi, pt: (0,) * len(shape))
    grid_spec = pltpu.PrefetchScalarGridSpec(
        num_scalar_prefetch=1,
        grid=(b,),
        in_specs=page_specs + [full((2, CMP_STRIDE * 128, 2 * CMP_HIDDEN)), full((2, 1, CMP_HIDDEN)),
                               full((2, HEAD_DIM, CMP_HIDDEN)), full((2, HEAD_DIM, N_HALF)),
                               full((HEAD_DIM, N_HALF))],
        out_specs=pl.BlockSpec((2, 1, KV_WIDTH, N_HALF), lambda bi, pt: (0, bi, 0, 0)),
        scratch_shapes=[pltpu.VMEM((2, 2, PAST_LEN, 128), F32),
                        pltpu.VMEM((2, NSA_KV * N_HALF, CMP_STRIDE * 128), BF16)],
    )
    return pl.pallas_call(
        _compress_sample_kernel,
        grid_spec=grid_spec,
        out_shape=jax.ShapeDtypeStruct((2, b, KV_WIDTH, N_HALF), F32),
        compiler_params=_params(("arbitrary",)),
        name="compress_sample",
    )(page_flat, *([cache_t] * N_PAGES), cw["w1pair"], cw["bias"], cw["w2t"], cw["b2col"], cw["gaincol"])


def _select_blocks(imp, cur, n_blocks):
    rows = imp.shape[0]
    j = lax.broadcasted_iota(jnp.int32, (rows, 1), 0)
    valid = j <= cur
    forced = (j == 0) | (j == cur) | (j == cur - 1)
    score = jnp.where(valid, jnp.where(forced, FORCED, imp), NEG)
    rank = jnp.zeros(score.shape, F32)
    for i in range(n_blocks):
        row = score[i:i + 1, :]
        ahead = (row > score) | ((row == score) & (j > i))
        rank = rank + jnp.where(ahead, 1.0, 0.0)
    return jnp.where(valid & (rank < float(SEL_TOPK)) & (j < n_blocks), 1.0, 0.0)


def _softmax_rows(s, valid):
    s = jnp.where(valid, s, NEG)
    m = jnp.max(s, axis=-1, keepdims=True)
    e = jnp.where(valid, jnp.exp(s - m), 0.0)
    return e, jnp.maximum(jnp.sum(e, axis=-1, keepdims=True), 1e-30)


def _softmax_cols(s, valid):
    s = jnp.where(valid, s, NEG)
    m = jnp.max(s, axis=0, keepdims=True)
    e = jnp.where(valid, jnp.exp(s - m), 0.0)
    return e, jnp.maximum(jnp.sum(e, axis=0, keepdims=True), 1e-30)


def _slope(g, r):
    return float(2.0 ** (-8.0 * (g * NSA_REP + r + 1) / NSA_HEADS))


NSA_TQ = 128
SEL_CHUNK_LOG2 = 9
SEL_CHUNK = 1 << SEL_CHUNK_LOG2
WIN_KEYS = WINDOW + NSA_TQ


def _nsa_prompt_kernel(q_ref, gate_ref, kc_ref, vc_ref, ks_ref, vs_ref, kw_ref, vw_ref, cover_ref, expand_ref,
                       o_ref, ksb, vsb, kwb, vwb):
    i = pl.program_id(1)
    tq = NSA_TQ
    rows = NSA_REP * tq

    @pl.when(i == 0)
    def _():
        ksb[...] = ks_ref[0].astype(BF16)
        vsb[...] = vs_ref[0].astype(BF16)
        kwb[...] = kw_ref[0].astype(BF16)
        vwb[...] = vw_ref[0].astype(BF16)

    q = q_ref[0] * SCALE
    gates = gate_ref[0]
    kcb = kc_ref[0, 0].astype(BF16)
    vcb = vc_ref[0, 0].astype(BF16)
    q0 = i * tq
    qpos_col = q0 + lax.broadcasted_iota(jnp.int32, (tq, 1), 0)
    qpos4 = jnp.concatenate([qpos_col] * NSA_REP, axis=0)
    qpos_row = q0 + lax.broadcasted_iota(jnp.int32, (1, tq), 1)
    cur_row = lax.shift_right_logical(qpos_row, 6)
    win_start = pl.multiple_of(jnp.maximum(q0 - WINDOW, 0), NSA_TQ)
    n_chunks = lax.shift_right_logical(q0 + tq + SEL_CHUNK - 1, SEL_CHUNK_LOG2)

    groups = range(NSA_KV)
    qgs = [jnp.concatenate(
        [jnp.where(_group_mask(KV_WIDTH, g), q[:, r * KV_WIDTH:(r + 1) * KV_WIDTH], 0.0) for r in range(NSA_REP)],
        axis=0).astype(BF16) for g in groups]

    def gate_col(g, branch):
        cols = [gates[:, g * 12 + r * 3 + branch:g * 12 + r * 3 + branch + 1] for r in range(NSA_REP)]
        return jnp.concatenate(cols, axis=0)

    def biased(scores, d, mask_bias, g):
        slabs = [mask_bias - _slope(g, r) * d for r in range(NSA_REP)]
        return scores + jnp.concatenate(slabs, axis=0)

    n_idx = lax.broadcasted_iota(jnp.int32, (1, N_HALF), 1)
    dist = (qpos4 - (n_idx * CMP_STRIDE + (CMP_LEN - 1))).astype(F32)
    valid = dist >= 0.0
    o_cmp, sel_qs = [], []
    for g in groups:
        slope = jnp.concatenate([jnp.full((tq, 1), _slope(g, r), F32) for r in range(NSA_REP)], axis=0)
        e, den = _softmax_rows(_dot_nt(qgs[g], kcb) - slope * dist, valid)
        pc = e / den
        o_cmp.append(_dot(pc.astype(BF16), vcb))
        pcsum = pc[0:tq]
        for r in range(1, NSA_REP):
            pcsum = pcsum + pc[r * tq:(r + 1) * tq]
        hi, lo = _split_bf16(pcsum)
        imp = _dot_nt(cover_ref[...], hi) + _dot_nt(cover_ref[...], lo)
        sel = _select_blocks(imp, cur_row, SEQ // SEL_BLK)
        sel = jnp.concatenate([sel, jnp.zeros((128 - sel.shape[0], tq), F32)], axis=0)
        sel_qs.append(sel.T.astype(BF16))

    def sel_step(kk, carry):
        ks = pl.multiple_of(kk * SEL_CHUNK, SEL_CHUNK)
        kpos = ks + lax.broadcasted_iota(jnp.int32, (1, SEL_CHUNK), 1)
        di = qpos_col - kpos
        d = di.astype(F32)
        kblk = ksb[pl.ds(ks, SEL_CHUNK), :]
        vblk = vsb[pl.ds(ks, SEL_CHUNK), :]
        new = []
        for g in groups:
            m, l, acc = carry[g]
            chosen = _dot(sel_qs[g], expand_ref[kk])
            mask_bias = jnp.where((chosen > 0.5) & (di >= 0), 0.0, NEG)
            s = biased(_dot_nt(qgs[g], kblk), d, mask_bias, g)
            m_new = jnp.maximum(m, jnp.max(s, axis=-1, keepdims=True))
            p = jnp.exp(s - m_new)
            alpha = jnp.exp(m - m_new)
            l = alpha * l + jnp.sum(p, axis=-1, keepdims=True)
            acc = alpha * acc + _dot(p.astype(BF16), vblk)
            new.append((m_new, l, acc))
        return tuple(new)

    init = tuple((jnp.full((rows, 1), NEG, F32), jnp.zeros((rows, 1), F32), jnp.zeros((rows, KV_WIDTH), F32))
                 for _ in groups)
    sel_state = lax.fori_loop(0, n_chunks, sel_step, init)

    kpos = win_start + lax.broadcasted_iota(jnp.int32, (1, WIN_KEYS), 1)
    di = qpos_col - kpos
    d = di.astype(F32)
    win_bias = jnp.where((di >= 0) & (di < WINDOW), 0.0, NEG)
    kwin = kwb[pl.ds(win_start, WIN_KEYS), :]
    vwin = vwb[pl.ds(win_start, WIN_KEYS), :]

    out = [jnp.zeros((tq, KV_WIDTH), F32) for _ in range(NSA_REP)]
    for g in groups:
        s = biased(_dot_nt(qgs[g], kwin), d, win_bias, g)
        e = jnp.exp(s - jnp.max(s, axis=-1, keepdims=True))
        o_win = _dot(e.astype(BF16), vwin) / jnp.sum(e, axis=-1, keepdims=True)
        _, l_sel, acc_sel = sel_state[g]
        o_sel = acc_sel / jnp.maximum(l_sel, 1e-30)
        mixed = gate_col(g, 0) * o_cmp[g] + gate_col(g, 1) * o_sel + gate_col(g, 2) * o_win
        gm = _group_mask(KV_WIDTH, g)
        for r in range(NSA_REP):
            out[r] = out[r] + jnp.where(gm, mixed[r * tq:(r + 1) * tq], 0.0)

    for r in range(NSA_REP):
        o_ref[0, :, r * KV_WIDTH:(r + 1) * KV_WIDTH] = out[r].astype(o_ref.dtype)


def _nsa_prompt(nq, gates, cmp_tok, kv, win, cover_t, expand):
    b, t, tq = BATCH, SEQ, NSA_TQ
    nq3 = nq.reshape(b, t, NSA_WIDTH)
    g3 = gates.reshape(b, t, 128)
    kv3 = kv.reshape(b, t, 4 * KV_WIDTH)
    win3 = win.reshape(b, t, 2 * KV_WIDTH)
    col = lambda c: pl.BlockSpec((1, t, KV_WIDTH), lambda bi, i: (bi, 0, c))
    tok = lambda c: pl.BlockSpec((1, 1, N_HALF, KV_WIDTH), lambda bi, i: (c, bi, 0, 0))
    return pl.pallas_call(
        _nsa_prompt_kernel,
        grid=(b, t // tq),
        in_specs=[pl.BlockSpec((1, tq, NSA_WIDTH), lambda bi, i: (bi, i, 0)),
                  pl.BlockSpec((1, tq, 128), lambda bi, i: (bi, i, 0)),
                  tok(0), tok(1), col(2), col(3), col(0), col(1),
                  pl.BlockSpec(cover_t.shape, lambda bi, i: (0, 0)),
                  pl.BlockSpec(expand.shape, lambda bi, i: (0, 0, 0))],
        out_specs=pl.BlockSpec((1, tq, NSA_WIDTH), lambda bi, i: (bi, i, 0)),
        out_shape=jax.ShapeDtypeStruct((b, t, NSA_WIDTH), BF16),
        scratch_shapes=[pltpu.VMEM((t, KV_WIDTH), BF16) for _ in range(4)],
        compiler_params=_params(("parallel", "arbitrary")),
        name="nsa_prompt",
    )(nq3, g3, cmp_tok, cmp_tok, kv3, kv3, win3, win3, cover_t, expand)


SAMPLE_SEL_BLOCKS = -(-(PAST_LEN + DEC_SEQ) // SEL_BLK)
SEL_ROWS = 48


def _nsa_sample_kernel(pt_ref, *refs):
    pages = refs[:N_PAGES]
    (q_ref, gate_ref, kc_ref, vc_ref, new_ref, cw_ref, wnew_ref, cover_ref, perm_ref, o_ref, wout_ref) = refs[N_PAGES:]
    t = DEC_SEQ
    nrow = NSA_REP * NSA_KV * t
    wb = cw_ref.shape[-1]

    q = q_ref[0] * SCALE
    pieces = []
    for r in range(NSA_REP):
        qr = q[:, r * KV_WIDTH:(r + 1) * KV_WIDTH]
        for g in range(NSA_KV):
            pieces.append(jnp.where(_group_mask(KV_WIDTH, g), qr, 0.0))
    qm = jnp.concatenate(pieces, axis=0).astype(BF16)

    rowi = lax.broadcasted_iota(jnp.int32, (nrow, 1), 0)
    g_col = lax.shift_right_logical(rowi, 3) & (NSA_KV - 1)
    r_col = lax.shift_right_logical(rowi, 5)
    slope = jnp.exp2(-8.0 * (g_col * NSA_REP + r_col + 1).astype(F32) / NSA_HEADS)
    qpos = PAST_LEN + (rowi & (t - 1))
    lane = lax.broadcasted_iota(jnp.int32, (1, 128), 1)
    pad_rows = lambda a: jnp.concatenate([a, jnp.zeros((128 - t, a.shape[-1]), F32)], axis=0)

    def bias_and_mask(s, kpos_row, extra_ok=None, window=False):
        d = (qpos - kpos_row).astype(F32)
        ok = d >= 0.0
        if window:
            ok = ok & (d < float(WINDOW))
        if extra_ok is not None:
            ok = ok & extra_ok
        return s - slope * d, ok

    s, ok = bias_and_mask(_dot(qm, kc_ref[0, 0].astype(BF16)), lane * CMP_STRIDE + (CMP_LEN - 1))
    e, den = _softmax_rows(s, ok)
    pc = e / den
    o_cmp = _dot_nt(pc.astype(BF16), vc_ref[0, 0].astype(BF16))
    hi, lo = _split_bf16(pc)
    imp = _dot(hi, cover_ref[...]) + _dot(lo, cover_ref[...])
    per_rep = NSA_KV * t
    imp = imp[0:per_rep] + imp[per_rep:2 * per_rep] + imp[2 * per_rep:3 * per_rep] + imp[3 * per_rep:]
    imp_t = jnp.concatenate([imp, jnp.zeros((nrow - per_rep, 128), F32)], axis=0).T
    cur = lax.shift_right_logical(PAST_LEN + (lane & (t - 1)), 6)
    sel_t = _select_blocks(imp_t[0:SEL_ROWS], cur, SAMPLE_SEL_BLOCKS)
    sel = jnp.concatenate([sel_t, jnp.zeros((128 - SEL_ROWS, 128), F32)], axis=0).T[0:per_rep]
    sel = jnp.concatenate([sel] * NSA_REP, axis=0)

    scores, oks = [], []
    for p in range(N_PAGES):
        chosen = jnp.where(lane < SEL_BLK, sel[:, 2 * p:2 * p + 1], sel[:, 2 * p + 1:2 * p + 2])
        s, ok = bias_and_mask(_dot(qm, pages[p][0, 0].astype(BF16)), p * PAGE_SIZE + lane, chosen > 0.5)
        scores.append(s)
        oks.append(ok)
    knew = pad_rows(new_ref[0, :, 0:KV_WIDTH]).astype(BF16)
    vnew = pad_rows(new_ref[0, :, KV_WIDTH:2 * KV_WIDTH]).astype(BF16)
    last = SAMPLE_SEL_BLOCKS - 1
    s, ok = bias_and_mask(_dot_nt(qm, knew), PAST_LEN + lane, (sel[:, last:last + 1] > 0.5) & (lane < t))
    scores.append(s)
    oks.append(ok)
    e, den = _softmax_rows(jnp.concatenate(scores, axis=-1), jnp.concatenate(oks, axis=-1))
    prob = (e / den).astype(BF16)
    o_sel = _dot(prob[:, PAST_LEN:], vnew)
    for p in range(N_PAGES):
        o_sel = o_sel + _dot_nt(prob[:, p * PAGE_SIZE:(p + 1) * PAGE_SIZE], pages[p][0, 1].astype(BF16))

    lane_w = lax.broadcasted_iota(jnp.int32, (1, wb), 1)
    s_old, ok_old = bias_and_mask(_dot(qm, cw_ref[0, 0].astype(BF16)), PAST_LEN - wb + lane_w, window=True)
    kwn = pad_rows(wnew_ref[0, :, 0:KV_WIDTH]).astype(BF16)
    vwn = pad_rows(wnew_ref[0, :, KV_WIDTH:2 * KV_WIDTH]).astype(BF16)
    s_new, ok_new = bias_and_mask(_dot_nt(qm, kwn), PAST_LEN + lane, lane < t, window=True)
    e, den = _softmax_rows(jnp.concatenate([s_old, s_new], axis=-1), jnp.concatenate([ok_old, ok_new], axis=-1))
    prob = (e / den).astype(BF16)
    o_win = _dot_nt(prob[:, 0:wb], cw_ref[0, 1].astype(BF16)) + _dot(prob[:, wb:], vwn)

    ghi, glo = _split_bf16(pad_rows(gate_ref[0]))
    grow = _dot(perm_ref[...], ghi) + _dot(perm_ref[...], glo)
    base = g_col * 12 + r_col * 3

    def gate_col(branch):
        return jnp.sum(jnp.where(lane == base + branch, grow, 0.0), axis=-1, keepdims=True)

    mixed = gate_col(0) * o_cmp + gate_col(1) * o_sel + gate_col(2) * o_win
    for r in range(NSA_REP):
        acc = jnp.zeros((t, KV_WIDTH), F32)
        for g in range(NSA_KV):
            lo_row = (r * NSA_KV + g) * t
            acc = acc + jnp.where(_group_mask(KV_WIDTH, g), mixed[lo_row:lo_row + t], 0.0)
        o_ref[0, :, r * KV_WIDTH:(r + 1) * KV_WIDTH] = acc.astype(o_ref.dtype)

    for c in range(2):
        new_t = pad_rows(wnew_ref[0, :, c * KV_WIDTH:(c + 1) * KV_WIDTH]).T
        tail = jnp.concatenate([jnp.zeros((KV_WIDTH, wb - 128), F32), pltpu.roll(new_t, 128 - t, 1)], axis=-1)
        wout_ref[0, c] = jnp.where(lane_w >= wb - t, tail, pltpu.roll(cw_ref[0, c], wb - t, 1))


def _nsa_sample(nq, gates, cmp_tok, cache_t, page_flat, kv_new, win_t, win_new, cover, perm):
    b, t = DEC_BATCH, DEC_SEQ
    wb = win_t.shape[-1]
    per_b = lambda shape: pl.BlockSpec(shape, lambda bi, pt: (bi,) + (0,) * (len(shape) - 1))
    page_specs = [pl.BlockSpec((1, 2, KV_WIDTH, PAGE_SIZE),
                               functools.partial(lambda bi, pt, p: (pt[bi * N_PAGES + p], 1, 0, 0), p=p))
                  for p in range(N_PAGES)]
    tok = lambda c: pl.BlockSpec((1, 1, KV_WIDTH, N_HALF), lambda bi, pt: (c, bi, 0, 0))
    grid_spec = pltpu.PrefetchScalarGridSpec(
        num_scalar_prefetch=1,
        grid=(b,),
        in_specs=page_specs + [per_b((1, t, NSA_WIDTH)), per_b((1, t, 128)), tok(0), tok(1),
                               pl.BlockSpec((1, t, 2 * KV_WIDTH), lambda bi, pt: (bi, 0, 1)),
                               per_b((1, 2, KV_WIDTH, wb)), per_b((1, t, 2 * KV_WIDTH)),
                               pl.BlockSpec(cover.shape, lambda bi, pt: (0, 0)),
                               pl.BlockSpec(perm.shape, lambda bi, pt: (0, 0))],
        out_specs=[per_b((1, t, NSA_WIDTH)), per_b((1, 2, KV_WIDTH, wb))],
    )
    return pl.pallas_call(
        _nsa_sample_kernel,
        grid_spec=grid_spec,
        out_shape=[jax.ShapeDtypeStruct((b, t, NSA_WIDTH), BF16), jax.ShapeDtypeStruct(win_t.shape, F32)],
        compiler_params=_params(("arbitrary",)),
        name="nsa_sample",
    )(page_flat, *([cache_t] * N_PAGES), nq.reshape(b, t, NSA_WIDTH), gates.reshape(b, t, 128), cmp_tok, cmp_tok,
      kv_new.reshape(b, t, 4 * KV_WIDTH), win_t, win_new.reshape(b, t, 2 * KV_WIDTH), cover, perm)


def _out_kernel(x_ref, ret_ref, nsa_ref, wr_ref, wn_ref, o_ref):
    o_ref[...] = x_ref[...] + _dot(ret_ref[...], wr_ref[...]) + _dot(nsa_ref[...], wn_ref[...])


def _mixer_out(x2d, ret, nsa, w_ret, w_nsa, tm=1024, tn=1024):
    n, d = x2d.shape
    return pl.pallas_call(
        _out_kernel,
        grid=(n // tm, d // tn),
        in_specs=[pl.BlockSpec((tm, tn), lambda i, j: (i, j)),
                  pl.BlockSpec((tm, RET_WIDTH), lambda i, j: (i, 0)),
                  pl.BlockSpec((tm, NSA_WIDTH), lambda i, j: (i, 0)),
                  pl.BlockSpec((RET_WIDTH, tn), lambda i, j: (0, j)),
                  pl.BlockSpec((NSA_WIDTH, tn), lambda i, j: (0, j))],
        out_specs=pl.BlockSpec((tm, tn), lambda i, j: (i, j)),
        out_shape=jax.ShapeDtypeStruct((n, d), F32),
        compiler_params=_params(("parallel", "arbitrary")),
        name="mixer_out",
    )(x2d, ret, nsa, w_ret, w_nsa)


FFN_TN = 512
FFN_COLS = D_FF // FFN_TN
FFN_SUB = 512


def _ffn_up_kernel(*refs, hist, step, tiles_per_seq, has_prev):
    if has_prev:
        x_ref, wa_ref, wb_ref, cwa_ref, cwb_ref, cba_ref, cbb_ref, pa_ref, pb_ref = refs[:9]
        act_ref, sta_ref, stb_ref, exta, extb = refs[9:]
    else:
        x_ref, wa_ref, wb_ref, cwa_ref, cwb_ref, cba_ref, cbb_ref = refs[:7]
        act_ref, sta_ref, stb_ref, exta, extb = refs[7:]
        pa_ref = pb_ref = None
    tm = x_ref.shape[0]
    first = (pl.program_id(1) % tiles_per_seq) == 0
    x = x_ref[...]
    halves = ((wa_ref, cwa_ref, cba_ref, pa_ref, sta_ref, exta), (wb_ref, cwb_ref, cbb_ref, pb_ref, stb_ref, extb))
    if not has_prev:
        @pl.when(first)
        def _():
            for half in halves:
                half[5][hist - 2 * step:hist, :] = jnp.zeros((2 * step, half[5].shape[-1]), F32)
    pieces = act_ref.shape[-1] // FFN_SUB
    ups = [[_dot(x, half[0][:, k * FFN_SUB:(k + 1) * FFN_SUB]) for half in halves] for k in range(pieces)]
    for k in range(pieces):
        cols = slice(k * FFN_SUB, (k + 1) * FFN_SUB)
        outs = []
        for (w_ref, cw_ref, cb_ref, p_ref, st_ref, ext), u in zip(halves, ups[k]):
            if has_prev:
                ext[hist - 2 * step:hist, cols] = p_ref[:, cols]
            ext[hist:hist + tm, cols] = u
            cw = cw_ref[:, cols]
            c = (cb_ref[:, cols] + cw[2:3] * u + cw[1:2] * ext[hist - step:hist - step + tm, cols]
                 + cw[0:1] * ext[hist - 2 * step:hist - 2 * step + tm, cols])
            outs.append(c)
            tail = ext[hist + tm - 2 * step:hist + tm, cols]
            if has_prev:
                st_ref[:, cols] = tail
            else:
                st_ref[0, :, cols] = tail
                ext[hist - 2 * step:hist, cols] = tail
        act_ref[:, cols] = (jax.nn.silu(outs[0]) * outs[1]).astype(act_ref.dtype)


def _ffn_up(hn, w_up, conv_w, conv_b, prev, *, tm, step, seqs, tiles_per_seq):
    n, d = hn.shape
    tn = FFN_TN
    hist = 8 if step == 1 else 2 * step
    has_prev = prev is not None
    half = lambda off: (lambda j, i: (0, j + off))
    in_specs = [pl.BlockSpec((tm, d), lambda j, i: (i, 0)),
                pl.BlockSpec((d, tn), half(0)), pl.BlockSpec((d, tn), half(FFN_COLS)),
                pl.BlockSpec((CONV_W, tn), half(0)), pl.BlockSpec((CONV_W, tn), half(FFN_COLS)),
                pl.BlockSpec((1, tn), half(0)), pl.BlockSpec((1, tn), half(FFN_COLS))]
    args = [hn, w_up, w_up, conv_w, conv_w, conv_b, conv_b]
    if has_prev:
        in_specs += [pl.BlockSpec((2 * step, tn), half(0)), pl.BlockSpec((2 * step, tn), half(FFN_COLS))]
        args += [prev, prev]
        st_spec = pl.BlockSpec((2 * step, tn), lambda j, i: (0, j))
        st_shape = jax.ShapeDtypeStruct((2 * step, D_FF), F32)
    else:
        st_spec = pl.BlockSpec((1, 2, tn), lambda j, i: (i // tiles_per_seq, 0, j))
        st_shape = jax.ShapeDtypeStruct((seqs, 2, D_FF), F32)
    kernel = functools.partial(_ffn_up_kernel, hist=hist, step=step, tiles_per_seq=tiles_per_seq, has_prev=has_prev)
    return pl.pallas_call(
        kernel,
        grid=(FFN_COLS, n // tm),
        in_specs=in_specs,
        out_specs=[pl.BlockSpec((tm, tn), lambda j, i: (i, j)), st_spec, st_spec],
        out_shape=[jax.ShapeDtypeStruct((n, D_FF), BF16), st_shape, st_shape],
        scratch_shapes=[pltpu.VMEM((hist + tm, tn), F32), pltpu.VMEM((hist + tm, tn), F32)],
        compiler_params=_params(("arbitrary", "arbitrary")),
        name="ffn_up",
    )(*args)


def _ffn_down_kernel(h_ref, a_ref, w_ref, o_ref):
    o_ref[...] = h_ref[...] + _dot(a_ref[...], w_ref[...])


def _ffn_down(h, act, w_down, tm=1024, tn=512):
    n, d = h.shape
    return pl.pallas_call(
        _ffn_down_kernel,
        grid=(n // tm, d // tn),
        in_specs=[pl.BlockSpec((tm, tn), lambda i, j: (i, j)),
                  pl.BlockSpec((tm, D_FF), lambda i, j: (i, 0)),
                  pl.BlockSpec((D_FF, tn), lambda i, j: (0, j))],
        out_specs=pl.BlockSpec((tm, tn), lambda i, j: (i, j)),
        out_shape=jax.ShapeDtypeStruct((n, d), F32),
        compiler_params=_params(("parallel", "arbitrary")),
        name="ffn_down",
    )(h, act, w_down)


def _block_cover_t(n_sel, rows):
    cs = np.arange(N_HALF - 1)[None, :] * CMP_STRIDE
    js = np.arange(n_sel)[:, None] * SEL_BLK
    cov = np.clip(np.minimum(cs + CMP_LEN, js + SEL_BLK) - np.maximum(cs, js), 0, None) / CMP_LEN
    out = np.zeros((rows, N_HALF), np.float32)
    out[:n_sel, :N_HALF - 1] = cov
    return jnp.asarray(out, dtype=BF16)


def _prepare_weights(g_attn, w_in, q_norm, k_norm_cmp, k_norm_slc, k_norm_win, cmp_pe, cmp_w1, cmp_b1, cmp_w2,
                     cmp_b2, w_out):
    d = D_MODEL
    o = 0
    w_rqk = w_in[:, o:o + 2 * RET_WIDTH]; o += 2 * RET_WIDTH
    w_rvg = w_in[:, o:o + 2 * RET_WIDTH]; o += 2 * RET_WIDTH
    w_nq = w_in[:, o:o + NSA_WIDTH]; o += NSA_WIDTH
    w_kv = w_in[:, o:o + 4 * KV_WIDTH]; o += 4 * KV_WIDTH
    w_win = w_in[:, o:o + 2 * KV_WIDTH]; o += 2 * KV_WIDTH
    w_ng = w_in[:, o:]
    w_nq = w_nq.reshape(d, NSA_KV, NSA_REP, HEAD_DIM).transpose(0, 2, 1, 3).reshape(d, NSA_WIDTH)
    w_ng = jnp.pad(w_ng, ((0, 0), (0, 128 - w_ng.shape[1])))
    tile4 = lambda v: jnp.tile(v, NSA_KV)
    zeros = jnp.zeros((KV_WIDTH,), F32)
    ones = jnp.ones((KV_WIDTH,), F32)
    wts = dict(
        g_attn=g_attn,
        w_rqk=w_rqk.astype(BF16), w_rvg=w_rvg.astype(BF16), w_nq=w_nq.astype(BF16),
        w_kv=w_kv.astype(BF16), w_win=w_win.astype(BF16), w_ng=w_ng.astype(BF16),
        rot_scale=jnp.concatenate([jnp.ones((RET_WIDTH,), F32),
                                   jnp.full((RET_WIDTH,), RET_DK ** -0.5, F32)]).reshape(1, -1),
        nq_gain=jnp.tile(q_norm, NSA_HEADS).reshape(1, -1), nq_mask=jnp.ones((1, NSA_WIDTH), F32),
        kv_gain=jnp.concatenate([zeros, zeros, tile4(k_norm_slc), zeros]).reshape(1, -1),
        kv_mask=jnp.concatenate([zeros, zeros, ones, zeros]).reshape(1, -1),
        win_gain=jnp.concatenate([tile4(k_norm_win), zeros]).reshape(1, -1),
        win_mask=jnp.concatenate([ones, zeros]).reshape(1, -1),
    )
    w1 = cmp_w1.reshape(2, 2, CMP_STRIDE, HEAD_DIM, CMP_HIDDEN)
    w1rep = jnp.broadcast_to(w1.transpose(0, 2, 3, 1, 4)[:, :, None],
                             (2, CMP_STRIDE, NSA_KV, HEAD_DIM, 2, CMP_HIDDEN))
    w1rep = w1rep.reshape(2, CMP_STRIDE * KV_WIDTH, 2 * CMP_HIDDEN).astype(BF16)
    w1flat = w1.reshape(2, 2, CMP_STRIDE * HEAD_DIM, CMP_HIDDEN).astype(BF16)
    pe8 = jnp.pad(cmp_pe.reshape(2, 2, 1, CMP_STRIDE * HEAD_DIM), ((0, 0), (0, 0), (0, 15), (0, 0)))
    bias = _cmp_bias(pe8, w1flat, cmp_b1.reshape(2, 1, CMP_HIDDEN))
    w1pair = jnp.broadcast_to(w1.transpose(0, 2, 3, 1, 4)[:, :, None], (2, CMP_STRIDE, 2, HEAD_DIM, 2, CMP_HIDDEN))
    col = lambda v: jnp.broadcast_to(v[:, :, None], (2, HEAD_DIM, N_HALF))
    cw = dict(
        w1pair=w1pair.reshape(2, CMP_STRIDE * 128, 2 * CMP_HIDDEN).astype(BF16),
        w2t=cmp_w2.transpose(0, 2, 1).astype(BF16), b2col=col(cmp_b2),
        gaincol=jnp.broadcast_to(k_norm_cmp[:, None], (HEAD_DIM, N_HALF)),
        w1rep=w1rep, bias=bias,
        w2rep=jnp.tile(cmp_w2, (1, 1, NSA_KV)).astype(BF16),
        b2rep=jnp.tile(cmp_b2, (1, NSA_KV)).reshape(2, 1, KV_WIDTH),
        gain=jnp.stack([tile4(k_norm_cmp), ones]).reshape(2, 1, KV_WIDTH),
    )
    w_ret = w_out[:RET_WIDTH].astype(BF16)
    w_nsa = w_out[RET_WIDTH:].reshape(NSA_KV, NSA_REP, HEAD_DIM, d).transpose(1, 0, 2, 3).reshape(NSA_WIDTH, d)
    return wts, cw, w_ret, w_nsa.astype(BF16)


def kernel(x_prompt, x_sample, cache_kv, cache_win, state_ret, state_conv, page_table, g_attn, w_in, q_norm,
           k_norm_cmp, k_norm_slc, k_norm_win, cmp_pe, cmp_w1, cmp_b1, cmp_w2, cmp_b2, ret_gn, w_out, g_ffn, w_up,
           conv_w, conv_b, w_down):
    assert x_prompt.shape == (BATCH, SEQ, D_MODEL) and x_sample.shape == (DEC_BATCH, DEC_SEQ, D_MODEL)
    assert g_attn.shape[0] == 1, "single layer"
    wts, cw, w_ret, w_nsa = _prepare_weights(g_attn[0], w_in[0], q_norm[0], k_norm_cmp[0], k_norm_slc[0],
                                              k_norm_win[0], cmp_pe[0], cmp_w1[0], cmp_b1[0], cmp_w2[0],
                                              cmp_b2[0], w_out[0])
    w_up_b = w_up[0].astype(BF16)
    w_down_b = w_down[0].astype(BF16)
    cb = conv_b[0].reshape(1, -1)
    n_p, n_s = BATCH * SEQ, DEC_BATCH * DEC_SEQ
    xp = x_prompt.reshape(n_p, D_MODEL)
    xs = x_sample.reshape(n_s, D_MODEL)

    pos_p = jnp.tile(jnp.arange(SEQ, dtype=jnp.int32), BATCH)
    qk, vg, nq, kv_p, win_p, gates = _project_all(xp, pos_p, wts)
    ret_mix, ret_state_p = _retention_prompt(qk, vg, ret_gn[0])
    kv3 = kv_p.reshape(BATCH, SEQ, 4 * KV_WIDTH)
    cmp_tok = _compress_prompt(kv3, cw)
    cover_p = _block_cover_t(SEQ // SEL_BLK, SEQ // SEL_BLK)
    key = np.arange(SEQ).reshape(SEQ // SEL_CHUNK, 1, SEL_CHUNK)
    expand = jnp.asarray((key // SEL_BLK) == np.arange(128)[None, :, None], dtype=BF16)
    nsa = _nsa_prompt(nq, gates, cmp_tok, kv_p, win_p, cover_p, expand)
    h_p = _mixer_out(xp, ret_mix.reshape(n_p, RET_WIDTH), nsa.reshape(n_p, NSA_WIDTH), w_ret, w_nsa)
    hn_p = _rmsnorm(h_p, g_ffn[0])
    act_p, st_a, st_b = _ffn_up(hn_p, w_up_b, conv_w[0], cb, None, tm=1024, step=1, seqs=BATCH,
                                tiles_per_seq=SEQ // 1024)
    y_p = _ffn_down(h_p, act_p, w_down_b)
    conv_p = jnp.concatenate([st_a, st_b], axis=-1)

    pos_s = jnp.tile(PAST_LEN + jnp.arange(DEC_SEQ, dtype=jnp.int32), DEC_BATCH)
    qk, vg, nq, kv_s, win_s, gates = _project_all(xs, pos_s, wts)
    ret_mix, ret_state_s = _retention_sample(qk, vg, state_ret[0], ret_gn[0])
    cache_t = cache_kv[0].transpose(0, 2, 3, 4, 1).reshape(cache_kv.shape[1], 4, KV_WIDTH, PAGE_SIZE)
    page_flat = page_table.reshape(-1).astype(jnp.int32)
    cmp_tok = _compress_sample(cache_t, page_flat, cw)
    wb = cache_win.shape[2]
    win_t = cache_win[0].transpose(0, 2, 3, 4, 1).reshape(DEC_BATCH, 2, KV_WIDTH, wb)
    cover_s = _block_cover_t(SAMPLE_SEL_BLOCKS, 128).T
    lanes = np.arange(128)
    perm = jnp.asarray((lanes[:, None] % DEC_SEQ) == lanes[None, :], dtype=BF16)
    nsa, win_out_t = _nsa_sample(nq, gates, cmp_tok, cache_t, page_flat, kv_s, win_t, win_s, cover_s, perm)
    h_s = _mixer_out(xs, ret_mix.reshape(n_s, RET_WIDTH), nsa.reshape(n_s, NSA_WIDTH), w_ret, w_nsa)
    to_tm = lambda a: a.reshape(DEC_BATCH, -1, a.shape[-1]).transpose(1, 0, 2).reshape(-1, a.shape[-1])
    from_tm = lambda a, t: a.reshape(t, DEC_BATCH, a.shape[-1]).transpose(1, 0, 2)
    h_tm = to_tm(h_s)
    hn_s = _rmsnorm(h_tm, g_ffn[0])
    act_s, st_a, st_b = _ffn_up(hn_s, w_up_b, conv_w[0], cb, to_tm(state_conv[0]), tm=n_s, step=DEC_BATCH,
                                seqs=DEC_BATCH, tiles_per_seq=1)
    y_s = from_tm(_ffn_down(h_tm, act_s, w_down_b), DEC_SEQ)
    conv_s = from_tm(jnp.concatenate([st_a, st_b], axis=-1), CONV_W - 1)

    win_sample = win_out_t.reshape(DEC_BATCH, 2, NSA_KV, HEAD_DIM, wb).transpose(0, 4, 1, 2, 3)
    return (
        y_p.reshape(BATCH, SEQ, D_MODEL),
        y_s,
        kv_p.reshape(1, BATCH, SEQ, 4, NSA_KV, HEAD_DIM),
        kv_s.reshape(1, DEC_BATCH, DEC_SEQ, 4, NSA_KV, HEAD_DIM),
        win_p.reshape(BATCH, SEQ, 2 * KV_WIDTH)[:, SEQ - WINDOW:].reshape(1, BATCH, WINDOW, 2, NSA_KV, HEAD_DIM),
        win_sample[None],
        ret_state_p[None],
        ret_state_s[None],
        conv_p[None],
        conv_s[None],
    )
```

```python
import functools

import numpy as np
import jax
import jax.numpy as jnp
from jax import lax
from jax.experimental import pallas as pl
from jax.experimental.pallas import tpu as pltpu

D_MODEL = 2048
BATCH = 4
SEQ = 2048
DEC_BATCH = 128
DEC_SEQ = 8
PAST_LEN = 2048
PAGE_SIZE = 128
N_PAGES = PAST_LEN // PAGE_SIZE
RET_HEADS = 4
RET_DK = 256
RET_DV = 256
RET_WIDTH = RET_HEADS * RET_DV
RET_CHUNK = 128
NSA_HEADS = 16
NSA_KV = 4
NSA_REP = NSA_HEADS // NSA_KV
HEAD_DIM = 64
NSA_WIDTH = NSA_HEADS * HEAD_DIM
KV_WIDTH = NSA_KV * HEAD_DIM
CMP_LEN = 32
CMP_STRIDE = 16
CMP_HIDDEN = 256
SEL_BLK = 64
SEL_TOPK = 16
WINDOW = 512
SCALE = HEAD_DIM ** -0.5
D_FF = 5632
CONV_W = 3
EPS = 1e-6

N_HALF = PAST_LEN // CMP_STRIDE
NEG = -1e30
FORCED = 1e30
VMEM_LIMIT = 56 * 1024 * 1024

BF16 = jnp.bfloat16
F32 = jnp.float32


def _params(semantics, vmem=VMEM_LIMIT):
    return pltpu.CompilerParams(dimension_semantics=semantics, vmem_limit_bytes=vmem)


def _dot(a, b):
    return jnp.dot(a, b, preferred_element_type=F32)


def _dot_nt(a, b):
    return lax.dot_general(a, b, (((1,), (1,)), ((), ())), preferred_element_type=F32)


def _split_bf16(x):
    hi = x.astype(BF16)
    lo = (x - hi.astype(F32)).astype(BF16)
    return hi, lo


def _group_mask(width, g):
    lane = lax.broadcasted_iota(jnp.int32, (1, width), 1)
    return (lane >= g * HEAD_DIM) & (lane < (g + 1) * HEAD_DIM)


def _rms_groups64(z, gain):
    width = z.shape[-1]
    outs = []
    for k in range(width // 128):
        zk = z[:, k * 128:(k + 1) * 128]
        zz = zk * zk
        lane = lax.broadcasted_iota(jnp.int32, (1, 128), 1)
        lo_half = lane < HEAD_DIM
        s_lo = jnp.sum(jnp.where(lo_half, zz, 0.0), axis=-1, keepdims=True)
        s_hi = jnp.sum(jnp.where(lo_half, 0.0, zz), axis=-1, keepdims=True)
        ms = jnp.where(lo_half, s_lo, s_hi) * (1.0 / HEAD_DIM)
        outs.append(zk * lax.rsqrt(ms + EPS))
    y = outs[0] if len(outs) == 1 else jnp.concatenate(outs, axis=-1)
    return y * gain


def _rmsnorm_kernel(x_ref, g_ref, o_ref):
    x = x_ref[...]
    ms = jnp.mean(x * x, axis=-1, keepdims=True)
    o_ref[...] = (x * lax.rsqrt(ms + EPS) * g_ref[...]).astype(o_ref.dtype)


def _rmsnorm(x, g, tm=512):
    n, d = x.shape
    return pl.pallas_call(
        _rmsnorm_kernel,
        grid=(n // tm,),
        in_specs=[pl.BlockSpec((tm, d), lambda i: (i, 0)), pl.BlockSpec((1, d), lambda i: (0, 0))],
        out_specs=pl.BlockSpec((tm, d), lambda i: (i, 0)),
        out_shape=jax.ShapeDtypeStruct((n, d), BF16),
        compiler_params=_params(("parallel",)),
        name="rmsnorm",
    )(x, g.reshape(1, d))


def _bf16_weights(w_ref, wb_ref, row_axis):
    @pl.when(pl.program_id(row_axis) == 0)
    def _():
        wb_ref[...] = w_ref[...].astype(BF16)
    return wb_ref


def _proj_plain_kernel(x_ref, w_ref, o_ref, wb_ref):
    o_ref[...] = _dot(x_ref[...], _bf16_weights(w_ref, wb_ref, 1)[...]).astype(o_ref.dtype)


def _proj_sigmoid_kernel(x_ref, w_ref, o_ref, wb_ref):
    o_ref[...] = jax.nn.sigmoid(_dot(x_ref[...], _bf16_weights(w_ref, wb_ref, 1)[...]))


def _sub_dots(x_ref, w_ref, width):
    x = x_ref[...]
    return [_dot(x, w_ref[:, k * width:(k + 1) * width]) for k in range(w_ref.shape[-1] // width)]


def _proj_norm_kernel(x_ref, w_ref, gain_ref, nmask_ref, o_ref, wb_ref):
    for k, z in enumerate(_sub_dots(x_ref, _bf16_weights(w_ref, wb_ref, 1), KV_WIDTH)):
        cols = slice(k * KV_WIDTH, (k + 1) * KV_WIDTH)
        o_ref[:, cols] = jnp.where(nmask_ref[:, cols] > 0.5, _rms_groups64(z, gain_ref[:, cols]), z)


def _proj_rot_kernel(x_ref, w_ref, cos_ref, sin_ref, scale_ref, o_ref, wb_ref):
    c = cos_ref[...]
    s = sin_ref[...]
    for hh, z in enumerate(_sub_dots(x_ref, _bf16_weights(w_ref, wb_ref, 1), RET_DK)):
        lo, mid, hi = hh * RET_DK, hh * RET_DK + 128, (hh + 1) * RET_DK
        x1 = z[:, :128]
        x2 = z[:, 128:]
        o_ref[:, lo:mid] = (x1 * c - x2 * s) * scale_ref[:, lo:mid]
        o_ref[:, mid:hi] = (x2 * c + x1 * s) * scale_ref[:, mid:hi]


PROJ_TN = 1024


def _proj(xn, w, col0, c, kernel, row_extras=(), col_extras=(), out_dtype=F32, tm=1024, name="proj"):
    n, d = xn.shape
    tn = min(PROJ_TN, c)
    off = col0 // tn
    assert col0 == off * tn and c % tn == 0
    extra_specs = ([pl.BlockSpec((tm, 128), lambda j, i: (i, 0)) for _ in row_extras]
                   + [pl.BlockSpec((1, tn), lambda j, i: (0, j)) for _ in col_extras])
    return pl.pallas_call(
        kernel,
        grid=(c // tn, n // tm),
        in_specs=[pl.BlockSpec((tm, d), lambda j, i: (i, 0)), pl.BlockSpec((d, tn), lambda j, i: (0, j + off))]
        + extra_specs,
        out_specs=pl.BlockSpec((tm, tn), lambda j, i: (i, j)),
        out_shape=jax.ShapeDtypeStruct((n, c), out_dtype),
        scratch_shapes=[pltpu.VMEM((d, tn), BF16)],
        compiler_params=_params(("arbitrary", "arbitrary")),
        name=name,
    )(xn, w, *row_extras, *col_extras)


def _project_all(x2d, pos, wts, tm=1024):
    xn = _rmsnorm(x2d, wts["g_attn"])
    half = RET_DK // 2
    inv = 1.0 / (10000.0 ** jnp.linspace(0.0, 1.0, half, dtype=F32))
    ang = pos.astype(F32)[:, None] * inv[None, :]
    cos = jnp.cos(ang)
    sin = jnp.sin(ang)
    w_in = wts["w_in"]
    rw = 2 * RET_WIDTH
    qk = _proj(xn, w_in, 0, rw, _proj_rot_kernel, (cos, sin), (wts["rot_scale"],), tm=tm, name="proj_rot")
    vg = _proj(xn, w_in, rw, rw, _proj_plain_kernel, tm=tm, name="proj_vg")
    nq = _proj(xn, wts["w_nq"], 0, NSA_WIDTH, _proj_norm_kernel, (), (wts["nq_gain"], wts["nq_mask"]), tm=tm,
               name="proj_nq")
    kv0 = 2 * rw + NSA_WIDTH
    kv = _proj(xn, w_in, kv0, 4 * KV_WIDTH, _proj_norm_kernel, (), (wts["kv_gain"], wts["kv_mask"]), tm=tm,
               name="proj_kv")
    win = _proj(xn, w_in, kv0 + 4 * KV_WIDTH, 2 * KV_WIDTH, _proj_norm_kernel, (),
                (wts["win_gain"], wts["win_mask"]), tm=tm, name="proj_win")
    gates = _proj(xn, wts["w_ng"], 0, 128, _proj_sigmoid_kernel, tm=tm, name="proj_gate")
    return qk, vg, nq, kv, win, gates


def _retention_step(q, k, v, state, dmask, xi, zeta, gch):
    qb = q.astype(BF16)
    kb = k.astype(BF16)
    vb = v.astype(BF16)
    s = _dot_nt(qb, kb) * dmask
    o = _dot(s.astype(BF16), vb) + _dot(qb, state.astype(BF16)) * xi
    kz = (k * zeta).astype(BF16)
    new_state = state * gch + _dot(kz.T, vb)
    return o, new_state


def _ret_mix(o, gn, gate):
    ms = jnp.mean(o * o, axis=-1, keepdims=True)
    return o * lax.rsqrt(ms + EPS) * gn * jax.nn.silu(gate)


def _ret_prompt_kernel(qk_ref, vg_ref, dmask_ref, xi_ref, zeta_ref, gch_ref, gn_ref, mix_ref, state_ref):
    @pl.when(pl.program_id(1) == 0)
    def _():
        state_ref[...] = jnp.zeros_like(state_ref)

    for h in range(RET_HEADS):
        lo, hi = h * RET_DK, (h + 1) * RET_DK
        o, new_state = _retention_step(qk_ref[0, :, lo:hi], qk_ref[0, :, RET_WIDTH + lo:RET_WIDTH + hi],
                                       vg_ref[0, :, lo:hi], state_ref[0, h], dmask_ref[h], xi_ref[h],
                                       zeta_ref[h], gch_ref[h])
        state_ref[0, h] = new_state
        mix_ref[0, :, lo:hi] = _ret_mix(o, gn_ref[h], vg_ref[0, :, RET_WIDTH + lo:RET_WIDTH + hi]).astype(mix_ref.dtype)


def _ret_tables(chunk):
    h = jnp.arange(RET_HEADS, dtype=F32)
    lg = jnp.log(1.0 - jnp.exp2(-5.0 - h))
    i = jnp.arange(chunk, dtype=F32)
    diff = i[:, None] - i[None, :]
    dmask = jnp.where(diff >= 0, jnp.exp(lg[:, None, None] * jnp.maximum(diff, 0.0)), 0.0)
    xi = jnp.exp(lg[:, None] * (i[None, :] + 1.0))
    zeta = jnp.exp(lg[:, None] * (chunk - 1.0 - i[None, :]))
    gch = jnp.exp(lg * chunk)
    bc = lambda a: jnp.broadcast_to(a[:, :, None], (RET_HEADS, chunk, RET_DV))
    return dmask, bc(xi), bc(zeta), jnp.broadcast_to(gch[:, None, None], (RET_HEADS, 1, RET_DV))


def _retention_prompt(qk, vg, gn):
    b, t = BATCH, SEQ
    c = RET_CHUNK
    qk3 = qk.reshape(b, t, 2 * RET_WIDTH)
    vg3 = vg.reshape(b, t, 2 * RET_WIDTH)
    dmask, xi, zeta, gch = _ret_tables(c)
    rows = pl.BlockSpec((1, c, 2 * RET_WIDTH), lambda bi, ci: (bi, ci, 0))
    full = lambda shape: pl.BlockSpec(shape, lambda bi, ci: (0,) * len(shape))
    return pl.pallas_call(
        _ret_prompt_kernel,
        grid=(b, t // c),
        in_specs=[rows, rows, full((RET_HEADS, c, c)), full((RET_HEADS, c, RET_DV)), full((RET_HEADS, c, RET_DV)),
                  full((RET_HEADS, 1, RET_DV)), full((RET_HEADS, 1, RET_DV))],
        out_specs=[pl.BlockSpec((1, c, RET_WIDTH), lambda bi, ci: (bi, ci, 0)),
                   pl.BlockSpec((1, RET_HEADS, RET_DK, RET_DV), lambda bi, ci: (bi, 0, 0, 0))],
        out_shape=[jax.ShapeDtypeStruct((b, t, RET_WIDTH), BF16),
                   jax.ShapeDtypeStruct((b, RET_HEADS, RET_DK, RET_DV), F32)],
        compiler_params=_params(("parallel", "arbitrary")),
        name="retention_prompt",
    )(qk3, vg3, dmask, xi, zeta, gch, gn.reshape(RET_HEADS, 1, RET_DV))


def _ret_sample_kernel(qk_ref, vg_ref, s0_ref, dmask_ref, xi_ref, zeta_ref, gch_ref, gn_ref, mix_ref, state_ref):
    t = qk_ref.shape[1]
    pad = lambda a: jnp.concatenate([a, jnp.zeros((RET_PAD - t, a.shape[-1]), F32)], axis=0)
    for s in range(qk_ref.shape[0]):
        for h in range(RET_HEADS):
            lo, hi = h * RET_DK, (h + 1) * RET_DK
            gate = vg_ref[s, :, RET_WIDTH + lo:RET_WIDTH + hi]
            o, new_state = _retention_step(pad(qk_ref[s, :, lo:hi]), pad(qk_ref[s, :, RET_WIDTH + lo:RET_WIDTH + hi]),
                                           pad(vg_ref[s, :, lo:hi]), s0_ref[s, h], dmask_ref[h], xi_ref[h],
                                           zeta_ref[h], gch_ref[h])
            state_ref[s, h] = new_state
            mix_ref[s, :, lo:hi] = _ret_mix(o[0:t], gn_ref[h], gate).astype(mix_ref.dtype)


RET_PAD = 128
RET_SEQS_PER_STEP = 4


def _retention_sample(qk, vg, state0, gn):
    b, t = DEC_BATCH, DEC_SEQ
    dmask, xi, zeta, gch = _ret_tables(t)
    dmask = jnp.pad(dmask, ((0, 0), (0, RET_PAD - t), (0, RET_PAD - t)))
    xi = jnp.pad(xi, ((0, 0), (0, RET_PAD - t), (0, 0)))
    zeta = jnp.pad(zeta, ((0, 0), (0, RET_PAD - t), (0, 0)))
    full = lambda shape: pl.BlockSpec(shape, lambda bi: (0,) * len(shape))
    return pl.pallas_call(
        _ret_sample_kernel,
        grid=(b // RET_SEQS_PER_STEP,),
        in_specs=[pl.BlockSpec((RET_SEQS_PER_STEP, t, 2 * RET_WIDTH), lambda bi: (bi, 0, 0)),
                  pl.BlockSpec((RET_SEQS_PER_STEP, t, 2 * RET_WIDTH), lambda bi: (bi, 0, 0)),
                  pl.BlockSpec((RET_SEQS_PER_STEP, RET_HEADS, RET_DK, RET_DV), lambda bi: (bi, 0, 0, 0)),
                  full((RET_HEADS, RET_PAD, RET_PAD)), full((RET_HEADS, RET_PAD, RET_DV)), full((RET_HEADS, RET_PAD, RET_DV)),
                  full((RET_HEADS, 1, RET_DV)), full((RET_HEADS, 1, RET_DV))],
        out_specs=[pl.BlockSpec((RET_SEQS_PER_STEP, t, RET_WIDTH), lambda bi: (bi, 0, 0)),
                   pl.BlockSpec((RET_SEQS_PER_STEP, RET_HEADS, RET_DK, RET_DV), lambda bi: (bi, 0, 0, 0))],
        out_shape=[jax.ShapeDtypeStruct((b, t, RET_WIDTH), BF16),
                   jax.ShapeDtypeStruct((b, RET_HEADS, RET_DK, RET_DV), F32)],
        compiler_params=_params(("parallel",)),
        name="retention_sample",
    )(qk.reshape(b, t, -1), vg.reshape(b, t, -1), state0, dmask, xi, zeta, gch, gn.reshape(RET_HEADS, 1, RET_DV))


def _cmp_bias_kernel(pe_ref, w_ref, b1_ref, o_ref):
    acc = b1_ref[0]
    for j in range(2):
        acc = acc + _dot(pe_ref[0, j].astype(BF16), w_ref[0, j])[0:1]
    o_ref[0] = acc


def _cmp_bias(pe8, w1flat, b1):
    return pl.pallas_call(
        _cmp_bias_kernel,
        grid=(2,),
        in_specs=[pl.BlockSpec((1, 2, 16, CMP_STRIDE * HEAD_DIM), lambda c: (c, 0, 0, 0)),
                  pl.BlockSpec((1, 2, CMP_STRIDE * HEAD_DIM, CMP_HIDDEN), lambda c: (c, 0, 0, 0)),
                  pl.BlockSpec((1, 1, CMP_HIDDEN), lambda c: (c, 0, 0))],
        out_specs=pl.BlockSpec((1, 1, CMP_HIDDEN), lambda c: (c, 0, 0)),
        out_shape=jax.ShapeDtypeStruct((2, 1, CMP_HIDDEN), F32),
        compiler_params=_params(("parallel",)),
        name="cmp_bias",
    )(pe8, w1flat, b1)


def _compress_body(load_rows, w1_ref, bias_ref, w2_ref, b2_ref, gain_ref, o_ref, xm_ref):
    for l in range(CMP_STRIDE):
        xl = load_rows(l)
        for g in range(NSA_KV):
            xm_ref[g * N_HALF:(g + 1) * N_HALF, l * KV_WIDTH:(l + 1) * KV_WIDTH] = jnp.where(
                _group_mask(KV_WIDTH, g), xl, 0.0).astype(BF16)
    acc = _dot(xm_ref[...], w1_ref[0])
    h0 = acc[:, :CMP_HIDDEN]
    h1 = pltpu.roll(acc[:, CMP_HIDDEN:], 4 * N_HALF - 1, 0)
    hid = jax.nn.silu(h0 + h1 + bias_ref[0])
    out = _dot(hid.astype(BF16), w2_ref[0]) + b2_ref[0]
    res = jnp.zeros((N_HALF, KV_WIDTH), F32)
    for g in range(NSA_KV):
        res = res + jnp.where(_group_mask(KV_WIDTH, g), out[g * N_HALF:(g + 1) * N_HALF], 0.0)
    normed = _rms_groups64(res, gain_ref[0])
    o_ref[0, 0] = jnp.where(pl.program_id(0) == 0, normed, res)


def _compress_prompt_kernel(lo_ref, hi_ref, w1_ref, bias_ref, w2_ref, b2_ref, gain_ref, o_ref, xm_ref):
    load = lambda l: jnp.concatenate(
        [r[0, pl.ds(l, N_HALF, stride=CMP_STRIDE), :] for r in (lo_ref, hi_ref)], axis=-1)
    _compress_body(load, w1_ref, bias_ref, w2_ref, b2_ref, gain_ref, o_ref, xm_ref)


def _compress_sample_kernel(pt_ref, *refs):
    pages = refs[:N_PAGES]
    w1_ref, bias_ref, w2t_ref, b2_ref, gain_ref, o_ref, t_ref, xm_ref = refs[N_PAGES:]
    lane = lax.broadcasted_iota(jnp.int32, (1, 128), 1)
    for c in range(2):
        for p in range(N_PAGES):
            xt = pages[p][0, c]
            for pair in range(2):
                t_ref[c, pair, p * PAGE_SIZE:(p + 1) * PAGE_SIZE, :] = xt[pair * 128:(pair + 1) * 128, :].T
        for l in range(CMP_STRIDE):
            for pair in range(2):
                rows = t_ref[c, pair, pl.ds(l, N_HALF, stride=CMP_STRIDE), :]
                for member in range(2):
                    g = 2 * pair + member
                    keep = (lane >= member * HEAD_DIM) & (lane < (member + 1) * HEAD_DIM)
                    xm_ref[c, g * N_HALF:(g + 1) * N_HALF, l * 128:(l + 1) * 128] = (
                        jnp.where(keep, rows, 0.0).astype(BF16))
        acc = _dot(xm_ref[c], w1_ref[c])
        h0 = acc[:, :CMP_HIDDEN]
        h1 = pltpu.roll(acc[:, CMP_HIDDEN:], 4 * N_HALF - 1, 0)
        hid = jax.nn.silu(h0 + h1 + bias_ref[c]).astype(BF16)
        outs = []
        for g in range(NSA_KV):
            og = _dot_nt(w2t_ref[c], hid[g * N_HALF:(g + 1) * N_HALF]) + b2_ref[c]
            if c == 0:
                og = og * lax.rsqrt(jnp.mean(og * og, axis=0, keepdims=True) + EPS) * gain_ref[...]
            outs.append(og)
        o_ref[c, 0] = jnp.concatenate(outs, axis=0)


def _cmp_weight_specs(nargs):
    cmap = (lambda c, b: (c, 0, 0)) if nargs == 2 else (lambda c, b, pt: (c, 0, 0))
    return [pl.BlockSpec((1, CMP_STRIDE * KV_WIDTH, 2 * CMP_HIDDEN), cmap),
            pl.BlockSpec((1, 1, CMP_HIDDEN), cmap),
            pl.BlockSpec((1, CMP_HIDDEN, KV_WIDTH), cmap),
            pl.BlockSpec((1, 1, KV_WIDTH), cmap),
            pl.BlockSpec((1, 1, KV_WIDTH), cmap)]


def _compress_prompt(kv3, cw):
    b = kv3.shape[0]
    return pl.pallas_call(
        _compress_prompt_kernel,
        grid=(2, b),
        in_specs=[pl.BlockSpec((1, SEQ, 128), lambda c, bi: (bi, 0, 2 * c)),
                  pl.BlockSpec((1, SEQ, 128), lambda c, bi: (bi, 0, 2 * c + 1))] + _cmp_weight_specs(2),
        out_specs=pl.BlockSpec((1, 1, N_HALF, KV_WIDTH), lambda c, bi: (c, bi, 0, 0)),
        out_shape=jax.ShapeDtypeStruct((2, b, N_HALF, KV_WIDTH), F32),
        scratch_shapes=[pltpu.VMEM((NSA_KV * N_HALF, CMP_STRIDE * KV_WIDTH), BF16)],
        compiler_params=_params(("arbitrary", "arbitrary")),
        name="compress_prompt",
    )(kv3, kv3, cw["w1rep"], cw["bias"], cw["w2rep"], cw["b2rep"], cw["gain"])


def _compress_sample(cache_t, page_flat, cw):
    b = DEC_BATCH
    page_specs = [pl.BlockSpec((1, 2, KV_WIDTH, PAGE_SIZE),
                               functools.partial(lambda bi, pt, p: (pt[bi * N_PAGES + p], 0, 0, 0), p=p))
                  for p in range(N_PAGES)]
    full = lambda shape: pl.BlockSpec(shape, lambda bi, pt: (0,) * len(shape))
    grid_spec = pltpu.PrefetchScalarGridSpec(
        num_scalar_prefetch=1,
        grid=(b,),
        in_specs=page_specs + [full((2, CMP_STRIDE * 128, 2 * CMP_HIDDEN)), full((2, 1, CMP_HIDDEN)),
                               full((2, HEAD_DIM, CMP_HIDDEN)), full((2, HEAD_DIM, N_HALF)),
                               full((HEAD_DIM, N_HALF))],
        out_specs=pl.BlockSpec((2, 1, KV_WIDTH, N_HALF), lambda bi, pt: (0, bi, 0, 0)),
        scratch_shapes=[pltpu.VMEM((2, 2, PAST_LEN, 128), F32),
                        pltpu.VMEM((2, NSA_KV * N_HALF, CMP_STRIDE * 128), BF16)],
    )
    return pl.pallas_call(
        _compress_sample_kernel,
        grid_spec=grid_spec,
        out_shape=jax.ShapeDtypeStruct((2, b, KV_WIDTH, N_HALF), F32),
        compiler_params=_params(("arbitrary",)),
        name="compress_sample",
    )(page_flat, *([cache_t] * N_PAGES), cw["w1pair"], cw["bias"], cw["w2t"], cw["b2col"], cw["gaincol"])


def _select_blocks(imp, cur, n_blocks):
    rows = imp.shape[0]
    j = lax.broadcasted_iota(jnp.int32, (rows, 1), 0)
    valid = j <= cur
    forced = (j == 0) | (j == cur) | (j == cur - 1)
    score = jnp.where(valid, jnp.where(forced, FORCED, imp), NEG)
    rank = jnp.zeros(score.shape, F32)
    for i in range(n_blocks):
        row = score[i:i + 1, :]
        ahead = (row > score) | ((row == score) & (j > i))
        rank = rank + jnp.where(ahead, 1.0, 0.0)
    return jnp.where(valid & (rank < float(SEL_TOPK)) & (j < n_blocks), 1.0, 0.0)


def _softmax_rows(s, valid):
    s = jnp.where(valid, s, NEG)
    m = jnp.max(s, axis=-1, keepdims=True)
    e = jnp.where(valid, jnp.exp(s - m), 0.0)
    return e, jnp.maximum(jnp.sum(e, axis=-1, keepdims=True), 1e-30)


def _softmax_cols(s, valid):
    s = jnp.where(valid, s, NEG)
    m = jnp.max(s, axis=0, keepdims=True)
    e = jnp.where(valid, jnp.exp(s - m), 0.0)
    return e, jnp.maximum(jnp.sum(e, axis=0, keepdims=True), 1e-30)


def _slope(g, r):
    return float(2.0 ** (-8.0 * (g * NSA_REP + r + 1) / NSA_HEADS))


NSA_TQ = 128
SEL_CHUNK_LOG2 = 9
SEL_CHUNK = 1 << SEL_CHUNK_LOG2
WIN_KEYS = WINDOW + NSA_TQ


def _nsa_prompt_kernel(q_ref, gate_ref, kc_ref, vc_ref, ks_ref, vs_ref, kw_ref, vw_ref, cover_ref, expand_ref,
                       o_ref, ksb, vsb, kwb, vwb):
    i = pl.program_id(1)
    tq = NSA_TQ
    rows = NSA_REP * tq

    @pl.when(i == 0)
    def _():
        ksb[...] = ks_ref[0].astype(BF16)
        vsb[...] = vs_ref[0].astype(BF16)
        kwb[...] = kw_ref[0].astype(BF16)
        vwb[...] = vw_ref[0].astype(BF16)

    q = q_ref[0] * SCALE
    gates = gate_ref[0]
    kcb = kc_ref[0, 0].astype(BF16)
    vcb = vc_ref[0, 0].astype(BF16)
    q0 = i * tq
    qpos_col = q0 + lax.broadcasted_iota(jnp.int32, (tq, 1), 0)
    qpos4 = jnp.concatenate([qpos_col] * NSA_REP, axis=0)
    qpos_row = q0 + lax.broadcasted_iota(jnp.int32, (1, tq), 1)
    cur_row = lax.shift_right_logical(qpos_row, 6)
    win_start = pl.multiple_of(jnp.maximum(q0 - WINDOW, 0), NSA_TQ)
    n_chunks = lax.shift_right_logical(q0 + tq + SEL_CHUNK - 1, SEL_CHUNK_LOG2)

    groups = range(NSA_KV)
    qgs = [jnp.concatenate(
        [jnp.where(_group_mask(KV_WIDTH, g), q[:, r * KV_WIDTH:(r + 1) * KV_WIDTH], 0.0) for r in range(NSA_REP)],
        axis=0).astype(BF16) for g in groups]

    def gate_col(g, branch):
        cols = [gates[:, g * 12 + r * 3 + branch:g * 12 + r * 3 + branch + 1] for r in range(NSA_REP)]
        return jnp.concatenate(cols, axis=0)

    def biased(scores, d, mask_bias, g):
        slabs = [mask_bias - _slope(g, r) * d for r in range(NSA_REP)]
        return scores + jnp.concatenate(slabs, axis=0)

    n_idx = lax.broadcasted_iota(jnp.int32, (1, N_HALF), 1)
    dist = (qpos4 - (n_idx * CMP_STRIDE + (CMP_LEN - 1))).astype(F32)
    valid = dist >= 0.0
    o_cmp, sel_qs = [], []
    for g in groups:
        slope = jnp.concatenate([jnp.full((tq, 1), _slope(g, r), F32) for r in range(NSA_REP)], axis=0)
        e, den = _softmax_rows(_dot_nt(qgs[g], kcb) - slope * dist, valid)
        pc = e / den
        o_cmp.append(_dot(pc.astype(BF16), vcb))
        pcsum = pc[0:tq]
        for r in range(1, NSA_REP):
            pcsum = pcsum + pc[r * tq:(r + 1) * tq]
        hi, lo = _split_bf16(pcsum)
        imp = _dot_nt(cover_ref[...], hi) + _dot_nt(cover_ref[...], lo)
        sel = _select_blocks(imp, cur_row, SEQ // SEL_BLK)
        sel = jnp.concatenate([sel, jnp.zeros((128 - sel.shape[0], tq), F32)], axis=0)
        sel_qs.append(sel.T.astype(BF16))

    def sel_step(kk, carry):
        ks = pl.multiple_of(kk * SEL_CHUNK, SEL_CHUNK)
        kpos = ks + lax.broadcasted_iota(jnp.int32, (1, SEL_CHUNK), 1)
        di = qpos_col - kpos
        d = di.astype(F32)
        kblk = ksb[pl.ds(ks, SEL_CHUNK), :]
        vblk = vsb[pl.ds(ks, SEL_CHUNK), :]
        new = []
        for g in groups:
            m, l, acc = carry[g]
            chosen = _dot(sel_qs[g], expand_ref[kk])
            mask_bias = jnp.where((chosen > 0.5) & (di >= 0), 0.0, NEG)
            s = biased(_dot_nt(qgs[g], kblk), d, mask_bias, g)
            m_new = jnp.maximum(m, jnp.max(s, axis=-1, keepdims=True))
            p = jnp.exp(s - m_new)
            alpha = jnp.exp(m - m_new)
            l = alpha * l + jnp.sum(p, axis=-1, keepdims=True)
            acc = alpha * acc + _dot(p.astype(BF16), vblk)
            new.append((m_new, l, acc))
        return tuple(new)

    init = tuple((jnp.full((rows, 1), NEG, F32), jnp.zeros((rows, 1), F32), jnp.zeros((rows, KV_WIDTH), F32))
                 for _ in groups)
    sel_state = lax.fori_loop(0, n_chunks, sel_step, init)

    kpos = win_start + lax.broadcasted_iota(jnp.int32, (1, WIN_KEYS), 1)
    di = qpos_col - kpos
    d = di.astype(F32)
    win_bias = jnp.where((di >= 0) & (di < WINDOW), 0.0, NEG)
    kwin = kwb[pl.ds(win_start, WIN_KEYS), :]
    vwin = vwb[pl.ds(win_start, WIN_KEYS), :]

    out = [jnp.zeros((tq, KV_WIDTH), F32) for _ in range(NSA_REP)]
    for g in groups:
        s = biased(_dot_nt(qgs[g], kwin), d, win_bias, g)
        e = jnp.exp(s - jnp.max(s, axis=-1, keepdims=True))
        o_win = _dot(e.astype(BF16), vwin) / jnp.sum(e, axis=-1, keepdims=True)
        _, l_sel, acc_sel = sel_state[g]
        o_sel = acc_sel / jnp.maximum(l_sel, 1e-30)
        mixed = gate_col(g, 0) * o_cmp[g] + gate_col(g, 1) * o_sel + gate_col(g, 2) * o_win
        gm = _group_mask(KV_WIDTH, g)
        for r in range(NSA_REP):
            out[r] = out[r] + jnp.where(gm, mixed[r * tq:(r + 1) * tq], 0.0)

    for r in range(NSA_REP):
        o_ref[0, :, r * KV_WIDTH:(r + 1) * KV_WIDTH] = out[r].astype(o_ref.dtype)


def _nsa_prompt(nq, gates, cmp_tok, kv, win, cover_t, expand):
    b, t, tq = BATCH, SEQ, NSA_TQ
    nq3 = nq.reshape(b, t, NSA_WIDTH)
    g3 = gates.reshape(b, t, 128)
    kv3 = kv.reshape(b, t, 4 * KV_WIDTH)
    win3 = win.reshape(b, t, 2 * KV_WIDTH)
    col = lambda c: pl.BlockSpec((1, t, KV_WIDTH), lambda bi, i: (bi, 0, c))
    tok = lambda c: pl.BlockSpec((1, 1, N_HALF, KV_WIDTH), lambda bi, i: (c, bi, 0, 0))
    return pl.pallas_call(
        _nsa_prompt_kernel,
        grid=(b, t // tq),
        in_specs=[pl.BlockSpec((1, tq, NSA_WIDTH), lambda bi, i: (bi, i, 0)),
                  pl.BlockSpec((1, tq, 128), lambda bi, i: (bi, i, 0)),
                  tok(0), tok(1), col(2), col(3), col(0), col(1),
                  pl.BlockSpec(cover_t.shape, lambda bi, i: (0, 0)),
                  pl.BlockSpec(expand.shape, lambda bi, i: (0, 0, 0))],
        out_specs=pl.BlockSpec((1, tq, NSA_WIDTH), lambda bi, i: (bi, i, 0)),
        out_shape=jax.ShapeDtypeStruct((b, t, NSA_WIDTH), BF16),
        scratch_shapes=[pltpu.VMEM((t, KV_WIDTH), BF16) for _ in range(4)],
        compiler_params=_params(("parallel", "arbitrary")),
        name="nsa_prompt",
    )(nq3, g3, cmp_tok, cmp_tok, kv3, kv3, win3, win3, cover_t, expand)


SAMPLE_SEL_BLOCKS = -(-(PAST_LEN + DEC_SEQ) // SEL_BLK)
SEL_ROWS = 48


def _nsa_sample_kernel(pt_ref, *refs):
    pages = refs[:N_PAGES]
    (q_ref, gate_ref, kc_ref, vc_ref, new_ref, cw_ref, wnew_ref, cover_ref, perm_ref, o_ref, wout_ref) = refs[N_PAGES:]
    t = DEC_SEQ
    nrow = NSA_REP * NSA_KV * t
    wb = cw_ref.shape[-1]

    q = q_ref[0] * SCALE
    pieces = []
    for r in range(NSA_REP):
        qr = q[:, r * KV_WIDTH:(r + 1) * KV_WIDTH]
        for g in range(NSA_KV):
            pieces.append(jnp.where(_group_mask(KV_WIDTH, g), qr, 0.0))
    qm = jnp.concatenate(pieces, axis=0).astype(BF16)

    rowi = lax.broadcasted_iota(jnp.int32, (nrow, 1), 0)
    g_col = lax.shift_right_logical(rowi, 3) & (NSA_KV - 1)
    r_col = lax.shift_right_logical(rowi, 5)
    slope = jnp.exp2(-8.0 * (g_col * NSA_REP + r_col + 1).astype(F32) / NSA_HEADS)
    qpos = PAST_LEN + (rowi & (t - 1))
    lane = lax.broadcasted_iota(jnp.int32, (1, 128), 1)
    pad_rows = lambda a: jnp.concatenate([a, jnp.zeros((128 - t, a.shape[-1]), F32)], axis=0)

    def bias_and_mask(s, kpos_row, extra_ok=None, window=False):
        d = (qpos - kpos_row).astype(F32)
        ok = d >= 0.0
        if window:
            ok = ok & (d < float(WINDOW))
        if extra_ok is not None:
            ok = ok & extra_ok
        return s - slope * d, ok

    s, ok = bias_and_mask(_dot(qm, kc_ref[0, 0].astype(BF16)), lane * CMP_STRIDE + (CMP_LEN - 1))
    e, den = _softmax_rows(s, ok)
    pc = e / den
    o_cmp = _dot_nt(pc.astype(BF16), vc_ref[0, 0].astype(BF16))
    hi, lo = _split_bf16(pc)
    imp = _dot(hi, cover_ref[...]) + _dot(lo, cover_ref[...])
    per_rep = NSA_KV * t
    imp = imp[0:per_rep] + imp[per_rep:2 * per_rep] + imp[2 * per_rep:3 * per_rep] + imp[3 * per_rep:]
    imp_t = jnp.concatenate([imp, jnp.zeros((nrow - per_rep, 128), F32)], axis=0).T
    cur = lax.shift_right_logical(PAST_LEN + (lane & (t - 1)), 6)
    sel_t = _select_blocks(imp_t[0:SEL_ROWS], cur, SAMPLE_SEL_BLOCKS)
    sel = jnp.concatenate([sel_t, jnp.zeros((128 - SEL_ROWS, 128), F32)], axis=0).T[0:per_rep]
    sel = jnp.concatenate([sel] * NSA_REP, axis=0)

    scores, oks = [], []
    for p in range(N_PAGES):
        chosen = jnp.where(lane < SEL_BLK, sel[:, 2 * p:2 * p + 1], sel[:, 2 * p + 1:2 * p + 2])
        s, ok = bias_and_mask(_dot(qm, pages[p][0, 0].astype(BF16)), p * PAGE_SIZE + lane, chosen > 0.5)
        scores.append(s)
        oks.append(ok)
    knew = pad_rows(new_ref[0, :, 0:KV_WIDTH]).astype(BF16)
    vnew = pad_rows(new_ref[0, :, KV_WIDTH:2 * KV_WIDTH]).astype(BF16)
    last = SAMPLE_SEL_BLOCKS - 1
    s, ok = bias_and_mask(_dot_nt(qm, knew), PAST_LEN + lane, (sel[:, last:last + 1] > 0.5) & (lane < t))
    scores.append(s)
    oks.append(ok)
    e, den = _softmax_rows(jnp.concatenate(scores, axis=-1), jnp.concatenate(oks, axis=-1))
    prob = (e / den).astype(BF16)
    o_sel = _dot(prob[:, PAST_LEN:], vnew)
    for p in range(N_PAGES):
        o_sel = o_sel + _dot_nt(prob[:, p * PAGE_SIZE:(p + 1) * PAGE_SIZE], pages[p][0, 1].astype(BF16))

    lane_w = lax.broadcasted_iota(jnp.int32, (1, wb), 1)
    s_old, ok_old = bias_and_mask(_dot(qm, cw_ref[0, 0].astype(BF16)), PAST_LEN - wb + lane_w, window=True)
    kwn = pad_rows(wnew_ref[0, :, 0:KV_WIDTH]).astype(BF16)
    vwn = pad_rows(wnew_ref[0, :, KV_WIDTH:2 * KV_WIDTH]).astype(BF16)
    s_new, ok_new = bias_and_mask(_dot_nt(qm, kwn), PAST_LEN + lane, lane < t, window=True)
    e, den = _softmax_rows(jnp.concatenate([s_old, s_new], axis=-1), jnp.concatenate([ok_old, ok_new], axis=-1))
    prob = (e / den).astype(BF16)
    o_win = _dot_nt(prob[:, 0:wb], cw_ref[0, 1].astype(BF16)) + _dot(prob[:, wb:], vwn)

    ghi, glo = _split_bf16(pad_rows(gate_ref[0]))
    grow = _dot(perm_ref[...], ghi) + _dot(perm_ref[...], glo)
    base = g_col * 12 + r_col * 3

    def gate_col(branch):
        return jnp.sum(jnp.where(lane == base + branch, grow, 0.0), axis=-1, keepdims=True)

    mixed = gate_col(0) * o_cmp + gate_col(1) * o_sel + gate_col(2) * o_win
    for r in range(NSA_REP):
        acc = jnp.zeros((t, KV_WIDTH), F32)
        for g in range(NSA_KV):
            lo_row = (r * NSA_KV + g) * t
            acc = acc + jnp.where(_group_mask(KV_WIDTH, g), mixed[lo_row:lo_row + t], 0.0)
        o_ref[0, :, r * KV_WIDTH:(r + 1) * KV_WIDTH] = acc.astype(o_ref.dtype)

    for c in range(2):
        new_t = pad_rows(wnew_ref[0, :, c * KV_WIDTH:(c + 1) * KV_WIDTH]).T
        tail = jnp.concatenate([jnp.zeros((KV_WIDTH, wb - 128), F32), pltpu.roll(new_t, 128 - t, 1)], axis=-1)
        wout_ref[0, c] = jnp.where(lane_w >= wb - t, tail, pltpu.roll(cw_ref[0, c], wb - t, 1))


def _nsa_sample(nq, gates, cmp_tok, cache_t, page_flat, kv_new, win_t, win_new, cover, perm):
    b, t = DEC_BATCH, DEC_SEQ
    wb = win_t.shape[-1]
    per_b = lambda shape: pl.BlockSpec(shape, lambda bi, pt: (bi,) + (0,) * (len(shape) - 1))
    page_specs = [pl.BlockSpec((1, 2, KV_WIDTH, PAGE_SIZE),
                               functools.partial(lambda bi, pt, p: (pt[bi * N_PAGES + p], 1, 0, 0), p=p))
                  for p in range(N_PAGES)]
    tok = lambda c: pl.BlockSpec((1, 1, KV_WIDTH, N_HALF), lambda bi, pt: (c, bi, 0, 0))
    grid_spec = pltpu.PrefetchScalarGridSpec(
        num_scalar_prefetch=1,
        grid=(b,),
        in_specs=page_specs + [per_b((1, t, NSA_WIDTH)), per_b((1, t, 128)), tok(0), tok(1),
                               pl.BlockSpec((1, t, 2 * KV_WIDTH), lambda bi, pt: (bi, 0, 1)),
                               per_b((1, 2, KV_WIDTH, wb)), per_b((1, t, 2 * KV_WIDTH)),
                               pl.BlockSpec(cover.shape, lambda bi, pt: (0, 0)),
                               pl.BlockSpec(perm.shape, lambda bi, pt: (0, 0))],
        out_specs=[per_b((1, t, NSA_WIDTH)), per_b((1, 2, KV_WIDTH, wb))],
    )
    return pl.pallas_call(
        _nsa_sample_kernel,
        grid_spec=grid_spec,
        out_shape=[jax.ShapeDtypeStruct((b, t, NSA_WIDTH), BF16), jax.ShapeDtypeStruct(win_t.shape, F32)],
        compiler_params=_params(("arbitrary",)),
        name="nsa_sample",
    )(page_flat, *([cache_t] * N_PAGES), nq.reshape(b, t, NSA_WIDTH), gates.reshape(b, t, 128), cmp_tok, cmp_tok,
      kv_new.reshape(b, t, 4 * KV_WIDTH), win_t, win_new.reshape(b, t, 2 * KV_WIDTH), cover, perm)


def _out_kernel(x_ref, ret_ref, nsa_ref, wr_ref, wn_ref, o_ref, wr16, wn16):
    wr = _bf16_weights(wr_ref, wr16, 1)
    wn = _bf16_weights(wn_ref, wn16, 1)
    o_ref[...] = x_ref[...] + _dot(ret_ref[...], wr[...]) + _dot(nsa_ref[...], wn[...])


def _mixer_out(x2d, ret, nsa, w_out, w_nsa, tm=1024, tn=1024):
    n, d = x2d.shape
    return pl.pallas_call(
        _out_kernel,
        grid=(d // tn, n // tm),
        in_specs=[pl.BlockSpec((tm, tn), lambda j, i: (i, j)),
                  pl.BlockSpec((tm, RET_WIDTH), lambda j, i: (i, 0)),
                  pl.BlockSpec((tm, NSA_WIDTH), lambda j, i: (i, 0)),
                  pl.BlockSpec((RET_WIDTH, tn), lambda j, i: (0, j)),
                  pl.BlockSpec((NSA_WIDTH, tn), lambda j, i: (0, j))],
        out_specs=pl.BlockSpec((tm, tn), lambda j, i: (i, j)),
        out_shape=jax.ShapeDtypeStruct((n, d), F32),
        scratch_shapes=[pltpu.VMEM((RET_WIDTH, tn), BF16), pltpu.VMEM((NSA_WIDTH, tn), BF16)],
        compiler_params=_params(("arbitrary", "arbitrary")),
        name="mixer_out",
    )(x2d, ret, nsa, w_out, w_nsa)


FFN_TN = 512
FFN_COLS = D_FF // FFN_TN
FFN_SUB = 512


def _ffn_up_kernel(*refs, hist, step, tiles_per_seq, has_prev):
    if has_prev:
        x_ref, wa_ref, wb_ref, cwa_ref, cwb_ref, cba_ref, cbb_ref, pa_ref, pb_ref = refs[:9]
        act_ref, sta_ref, stb_ref, exta, extb, wa16, wb16 = refs[9:]
    else:
        x_ref, wa_ref, wb_ref, cwa_ref, cwb_ref, cba_ref, cbb_ref = refs[:7]
        act_ref, sta_ref, stb_ref, exta, extb, wa16, wb16 = refs[7:]
        pa_ref = pb_ref = None
    tm = x_ref.shape[0]
    first = (pl.program_id(1) % tiles_per_seq) == 0
    x = x_ref[...]
    wa_ref = _bf16_weights(wa_ref, wa16, 1)
    wb_ref = _bf16_weights(wb_ref, wb16, 1)
    halves = ((wa_ref, cwa_ref, cba_ref, pa_ref, sta_ref, exta), (wb_ref, cwb_ref, cbb_ref, pb_ref, stb_ref, extb))
    if not has_prev:
        @pl.when(first)
        def _():
            for half in halves:
                half[5][hist - 2 * step:hist, :] = jnp.zeros((2 * step, half[5].shape[-1]), F32)
    pieces = act_ref.shape[-1] // FFN_SUB
    ups = [[_dot(x, half[0][:, k * FFN_SUB:(k + 1) * FFN_SUB]) for half in halves] for k in range(pieces)]
    for k in range(pieces):
        cols = slice(k * FFN_SUB, (k + 1) * FFN_SUB)
        outs = []
        for (w_ref, cw_ref, cb_ref, p_ref, st_ref, ext), u in zip(halves, ups[k]):
            if has_prev:
                ext[hist - 2 * step:hist, cols] = p_ref[:, cols]
            ext[hist:hist + tm, cols] = u
            cw = cw_ref[:, cols]
            c = (cb_ref[:, cols] + cw[2:3] * u + cw[1:2] * ext[hist - step:hist - step + tm, cols]
                 + cw[0:1] * ext[hist - 2 * step:hist - 2 * step + tm, cols])
            outs.append(c)
            tail = ext[hist + tm - 2 * step:hist + tm, cols]
            if has_prev:
                st_ref[:, cols] = tail
            else:
                st_ref[0, :, cols] = tail
                ext[hist - 2 * step:hist, cols] = tail
        act_ref[:, cols] = (jax.nn.silu(outs[0]) * outs[1]).astype(act_ref.dtype)


def _ffn_up(hn, w_up, conv_w, conv_b, prev, *, tm, step, seqs, tiles_per_seq):
    n, d = hn.shape
    tn = FFN_TN
    hist = 8 if step == 1 else 2 * step
    has_prev = prev is not None
    half = lambda off: (lambda j, i: (0, j + off))
    in_specs = [pl.BlockSpec((tm, d), lambda j, i: (i, 0)),
                pl.BlockSpec((d, tn), half(0)), pl.BlockSpec((d, tn), half(FFN_COLS)),
                pl.BlockSpec((CONV_W, tn), half(0)), pl.BlockSpec((CONV_W, tn), half(FFN_COLS)),
                pl.BlockSpec((1, tn), half(0)), pl.BlockSpec((1, tn), half(FFN_COLS))]
    args = [hn, w_up, w_up, conv_w, conv_w, conv_b, conv_b]
    if has_prev:
        in_specs += [pl.BlockSpec((2 * step, tn), half(0)), pl.BlockSpec((2 * step, tn), half(FFN_COLS))]
        args += [prev, prev]
        st_spec = pl.BlockSpec((2 * step, tn), lambda j, i: (0, j))
        st_shape = jax.ShapeDtypeStruct((2 * step, D_FF), F32)
    else:
        st_spec = pl.BlockSpec((1, 2, tn), lambda j, i: (i // tiles_per_seq, 0, j))
        st_shape = jax.ShapeDtypeStruct((seqs, 2, D_FF), F32)
    kernel = functools.partial(_ffn_up_kernel, hist=hist, step=step, tiles_per_seq=tiles_per_seq, has_prev=has_prev)
    return pl.pallas_call(
        kernel,
        grid=(FFN_COLS, n // tm),
        in_specs=in_specs,
        out_specs=[pl.BlockSpec((tm, tn), lambda j, i: (i, j)), st_spec, st_spec],
        out_shape=[jax.ShapeDtypeStruct((n, D_FF), BF16), st_shape, st_shape],
        scratch_shapes=[pltpu.VMEM((hist + tm, tn), F32), pltpu.VMEM((hist + tm, tn), F32),
                        pltpu.VMEM((d, tn), BF16), pltpu.VMEM((d, tn), BF16)],
        compiler_params=_params(("arbitrary", "arbitrary")),
        name="ffn_up",
    )(*args)


def _ffn_down_kernel(h_ref, a_ref, w_ref, o_ref):
    o_ref[...] = h_ref[...] + _dot(a_ref[...], w_ref[...])


def _ffn_down(h, act, w_down, tm=1024, tn=512):
    n, d = h.shape
    return pl.pallas_call(
        _ffn_down_kernel,
        grid=(n // tm, d // tn),
        in_specs=[pl.BlockSpec((tm, tn), lambda i, j: (i, j)),
                  pl.BlockSpec((tm, D_FF), lambda i, j: (i, 0)),
                  pl.BlockSpec((D_FF, tn), lambda i, j: (0, j))],
        out_specs=pl.BlockSpec((tm, tn), lambda i, j: (i, j)),
        out_shape=jax.ShapeDtypeStruct((n, d), F32),
        compiler_params=_params(("parallel", "arbitrary")),
        name="ffn_down",
    )(h, act, w_down)


def _block_cover_t(n_sel, rows):
    cs = np.arange(N_HALF - 1)[None, :] * CMP_STRIDE
    js = np.arange(n_sel)[:, None] * SEL_BLK
    cov = np.clip(np.minimum(cs + CMP_LEN, js + SEL_BLK) - np.maximum(cs, js), 0, None) / CMP_LEN
    out = np.zeros((rows, N_HALF), np.float32)
    out[:n_sel, :N_HALF - 1] = cov
    return jnp.asarray(out, dtype=BF16)


def _prepare_weights(g_attn, w_in, q_norm, k_norm_cmp, k_norm_slc, k_norm_win, cmp_pe, cmp_w1, cmp_b1, cmp_w2,
                     cmp_b2, w_out):
    d = D_MODEL
    q0 = 4 * RET_WIDTH
    w_nq = w_in[:, q0:q0 + NSA_WIDTH].reshape(d, NSA_KV, NSA_REP, HEAD_DIM).transpose(0, 2, 1, 3).reshape(d, NSA_WIDTH)
    w_ng = w_in[:, q0 + NSA_WIDTH + 6 * KV_WIDTH:]
    w_ng = jnp.pad(w_ng, ((0, 0), (0, 128 - w_ng.shape[1])))
    tile4 = lambda v: jnp.tile(v, NSA_KV)
    zeros = jnp.zeros((KV_WIDTH,), F32)
    ones = jnp.ones((KV_WIDTH,), F32)
    wts = dict(
        g_attn=g_attn, w_in=w_in, w_nq=w_nq, w_ng=w_ng,
        rot_scale=jnp.concatenate([jnp.ones((RET_WIDTH,), F32),
                                   jnp.full((RET_WIDTH,), RET_DK ** -0.5, F32)]).reshape(1, -1),
        nq_gain=jnp.tile(q_norm, NSA_HEADS).reshape(1, -1), nq_mask=jnp.ones((1, NSA_WIDTH), F32),
        kv_gain=jnp.concatenate([zeros, zeros, tile4(k_norm_slc), zeros]).reshape(1, -1),
        kv_mask=jnp.concatenate([zeros, zeros, ones, zeros]).reshape(1, -1),
        win_gain=jnp.concatenate([tile4(k_norm_win), zeros]).reshape(1, -1),
        win_mask=jnp.concatenate([ones, zeros]).reshape(1, -1),
    )
    w1 = cmp_w1.reshape(2, 2, CMP_STRIDE, HEAD_DIM, CMP_HIDDEN)
    w1rep = jnp.broadcast_to(w1.transpose(0, 2, 3, 1, 4)[:, :, None],
                             (2, CMP_STRIDE, NSA_KV, HEAD_DIM, 2, CMP_HIDDEN))
    w1rep = w1rep.reshape(2, CMP_STRIDE * KV_WIDTH, 2 * CMP_HIDDEN).astype(BF16)
    w1flat = w1.reshape(2, 2, CMP_STRIDE * HEAD_DIM, CMP_HIDDEN).astype(BF16)
    pe8 = jnp.pad(cmp_pe.reshape(2, 2, 1, CMP_STRIDE * HEAD_DIM), ((0, 0), (0, 0), (0, 15), (0, 0)))
    bias = _cmp_bias(pe8, w1flat, cmp_b1.reshape(2, 1, CMP_HIDDEN))
    w1pair = jnp.broadcast_to(w1.transpose(0, 2, 3, 1, 4)[:, :, None], (2, CMP_STRIDE, 2, HEAD_DIM, 2, CMP_HIDDEN))
    col = lambda v: jnp.broadcast_to(v[:, :, None], (2, HEAD_DIM, N_HALF))
    cw = dict(
        w1pair=w1pair.reshape(2, CMP_STRIDE * 128, 2 * CMP_HIDDEN).astype(BF16),
        w2t=cmp_w2.transpose(0, 2, 1).astype(BF16), b2col=col(cmp_b2),
        gaincol=jnp.broadcast_to(k_norm_cmp[:, None], (HEAD_DIM, N_HALF)),
        w1rep=w1rep, bias=bias,
        w2rep=jnp.tile(cmp_w2, (1, 1, NSA_KV)).astype(BF16),
        b2rep=jnp.tile(cmp_b2, (1, NSA_KV)).reshape(2, 1, KV_WIDTH),
        gain=jnp.stack([tile4(k_norm_cmp), ones]).reshape(2, 1, KV_WIDTH),
    )
    w_nsa = w_out[RET_WIDTH:].reshape(NSA_KV, NSA_REP, HEAD_DIM, d).transpose(1, 0, 2, 3).reshape(NSA_WIDTH, d)
    return wts, cw, w_out, w_nsa


def kernel(x_prompt, x_sample, cache_kv, cache_win, state_ret, state_conv, page_table, g_attn, w_in, q_norm,
           k_norm_cmp, k_norm_slc, k_norm_win, cmp_pe, cmp_w1, cmp_b1, cmp_w2, cmp_b2, ret_gn, w_out, g_ffn, w_up,
           conv_w, conv_b, w_down):
    assert x_prompt.shape == (BATCH, SEQ, D_MODEL) and x_sample.shape == (DEC_BATCH, DEC_SEQ, D_MODEL)
    assert g_attn.shape[0] == 1, "single layer"
    wts, cw, w_ret, w_nsa = _prepare_weights(g_attn[0], w_in[0], q_norm[0], k_norm_cmp[0], k_norm_slc[0],
                                              k_norm_win[0], cmp_pe[0], cmp_w1[0], cmp_b1[0], cmp_w2[0],
                                              cmp_b2[0], w_out[0])
    w_up_b = w_up[0]
    w_down_b = w_down[0].astype(BF16)
    cb = conv_b[0].reshape(1, -1)
    n_p, n_s = BATCH * SEQ, DEC_BATCH * DEC_SEQ
    xp = x_prompt.reshape(n_p, D_MODEL)
    xs = x_sample.reshape(n_s, D_MODEL)

    pos_p = jnp.tile(jnp.arange(SEQ, dtype=jnp.int32), BATCH)
    qk, vg, nq, kv_p, win_p, gates = _project_all(xp, pos_p, wts)
    ret_mix, ret_state_p = _retention_prompt(qk, vg, ret_gn[0])
    kv3 = kv_p.reshape(BATCH, SEQ, 4 * KV_WIDTH)
    cmp_tok = _compress_prompt(kv3, cw)
    cover_p = _block_cover_t(SEQ // SEL_BLK, SEQ // SEL_BLK)
    key = np.arange(SEQ).reshape(SEQ // SEL_CHUNK, 1, SEL_CHUNK)
    expand = jnp.asarray((key // SEL_BLK) == np.arange(128)[None, :, None], dtype=BF16)
    nsa = _nsa_prompt(nq, gates, cmp_tok, kv_p, win_p, cover_p, expand)
    h_p = _mixer_out(xp, ret_mix.reshape(n_p, RET_WIDTH), nsa.reshape(n_p, NSA_WIDTH), w_ret, w_nsa)
    hn_p = _rmsnorm(h_p, g_ffn[0])
    act_p, st_a, st_b = _ffn_up(hn_p, w_up_b, conv_w[0], cb, None, tm=1024, step=1, seqs=BATCH,
                                tiles_per_seq=SEQ // 1024)
    y_p = _ffn_down(h_p, act_p, w_down_b)
    conv_p = jnp.concatenate([st_a, st_b], axis=-1)

    pos_s = jnp.tile(PAST_LEN + jnp.arange(DEC_SEQ, dtype=jnp.int32), DEC_BATCH)
    qk, vg, nq, kv_s, win_s, gates = _project_all(xs, pos_s, wts)
    ret_mix, ret_state_s = _retention_sample(qk, vg, state_ret[0], ret_gn[0])
    cache_t = cache_kv[0].transpose(0, 2, 3, 4, 1).reshape(cache_kv.shape[1], 4, KV_WIDTH, PAGE_SIZE)
    page_flat = page_table.reshape(-1).astype(jnp.int32)
    cmp_tok = _compress_sample(cache_t, page_flat, cw)
    wb = cache_win.shape[2]
    win_t = cache_win[0].transpose(0, 2, 3, 4, 1).reshape(DEC_BATCH, 2, KV_WIDTH, wb)
    cover_s = _block_cover_t(SAMPLE_SEL_BLOCKS, 128).T
    lanes = np.arange(128)
    perm = jnp.asarray((lanes[:, None] % DEC_SEQ) == lanes[None, :], dtype=BF16)
    nsa, win_out_t = _nsa_sample(nq, gates, cmp_tok, cache_t, page_flat, kv_s, win_t, win_s, cover_s, perm)
    h_s = _mixer_out(xs, ret_mix.reshape(n_s, RET_WIDTH), nsa.reshape(n_s, NSA_WIDTH), w_ret, w_nsa)
    to_tm = lambda a: a.reshape(DEC_BATCH, -1, a.shape[-1]).transpose(1, 0, 2).reshape(-1, a.shape[-1])
    from_tm = lambda a, t: a.reshape(t, DEC_BATCH, a.shape[-1]).transpose(1, 0, 2)
    h_tm = to_tm(h_s)
    hn_s = _rmsnorm(h_tm, g_ffn[0])
    act_s, st_a, st_b = _ffn_up(hn_s, w_up_b, conv_w[0], cb, to_tm(state_conv[0]), tm=n_s, step=DEC_BATCH,
                                seqs=DEC_BATCH, tiles_per_seq=1)
    y_s = from_tm(_ffn_down(h_tm, act_s, w_down_b), DEC_SEQ)
    conv_s = from_tm(jnp.concatenate([st_a, st_b], axis=-1), CONV_W - 1)

    win_sample = win_out_t.reshape(DEC_BATCH, 2, NSA_KV, HEAD_DIM, wb).transpose(0, 4, 1, 2, 3)
    return (
        y_p.reshape(BATCH, SEQ, D_MODEL),
        y_s,
        kv_p.reshape(1, BATCH, SEQ, 4, NSA_KV, HEAD_DIM),
        kv_s.reshape(1, DEC_BATCH, DEC_SEQ, 4, NSA_KV, HEAD_DIM),
        win_p.reshape(BATCH, SEQ, 2 * KV_WIDTH)[:, SEQ - WINDOW:].reshape(1, BATCH, WINDOW, 2, NSA_KV, HEAD_DIM),
        win_sample[None],
        ret_state_p[None],
        ret_state_s[None],
        conv_p[None],
        conv_s[None],
    )
```

```python
import functools

import numpy as np
import jax
import jax.numpy as jnp
from jax import lax
from jax.experimental import pallas as pl
from jax.experimental.pallas import tpu as pltpu

D_MODEL = 2048
BATCH = 4
SEQ = 2048
DEC_BATCH = 128
DEC_SEQ = 8
PAST_LEN = 2048
PAGE_SIZE = 128
N_PAGES = PAST_LEN // PAGE_SIZE
RET_HEADS = 4
RET_DK = 256
RET_DV = 256
RET_WIDTH = RET_HEADS * RET_DV
RET_CHUNK = 128
NSA_HEADS = 16
NSA_KV = 4
NSA_REP = NSA_HEADS // NSA_KV
HEAD_DIM = 64
NSA_WIDTH = NSA_HEADS * HEAD_DIM
KV_WIDTH = NSA_KV * HEAD_DIM
CMP_LEN = 32
CMP_STRIDE = 16
CMP_HIDDEN = 256
SEL_BLK = 64
SEL_TOPK = 16
WINDOW = 512
SCALE = HEAD_DIM ** -0.5
D_FF = 5632
CONV_W = 3
EPS = 1e-6

N_HALF = PAST_LEN // CMP_STRIDE
NEG = -1e30
FORCED = 1e30
VMEM_LIMIT = 56 * 1024 * 1024

BF16 = jnp.bfloat16
F32 = jnp.float32


def _params(semantics, vmem=VMEM_LIMIT):
    return pltpu.CompilerParams(dimension_semantics=semantics, vmem_limit_bytes=vmem)


def _dot(a, b):
    return jnp.dot(a, b, preferred_element_type=F32)


def _dot_nt(a, b):
    return lax.dot_general(a, b, (((1,), (1,)), ((), ())), preferred_element_type=F32)


def _split_bf16(x):
    hi = x.astype(BF16)
    lo = (x - hi.astype(F32)).astype(BF16)
    return hi, lo


def _group_mask(width, g):
    lane = lax.broadcasted_iota(jnp.int32, (1, width), 1)
    return (lane >= g * HEAD_DIM) & (lane < (g + 1) * HEAD_DIM)


def _rms_groups64(z, gain):
    width = z.shape[-1]
    outs = []
    for k in range(width // 128):
        zk = z[:, k * 128:(k + 1) * 128]
        zz = zk * zk
        lane = lax.broadcasted_iota(jnp.int32, (1, 128), 1)
        lo_half = lane < HEAD_DIM
        s_lo = jnp.sum(jnp.where(lo_half, zz, 0.0), axis=-1, keepdims=True)
        s_hi = jnp.sum(jnp.where(lo_half, 0.0, zz), axis=-1, keepdims=True)
        ms = jnp.where(lo_half, s_lo, s_hi) * (1.0 / HEAD_DIM)
        outs.append(zk * lax.rsqrt(ms + EPS))
    y = outs[0] if len(outs) == 1 else jnp.concatenate(outs, axis=-1)
    return y * gain


def _rmsnorm_kernel(x_ref, g_ref, o_ref):
    x = x_ref[...]
    ms = jnp.mean(x * x, axis=-1, keepdims=True)
    o_ref[...] = (x * lax.rsqrt(ms + EPS) * g_ref[...]).astype(o_ref.dtype)


def _rmsnorm(x, g, tm=512):
    n, d = x.shape
    return pl.pallas_call(
        _rmsnorm_kernel,
        grid=(n // tm,),
        in_specs=[pl.BlockSpec((tm, d), lambda i: (i, 0)), pl.BlockSpec((1, d), lambda i: (0, 0))],
        out_specs=pl.BlockSpec((tm, d), lambda i: (i, 0)),
        out_shape=jax.ShapeDtypeStruct((n, d), BF16),
        compiler_params=_params(("parallel",)),
        name="rmsnorm",
    )(x, g.reshape(1, d))


def _bf16_weights(w_ref, wb_ref, row_axis):
    @pl.when(pl.program_id(row_axis) == 0)
    def _():
        wb_ref[...] = w_ref[...].astype(BF16)
    return wb_ref


def _proj_plain_kernel(x_ref, w_ref, o_ref, wb_ref):
    o_ref[...] = _dot_nt(x_ref[...], _bf16_weights(w_ref, wb_ref, 1)[...]).astype(o_ref.dtype)


def _proj_sigmoid_kernel(x_ref, w_ref, o_ref, wb_ref):
    o_ref[...] = jax.nn.sigmoid(_dot_nt(x_ref[...], _bf16_weights(w_ref, wb_ref, 1)[...]))


def _sub_dots(x_ref, wt_ref, width):
    x = x_ref[...]
    return [_dot_nt(x, wt_ref[k * width:(k + 1) * width, :]) for k in range(wt_ref.shape[0] // width)]


def _proj_norm_kernel(x_ref, w_ref, gain_ref, nmask_ref, o_ref, *rest):
    wb_ref = rest[-1]
    for k, z in enumerate(_sub_dots(x_ref, _bf16_weights(w_ref, wb_ref, 1), KV_WIDTH)):
        cols = slice(k * KV_WIDTH, (k + 1) * KV_WIDTH)
        y = jnp.where(nmask_ref[:, cols] > 0.5, _rms_groups64(z, gain_ref[:, cols]), z)
        o_ref[:, cols] = y
        if len(rest) == 2:
            rest[0][0, cols, :] = y.T


def _proj_rot_kernel(x_ref, w_ref, cos_ref, sin_ref, scale_ref, o_ref, wb_ref):
    c = cos_ref[...]
    s = sin_ref[...]
    for hh, z in enumerate(_sub_dots(x_ref, _bf16_weights(w_ref, wb_ref, 1), RET_DK)):
        lo, mid, hi = hh * RET_DK, hh * RET_DK + 128, (hh + 1) * RET_DK
        x1 = z[:, :128]
        x2 = z[:, 128:]
        o_ref[:, lo:mid] = (x1 * c - x2 * s) * scale_ref[:, lo:mid]
        o_ref[:, mid:hi] = (x2 * c + x1 * s) * scale_ref[:, mid:hi]


PROJ_TN = 1024


def _proj(xn, wt, row0, c, kernel, row_extras=(), col_extras=(), out_dtype=F32, tm=1024, name="proj",
          transposed_seq=None):
    n, d = xn.shape
    tn = min(PROJ_TN, c)
    off = row0 // tn
    assert row0 == off * tn and c % tn == 0
    extra_specs = ([pl.BlockSpec((tm, 128), lambda j, i: (i, 0)) for _ in row_extras]
                   + [pl.BlockSpec((1, tn), lambda j, i: (0, j)) for _ in col_extras])
    out_specs = pl.BlockSpec((tm, tn), lambda j, i: (i, j))
    out_shape = jax.ShapeDtypeStruct((n, c), out_dtype)
    if transposed_seq is not None:
        per_seq = transposed_seq // tm
        out_specs = [out_specs, pl.BlockSpec((1, tn, tm), lambda j, i: (i // per_seq, j, i % per_seq))]
        out_shape = [out_shape, jax.ShapeDtypeStruct((n // transposed_seq, c, transposed_seq), out_dtype)]
    return pl.pallas_call(
        kernel,
        grid=(c // tn, n // tm),
        in_specs=[pl.BlockSpec((tm, d), lambda j, i: (i, 0)), pl.BlockSpec((tn, d), lambda j, i: (j + off, 0))]
        + extra_specs,
        out_specs=out_specs,
        out_shape=out_shape,
        scratch_shapes=[pltpu.VMEM((tn, d), BF16)],
        compiler_params=_params(("arbitrary", "arbitrary")),
        name=name,
    )(xn, wt, *row_extras, *col_extras)


def _project_all(x2d, pos, wts, tm=1024, kv_transposed_seq=None):
    xn = _rmsnorm(x2d, wts["g_attn"])
    half = RET_DK // 2
    inv = 1.0 / (10000.0 ** jnp.linspace(0.0, 1.0, half, dtype=F32))
    ang = pos.astype(F32)[:, None] * inv[None, :]
    cos = jnp.cos(ang)
    sin = jnp.sin(ang)
    w_in_t = wts["w_in_t"]
    rw = 2 * RET_WIDTH
    qk = _proj(xn, w_in_t, 0, rw, _proj_rot_kernel, (cos, sin), (wts["rot_scale"],), tm=tm, name="proj_rot")
    vg = _proj(xn, w_in_t, rw, rw, _proj_plain_kernel, tm=tm, name="proj_vg")
    nq = _proj(xn, wts["w_nq_t"], 0, NSA_WIDTH, _proj_norm_kernel, (), (wts["nq_gain"], wts["nq_mask"]), tm=tm,
               name="proj_nq")
    kv0 = 2 * rw + NSA_WIDTH
    kv = _proj(xn, w_in_t, kv0, 4 * KV_WIDTH, _proj_norm_kernel, (), (wts["kv_gain"], wts["kv_mask"]), tm=tm,
               name="proj_kv", transposed_seq=kv_transposed_seq)
    win = _proj(xn, w_in_t, kv0 + 4 * KV_WIDTH, 2 * KV_WIDTH, _proj_norm_kernel, (),
                (wts["win_gain"], wts["win_mask"]), tm=tm, name="proj_win")
    gates = _proj(xn, wts["w_ng_t"], 0, 128, _proj_sigmoid_kernel, tm=tm, name="proj_gate")
    return qk, vg, nq, kv, win, gates


def _retention_step(q, k, v, state, dmask, xi, zeta, gch):
    qb = q.astype(BF16)
    kb = k.astype(BF16)
    vb = v.astype(BF16)
    s = _dot_nt(qb, kb) * dmask
    o = _dot(s.astype(BF16), vb) + _dot(qb, state.astype(BF16)) * xi
    kz = (k * zeta).astype(BF16)
    new_state = state * gch + _dot(kz.T, vb)
    return o, new_state


def _ret_mix(o, gn, gate):
    ms = jnp.mean(o * o, axis=-1, keepdims=True)
    return o * lax.rsqrt(ms + EPS) * gn * jax.nn.silu(gate)


def _ret_prompt_kernel(qk_ref, vg_ref, dmask_ref, xi_ref, zeta_ref, gch_ref, gn_ref, mix_ref, state_ref):
    @pl.when(pl.program_id(1) == 0)
    def _():
        state_ref[...] = jnp.zeros_like(state_ref)

    for h in range(RET_HEADS):
        lo, hi = h * RET_DK, (h + 1) * RET_DK
        o, new_state = _retention_step(qk_ref[0, :, lo:hi], qk_ref[0, :, RET_WIDTH + lo:RET_WIDTH + hi],
                                       vg_ref[0, :, lo:hi], state_ref[0, h], dmask_ref[h], xi_ref[h],
                                       zeta_ref[h], gch_ref[h])
        state_ref[0, h] = new_state
        mix_ref[0, :, lo:hi] = _ret_mix(o, gn_ref[h], vg_ref[0, :, RET_WIDTH + lo:RET_WIDTH + hi]).astype(mix_ref.dtype)


def _ret_tables(chunk):
    h = jnp.arange(RET_HEADS, dtype=F32)
    lg = jnp.log(1.0 - jnp.exp2(-5.0 - h))
    i = jnp.arange(chunk, dtype=F32)
    diff = i[:, None] - i[None, :]
    dmask = jnp.where(diff >= 0, jnp.exp(lg[:, None, None] * jnp.maximum(diff, 0.0)), 0.0)
    xi = jnp.exp(lg[:, None] * (i[None, :] + 1.0))
    zeta = jnp.exp(lg[:, None] * (chunk - 1.0 - i[None, :]))
    gch = jnp.exp(lg * chunk)
    bc = lambda a: jnp.broadcast_to(a[:, :, None], (RET_HEADS, chunk, RET_DV))
    return dmask, bc(xi), bc(zeta), jnp.broadcast_to(gch[:, None, None], (RET_HEADS, 1, RET_DV))


def _retention_prompt(qk, vg, gn):
    b, t = BATCH, SEQ
    c = RET_CHUNK
    qk3 = qk.reshape(b, t, 2 * RET_WIDTH)
    vg3 = vg.reshape(b, t, 2 * RET_WIDTH)
    dmask, xi, zeta, gch = _ret_tables(c)
    rows = pl.BlockSpec((1, c, 2 * RET_WIDTH), lambda bi, ci: (bi, ci, 0))
    full = lambda shape: pl.BlockSpec(shape, lambda bi, ci: (0,) * len(shape))
    return pl.pallas_call(
        _ret_prompt_kernel,
        grid=(b, t // c),
        in_specs=[rows, rows, full((RET_HEADS, c, c)), full((RET_HEADS, c, RET_DV)), full((RET_HEADS, c, RET_DV)),
                  full((RET_HEADS, 1, RET_DV)), full((RET_HEADS, 1, RET_DV))],
        out_specs=[pl.BlockSpec((1, c, RET_WIDTH), lambda bi, ci: (bi, ci, 0)),
                   pl.BlockSpec((1, RET_HEADS, RET_DK, RET_DV), lambda bi, ci: (bi, 0, 0, 0))],
        out_shape=[jax.ShapeDtypeStruct((b, t, RET_WIDTH), BF16),
                   jax.ShapeDtypeStruct((b, RET_HEADS, RET_DK, RET_DV), F32)],
        compiler_params=_params(("parallel", "arbitrary")),
        name="retention_prompt",
    )(qk3, vg3, dmask, xi, zeta, gch, gn.reshape(RET_HEADS, 1, RET_DV))


def _ret_sample_kernel(qk_ref, vg_ref, s0_ref, dmask_ref, xi_ref, zeta_ref, gch_ref, gn_ref, mix_ref, state_ref):
    t = qk_ref.shape[1]
    pad = lambda a: jnp.concatenate([a, jnp.zeros((RET_PAD - t, a.shape[-1]), F32)], axis=0)
    for s in range(qk_ref.shape[0]):
        for h in range(RET_HEADS):
            lo, hi = h * RET_DK, (h + 1) * RET_DK
            gate = vg_ref[s, :, RET_WIDTH + lo:RET_WIDTH + hi]
            o, new_state = _retention_step(pad(qk_ref[s, :, lo:hi]), pad(qk_ref[s, :, RET_WIDTH + lo:RET_WIDTH + hi]),
                                           pad(vg_ref[s, :, lo:hi]), s0_ref[s, h], dmask_ref[h], xi_ref[h],
                                           zeta_ref[h], gch_ref[h])
            state_ref[s, h] = new_state
            mix_ref[s, :, lo:hi] = _ret_mix(o[0:t], gn_ref[h], gate).astype(mix_ref.dtype)


RET_PAD = 128
RET_SEQS_PER_STEP = 4


def _retention_sample(qk, vg, state0, gn):
    b, t = DEC_BATCH, DEC_SEQ
    dmask, xi, zeta, gch = _ret_tables(t)
    dmask = jnp.pad(dmask, ((0, 0), (0, RET_PAD - t), (0, RET_PAD - t)))
    xi = jnp.pad(xi, ((0, 0), (0, RET_PAD - t), (0, 0)))
    zeta = jnp.pad(zeta, ((0, 0), (0, RET_PAD - t), (0, 0)))
    full = lambda shape: pl.BlockSpec(shape, lambda bi: (0,) * len(shape))
    return pl.pallas_call(
        _ret_sample_kernel,
        grid=(b // RET_SEQS_PER_STEP,),
        in_specs=[pl.BlockSpec((RET_SEQS_PER_STEP, t, 2 * RET_WIDTH), lambda bi: (bi, 0, 0)),
                  pl.BlockSpec((RET_SEQS_PER_STEP, t, 2 * RET_WIDTH), lambda bi: (bi, 0, 0)),
                  pl.BlockSpec((RET_SEQS_PER_STEP, RET_HEADS, RET_DK, RET_DV), lambda bi: (bi, 0, 0, 0)),
                  full((RET_HEADS, RET_PAD, RET_PAD)), full((RET_HEADS, RET_PAD, RET_DV)), full((RET_HEADS, RET_PAD, RET_DV)),
                  full((RET_HEADS, 1, RET_DV)), full((RET_HEADS, 1, RET_DV))],
        out_specs=[pl.BlockSpec((RET_SEQS_PER_STEP, t, RET_WIDTH), lambda bi: (bi, 0, 0)),
                   pl.BlockSpec((RET_SEQS_PER_STEP, RET_HEADS, RET_DK, RET_DV), lambda bi: (bi, 0, 0, 0))],
        out_shape=[jax.ShapeDtypeStruct((b, t, RET_WIDTH), BF16),
                   jax.ShapeDtypeStruct((b, RET_HEADS, RET_DK, RET_DV), F32)],
        compiler_params=_params(("parallel",)),
        name="retention_sample",
    )(qk.reshape(b, t, -1), vg.reshape(b, t, -1), state0, dmask, xi, zeta, gch, gn.reshape(RET_HEADS, 1, RET_DV))


def _cmp_bias_kernel(pe_ref, w_ref, b1_ref, o_ref):
    acc = b1_ref[0]
    for j in range(2):
        acc = acc + _dot(pe_ref[0, j].astype(BF16), w_ref[0, j])[0:1]
    o_ref[0] = acc


def _cmp_bias(pe8, w1flat, b1):
    return pl.pallas_call(
        _cmp_bias_kernel,
        grid=(2,),
        in_specs=[pl.BlockSpec((1, 2, 16, CMP_STRIDE * HEAD_DIM), lambda c: (c, 0, 0, 0)),
                  pl.BlockSpec((1, 2, CMP_STRIDE * HEAD_DIM, CMP_HIDDEN), lambda c: (c, 0, 0, 0)),
                  pl.BlockSpec((1, 1, CMP_HIDDEN), lambda c: (c, 0, 0))],
        out_specs=pl.BlockSpec((1, 1, CMP_HIDDEN), lambda c: (c, 0, 0)),
        out_shape=jax.ShapeDtypeStruct((2, 1, CMP_HIDDEN), F32),
        compiler_params=_params(("parallel",)),
        name="cmp_bias",
    )(pe8, w1flat, b1)


def _compress_body(load_rows, w1_ref, bias_ref, w2_ref, b2_ref, gain_ref, o_ref, xm_ref):
    for l in range(CMP_STRIDE):
        xl = load_rows(l)
        for g in range(NSA_KV):
            xm_ref[g * N_HALF:(g + 1) * N_HALF, l * KV_WIDTH:(l + 1) * KV_WIDTH] = jnp.where(
                _group_mask(KV_WIDTH, g), xl, 0.0).astype(BF16)
    acc = _dot(xm_ref[...], w1_ref[0])
    h0 = acc[:, :CMP_HIDDEN]
    h1 = pltpu.roll(acc[:, CMP_HIDDEN:], 4 * N_HALF - 1, 0)
    hid = jax.nn.silu(h0 + h1 + bias_ref[0])
    out = _dot(hid.astype(BF16), w2_ref[0]) + b2_ref[0]
    res = jnp.zeros((N_HALF, KV_WIDTH), F32)
    for g in range(NSA_KV):
        res = res + jnp.where(_group_mask(KV_WIDTH, g), out[g * N_HALF:(g + 1) * N_HALF], 0.0)
    normed = _rms_groups64(res, gain_ref[0])
    o_ref[0, 0] = jnp.where(pl.program_id(0) == 0, normed, res)


def _compress_prompt_kernel(lo_ref, hi_ref, w1_ref, bias_ref, w2_ref, b2_ref, gain_ref, o_ref, xm_ref):
    load = lambda l: jnp.concatenate(
        [r[0, pl.ds(l, N_HALF, stride=CMP_STRIDE), :] for r in (lo_ref, hi_ref)], axis=-1)
    _compress_body(load, w1_ref, bias_ref, w2_ref, b2_ref, gain_ref, o_ref, xm_ref)


def _compress_sample_kernel(pt_ref, *refs):
    pages = refs[:N_PAGES]
    w1_ref, bias_ref, w2t_ref, b2_ref, gain_ref, o_ref, t_ref, xm_ref = refs[N_PAGES:]
    lane = lax.broadcasted_iota(jnp.int32, (1, 128), 1)
    for c in range(2):
        for p in range(N_PAGES):
            xt = pages[p][0, c]
            for pair in range(2):
                t_ref[c, pair, p * PAGE_SIZE:(p + 1) * PAGE_SIZE, :] = xt[pair * 128:(pair + 1) * 128, :].T
        for l in range(CMP_STRIDE):
            for pair in range(2):
                rows = t_ref[c, pair, pl.ds(l, N_HALF, stride=CMP_STRIDE), :]
                for member in range(2):
                    g = 2 * pair + member
                    keep = (lane >= member * HEAD_DIM) & (lane < (member + 1) * HEAD_DIM)
                    xm_ref[c, g * N_HALF:(g + 1) * N_HALF, l * 128:(l + 1) * 128] = (
                        jnp.where(keep, rows, 0.0).astype(BF16))
        acc = _dot(xm_ref[c], w1_ref[c])
        h0 = acc[:, :CMP_HIDDEN]
        h1 = pltpu.roll(acc[:, CMP_HIDDEN:], 4 * N_HALF - 1, 0)
        hid = jax.nn.silu(h0 + h1 + bias_ref[c]).astype(BF16)
        outs = []
        for g in range(NSA_KV):
            og = _dot_nt(w2t_ref[c], hid[g * N_HALF:(g + 1) * N_HALF]) + b2_ref[c]
            if c == 0:
                og = og * lax.rsqrt(jnp.mean(og * og, axis=0, keepdims=True) + EPS) * gain_ref[...]
            outs.append(og)
        o_ref[c, 0] = jnp.concatenate(outs, axis=0)


def _cmp_weight_specs(nargs):
    cmap = (lambda c, b: (c, 0, 0)) if nargs == 2 else (lambda c, b, pt: (c, 0, 0))
    return [pl.BlockSpec((1, CMP_STRIDE * KV_WIDTH, 2 * CMP_HIDDEN), cmap),
            pl.BlockSpec((1, 1, CMP_HIDDEN), cmap),
            pl.BlockSpec((1, CMP_HIDDEN, KV_WIDTH), cmap),
            pl.BlockSpec((1, 1, KV_WIDTH), cmap),
            pl.BlockSpec((1, 1, KV_WIDTH), cmap)]


def _compress_prompt(kv3, cw):
    b = kv3.shape[0]
    return pl.pallas_call(
        _compress_prompt_kernel,
        grid=(2, b),
        in_specs=[pl.BlockSpec((1, SEQ, 128), lambda c, bi: (bi, 0, 2 * c)),
                  pl.BlockSpec((1, SEQ, 128), lambda c, bi: (bi, 0, 2 * c + 1))] + _cmp_weight_specs(2),
        out_specs=pl.BlockSpec((1, 1, N_HALF, KV_WIDTH), lambda c, bi: (c, bi, 0, 0)),
        out_shape=jax.ShapeDtypeStruct((2, b, N_HALF, KV_WIDTH), F32),
        scratch_shapes=[pltpu.VMEM((NSA_KV * N_HALF, CMP_STRIDE * KV_WIDTH), BF16)],
        compiler_params=_params(("arbitrary", "arbitrary")),
        name="compress_prompt",
    )(kv3, kv3, cw["w1rep"], cw["bias"], cw["w2rep"], cw["b2rep"], cw["gain"])


def _compress_sample(cache_t, page_flat, cw):
    b = DEC_BATCH
    page_specs = [pl.BlockSpec((1, 2, KV_WIDTH, PAGE_SIZE),
                               functools.partial(lambda bi, pt, p: (pt[bi * N_PAGES + p], 0, 0, 0), p=p))
                  for p in range(N_PAGES)]
    full = lambda shape: pl.BlockSpec(shape, lambda bi, pt: (0,) * len(shape))
    grid_spec = pltpu.PrefetchScalarGridSpec(
        num_scalar_prefetch=1,
        grid=(b,),
        in_specs=page_specs + [full((2, CMP_STRIDE * 128, 2 * CMP_HIDDEN)), full((2, 1, CMP_HIDDEN)),
                               full((2, HEAD_DIM, CMP_HIDDEN)), full((2, HEAD_DIM, N_HALF)),
                               full((HEAD_DIM, N_HALF))],
        out_specs=pl.BlockSpec((2, 1, KV_WIDTH, N_HALF), lambda bi, pt: (0, bi, 0, 0)),
        scratch_shapes=[pltpu.VMEM((2, 2, PAST_LEN, 128), F32),
                        pltpu.VMEM((2, NSA_KV * N_HALF, CMP_STRIDE * 128), BF16)],
    )
    return pl.pallas_call(
        _compress_sample_kernel,
        grid_spec=grid_spec,
        out_shape=jax.ShapeDtypeStruct((2, b, KV_WIDTH, N_HALF), F32),
        compiler_params=_params(("arbitrary",)),
        name="compress_sample",
    )(page_flat, *([cache_t] * N_PAGES), cw["w1pair"], cw["bias"], cw["w2t"], cw["b2col"], cw["gaincol"])


def _select_blocks(imp, cur, n_blocks):
    rows = imp.shape[0]
    j = lax.broadcasted_iota(jnp.int32, (rows, 1), 0)
    valid = j <= cur
    forced = (j == 0) | (j == cur) | (j == cur - 1)
    score = jnp.where(valid, jnp.where(forced, FORCED, imp), NEG)
    rank = jnp.zeros(score.shape, F32)
    for i in range(n_blocks):
        row = score[i:i + 1, :]
        ahead = (row > score) | ((row == score) & (j > i))
        rank = rank + jnp.where(ahead, 1.0, 0.0)
    return jnp.where(valid & (rank < float(SEL_TOPK)) & (j < n_blocks), 1.0, 0.0)


def _softmax_rows(s, valid):
    s = jnp.where(valid, s, NEG)
    m = jnp.max(s, axis=-1, keepdims=True)
    e = jnp.where(valid, jnp.exp(s - m), 0.0)
    return e, jnp.maximum(jnp.sum(e, axis=-1, keepdims=True), 1e-30)


def _softmax_cols(s, valid):
    s = jnp.where(valid, s, NEG)
    m = jnp.max(s, axis=0, keepdims=True)
    e = jnp.where(valid, jnp.exp(s - m), 0.0)
    return e, jnp.maximum(jnp.sum(e, axis=0, keepdims=True), 1e-30)


def _slope(g, r):
    return float(2.0 ** (-8.0 * (g * NSA_REP + r + 1) / NSA_HEADS))


NSA_TQ = 128
SEL_CHUNK_LOG2 = 9
SEL_CHUNK = 1 << SEL_CHUNK_LOG2
WIN_KEYS = WINDOW + NSA_TQ


def _nsa_prompt_kernel(q_ref, gate_ref, kc_ref, vc_ref, ks_ref, vs_ref, kw_ref, vw_ref, cover_ref, expand_ref,
                       o_ref, ksb, vsb, kwb, vwb):
    i = pl.program_id(1)
    tq = NSA_TQ
    rows = NSA_REP * tq

    @pl.when(i == 0)
    def _():
        ksb[...] = ks_ref[0].astype(BF16)
        vsb[...] = vs_ref[0].astype(BF16)
        kwb[...] = kw_ref[0].astype(BF16)
        vwb[...] = vw_ref[0].astype(BF16)

    q = q_ref[0] * SCALE
    gates = gate_ref[0]
    kcb = kc_ref[0, 0].astype(BF16)
    vcb = vc_ref[0, 0].astype(BF16)
    q0 = i * tq
    qpos_col = q0 + lax.broadcasted_iota(jnp.int32, (tq, 1), 0)
    qpos4 = jnp.concatenate([qpos_col] * NSA_REP, axis=0)
    qpos_row = q0 + lax.broadcasted_iota(jnp.int32, (1, tq), 1)
    cur_row = lax.shift_right_logical(qpos_row, 6)
    win_start = pl.multiple_of(jnp.maximum(q0 - WINDOW, 0), NSA_TQ)
    n_chunks = lax.shift_right_logical(q0 + tq + SEL_CHUNK - 1, SEL_CHUNK_LOG2)

    groups = range(NSA_KV)
    qgs = [jnp.concatenate(
        [jnp.where(_group_mask(KV_WIDTH, g), q[:, r * KV_WIDTH:(r + 1) * KV_WIDTH], 0.0) for r in range(NSA_REP)],
        axis=0).astype(BF16) for g in groups]

    def gate_col(g, branch):
        cols = [gates[:, g * 12 + r * 3 + branch:g * 12 + r * 3 + branch + 1] for r in range(NSA_REP)]
        return jnp.concatenate(cols, axis=0)

    def biased(scores, d, mask_bias, g):
        slabs = [mask_bias - _slope(g, r) * d for r in range(NSA_REP)]
        return scores + jnp.concatenate(slabs, axis=0)

    n_idx = lax.broadcasted_iota(jnp.int32, (1, N_HALF), 1)
    dist = (qpos4 - (n_idx * CMP_STRIDE + (CMP_LEN - 1))).astype(F32)
    valid = dist >= 0.0
    o_cmp, sel_qs = [], []
    for g in groups:
        slope = jnp.concatenate([jnp.full((tq, 1), _slope(g, r), F32) for r in range(NSA_REP)], axis=0)
        e, den = _softmax_rows(_dot_nt(qgs[g], kcb) - slope * dist, valid)
        pc = e / den
        o_cmp.append(_dot(pc.astype(BF16), vcb))
        pcsum = pc[0:tq]
        for r in range(1, NSA_REP):
            pcsum = pcsum + pc[r * tq:(r + 1) * tq]
        hi, lo = _split_bf16(pcsum)
        imp = _dot_nt(cover_ref[...], hi) + _dot_nt(cover_ref[...], lo)
        sel = _select_blocks(imp, cur_row, SEQ // SEL_BLK)
        sel = jnp.concatenate([sel, jnp.zeros((128 - sel.shape[0], tq), F32)], axis=0)
        sel_qs.append(sel.T.astype(BF16))

    def sel_step(kk, carry):
        ks = pl.multiple_of(kk * SEL_CHUNK, SEL_CHUNK)
        kpos = ks + lax.broadcasted_iota(jnp.int32, (1, SEL_CHUNK), 1)
        di = qpos_col - kpos
        d = di.astype(F32)
        kblk = ksb[pl.ds(ks, SEL_CHUNK), :]
        vblk = vsb[pl.ds(ks, SEL_CHUNK), :]
        new = []
        for g in groups:
            m, l, acc = carry[g]
            chosen = _dot(sel_qs[g], expand_ref[kk])
            mask_bias = jnp.where((chosen > 0.5) & (di >= 0), 0.0, NEG)
            s = biased(_dot_nt(qgs[g], kblk), d, mask_bias, g)
            m_new = jnp.maximum(m, jnp.max(s, axis=-1, keepdims=True))
            p = jnp.exp(s - m_new)
            alpha = jnp.exp(m - m_new)
            l = alpha * l + jnp.sum(p, axis=-1, keepdims=True)
            acc = alpha * acc + _dot(p.astype(BF16), vblk)
            new.append((m_new, l, acc))
        return tuple(new)

    init = tuple((jnp.full((rows, 1), NEG, F32), jnp.zeros((rows, 1), F32), jnp.zeros((rows, KV_WIDTH), F32))
                 for _ in groups)
    sel_state = lax.fori_loop(0, n_chunks, sel_step, init)

    kpos = win_start + lax.broadcasted_iota(jnp.int32, (1, WIN_KEYS), 1)
    di = qpos_col - kpos
    d = di.astype(F32)
    win_bias = jnp.where((di >= 0) & (di < WINDOW), 0.0, NEG)
    kwin = kwb[pl.ds(win_start, WIN_KEYS), :]
    vwin = vwb[pl.ds(win_start, WIN_KEYS), :]

    out = [jnp.zeros((tq, KV_WIDTH), F32) for _ in range(NSA_REP)]
    for g in groups:
        s = biased(_dot_nt(qgs[g], kwin), d, win_bias, g)
        e = jnp.exp(s - jnp.max(s, axis=-1, keepdims=True))
        o_win = _dot(e.astype(BF16), vwin) / jnp.sum(e, axis=-1, keepdims=True)
        _, l_sel, acc_sel = sel_state[g]
        o_sel = acc_sel / jnp.maximum(l_sel, 1e-30)
        mixed = gate_col(g, 0) * o_cmp[g] + gate_col(g, 1) * o_sel + gate_col(g, 2) * o_win
        gm = _group_mask(KV_WIDTH, g)
        for r in range(NSA_REP):
            out[r] = out[r] + jnp.where(gm, mixed[r * tq:(r + 1) * tq], 0.0)

    for r in range(NSA_REP):
        o_ref[0, :, r * KV_WIDTH:(r + 1) * KV_WIDTH] = out[r].astype(o_ref.dtype)


def _nsa_prompt(nq, gates, cmp_tok, kv, win, cover_t, expand):
    b, t, tq = BATCH, SEQ, NSA_TQ
    nq3 = nq.reshape(b, t, NSA_WIDTH)
    g3 = gates.reshape(b, t, 128)
    kv3 = kv.reshape(b, t, 4 * KV_WIDTH)
    win3 = win.reshape(b, t, 2 * KV_WIDTH)
    col = lambda c: pl.BlockSpec((1, t, KV_WIDTH), lambda bi, i: (bi, 0, c))
    tok = lambda c: pl.BlockSpec((1, 1, N_HALF, KV_WIDTH), lambda bi, i: (c, bi, 0, 0))
    return pl.pallas_call(
        _nsa_prompt_kernel,
        grid=(b, t // tq),
        in_specs=[pl.BlockSpec((1, tq, NSA_WIDTH), lambda bi, i: (bi, i, 0)),
                  pl.BlockSpec((1, tq, 128), lambda bi, i: (bi, i, 0)),
                  tok(0), tok(1), col(2), col(3), col(0), col(1),
                  pl.BlockSpec(cover_t.shape, lambda bi, i: (0, 0)),
                  pl.BlockSpec(expand.shape, lambda bi, i: (0, 0, 0))],
        out_specs=pl.BlockSpec((1, tq, NSA_WIDTH), lambda bi, i: (bi, i, 0)),
        out_shape=jax.ShapeDtypeStruct((b, t, NSA_WIDTH), BF16),
        scratch_shapes=[pltpu.VMEM((t, KV_WIDTH), BF16) for _ in range(4)],
        compiler_params=_params(("parallel", "arbitrary")),
        name="nsa_prompt",
    )(nq3, g3, cmp_tok, cmp_tok, kv3, kv3, win3, win3, cover_t, expand)


SAMPLE_SEL_BLOCKS = -(-(PAST_LEN + DEC_SEQ) // SEL_BLK)
SEL_ROWS = 48


def _nsa_sample_kernel(pt_ref, *refs):
    pages = refs[:N_PAGES]
    (q_ref, gate_ref, kc_ref, vc_ref, new_ref, cw_ref, wnew_ref, cover_ref, perm_ref, o_ref, wout_ref) = refs[N_PAGES:]
    t = DEC_SEQ
    nrow = NSA_REP * NSA_KV * t
    wb = cw_ref.shape[-1]

    q = q_ref[0] * SCALE
    pieces = []
    for r in range(NSA_REP):
        qr = q[:, r * KV_WIDTH:(r + 1) * KV_WIDTH]
        for g in range(NSA_KV):
            pieces.append(jnp.where(_group_mask(KV_WIDTH, g), qr, 0.0))
    qm = jnp.concatenate(pieces, axis=0).astype(BF16)

    rowi = lax.broadcasted_iota(jnp.int32, (nrow, 1), 0)
    g_col = lax.shift_right_logical(rowi, 3) & (NSA_KV - 1)
    r_col = lax.shift_right_logical(rowi, 5)
    slope = jnp.exp2(-8.0 * (g_col * NSA_REP + r_col + 1).astype(F32) / NSA_HEADS)
    qpos = PAST_LEN + (rowi & (t - 1))
    lane = lax.broadcasted_iota(jnp.int32, (1, 128), 1)
    pad_rows = lambda a: jnp.concatenate([a, jnp.zeros((128 - t, a.shape[-1]), F32)], axis=0)

    def bias_and_mask(s, kpos_row, extra_ok=None, window=False):
        d = (qpos - kpos_row).astype(F32)
        ok = d >= 0.0
        if window:
            ok = ok & (d < float(WINDOW))
        if extra_ok is not None:
            ok = ok & extra_ok
        return s - slope * d, ok

    s, ok = bias_and_mask(_dot(qm, kc_ref[0, 0].astype(BF16)), lane * CMP_STRIDE + (CMP_LEN - 1))
    e, den = _softmax_rows(s, ok)
    pc = e / den
    o_cmp = _dot_nt(pc.astype(BF16), vc_ref[0, 0].astype(BF16))
    hi, lo = _split_bf16(pc)
    imp = _dot(hi, cover_ref[...]) + _dot(lo, cover_ref[...])
    per_rep = NSA_KV * t
    imp = imp[0:per_rep] + imp[per_rep:2 * per_rep] + imp[2 * per_rep:3 * per_rep] + imp[3 * per_rep:]
    imp_t = jnp.concatenate([imp, jnp.zeros((nrow - per_rep, 128), F32)], axis=0).T
    cur = lax.shift_right_logical(PAST_LEN + (lane & (t - 1)), 6)
    sel_t = _select_blocks(imp_t[0:SEL_ROWS], cur, SAMPLE_SEL_BLOCKS)
    sel = jnp.concatenate([sel_t, jnp.zeros((128 - SEL_ROWS, 128), F32)], axis=0).T[0:per_rep]
    sel = jnp.concatenate([sel] * NSA_REP, axis=0)

    scores, oks = [], []
    for p in range(N_PAGES):
        chosen = jnp.where(lane < SEL_BLK, sel[:, 2 * p:2 * p + 1], sel[:, 2 * p + 1:2 * p + 2])
        s, ok = bias_and_mask(_dot(qm, pages[p][0, 0].astype(BF16)), p * PAGE_SIZE + lane, chosen > 0.5)
        scores.append(s)
        oks.append(ok)
    knew = pad_rows(new_ref[0, :, 0:KV_WIDTH]).astype(BF16)
    vnew = pad_rows(new_ref[0, :, KV_WIDTH:2 * KV_WIDTH]).astype(BF16)
    last = SAMPLE_SEL_BLOCKS - 1
    s, ok = bias_and_mask(_dot_nt(qm, knew), PAST_LEN + lane, (sel[:, last:last + 1] > 0.5) & (lane < t))
    scores.append(s)
    oks.append(ok)
    e, den = _softmax_rows(jnp.concatenate(scores, axis=-1), jnp.concatenate(oks, axis=-1))
    prob = (e / den).astype(BF16)
    o_sel = _dot(prob[:, PAST_LEN:], vnew)
    for p in range(N_PAGES):
        o_sel = o_sel + _dot_nt(prob[:, p * PAGE_SIZE:(p + 1) * PAGE_SIZE], pages[p][0, 1].astype(BF16))

    lane_w = lax.broadcasted_iota(jnp.int32, (1, wb), 1)
    s_old, ok_old = bias_and_mask(_dot(qm, cw_ref[0, 0].astype(BF16)), PAST_LEN - wb + lane_w, window=True)
    kwn = pad_rows(wnew_ref[0, :, 0:KV_WIDTH]).astype(BF16)
    vwn = pad_rows(wnew_ref[0, :, KV_WIDTH:2 * KV_WIDTH]).astype(BF16)
    s_new, ok_new = bias_and_mask(_dot_nt(qm, kwn), PAST_LEN + lane, lane < t, window=True)
    e, den = _softmax_rows(jnp.concatenate([s_old, s_new], axis=-1), jnp.concatenate([ok_old, ok_new], axis=-1))
    prob = (e / den).astype(BF16)
    o_win = _dot_nt(prob[:, 0:wb], cw_ref[0, 1].astype(BF16)) + _dot(prob[:, wb:], vwn)

    ghi, glo = _split_bf16(pad_rows(gate_ref[0]))
    grow = _dot(perm_ref[...], ghi) + _dot(perm_ref[...], glo)
    base = g_col * 12 + r_col * 3

    def gate_col(branch):
        return jnp.sum(jnp.where(lane == base + branch, grow, 0.0), axis=-1, keepdims=True)

    mixed = gate_col(0) * o_cmp + gate_col(1) * o_sel + gate_col(2) * o_win
    for r in range(NSA_REP):
        acc = jnp.zeros((t, KV_WIDTH), F32)
        for g in range(NSA_KV):
            lo_row = (r * NSA_KV + g) * t
            acc = acc + jnp.where(_group_mask(KV_WIDTH, g), mixed[lo_row:lo_row + t], 0.0)
        o_ref[0, :, r * KV_WIDTH:(r + 1) * KV_WIDTH] = acc.astype(o_ref.dtype)

    for c in range(2):
        new_t = pad_rows(wnew_ref[0, :, c * KV_WIDTH:(c + 1) * KV_WIDTH]).T
        tail = jnp.concatenate([jnp.zeros((KV_WIDTH, wb - 128), F32), pltpu.roll(new_t, 128 - t, 1)], axis=-1)
        wout_ref[0, c] = jnp.where(lane_w >= wb - t, tail, pltpu.roll(cw_ref[0, c], wb - t, 1))


def _nsa_sample(nq, gates, cmp_tok, cache_t, page_flat, kv_new, win_t, win_new, cover, perm):
    b, t = DEC_BATCH, DEC_SEQ
    wb = win_t.shape[-1]
    per_b = lambda shape: pl.BlockSpec(shape, lambda bi, pt: (bi,) + (0,) * (len(shape) - 1))
    page_specs = [pl.BlockSpec((1, 2, KV_WIDTH, PAGE_SIZE),
                               functools.partial(lambda bi, pt, p: (pt[bi * N_PAGES + p], 1, 0, 0), p=p))
                  for p in range(N_PAGES)]
    tok = lambda c: pl.BlockSpec((1, 1, KV_WIDTH, N_HALF), lambda bi, pt: (c, bi, 0, 0))
    grid_spec = pltpu.PrefetchScalarGridSpec(
        num_scalar_prefetch=1,
        grid=(b,),
        in_specs=page_specs + [per_b((1, t, NSA_WIDTH)), per_b((1, t, 128)), tok(0), tok(1),
                               pl.BlockSpec((1, t, 2 * KV_WIDTH), lambda bi, pt: (bi, 0, 1)),
                               per_b((1, 2, KV_WIDTH, wb)), per_b((1, t, 2 * KV_WIDTH)),
                               pl.BlockSpec(cover.shape, lambda bi, pt: (0, 0)),
                               pl.BlockSpec(perm.shape, lambda bi, pt: (0, 0))],
        out_specs=[per_b((1, t, NSA_WIDTH)), per_b((1, 2, KV_WIDTH, wb))],
    )
    return pl.pallas_call(
        _nsa_sample_kernel,
        grid_spec=grid_spec,
        out_shape=[jax.ShapeDtypeStruct((b, t, NSA_WIDTH), BF16), jax.ShapeDtypeStruct(win_t.shape, F32)],
        compiler_params=_params(("arbitrary",)),
        name="nsa_sample",
    )(page_flat, *([cache_t] * N_PAGES), nq.reshape(b, t, NSA_WIDTH), gates.reshape(b, t, 128), cmp_tok, cmp_tok,
      kv_new.reshape(b, t, 4 * KV_WIDTH), win_t, win_new.reshape(b, t, 2 * KV_WIDTH), cover, perm)


def _out_kernel(x_ref, ret_ref, nsa_ref, wr_ref, wn_ref, o_ref, wr16, wn16):
    wr = _bf16_weights(wr_ref, wr16, 1)
    wn = _bf16_weights(wn_ref, wn16, 1)
    o_ref[...] = x_ref[...] + _dot(ret_ref[...], wr[...]) + _dot(nsa_ref[...], wn[...])


def _mixer_out(x2d, ret, nsa, w_out, w_nsa, tm=1024, tn=1024):
    n, d = x2d.shape
    return pl.pallas_call(
        _out_kernel,
        grid=(d // tn, n // tm),
        in_specs=[pl.BlockSpec((tm, tn), lambda j, i: (i, j)),
                  pl.BlockSpec((tm, RET_WIDTH), lambda j, i: (i, 0)),
                  pl.BlockSpec((tm, NSA_WIDTH), lambda j, i: (i, 0)),
                  pl.BlockSpec((RET_WIDTH, tn), lambda j, i: (0, j)),
                  pl.BlockSpec((NSA_WIDTH, tn), lambda j, i: (0, j))],
        out_specs=pl.BlockSpec((tm, tn), lambda j, i: (i, j)),
        out_shape=jax.ShapeDtypeStruct((n, d), F32),
        scratch_shapes=[pltpu.VMEM((RET_WIDTH, tn), BF16), pltpu.VMEM((NSA_WIDTH, tn), BF16)],
        compiler_params=_params(("arbitrary", "arbitrary")),
        name="mixer_out",
    )(x2d, ret, nsa, w_out, w_nsa)


FFN_TN = 512
FFN_COLS = D_FF // FFN_TN
FFN_SUB = 512


def _ffn_up_kernel(*refs, hist, step, tiles_per_seq, has_prev):
    if has_prev:
        x_ref, wa_ref, wb_ref, cwa_ref, cwb_ref, cba_ref, cbb_ref, pa_ref, pb_ref = refs[:9]
        act_ref, sta_ref, stb_ref, exta, extb, wa16, wb16 = refs[9:]
    else:
        x_ref, wa_ref, wb_ref, cwa_ref, cwb_ref, cba_ref, cbb_ref = refs[:7]
        act_ref, sta_ref, stb_ref, exta, extb, wa16, wb16 = refs[7:]
        pa_ref = pb_ref = None
    tm = x_ref.shape[0]
    first = (pl.program_id(1) % tiles_per_seq) == 0
    x = x_ref[...]
    wa_ref = _bf16_weights(wa_ref, wa16, 1)
    wb_ref = _bf16_weights(wb_ref, wb16, 1)
    halves = ((wa_ref, cwa_ref, cba_ref, pa_ref, sta_ref, exta), (wb_ref, cwb_ref, cbb_ref, pb_ref, stb_ref, extb))
    if not has_prev:
        @pl.when(first)
        def _():
            for half in halves:
                half[5][hist - 2 * step:hist, :] = jnp.zeros((2 * step, half[5].shape[-1]), F32)
    pieces = act_ref.shape[-1] // FFN_SUB
    ups = [[_dot(x, half[0][:, k * FFN_SUB:(k + 1) * FFN_SUB]) for half in halves] for k in range(pieces)]
    for k in range(pieces):
        cols = slice(k * FFN_SUB, (k + 1) * FFN_SUB)
        outs = []
        for (w_ref, cw_ref, cb_ref, p_ref, st_ref, ext), u in zip(halves, ups[k]):
            if has_prev:
                ext[hist - 2 * step:hist, cols] = p_ref[:, cols]
            ext[hist:hist + tm, cols] = u
            cw = cw_ref[:, cols]
            c = (cb_ref[:, cols] + cw[2:3] * u + cw[1:2] * ext[hist - step:hist - step + tm, cols]
                 + cw[0:1] * ext[hist - 2 * step:hist - 2 * step + tm, cols])
            outs.append(c)
            tail = ext[hist + tm - 2 * step:hist + tm, cols]
            if has_prev:
                st_ref[:, cols] = tail
            else:
                st_ref[0, :, cols] = tail
                ext[hist - 2 * step:hist, cols] = tail
        act_ref[:, cols] = (jax.nn.silu(outs[0]) * outs[1]).astype(act_ref.dtype)


def _ffn_up(hn, w_up, conv_w, conv_b, prev, *, tm, step, seqs, tiles_per_seq):
    n, d = hn.shape
    tn = FFN_TN
    hist = 8 if step == 1 else 2 * step
    has_prev = prev is not None
    half = lambda off: (lambda j, i: (0, j + off))
    in_specs = [pl.BlockSpec((tm, d), lambda j, i: (i, 0)),
                pl.BlockSpec((d, tn), half(0)), pl.BlockSpec((d, tn), half(FFN_COLS)),
                pl.BlockSpec((CONV_W, tn), half(0)), pl.BlockSpec((CONV_W, tn), half(FFN_COLS)),
                pl.BlockSpec((1, tn), half(0)), pl.BlockSpec((1, tn), half(FFN_COLS))]
    args = [hn, w_up, w_up, conv_w, conv_w, conv_b, conv_b]
    if has_prev:
        in_specs += [pl.BlockSpec((2 * step, tn), half(0)), pl.BlockSpec((2 * step, tn), half(FFN_COLS))]
        args += [prev, prev]
        st_spec = pl.BlockSpec((2 * step, tn), lambda j, i: (0, j))
        st_shape = jax.ShapeDtypeStruct((2 * step, D_FF), F32)
    else:
        st_spec = pl.BlockSpec((1, 2, tn), lambda j, i: (i // tiles_per_seq, 0, j))
        st_shape = jax.ShapeDtypeStruct((seqs, 2, D_FF), F32)
    kernel = functools.partial(_ffn_up_kernel, hist=hist, step=step, tiles_per_seq=tiles_per_seq, has_prev=has_prev)
    return pl.pallas_call(
        kernel,
        grid=(FFN_COLS, n // tm),
        in_specs=in_specs,
        out_specs=[pl.BlockSpec((tm, tn), lambda j, i: (i, j)), st_spec, st_spec],
        out_shape=[jax.ShapeDtypeStruct((n, D_FF), BF16), st_shape, st_shape],
        scratch_shapes=[pltpu.VMEM((hist + tm, tn), F32), pltpu.VMEM((hist + tm, tn), F32),
                        pltpu.VMEM((d, tn), BF16), pltpu.VMEM((d, tn), BF16)],
        compiler_params=_params(("arbitrary", "arbitrary")),
        name="ffn_up",
    )(*args)


def _ffn_down_kernel(h_ref, a_ref, w_ref, o_ref):
    o_ref[...] = h_ref[...] + _dot(a_ref[...], w_ref[...])


def _ffn_down(h, act, w_down, tm=1024, tn=512):
    n, d = h.shape
    return pl.pallas_call(
        _ffn_down_kernel,
        grid=(n // tm, d // tn),
        in_specs=[pl.BlockSpec((tm, tn), lambda i, j: (i, j)),
                  pl.BlockSpec((tm, D_FF), lambda i, j: (i, 0)),
                  pl.BlockSpec((D_FF, tn), lambda i, j: (0, j))],
        out_specs=pl.BlockSpec((tm, tn), lambda i, j: (i, j)),
        out_shape=jax.ShapeDtypeStruct((n, d), F32),
        compiler_params=_params(("parallel", "arbitrary")),
        name="ffn_down",
    )(h, act, w_down)


def _block_cover_t(n_sel, rows):
    cs = np.arange(N_HALF - 1)[None, :] * CMP_STRIDE
    js = np.arange(n_sel)[:, None] * SEL_BLK
    cov = np.clip(np.minimum(cs + CMP_LEN, js + SEL_BLK) - np.maximum(cs, js), 0, None) / CMP_LEN
    out = np.zeros((rows, N_HALF), np.float32)
    out[:n_sel, :N_HALF - 1] = cov
    return jnp.asarray(out, dtype=BF16)


def _prepare_weights(g_attn, w_in, q_norm, k_norm_cmp, k_norm_slc, k_norm_win, cmp_pe, cmp_w1, cmp_b1, cmp_w2,
                     cmp_b2, w_out):
    d = D_MODEL
    w_in_t = w_in.T
    q0 = 4 * RET_WIDTH
    w_nq_t = w_in_t[q0:q0 + NSA_WIDTH].reshape(NSA_KV, NSA_REP, HEAD_DIM, d).transpose(1, 0, 2, 3).reshape(NSA_WIDTH, d)
    w_ng_t = w_in_t[q0 + NSA_WIDTH + 6 * KV_WIDTH:]
    w_ng_t = jnp.pad(w_ng_t, ((0, 128 - w_ng_t.shape[0]), (0, 0)))
    tile4 = lambda v: jnp.tile(v, NSA_KV)
    zeros = jnp.zeros((KV_WIDTH,), F32)
    ones = jnp.ones((KV_WIDTH,), F32)
    wts = dict(
        g_attn=g_attn, w_in_t=w_in_t, w_nq_t=w_nq_t, w_ng_t=w_ng_t,
        rot_scale=jnp.concatenate([jnp.ones((RET_WIDTH,), F32),
                                   jnp.full((RET_WIDTH,), RET_DK ** -0.5, F32)]).reshape(1, -1),
        nq_gain=jnp.tile(q_norm, NSA_HEADS).reshape(1, -1), nq_mask=jnp.ones((1, NSA_WIDTH), F32),
        kv_gain=jnp.concatenate([zeros, zeros, tile4(k_norm_slc), zeros]).reshape(1, -1),
        kv_mask=jnp.concatenate([zeros, zeros, ones, zeros]).reshape(1, -1),
        win_gain=jnp.concatenate([tile4(k_norm_win), zeros]).reshape(1, -1),
        win_mask=jnp.concatenate([ones, zeros]).reshape(1, -1),
    )
    w1 = cmp_w1.reshape(2, 2, CMP_STRIDE, HEAD_DIM, CMP_HIDDEN)
    w1rep = jnp.broadcast_to(w1.transpose(0, 2, 3, 1, 4)[:, :, None],
                             (2, CMP_STRIDE, NSA_KV, HEAD_DIM, 2, CMP_HIDDEN))
    w1rep = w1rep.reshape(2, CMP_STRIDE * KV_WIDTH, 2 * CMP_HIDDEN).astype(BF16)
    w1flat = w1.reshape(2, 2, CMP_STRIDE * HEAD_DIM, CMP_HIDDEN).astype(BF16)
    pe8 = jnp.pad(cmp_pe.reshape(2, 2, 1, CMP_STRIDE * HEAD_DIM), ((0, 0), (0, 0), (0, 15), (0, 0)))
    bias = _cmp_bias(pe8, w1flat, cmp_b1.reshape(2, 1, CMP_HIDDEN))
    w1pair = jnp.broadcast_to(w1.transpose(0, 2, 3, 1, 4)[:, :, None], (2, CMP_STRIDE, 2, HEAD_DIM, 2, CMP_HIDDEN))
    col = lambda v: jnp.broadcast_to(v[:, :, None], (2, HEAD_DIM, N_HALF))
    cw = dict(
        w1pair=w1pair.reshape(2, CMP_STRIDE * 128, 2 * CMP_HIDDEN).astype(BF16),
        w2t=cmp_w2.transpose(0, 2, 1).astype(BF16), b2col=col(cmp_b2),
        gaincol=jnp.broadcast_to(k_norm_cmp[:, None], (HEAD_DIM, N_HALF)),
        w1rep=w1rep, bias=bias,
        w2rep=jnp.tile(cmp_w2, (1, 1, NSA_KV)).astype(BF16),
        b2rep=jnp.tile(cmp_b2, (1, NSA_KV)).reshape(2, 1, KV_WIDTH),
        gain=jnp.stack([tile4(k_norm_cmp), ones]).reshape(2, 1, KV_WIDTH),
    )
    w_nsa = w_out[RET_WIDTH:].reshape(NSA_KV, NSA_REP, HEAD_DIM, d).transpose(1, 0, 2, 3).reshape(NSA_WIDTH, d)
    return wts, cw, w_out, w_nsa


def kernel(x_prompt, x_sample, cache_kv, cache_win, state_ret, state_conv, page_table, g_attn, w_in, q_norm,
           k_norm_cmp, k_norm_slc, k_norm_win, cmp_pe, cmp_w1, cmp_b1, cmp_w2, cmp_b2, ret_gn, w_out, g_ffn, w_up,
           conv_w, conv_b, w_down):
    assert x_prompt.shape == (BATCH, SEQ, D_MODEL) and x_sample.shape == (DEC_BATCH, DEC_SEQ, D_MODEL)
    assert g_attn.shape[0] == 1, "single layer"
    wts, cw, w_ret, w_nsa = _prepare_weights(g_attn[0], w_in[0], q_norm[0], k_norm_cmp[0], k_norm_slc[0],
                                              k_norm_win[0], cmp_pe[0], cmp_w1[0], cmp_b1[0], cmp_w2[0],
                                              cmp_b2[0], w_out[0])
    w_up_b = w_up[0]
    w_down_b = w_down[0].astype(BF16)
    cb = conv_b[0].reshape(1, -1)
    n_p, n_s = BATCH * SEQ, DEC_BATCH * DEC_SEQ
    xp = x_prompt.reshape(n_p, D_MODEL)
    xs = x_sample.reshape(n_s, D_MODEL)

    pos_p = jnp.tile(jnp.arange(SEQ, dtype=jnp.int32), BATCH)
    qk, vg, nq, (kv_p, kv_p_t), win_p, gates = _project_all(xp, pos_p, wts, kv_transposed_seq=SEQ)
    ret_mix, ret_state_p = _retention_prompt(qk, vg, ret_gn[0])
    kv3 = kv_p.reshape(BATCH, SEQ, 4 * KV_WIDTH)
    cmp_tok = _compress_prompt(kv3, cw)
    cover_p = _block_cover_t(SEQ // SEL_BLK, SEQ // SEL_BLK)
    key = np.arange(SEQ).reshape(SEQ // SEL_CHUNK, 1, SEL_CHUNK)
    expand = jnp.asarray((key // SEL_BLK) == np.arange(128)[None, :, None], dtype=BF16)
    nsa = _nsa_prompt(nq, gates, cmp_tok, kv_p, win_p, cover_p, expand)
    h_p = _mixer_out(xp, ret_mix.reshape(n_p, RET_WIDTH), nsa.reshape(n_p, NSA_WIDTH), w_ret, w_nsa)
    hn_p = _rmsnorm(h_p, g_ffn[0])
    act_p, st_a, st_b = _ffn_up(hn_p, w_up_b, conv_w[0], cb, None, tm=1024, step=1, seqs=BATCH,
                                tiles_per_seq=SEQ // 1024)
    y_p = _ffn_down(h_p, act_p, w_down_b)
    conv_p = jnp.concatenate([st_a, st_b], axis=-1)

    pos_s = jnp.tile(PAST_LEN + jnp.arange(DEC_SEQ, dtype=jnp.int32), DEC_BATCH)
    qk, vg, nq, kv_s, win_s, gates = _project_all(xs, pos_s, wts)
    ret_mix, ret_state_s = _retention_sample(qk, vg, state_ret[0], ret_gn[0])
    cache_t = cache_kv[0].transpose(0, 2, 3, 4, 1).reshape(cache_kv.shape[1], 4, KV_WIDTH, PAGE_SIZE)
    page_flat = page_table.reshape(-1).astype(jnp.int32)
    cmp_tok = _compress_sample(cache_t, page_flat, cw)
    wb = cache_win.shape[2]
    win_t = cache_win[0].transpose(0, 2, 3, 4, 1).reshape(DEC_BATCH, 2, KV_WIDTH, wb)
    cover_s = _block_cover_t(SAMPLE_SEL_BLOCKS, 128).T
    lanes = np.arange(128)
    perm = jnp.asarray((lanes[:, None] % DEC_SEQ) == lanes[None, :], dtype=BF16)
    nsa, win_out_t = _nsa_sample(nq, gates, cmp_tok, cache_t, page_flat, kv_s, win_t, win_s, cover_s, perm)
    h_s = _mixer_out(xs, ret_mix.reshape(n_s, RET_WIDTH), nsa.reshape(n_s, NSA_WIDTH), w_ret, w_nsa)
    to_tm = lambda a: a.reshape(DEC_BATCH, -1, a.shape[-1]).transpose(1, 0, 2).reshape(-1, a.shape[-1])
    from_tm = lambda a, t: a.reshape(t, DEC_BATCH, a.shape[-1]).transpose(1, 0, 2)
    h_tm = to_tm(h_s)
    hn_s = _rmsnorm(h_tm, g_ffn[0])
    act_s, st_a, st_b = _ffn_up(hn_s, w_up_b, conv_w[0], cb, to_tm(state_conv[0]), tm=n_s, step=DEC_BATCH,
                                seqs=DEC_BATCH, tiles_per_seq=1)
    y_s = from_tm(_ffn_down(h_tm, act_s, w_down_b), DEC_SEQ)
    conv_s = from_tm(jnp.concatenate([st_a, st_b], axis=-1), CONV_W - 1)

    win_sample = win_out_t.reshape(DEC_BATCH, 2, NSA_KV, HEAD_DIM, wb).transpose(0, 4, 1, 2, 3)
    return (
        y_p.reshape(BATCH, SEQ, D_MODEL),
        y_s,
        kv_p_t.reshape(BATCH, 4, NSA_KV, HEAD_DIM, SEQ).transpose(0, 4, 1, 2, 3)[None],
        kv_s.reshape(1, DEC_BATCH, DEC_SEQ, 4, NSA_KV, HEAD_DIM),
        win_p.reshape(BATCH, SEQ, 2 * KV_WIDTH)[:, SEQ - WINDOW:].reshape(1, BATCH, WINDOW, 2, NSA_KV, HEAD_DIM),
        win_sample[None],
        ret_state_p[None],
        ret_state_s[None],
        conv_p[None],
        conv_s[None],
    )
```

```python
import functools

import numpy as np
import jax
import jax.numpy as jnp
from jax import lax
from jax.experimental import pallas as pl
from jax.experimental.pallas import tpu as pltpu

D_MODEL = 2048
BATCH = 4
SEQ = 2048
DEC_BATCH = 128
DEC_SEQ = 8
PAST_LEN = 2048
PAGE_SIZE = 128
N_PAGES = PAST_LEN // PAGE_SIZE
RET_HEADS = 4
RET_DK = 256
RET_DV = 256
RET_WIDTH = RET_HEADS * RET_DV
RET_CHUNK = 128
NSA_HEADS = 16
NSA_KV = 4
NSA_REP = NSA_HEADS // NSA_KV
HEAD_DIM = 64
NSA_WIDTH = NSA_HEADS * HEAD_DIM
KV_WIDTH = NSA_KV * HEAD_DIM
CMP_LEN = 32
CMP_STRIDE = 16
CMP_HIDDEN = 256
SEL_BLK = 64
SEL_TOPK = 16
WINDOW = 512
SCALE = HEAD_DIM ** -0.5
D_FF = 5632
CONV_W = 3
EPS = 1e-6

N_HALF = PAST_LEN // CMP_STRIDE
NEG = -1e30
FORCED = 1e30
VMEM_LIMIT = 56 * 1024 * 1024

BF16 = jnp.bfloat16
F32 = jnp.float32


def _params(semantics, vmem=VMEM_LIMIT):
    return pltpu.CompilerParams(dimension_semantics=semantics, vmem_limit_bytes=vmem)


def _dot(a, b):
    return jnp.dot(a, b, preferred_element_type=F32)


def _dot_nt(a, b):
    return lax.dot_general(a, b, (((1,), (1,)), ((), ())), preferred_element_type=F32)


def _split_bf16(x):
    hi = x.astype(BF16)
    lo = (x - hi.astype(F32)).astype(BF16)
    return hi, lo


def _group_mask(width, g):
    lane = lax.broadcasted_iota(jnp.int32, (1, width), 1)
    return (lane >= g * HEAD_DIM) & (lane < (g + 1) * HEAD_DIM)


def _rms_groups64(z, gain):
    width = z.shape[-1]
    outs = []
    for k in range(width // 128):
        zk = z[:, k * 128:(k + 1) * 128]
        zz = zk * zk
        lane = lax.broadcasted_iota(jnp.int32, (1, 128), 1)
        lo_half = lane < HEAD_DIM
        s_lo = jnp.sum(jnp.where(lo_half, zz, 0.0), axis=-1, keepdims=True)
        s_hi = jnp.sum(jnp.where(lo_half, 0.0, zz), axis=-1, keepdims=True)
        ms = jnp.where(lo_half, s_lo, s_hi) * (1.0 / HEAD_DIM)
        outs.append(zk * lax.rsqrt(ms + EPS))
    y = outs[0] if len(outs) == 1 else jnp.concatenate(outs, axis=-1)
    return y * gain


def _rmsnorm_kernel(x_ref, g_ref, o_ref):
    x = x_ref[...]
    ms = jnp.mean(x * x, axis=-1, keepdims=True)
    o_ref[...] = (x * lax.rsqrt(ms + EPS) * g_ref[...]).astype(o_ref.dtype)


def _rmsnorm(x, g, tm=512):
    n, d = x.shape
    return pl.pallas_call(
        _rmsnorm_kernel,
        grid=(n // tm,),
        in_specs=[pl.BlockSpec((tm, d), lambda i: (i, 0)), pl.BlockSpec((1, d), lambda i: (0, 0))],
        out_specs=pl.BlockSpec((tm, d), lambda i: (i, 0)),
        out_shape=jax.ShapeDtypeStruct((n, d), BF16),
        compiler_params=_params(("parallel",)),
        name="rmsnorm",
    )(x, g.reshape(1, d))


def _bf16_weights(w_ref, wb_ref, row_axis):
    @pl.when(pl.program_id(row_axis) == 0)
    def _():
        wb_ref[...] = w_ref[...].astype(BF16)
    return wb_ref


def _proj_plain_kernel(x_ref, w_ref, o_ref, wb_ref):
    o_ref[...] = _dot_nt(x_ref[...], _bf16_weights(w_ref, wb_ref, 1)[...]).astype(o_ref.dtype)


def _proj_sigmoid_kernel(x_ref, w_ref, o_ref, wb_ref):
    o_ref[...] = jax.nn.sigmoid(_dot_nt(x_ref[...], _bf16_weights(w_ref, wb_ref, 1)[...]))


def _sub_dots(x_ref, wt_ref, width):
    x = x_ref[...]
    return [_dot_nt(x, wt_ref[k * width:(k + 1) * width, :]) for k in range(wt_ref.shape[0] // width)]


def _proj_norm_kernel(x_ref, w_ref, gain_ref, nmask_ref, o_ref, *rest):
    wb_ref = rest[-1]
    for k, z in enumerate(_sub_dots(x_ref, _bf16_weights(w_ref, wb_ref, 1), KV_WIDTH)):
        cols = slice(k * KV_WIDTH, (k + 1) * KV_WIDTH)
        y = jnp.where(nmask_ref[:, cols] > 0.5, _rms_groups64(z, gain_ref[:, cols]), z)
        o_ref[:, cols] = y
        if len(rest) == 2:
            rest[0][0, cols, :] = y.T


def _proj_rot_kernel(x_ref, w_ref, cos_ref, sin_ref, scale_ref, o_ref, wb_ref):
    c = cos_ref[...]
    s = sin_ref[...]
    for hh, z in enumerate(_sub_dots(x_ref, _bf16_weights(w_ref, wb_ref, 1), RET_DK)):
        lo, mid, hi = hh * RET_DK, hh * RET_DK + 128, (hh + 1) * RET_DK
        x1 = z[:, :128]
        x2 = z[:, 128:]
        o_ref[:, lo:mid] = (x1 * c - x2 * s) * scale_ref[:, lo:mid]
        o_ref[:, mid:hi] = (x2 * c + x1 * s) * scale_ref[:, mid:hi]


PROJ_TN = 1024


def _proj(xn, wt, row0, c, kernel, row_extras=(), col_extras=(), out_dtype=F32, tm=1024, name="proj",
          transposed_seq=None):
    n, d = xn.shape
    tn = min(PROJ_TN, c)
    off = row0 // tn
    assert row0 == off * tn and c % tn == 0
    extra_specs = ([pl.BlockSpec((tm, 128), lambda j, i: (i, 0)) for _ in row_extras]
                   + [pl.BlockSpec((1, tn), lambda j, i: (0, j)) for _ in col_extras])
    out_specs = pl.BlockSpec((tm, tn), lambda j, i: (i, j))
    out_shape = jax.ShapeDtypeStruct((n, c), out_dtype)
    if transposed_seq is not None:
        per_seq = transposed_seq // tm
        out_specs = [out_specs, pl.BlockSpec((1, tn, tm), lambda j, i: (i // per_seq, j, i % per_seq))]
        out_shape = [out_shape, jax.ShapeDtypeStruct((n // transposed_seq, c, transposed_seq), out_dtype)]
    return pl.pallas_call(
        kernel,
        grid=(c // tn, n // tm),
        in_specs=[pl.BlockSpec((tm, d), lambda j, i: (i, 0)), pl.BlockSpec((tn, d), lambda j, i: (j + off, 0))]
        + extra_specs,
        out_specs=out_specs,
        out_shape=out_shape,
        scratch_shapes=[pltpu.VMEM((tn, d), BF16)],
        compiler_params=_params(("arbitrary", "arbitrary")),
        name=name,
    )(xn, wt, *row_extras, *col_extras)


def _project_all(x2d, pos, wts, tm=1024, kv_transposed_seq=None):
    xn = _rmsnorm(x2d, wts["g_attn"])
    half = RET_DK // 2
    inv = 1.0 / (10000.0 ** jnp.linspace(0.0, 1.0, half, dtype=F32))
    ang = pos.astype(F32)[:, None] * inv[None, :]
    cos = jnp.cos(ang)
    sin = jnp.sin(ang)
    w_in_t = wts["w_in_t"]
    rw = 2 * RET_WIDTH
    qk = _proj(xn, w_in_t, 0, rw, _proj_rot_kernel, (cos, sin), (wts["rot_scale"],), tm=tm, name="proj_rot")
    vg = _proj(xn, w_in_t, rw, rw, _proj_plain_kernel, tm=tm, name="proj_vg")
    nq = _proj(xn, wts["w_nq_t"], 0, NSA_WIDTH, _proj_norm_kernel, (), (wts["nq_gain"], wts["nq_mask"]), tm=tm,
               name="proj_nq")
    kv0 = 2 * rw + NSA_WIDTH
    kv = _proj(xn, w_in_t, kv0, 4 * KV_WIDTH, _proj_norm_kernel, (), (wts["kv_gain"], wts["kv_mask"]), tm=tm,
               name="proj_kv", transposed_seq=kv_transposed_seq)
    win = _proj(xn, w_in_t, kv0 + 4 * KV_WIDTH, 2 * KV_WIDTH, _proj_norm_kernel, (),
                (wts["win_gain"], wts["win_mask"]), tm=tm, name="proj_win")
    gates = _proj(xn, wts["w_ng_t"], 0, 128, _proj_sigmoid_kernel, tm=tm, name="proj_gate")
    return qk, vg, nq, kv, win, gates


def _retention_step(q, k, v, state, dmask, xi, zeta, gch):
    qb = q.astype(BF16)
    kb = k.astype(BF16)
    vb = v.astype(BF16)
    s = _dot_nt(qb, kb) * dmask
    o = _dot(s.astype(BF16), vb) + _dot(qb, state.astype(BF16)) * xi
    kz = (k * zeta).astype(BF16)
    new_state = state * gch + _dot(kz.T, vb)
    return o, new_state


def _ret_mix(o, gn, gate):
    ms = jnp.mean(o * o, axis=-1, keepdims=True)
    return o * lax.rsqrt(ms + EPS) * gn * jax.nn.silu(gate)


def _ret_prompt_kernel(qk_ref, vg_ref, dmask_ref, xi_ref, zeta_ref, gch_ref, gn_ref, mix_ref, state_ref):
    @pl.when(pl.program_id(1) == 0)
    def _():
        state_ref[...] = jnp.zeros_like(state_ref)

    for h in range(RET_HEADS):
        lo, hi = h * RET_DK, (h + 1) * RET_DK
        o, new_state = _retention_step(qk_ref[0, :, lo:hi], qk_ref[0, :, RET_WIDTH + lo:RET_WIDTH + hi],
                                       vg_ref[0, :, lo:hi], state_ref[0, h], dmask_ref[h], xi_ref[h],
                                       zeta_ref[h], gch_ref[h])
        state_ref[0, h] = new_state
        mix_ref[0, :, lo:hi] = _ret_mix(o, gn_ref[h], vg_ref[0, :, RET_WIDTH + lo:RET_WIDTH + hi]).astype(mix_ref.dtype)


def _ret_tables(chunk):
    h = jnp.arange(RET_HEADS, dtype=F32)
    lg = jnp.log(1.0 - jnp.exp2(-5.0 - h))
    i = jnp.arange(chunk, dtype=F32)
    diff = i[:, None] - i[None, :]
    dmask = jnp.where(diff >= 0, jnp.exp(lg[:, None, None] * jnp.maximum(diff, 0.0)), 0.0)
    xi = jnp.exp(lg[:, None] * (i[None, :] + 1.0))
    zeta = jnp.exp(lg[:, None] * (chunk - 1.0 - i[None, :]))
    gch = jnp.exp(lg * chunk)
    bc = lambda a: jnp.broadcast_to(a[:, :, None], (RET_HEADS, chunk, RET_DV))
    return dmask, bc(xi), bc(zeta), jnp.broadcast_to(gch[:, None, None], (RET_HEADS, 1, RET_DV))


def _retention_prompt(qk, vg, gn):
    b, t = BATCH, SEQ
    c = RET_CHUNK
    qk3 = qk.reshape(b, t, 2 * RET_WIDTH)
    vg3 = vg.reshape(b, t, 2 * RET_WIDTH)
    dmask, xi, zeta, gch = _ret_tables(c)
    rows = pl.BlockSpec((1, c, 2 * RET_WIDTH), lambda bi, ci: (bi, ci, 0))
    full = lambda shape: pl.BlockSpec(shape, lambda bi, ci: (0,) * len(shape))
    return pl.pallas_call(
        _ret_prompt_kernel,
        grid=(b, t // c),
        in_specs=[rows, rows, full((RET_HEADS, c, c)), full((RET_HEADS, c, RET_DV)), full((RET_HEADS, c, RET_DV)),
                  full((RET_HEADS, 1, RET_DV)), full((RET_HEADS, 1, RET_DV))],
        out_specs=[pl.BlockSpec((1, c, RET_WIDTH), lambda bi, ci: (bi, ci, 0)),
                   pl.BlockSpec((1, RET_HEADS, RET_DK, RET_DV), lambda bi, ci: (bi, 0, 0, 0))],
        out_shape=[jax.ShapeDtypeStruct((b, t, RET_WIDTH), BF16),
                   jax.ShapeDtypeStruct((b, RET_HEADS, RET_DK, RET_DV), F32)],
        compiler_params=_params(("parallel", "arbitrary")),
        name="retention_prompt",
    )(qk3, vg3, dmask, xi, zeta, gch, gn.reshape(RET_HEADS, 1, RET_DV))


def _ret_sample_kernel(qk_ref, vg_ref, s0_ref, dmask_ref, xi_ref, zeta_ref, gch_ref, gn_ref, mix_ref, state_ref):
    t = qk_ref.shape[1]
    pad = lambda a: jnp.concatenate([a, jnp.zeros((RET_PAD - t, a.shape[-1]), F32)], axis=0)
    for s in range(qk_ref.shape[0]):
        for h in range(RET_HEADS):
            lo, hi = h * RET_DK, (h + 1) * RET_DK
            gate = vg_ref[s, :, RET_WIDTH + lo:RET_WIDTH + hi]
            o, new_state = _retention_step(pad(qk_ref[s, :, lo:hi]), pad(qk_ref[s, :, RET_WIDTH + lo:RET_WIDTH + hi]),
                                           pad(vg_ref[s, :, lo:hi]), s0_ref[s, h], dmask_ref[h], xi_ref[h],
                                           zeta_ref[h], gch_ref[h])
            state_ref[s, h] = new_state
            mix_ref[s, :, lo:hi] = _ret_mix(o[0:t], gn_ref[h], gate).astype(mix_ref.dtype)


RET_PAD = 128
RET_SEQS_PER_STEP = 4


def _retention_sample(qk, vg, state0, gn):
    b, t = DEC_BATCH, DEC_SEQ
    dmask, xi, zeta, gch = _ret_tables(t)
    dmask = jnp.pad(dmask, ((0, 0), (0, RET_PAD - t), (0, RET_PAD - t)))
    xi = jnp.pad(xi, ((0, 0), (0, RET_PAD - t), (0, 0)))
    zeta = jnp.pad(zeta, ((0, 0), (0, RET_PAD - t), (0, 0)))
    full = lambda shape: pl.BlockSpec(shape, lambda bi: (0,) * len(shape))
    return pl.pallas_call(
        _ret_sample_kernel,
        grid=(b // RET_SEQS_PER_STEP,),
        in_specs=[pl.BlockSpec((RET_SEQS_PER_STEP, t, 2 * RET_WIDTH), lambda bi: (bi, 0, 0)),
                  pl.BlockSpec((RET_SEQS_PER_STEP, t, 2 * RET_WIDTH), lambda bi: (bi, 0, 0)),
                  pl.BlockSpec((RET_SEQS_PER_STEP, RET_HEADS, RET_DK, RET_DV), lambda bi: (bi, 0, 0, 0)),
                  full((RET_HEADS, RET_PAD, RET_PAD)), full((RET_HEADS, RET_PAD, RET_DV)), full((RET_HEADS, RET_PAD, RET_DV)),
                  full((RET_HEADS, 1, RET_DV)), full((RET_HEADS, 1, RET_DV))],
        out_specs=[pl.BlockSpec((RET_SEQS_PER_STEP, t, RET_WIDTH), lambda bi: (bi, 0, 0)),
                   pl.BlockSpec((RET_SEQS_PER_STEP, RET_HEADS, RET_DK, RET_DV), lambda bi: (bi, 0, 0, 0))],
        out_shape=[jax.ShapeDtypeStruct((b, t, RET_WIDTH), BF16),
                   jax.ShapeDtypeStruct((b, RET_HEADS, RET_DK, RET_DV), F32)],
        compiler_params=_params(("parallel",)),
        name="retention_sample",
    )(qk.reshape(b, t, -1), vg.reshape(b, t, -1), state0, dmask, xi, zeta, gch, gn.reshape(RET_HEADS, 1, RET_DV))


def _cmp_bias_kernel(pe_ref, w_ref, b1_ref, o_ref):
    acc = b1_ref[0]
    for j in range(2):
        acc = acc + _dot(pe_ref[0, j].astype(BF16), w_ref[0, j])[0:1]
    o_ref[0] = acc


def _cmp_bias(pe8, w1flat, b1):
    return pl.pallas_call(
        _cmp_bias_kernel,
        grid=(2,),
        in_specs=[pl.BlockSpec((1, 2, 16, CMP_STRIDE * HEAD_DIM), lambda c: (c, 0, 0, 0)),
                  pl.BlockSpec((1, 2, CMP_STRIDE * HEAD_DIM, CMP_HIDDEN), lambda c: (c, 0, 0, 0)),
                  pl.BlockSpec((1, 1, CMP_HIDDEN), lambda c: (c, 0, 0))],
        out_specs=pl.BlockSpec((1, 1, CMP_HIDDEN), lambda c: (c, 0, 0)),
        out_shape=jax.ShapeDtypeStruct((2, 1, CMP_HIDDEN), F32),
        compiler_params=_params(("parallel",)),
        name="cmp_bias",
    )(pe8, w1flat, b1)


def _compress_body(load_rows, w1_ref, bias_ref, w2_ref, b2_ref, gain_ref, o_ref, xm_ref):
    for l in range(CMP_STRIDE):
        xl = load_rows(l)
        for g in range(NSA_KV):
            xm_ref[g * N_HALF:(g + 1) * N_HALF, l * KV_WIDTH:(l + 1) * KV_WIDTH] = jnp.where(
                _group_mask(KV_WIDTH, g), xl, 0.0).astype(BF16)
    acc = _dot(xm_ref[...], w1_ref[0])
    h0 = acc[:, :CMP_HIDDEN]
    h1 = pltpu.roll(acc[:, CMP_HIDDEN:], 4 * N_HALF - 1, 0)
    hid = jax.nn.silu(h0 + h1 + bias_ref[0])
    out = _dot(hid.astype(BF16), w2_ref[0]) + b2_ref[0]
    res = jnp.zeros((N_HALF, KV_WIDTH), F32)
    for g in range(NSA_KV):
        res = res + jnp.where(_group_mask(KV_WIDTH, g), out[g * N_HALF:(g + 1) * N_HALF], 0.0)
    normed = _rms_groups64(res, gain_ref[0])
    o_ref[0, 0] = jnp.where(pl.program_id(0) == 0, normed, res)


def _compress_prompt_kernel(lo_ref, hi_ref, w1_ref, bias_ref, w2_ref, b2_ref, gain_ref, o_ref, xm_ref):
    load = lambda l: jnp.concatenate(
        [r[0, pl.ds(l, N_HALF, stride=CMP_STRIDE), :] for r in (lo_ref, hi_ref)], axis=-1)
    _compress_body(load, w1_ref, bias_ref, w2_ref, b2_ref, gain_ref, o_ref, xm_ref)


def _compress_sample_kernel(pt_ref, *refs):
    pages = refs[:N_PAGES]
    w1_ref, bias_ref, w2t_ref, b2_ref, gain_ref, o_ref, t_ref, xm_ref = refs[N_PAGES:]
    lane = lax.broadcasted_iota(jnp.int32, (1, 128), 1)
    for c in range(2):
        for p in range(N_PAGES):
            xt = pages[p][0, c]
            for pair in range(2):
                t_ref[c, pair, p * PAGE_SIZE:(p + 1) * PAGE_SIZE, :] = xt[pair * 128:(pair + 1) * 128, :].T
        for l in range(CMP_STRIDE):
            for pair in range(2):
                rows = t_ref[c, pair, pl.ds(l, N_HALF, stride=CMP_STRIDE), :]
                for member in range(2):
                    g = 2 * pair + member
                    keep = (lane >= member * HEAD_DIM) & (lane < (member + 1) * HEAD_DIM)
                    xm_ref[c, g * N_HALF:(g + 1) * N_HALF, l * 128:(l + 1) * 128] = (
                        jnp.where(keep, rows, 0.0).astype(BF16))
        acc = _dot(xm_ref[c], w1_ref[c])
        h0 = acc[:, :CMP_HIDDEN]
        h1 = pltpu.roll(acc[:, CMP_HIDDEN:], 4 * N_HALF - 1, 0)
        hid = jax.nn.silu(h0 + h1 + bias_ref[c]).astype(BF16)
        outs = []
        for g in range(NSA_KV):
            og = _dot_nt(w2t_ref[c], hid[g * N_HALF:(g + 1) * N_HALF]) + b2_ref[c]
            if c == 0:
                og = og * lax.rsqrt(jnp.mean(og * og, axis=0, keepdims=True) + EPS) * gain_ref[...]
            outs.append(og)
        o_ref[c, 0] = jnp.concatenate(outs, axis=0)


def _cmp_weight_specs(nargs):
    cmap = (lambda c, b: (c, 0, 0)) if nargs == 2 else (lambda c, b, pt: (c, 0, 0))
    return [pl.BlockSpec((1, CMP_STRIDE * KV_WIDTH, 2 * CMP_HIDDEN), cmap),
            pl.BlockSpec((1, 1, CMP_HIDDEN), cmap),
            pl.BlockSpec((1, CMP_HIDDEN, KV_WIDTH), cmap),
            pl.BlockSpec((1, 1, KV_WIDTH), cmap),
            pl.BlockSpec((1, 1, KV_WIDTH), cmap)]


def _compress_prompt(kv3, cw):
    b = kv3.shape[0]
    return pl.pallas_call(
        _compress_prompt_kernel,
        grid=(2, b),
        in_specs=[pl.BlockSpec((1, SEQ, 128), lambda c, bi: (bi, 0, 2 * c)),
                  pl.BlockSpec((1, SEQ, 128), lambda c, bi: (bi, 0, 2 * c + 1))] + _cmp_weight_specs(2),
        out_specs=pl.BlockSpec((1, 1, N_HALF, KV_WIDTH), lambda c, bi: (c, bi, 0, 0)),
        out_shape=jax.ShapeDtypeStruct((2, b, N_HALF, KV_WIDTH), F32),
        scratch_shapes=[pltpu.VMEM((NSA_KV * N_HALF, CMP_STRIDE * KV_WIDTH), BF16)],
        compiler_params=_params(("arbitrary", "arbitrary")),
        name="compress_prompt",
    )(kv3, kv3, cw["w1rep"], cw["bias"], cw["w2rep"], cw["b2rep"], cw["gain"])


def _compress_sample(cache_t, page_flat, cw):
    b = DEC_BATCH
    page_specs = [pl.BlockSpec((1, 2, KV_WIDTH, PAGE_SIZE),
                               functools.partial(lambda bi, pt, p: (pt[bi * N_PAGES + p], 0, 0, 0), p=p))
                  for p in range(N_PAGES)]
    full = lambda shape: pl.BlockSpec(shape, lambda bi, pt: (0,) * len(shape))
    grid_spec = pltpu.PrefetchScalarGridSpec(
        num_scalar_prefetch=1,
        grid=(b,),
        in_specs=page_specs + [full((2, CMP_STRIDE * 128, 2 * CMP_HIDDEN)), full((2, 1, CMP_HIDDEN)),
                               full((2, HEAD_DIM, CMP_HIDDEN)), full((2, HEAD_DIM, N_HALF)),
                               full((HEAD_DIM, N_HALF))],
        out_specs=pl.BlockSpec((2, 1, KV_WIDTH, N_HALF), lambda bi, pt: (0, bi, 0, 0)),
        scratch_shapes=[pltpu.VMEM((2, 2, PAST_LEN, 128), F32),
                        pltpu.VMEM((2, NSA_KV * N_HALF, CMP_STRIDE * 128), BF16)],
    )
    return pl.pallas_call(
        _compress_sample_kernel,
        grid_spec=grid_spec,
        out_shape=jax.ShapeDtypeStruct((2, b, KV_WIDTH, N_HALF), F32),
        compiler_params=_params(("arbitrary",)),
        name="compress_sample",
    )(page_flat, *([cache_t] * N_PAGES), cw["w1pair"], cw["bias"], cw["w2t"], cw["b2col"], cw["gaincol"])


def _select_blocks(imp, cur, n_blocks):
    rows = imp.shape[0]
    j = lax.broadcasted_iota(jnp.int32, (rows, 1), 0)
    valid = j <= cur
    forced = (j == 0) | (j == cur) | (j == cur - 1)
    score = jnp.where(valid, jnp.where(forced, FORCED, imp), NEG)
    rank = jnp.zeros(score.shape, F32)
    for i in range(n_blocks):
        row = score[i:i + 1, :]
        ahead = (row > score) | ((row == score) & (j > i))
        rank = rank + jnp.where(ahead, 1.0, 0.0)
    return jnp.where(valid & (rank < float(SEL_TOPK)) & (j < n_blocks), 1.0, 0.0)


def _softmax_rows(s, valid):
    s = jnp.where(valid, s, NEG)
    m = jnp.max(s, axis=-1, keepdims=True)
    e = jnp.where(valid, jnp.exp(s - m), 0.0)
    return e, jnp.maximum(jnp.sum(e, axis=-1, keepdims=True), 1e-30)


def _softmax_cols(s, valid):
    s = jnp.where(valid, s, NEG)
    m = jnp.max(s, axis=0, keepdims=True)
    e = jnp.where(valid, jnp.exp(s - m), 0.0)
    return e, jnp.maximum(jnp.sum(e, axis=0, keepdims=True), 1e-30)


def _slope(g, r):
    return float(2.0 ** (-8.0 * (g * NSA_REP + r + 1) / NSA_HEADS))


NSA_TQ = 128
SEL_CHUNK_LOG2 = 9
SEL_CHUNK = 1 << SEL_CHUNK_LOG2
WIN_KEYS = WINDOW + NSA_TQ


def _nsa_prompt_kernel(q_ref, gate_ref, kc_ref, vc_ref, ks_ref, vs_ref, kw_ref, vw_ref, cover_ref, expand_ref,
                       slope_ref, o_ref, ksb, vsb, kwb, vwb):
    i = pl.program_id(1)
    tq = NSA_TQ
    rows = NSA_REP * tq

    @pl.when(i == 0)
    def _():
        pos = lax.broadcasted_iota(jnp.int32, (ks_ref.shape[1], 1), 0)
        lane = lax.broadcasted_iota(jnp.int32, (1, KV_WIDTH), 1)
        hi_part = lax.shift_right_logical(pos, 6).astype(F32)
        lo_part = (pos & (SEL_BLK - 1)).astype(F32)
        ks = ks_ref[0]
        kw = kw_ref[0]
        for g in range(NSA_KV):
            off = lane - ((g + 1) % NSA_KV) * HEAD_DIM
            feat = jnp.where(off < 3, hi_part, jnp.where(off < 6, lo_part, 0.0))
            spare = (off >= 0) & (off < HEAD_DIM)
            ksb[g] = jnp.where(spare, feat, ks).astype(BF16)
            kwb[g] = jnp.where(spare, feat, kw).astype(BF16)
        vsb[...] = vs_ref[0].astype(BF16)
        vwb[...] = vw_ref[0].astype(BF16)

    q = q_ref[0] * SCALE
    gates = gate_ref[0]
    kcb = kc_ref[0, 0].astype(BF16)
    vcb = vc_ref[0, 0].astype(BF16)
    q0 = i * tq
    qpos_col = q0 + lax.broadcasted_iota(jnp.int32, (tq, 1), 0)
    qpos4 = jnp.concatenate([qpos_col] * NSA_REP, axis=0)
    qpos_row = q0 + lax.broadcasted_iota(jnp.int32, (1, tq), 1)
    cur_row = lax.shift_right_logical(qpos_row, 6)
    win_start = pl.multiple_of(jnp.maximum(q0 - WINDOW, 0), NSA_TQ)
    n_chunks = lax.shift_right_logical(q0 + tq + SEL_CHUNK - 1, SEL_CHUNK_LOG2)

    groups = range(NSA_KV)
    qgs = [jnp.concatenate(
        [jnp.where(_group_mask(KV_WIDTH, g), q[:, r * KV_WIDTH:(r + 1) * KV_WIDTH], 0.0) for r in range(NSA_REP)],
        axis=0).astype(BF16) for g in groups]

    def gate_col(g, branch):
        cols = [gates[:, g * 12 + r * 3 + branch:g * 12 + r * 3 + branch + 1] for r in range(NSA_REP)]
        return jnp.concatenate(cols, axis=0)

    q_pos = [qgs[g] + slope_ref[g] for g in groups]
    all_reps = lambda a: jnp.concatenate([a] * NSA_REP, axis=0)

    n_idx = lax.broadcasted_iota(jnp.int32, (1, N_HALF), 1)
    dist = (qpos4 - (n_idx * CMP_STRIDE + (CMP_LEN - 1))).astype(F32)
    valid = dist >= 0.0
    o_cmp, sel_qs = [], []
    for g in groups:
        slope = jnp.concatenate([jnp.full((tq, 1), _slope(g, r), F32) for r in range(NSA_REP)], axis=0)
        e, den = _softmax_rows(_dot_nt(qgs[g], kcb) - slope * dist, valid)
        pc = e / den
        o_cmp.append(_dot(pc.astype(BF16), vcb))
        pcsum = pc[0:tq]
        for r in range(1, NSA_REP):
            pcsum = pcsum + pc[r * tq:(r + 1) * tq]
        hi, lo = _split_bf16(pcsum)
        imp = _dot_nt(cover_ref[...], hi) + _dot_nt(cover_ref[...], lo)
        sel = _select_blocks(imp, cur_row, SEQ // SEL_BLK)
        sel = jnp.concatenate([sel, jnp.zeros((128 - sel.shape[0], tq), F32)], axis=0)
        sel_qs.append(sel.T.astype(BF16))

    def sel_step(kk, carry):
        ks = pl.multiple_of(kk * SEL_CHUNK, SEL_CHUNK)
        kpos = ks + lax.broadcasted_iota(jnp.int32, (1, SEL_CHUNK), 1)
        causal = (qpos_col - kpos) >= 0
        vblk = vsb[pl.ds(ks, SEL_CHUNK), :]
        new = []
        for g in groups:
            m, l, acc = carry[g]
            chosen = _dot(sel_qs[g], expand_ref[kk])
            mask_bias = jnp.where((chosen > 0.5) & causal, 0.0, NEG)
            s = _dot_nt(q_pos[g], ksb[g, pl.ds(ks, SEL_CHUNK), :]) + all_reps(mask_bias)
            m_new = jnp.maximum(m, jnp.max(s, axis=-1, keepdims=True))
            p = jnp.exp(s - m_new)
            alpha = jnp.exp(m - m_new)
            l = alpha * l + jnp.sum(p, axis=-1, keepdims=True)
            acc = alpha * acc + _dot(p.astype(BF16), vblk)
            new.append((m_new, l, acc))
        return tuple(new)

    init = tuple((jnp.full((rows, 1), NEG, F32), jnp.zeros((rows, 1), F32), jnp.zeros((rows, KV_WIDTH), F32))
                 for _ in groups)
    sel_state = lax.fori_loop(0, n_chunks, sel_step, init)

    kpos = win_start + lax.broadcasted_iota(jnp.int32, (1, WIN_KEYS), 1)
    di = qpos_col - kpos
    win_bias = all_reps(jnp.where((di >= 0) & (di < WINDOW), 0.0, NEG))
    vwin = vwb[pl.ds(win_start, WIN_KEYS), :]

    out = [jnp.zeros((tq, KV_WIDTH), F32) for _ in range(NSA_REP)]
    for g in groups:
        s = _dot_nt(q_pos[g], kwb[g, pl.ds(win_start, WIN_KEYS), :]) + win_bias
        e = jnp.exp(s - jnp.max(s, axis=-1, keepdims=True))
        o_win = _dot(e.astype(BF16), vwin) / jnp.sum(e, axis=-1, keepdims=True)
        _, l_sel, acc_sel = sel_state[g]
        o_sel = acc_sel / jnp.maximum(l_sel, 1e-30)
        mixed = gate_col(g, 0) * o_cmp[g] + gate_col(g, 1) * o_sel + gate_col(g, 2) * o_win
        gm = _group_mask(KV_WIDTH, g)
        for r in range(NSA_REP):
            out[r] = out[r] + jnp.where(gm, mixed[r * tq:(r + 1) * tq], 0.0)

    for r in range(NSA_REP):
        o_ref[0, :, r * KV_WIDTH:(r + 1) * KV_WIDTH] = out[r].astype(o_ref.dtype)


def _nsa_prompt(nq, gates, cmp_tok, kv, win, cover_t, expand):
    b, t, tq = BATCH, SEQ, NSA_TQ
    nq3 = nq.reshape(b, t, NSA_WIDTH)
    g3 = gates.reshape(b, t, 128)
    kv3 = kv.reshape(b, t, 4 * KV_WIDTH)
    win3 = win.reshape(b, t, 2 * KV_WIDTH)
    col = lambda c: pl.BlockSpec((1, t, KV_WIDTH), lambda bi, i: (bi, 0, c))
    tok = lambda c: pl.BlockSpec((1, 1, N_HALF, KV_WIDTH), lambda bi, i: (c, bi, 0, 0))
    slope = jnp.asarray([[_slope(g, r) for r in range(NSA_REP)] for g in range(NSA_KV)], F32)
    p1 = slope.astype(BF16)
    p2 = (slope - p1.astype(F32)).astype(BF16)
    p3 = (slope - p1.astype(F32) - p2.astype(F32)).astype(BF16)
    pieces = jnp.stack([p1, p2, p3], axis=-1).astype(F32)
    six = jnp.concatenate([pieces * float(SEL_BLK), pieces], axis=-1)
    lanes = jnp.zeros((NSA_KV, NSA_REP, KV_WIDTH), F32)
    for g in range(NSA_KV):
        start = ((g + 1) % NSA_KV) * HEAD_DIM
        lanes = lanes.at[g, :, start:start + 6].set(six[g])
    slope_tab = jnp.broadcast_to(lanes[:, :, None, :], (NSA_KV, NSA_REP, tq, KV_WIDTH))
    slope_tab = slope_tab.reshape(NSA_KV, NSA_REP * tq, KV_WIDTH).astype(BF16)
    return pl.pallas_call(
        _nsa_prompt_kernel,
        grid=(b, t // tq),
        in_specs=[pl.BlockSpec((1, tq, NSA_WIDTH), lambda bi, i: (bi, i, 0)),
                  pl.BlockSpec((1, tq, 128), lambda bi, i: (bi, i, 0)),
                  tok(0), tok(1), col(2), col(3), col(0), col(1),
                  pl.BlockSpec(cover_t.shape, lambda bi, i: (0, 0)),
                  pl.BlockSpec(expand.shape, lambda bi, i: (0, 0, 0)),
                  pl.BlockSpec(slope_tab.shape, lambda bi, i: (0, 0, 0))],
        out_specs=pl.BlockSpec((1, tq, NSA_WIDTH), lambda bi, i: (bi, i, 0)),
        out_shape=jax.ShapeDtypeStruct((b, t, NSA_WIDTH), BF16),
        scratch_shapes=[pltpu.VMEM((NSA_KV, t, KV_WIDTH), BF16), pltpu.VMEM((t, KV_WIDTH), BF16),
                        pltpu.VMEM((NSA_KV, t, KV_WIDTH), BF16), pltpu.VMEM((t, KV_WIDTH), BF16)],
        compiler_params=_params(("parallel", "arbitrary")),
        name="nsa_prompt",
    )(nq3, g3, cmp_tok, cmp_tok, kv3, kv3, win3, win3, cover_t, expand, slope_tab)


SAMPLE_SEL_BLOCKS = -(-(PAST_LEN + DEC_SEQ) // SEL_BLK)
SEL_ROWS = 48


def _nsa_sample_kernel(pt_ref, *refs):
    pages = refs[:N_PAGES]
    (q_ref, gate_ref, kc_ref, vc_ref, new_ref, cw_ref, wnew_ref, cover_ref, perm_ref, o_ref, wout_ref) = refs[N_PAGES:]
    t = DEC_SEQ
    nrow = NSA_REP * NSA_KV * t
    wb = cw_ref.shape[-1]

    q = q_ref[0] * SCALE
    pieces = []
    for r in range(NSA_REP):
        qr = q[:, r * KV_WIDTH:(r + 1) * KV_WIDTH]
        for g in range(NSA_KV):
            pieces.append(jnp.where(_group_mask(KV_WIDTH, g), qr, 0.0))
    qm = jnp.concatenate(pieces, axis=0).astype(BF16)

    rowi = lax.broadcasted_iota(jnp.int32, (nrow, 1), 0)
    g_col = lax.shift_right_logical(rowi, 3) & (NSA_KV - 1)
    r_col = lax.shift_right_logical(rowi, 5)
    slope = jnp.exp2(-8.0 * (g_col * NSA_REP + r_col + 1).astype(F32) / NSA_HEADS)
    qpos = PAST_LEN + (rowi & (t - 1))
    lane = lax.broadcasted_iota(jnp.int32, (1, 128), 1)
    pad_rows = lambda a: jnp.concatenate([a, jnp.zeros((128 - t, a.shape[-1]), F32)], axis=0)

    def bias_and_mask(s, kpos_row, extra_ok=None, window=False):
        d = (qpos - kpos_row).astype(F32)
        ok = d >= 0.0
        if window:
            ok = ok & (d < float(WINDOW))
        if extra_ok is not None:
            ok = ok & extra_ok
        return s - slope * d, ok

    s, ok = bias_and_mask(_dot(qm, kc_ref[0, 0].astype(BF16)), lane * CMP_STRIDE + (CMP_LEN - 1))
    e, den = _softmax_rows(s, ok)
    pc = e / den
    o_cmp = _dot_nt(pc.astype(BF16), vc_ref[0, 0].astype(BF16))
    hi, lo = _split_bf16(pc)
    imp = _dot(hi, cover_ref[...]) + _dot(lo, cover_ref[...])
    per_rep = NSA_KV * t
    imp = imp[0:per_rep] + imp[per_rep:2 * per_rep] + imp[2 * per_rep:3 * per_rep] + imp[3 * per_rep:]
    imp_t = jnp.concatenate([imp, jnp.zeros((nrow - per_rep, 128), F32)], axis=0).T
    cur = lax.shift_right_logical(PAST_LEN + (lane & (t - 1)), 6)
    sel_t = _select_blocks(imp_t[0:SEL_ROWS], cur, SAMPLE_SEL_BLOCKS)
    sel = jnp.concatenate([sel_t, jnp.zeros((128 - SEL_ROWS, 128), F32)], axis=0).T[0:per_rep]
    sel = jnp.concatenate([sel] * NSA_REP, axis=0)

    scores, oks = [], []
    for p in range(N_PAGES):
        chosen = jnp.where(lane < SEL_BLK, sel[:, 2 * p:2 * p + 1], sel[:, 2 * p + 1:2 * p + 2])
        s, ok = bias_and_mask(_dot(qm, pages[p][0, 0].astype(BF16)), p * PAGE_SIZE + lane, chosen > 0.5)
        scores.append(s)
        oks.append(ok)
    knew = pad_rows(new_ref[0, :, 0:KV_WIDTH]).astype(BF16)
    vnew = pad_rows(new_ref[0, :, KV_WIDTH:2 * KV_WIDTH]).astype(BF16)
    last = SAMPLE_SEL_BLOCKS - 1
    s, ok = bias_and_mask(_dot_nt(qm, knew), PAST_LEN + lane, (sel[:, last:last + 1] > 0.5) & (lane < t))
    scores.append(s)
    oks.append(ok)
    e, den = _softmax_rows(jnp.concatenate(scores, axis=-1), jnp.concatenate(oks, axis=-1))
    prob = (e / den).astype(BF16)
    o_sel = _dot(prob[:, PAST_LEN:], vnew)
    for p in range(N_PAGES):
        o_sel = o_sel + _dot_nt(prob[:, p * PAGE_SIZE:(p + 1) * PAGE_SIZE], pages[p][0, 1].astype(BF16))

    lane_w = lax.broadcasted_iota(jnp.int32, (1, wb), 1)
    s_old, ok_old = bias_and_mask(_dot(qm, cw_ref[0, 0].astype(BF16)), PAST_LEN - wb + lane_w, window=True)
    kwn = pad_rows(wnew_ref[0, :, 0:KV_WIDTH]).astype(BF16)
    vwn = pad_rows(wnew_ref[0, :, KV_WIDTH:2 * KV_WIDTH]).astype(BF16)
    s_new, ok_new = bias_and_mask(_dot_nt(qm, kwn), PAST_LEN + lane, lane < t, window=True)
    e, den = _softmax_rows(jnp.concatenate([s_old, s_new], axis=-1), jnp.concatenate([ok_old, ok_new], axis=-1))
    prob = (e / den).astype(BF16)
    o_win = _dot_nt(prob[:, 0:wb], cw_ref[0, 1].astype(BF16)) + _dot(prob[:, wb:], vwn)

    ghi, glo = _split_bf16(pad_rows(gate_ref[0]))
    grow = _dot(perm_ref[...], ghi) + _dot(perm_ref[...], glo)
    base = g_col * 12 + r_col * 3

    def gate_col(branch):
        return jnp.sum(jnp.where(lane == base + branch, grow, 0.0), axis=-1, keepdims=True)

    mixed = gate_col(0) * o_cmp + gate_col(1) * o_sel + gate_col(2) * o_win
    for r in range(NSA_REP):
        acc = jnp.zeros((t, KV_WIDTH), F32)
        for g in range(NSA_KV):
            lo_row = (r * NSA_KV + g) * t
            acc = acc + jnp.where(_group_mask(KV_WIDTH, g), mixed[lo_row:lo_row + t], 0.0)
        o_ref[0, :, r * KV_WIDTH:(r + 1) * KV_WIDTH] = acc.astype(o_ref.dtype)

    for c in range(2):
        new_t = pad_rows(wnew_ref[0, :, c * KV_WIDTH:(c + 1) * KV_WIDTH]).T
        tail = jnp.concatenate([jnp.zeros((KV_WIDTH, wb - 128), F32), pltpu.roll(new_t, 128 - t, 1)], axis=-1)
        wout_ref[0, c] = jnp.where(lane_w >= wb - t, tail, pltpu.roll(cw_ref[0, c], wb - t, 1))


def _nsa_sample(nq, gates, cmp_tok, cache_t, page_flat, kv_new, win_t, win_new, cover, perm):
    b, t = DEC_BATCH, DEC_SEQ
    wb = win_t.shape[-1]
    per_b = lambda shape: pl.BlockSpec(shape, lambda bi, pt: (bi,) + (0,) * (len(shape) - 1))
    page_specs = [pl.BlockSpec((1, 2, KV_WIDTH, PAGE_SIZE),
                               functools.partial(lambda bi, pt, p: (pt[bi * N_PAGES + p], 1, 0, 0), p=p))
                  for p in range(N_PAGES)]
    tok = lambda c: pl.BlockSpec((1, 1, KV_WIDTH, N_HALF), lambda bi, pt: (c, bi, 0, 0))
    grid_spec = pltpu.PrefetchScalarGridSpec(
        num_scalar_prefetch=1,
        grid=(b,),
        in_specs=page_specs + [per_b((1, t, NSA_WIDTH)), per_b((1, t, 128)), tok(0), tok(1),
                               pl.BlockSpec((1, t, 2 * KV_WIDTH), lambda bi, pt: (bi, 0, 1)),
                               per_b((1, 2, KV_WIDTH, wb)), per_b((1, t, 2 * KV_WIDTH)),
                               pl.BlockSpec(cover.shape, lambda bi, pt: (0, 0)),
                               pl.BlockSpec(perm.shape, lambda bi, pt: (0, 0))],
        out_specs=[per_b((1, t, NSA_WIDTH)), per_b((1, 2, KV_WIDTH, wb))],
    )
    return pl.pallas_call(
        _nsa_sample_kernel,
        grid_spec=grid_spec,
        out_shape=[jax.ShapeDtypeStruct((b, t, NSA_WIDTH), BF16), jax.ShapeDtypeStruct(win_t.shape, F32)],
        compiler_params=_params(("arbitrary",)),
        name="nsa_sample",
    )(page_flat, *([cache_t] * N_PAGES), nq.reshape(b, t, NSA_WIDTH), gates.reshape(b, t, 128), cmp_tok, cmp_tok,
      kv_new.reshape(b, t, 4 * KV_WIDTH), win_t, win_new.reshape(b, t, 2 * KV_WIDTH), cover, perm)


def _out_kernel(x_ref, ret_ref, nsa_ref, wr_ref, wn_ref, o_ref, wr16, wn16):
    wr = _bf16_weights(wr_ref, wr16, 1)
    wn = _bf16_weights(wn_ref, wn16, 1)
    o_ref[...] = x_ref[...] + _dot(ret_ref[...], wr[...]) + _dot(nsa_ref[...], wn[...])


def _mixer_out(x2d, ret, nsa, w_out, w_nsa, tm=1024, tn=1024):
    n, d = x2d.shape
    return pl.pallas_call(
        _out_kernel,
        grid=(d // tn, n // tm),
        in_specs=[pl.BlockSpec((tm, tn), lambda j, i: (i, j)),
                  pl.BlockSpec((tm, RET_WIDTH), lambda j, i: (i, 0)),
                  pl.BlockSpec((tm, NSA_WIDTH), lambda j, i: (i, 0)),
                  pl.BlockSpec((RET_WIDTH, tn), lambda j, i: (0, j)),
                  pl.BlockSpec((NSA_WIDTH, tn), lambda j, i: (0, j))],
        out_specs=pl.BlockSpec((tm, tn), lambda j, i: (i, j)),
        out_shape=jax.ShapeDtypeStruct((n, d), F32),
        scratch_shapes=[pltpu.VMEM((RET_WIDTH, tn), BF16), pltpu.VMEM((NSA_WIDTH, tn), BF16)],
        compiler_params=_params(("arbitrary", "arbitrary")),
        name="mixer_out",
    )(x2d, ret, nsa, w_out, w_nsa)


FFN_TN = 512
FFN_COLS = D_FF // FFN_TN
FFN_SUB = 512


def _ffn_up_kernel(*refs, hist, step, tiles_per_seq, has_prev):
    if has_prev:
        x_ref, wa_ref, wb_ref, cwa_ref, cwb_ref, cba_ref, cbb_ref, pa_ref, pb_ref = refs[:9]
        act_ref, sta_ref, stb_ref, exta, extb, wa16, wb16 = refs[9:]
    else:
        x_ref, wa_ref, wb_ref, cwa_ref, cwb_ref, cba_ref, cbb_ref = refs[:7]
        act_ref, sta_ref, stb_ref, exta, extb, wa16, wb16 = refs[7:]
        pa_ref = pb_ref = None
    tm = x_ref.shape[0]
    first = (pl.program_id(1) % tiles_per_seq) == 0
    x = x_ref[...]
    wa_ref = _bf16_weights(wa_ref, wa16, 1)
    wb_ref = _bf16_weights(wb_ref, wb16, 1)
    halves = ((wa_ref, cwa_ref, cba_ref, pa_ref, sta_ref, exta), (wb_ref, cwb_ref, cbb_ref, pb_ref, stb_ref, extb))
    if not has_prev:
        @pl.when(first)
        def _():
            for half in halves:
                half[5][hist - 2 * step:hist, :] = jnp.zeros((2 * step, half[5].shape[-1]), F32)
    pieces = act_ref.shape[-1] // FFN_SUB
    ups = [[_dot(x, half[0][:, k * FFN_SUB:(k + 1) * FFN_SUB]) for half in halves] for k in range(pieces)]
    for k in range(pieces):
        cols = slice(k * FFN_SUB, (k + 1) * FFN_SUB)
        outs = []
        for (w_ref, cw_ref, cb_ref, p_ref, st_ref, ext), u in zip(halves, ups[k]):
            if has_prev:
                ext[hist - 2 * step:hist, cols] = p_ref[:, cols]
            ext[hist:hist + tm, cols] = u
            cw = cw_ref[:, cols]
            c = (cb_ref[:, cols] + cw[2:3] * u + cw[1:2] * ext[hist - step:hist - step + tm, cols]
                 + cw[0:1] * ext[hist - 2 * step:hist - 2 * step + tm, cols])
            outs.append(c)
            tail = ext[hist + tm - 2 * step:hist + tm, cols]
            if has_prev:
                st_ref[:, cols] = tail
            else:
                st_ref[0, :, cols] = tail
                ext[hist - 2 * step:hist, cols] = tail
        act_ref[:, cols] = (jax.nn.silu(outs[0]) * outs[1]).astype(act_ref.dtype)


def _ffn_up(hn, w_up, conv_w, conv_b, prev, *, tm, step, seqs, tiles_per_seq):
    n, d = hn.shape
    tn = FFN_TN
    hist = 8 if step == 1 else 2 * step
    has_prev = prev is not None
    half = lambda off: (lambda j, i: (0, j + off))
    in_specs = [pl.BlockSpec((tm, d), lambda j, i: (i, 0)),
                pl.BlockSpec((d, tn), half(0)), pl.BlockSpec((d, tn), half(FFN_COLS)),
                pl.BlockSpec((CONV_W, tn), half(0)), pl.BlockSpec((CONV_W, tn), half(FFN_COLS)),
                pl.BlockSpec((1, tn), half(0)), pl.BlockSpec((1, tn), half(FFN_COLS))]
    args = [hn, w_up, w_up, conv_w, conv_w, conv_b, conv_b]
    if has_prev:
        in_specs += [pl.BlockSpec((2 * step, tn), half(0)), pl.BlockSpec((2 * step, tn), half(FFN_COLS))]
        args += [prev, prev]
        st_spec = pl.BlockSpec((2 * step, tn), lambda j, i: (0, j))
        st_shape = jax.ShapeDtypeStruct((2 * step, D_FF), F32)
    else:
        st_spec = pl.BlockSpec((1, 2, tn), lambda j, i: (i // tiles_per_seq, 0, j))
        st_shape = jax.ShapeDtypeStruct((seqs, 2, D_FF), F32)
    kernel = functools.partial(_ffn_up_kernel, hist=hist, step=step, tiles_per_seq=tiles_per_seq, has_prev=has_prev)
    return pl.pallas_call(
        kernel,
        grid=(FFN_COLS, n // tm),
        in_specs=in_specs,
        out_specs=[pl.BlockSpec((tm, tn), lambda j, i: (i, j)), st_spec, st_spec],
        out_shape=[jax.ShapeDtypeStruct((n, D_FF), BF16), st_shape, st_shape],
        scratch_shapes=[pltpu.VMEM((hist + tm, tn), F32), pltpu.VMEM((hist + tm, tn), F32),
                        pltpu.VMEM((d, tn), BF16), pltpu.VMEM((d, tn), BF16)],
        compiler_params=_params(("arbitrary", "arbitrary")),
        name="ffn_up",
    )(*args)


def _ffn_down_kernel(h_ref, a_ref, w_ref, o_ref):
    o_ref[...] = h_ref[...] + _dot(a_ref[...], w_ref[...])


def _ffn_down(h, act, w_down, tm=1024, tn=512):
    n, d = h.shape
    return pl.pallas_call(
        _ffn_down_kernel,
        grid=(n // tm, d // tn),
        in_specs=[pl.BlockSpec((tm, tn), lambda i, j: (i, j)),
                  pl.BlockSpec((tm, D_FF), lambda i, j: (i, 0)),
                  pl.BlockSpec((D_FF, tn), lambda i, j: (0, j))],
        out_specs=pl.BlockSpec((tm, tn), lambda i, j: (i, j)),
        out_shape=jax.ShapeDtypeStruct((n, d), F32),
        compiler_params=_params(("parallel", "arbitrary")),
        name="ffn_down",
    )(h, act, w_down)


def _block_cover_t(n_sel, rows):
    cs = np.arange(N_HALF - 1)[None, :] * CMP_STRIDE
    js = np.arange(n_sel)[:, None] * SEL_BLK
    cov = np.clip(np.minimum(cs + CMP_LEN, js + SEL_BLK) - np.maximum(cs, js), 0, None) / CMP_LEN
    out = np.zeros((rows, N_HALF), np.float32)
    out[:n_sel, :N_HALF - 1] = cov
    return jnp.asarray(out, dtype=BF16)


def _prepare_weights(g_attn, w_in, q_norm, k_norm_cmp, k_norm_slc, k_norm_win, cmp_pe, cmp_w1, cmp_b1, cmp_w2,
                     cmp_b2, w_out):
    d = D_MODEL
    w_in_t = w_in.T
    q0 = 4 * RET_WIDTH
    w_nq_t = w_in_t[q0:q0 + NSA_WIDTH].reshape(NSA_KV, NSA_REP, HEAD_DIM, d).transpose(1, 0, 2, 3).reshape(NSA_WIDTH, d)
    w_ng_t = w_in_t[q0 + NSA_WIDTH + 6 * KV_WIDTH:]
    w_ng_t = jnp.pad(w_ng_t, ((0, 128 - w_ng_t.shape[0]), (0, 0)))
    tile4 = lambda v: jnp.tile(v, NSA_KV)
    zeros = jnp.zeros((KV_WIDTH,), F32)
    ones = jnp.ones((KV_WIDTH,), F32)
    wts = dict(
        g_attn=g_attn, w_in_t=w_in_t, w_nq_t=w_nq_t, w_ng_t=w_ng_t,
        rot_scale=jnp.concatenate([jnp.ones((RET_WIDTH,), F32),
                                   jnp.full((RET_WIDTH,), RET_DK ** -0.5, F32)]).reshape(1, -1),
        nq_gain=jnp.tile(q_norm, NSA_HEADS).reshape(1, -1), nq_mask=jnp.ones((1, NSA_WIDTH), F32),
        kv_gain=jnp.concatenate([zeros, zeros, tile4(k_norm_slc), zeros]).reshape(1, -1),
        kv_mask=jnp.concatenate([zeros, zeros, ones, zeros]).reshape(1, -1),
        win_gain=jnp.concatenate([tile4(k_norm_win), zeros]).reshape(1, -1),
        win_mask=jnp.concatenate([ones, zeros]).reshape(1, -1),
    )
    w1 = cmp_w1.reshape(2, 2, CMP_STRIDE, HEAD_DIM, CMP_HIDDEN)
    w1rep = jnp.broadcast_to(w1.transpose(0, 2, 3, 1, 4)[:, :, None],
                             (2, CMP_STRIDE, NSA_KV, HEAD_DIM, 2, CMP_HIDDEN))
    w1rep = w1rep.reshape(2, CMP_STRIDE * KV_WIDTH, 2 * CMP_HIDDEN).astype(BF16)
    w1flat = w1.reshape(2, 2, CMP_STRIDE * HEAD_DIM, CMP_HIDDEN).astype(BF16)
    pe8 = jnp.pad(cmp_pe.reshape(2, 2, 1, CMP_STRIDE * HEAD_DIM), ((0, 0), (0, 0), (0, 15), (0, 0)))
    bias = _cmp_bias(pe8, w1flat, cmp_b1.reshape(2, 1, CMP_HIDDEN))
    w1pair = jnp.broadcast_to(w1.transpose(0, 2, 3, 1, 4)[:, :, None], (2, CMP_STRIDE, 2, HEAD_DIM, 2, CMP_HIDDEN))
    col = lambda v: jnp.broadcast_to(v[:, :, None], (2, HEAD_DIM, N_HALF))
    cw = dict(
        w1pair=w1pair.reshape(2, CMP_STRIDE * 128, 2 * CMP_HIDDEN).astype(BF16),
        w2t=cmp_w2.transpose(0, 2, 1).astype(BF16), b2col=col(cmp_b2),
        gaincol=jnp.broadcast_to(k_norm_cmp[:, None], (HEAD_DIM, N_HALF)),
        w1rep=w1rep, bias=bias,
        w2rep=jnp.tile(cmp_w2, (1, 1, NSA_KV)).astype(BF16),
        b2rep=jnp.tile(cmp_b2, (1, NSA_KV)).reshape(2, 1, KV_WIDTH),
        gain=jnp.stack([tile4(k_norm_cmp), ones]).reshape(2, 1, KV_WIDTH),
    )
    w_nsa = w_out[RET_WIDTH:].reshape(NSA_KV, NSA_REP, HEAD_DIM, d).transpose(1, 0, 2, 3).reshape(NSA_WIDTH, d)
    return wts, cw, w_out, w_nsa


def kernel(x_prompt, x_sample, cache_kv, cache_win, state_ret, state_conv, page_table, g_attn, w_in, q_norm,
           k_norm_cmp, k_norm_slc, k_norm_win, cmp_pe, cmp_w1, cmp_b1, cmp_w2, cmp_b2, ret_gn, w_out, g_ffn, w_up,
           conv_w, conv_b, w_down):
    assert x_prompt.shape == (BATCH, SEQ, D_MODEL) and x_sample.shape == (DEC_BATCH, DEC_SEQ, D_MODEL)
    assert g_attn.shape[0] == 1, "single layer"
    wts, cw, w_ret, w_nsa = _prepare_weights(g_attn[0], w_in[0], q_norm[0], k_norm_cmp[0], k_norm_slc[0],
                                              k_norm_win[0], cmp_pe[0], cmp_w1[0], cmp_b1[0], cmp_w2[0],
                                              cmp_b2[0], w_out[0])
    w_up_b = w_up[0]
    w_down_b = w_down[0].astype(BF16)
    cb = conv_b[0].reshape(1, -1)
    n_p, n_s = BATCH * SEQ, DEC_BATCH * DEC_SEQ
    xp = x_prompt.reshape(n_p, D_MODEL)
    xs = x_sample.reshape(n_s, D_MODEL)

    pos_p = jnp.tile(jnp.arange(SEQ, dtype=jnp.int32), BATCH)
    qk, vg, nq, (kv_p, kv_p_t), win_p, gates = _project_all(xp, pos_p, wts, kv_transposed_seq=SEQ)
    ret_mix, ret_state_p = _retention_prompt(qk, vg, ret_gn[0])
    kv3 = kv_p.reshape(BATCH, SEQ, 4 * KV_WIDTH)
    cmp_tok = _compress_prompt(kv3, cw)
    cover_p = _block_cover_t(SEQ // SEL_BLK, SEQ // SEL_BLK)
    key = np.arange(SEQ).reshape(SEQ // SEL_CHUNK, 1, SEL_CHUNK)
    expand = jnp.asarray((key // SEL_BLK) == np.arange(128)[None, :, None], dtype=BF16)
    nsa = _nsa_prompt(nq, gates, cmp_tok, kv_p, win_p, cover_p, expand)
    h_p = _mixer_out(xp, ret_mix.reshape(n_p, RET_WIDTH), nsa.reshape(n_p, NSA_WIDTH), w_ret, w_nsa)
    hn_p = _rmsnorm(h_p, g_ffn[0])
    act_p, st_a, st_b = _ffn_up(hn_p, w_up_b, conv_w[0], cb, None, tm=1024, step=1, seqs=BATCH,
                                tiles_per_seq=SEQ // 1024)
    y_p = _ffn_down(h_p, act_p, w_down_b)
    conv_p = jnp.concatenate([st_a, st_b], axis=-1)

    pos_s = jnp.tile(PAST_LEN + jnp.arange(DEC_SEQ, dtype=jnp.int32), DEC_BATCH)
    qk, vg, nq, kv_s, win_s, gates = _project_all(xs, pos_s, wts)
    ret_mix, ret_state_s = _retention_sample(qk, vg, state_ret[0], ret_gn[0])
    cache_t = cache_kv[0].transpose(0, 2, 3, 4, 1).reshape(cache_kv.shape[1], 4, KV_WIDTH, PAGE_SIZE)
    page_flat = page_table.reshape(-1).astype(jnp.int32)
    cmp_tok = _compress_sample(cache_t, page_flat, cw)
    wb = cache_win.shape[2]
    win_t = cache_win[0].transpose(0, 2, 3, 4, 1).reshape(DEC_BATCH, 2, KV_WIDTH, wb)
    cover_s = _block_cover_t(SAMPLE_SEL_BLOCKS, 128).T
    lanes = np.arange(128)
    perm = jnp.asarray((lanes[:, None] % DEC_SEQ) == lanes[None, :], dtype=BF16)
    nsa, win_out_t = _nsa_sample(nq, gates, cmp_tok, cache_t, page_flat, kv_s, win_t, win_s, cover_s, perm)
    h_s = _mixer_out(xs, ret_mix.reshape(n_s, RET_WIDTH), nsa.reshape(n_s, NSA_WIDTH), w_ret, w_nsa)
    to_tm = lambda a: a.reshape(DEC_BATCH, -1, a.shape[-1]).transpose(1, 0, 2).reshape(-1, a.shape[-1])
    from_tm = lambda a, t: a.reshape(t, DEC_BATCH, a.shape[-1]).transpose(1, 0, 2)
    h_tm = to_tm(h_s)
    hn_s = _rmsnorm(h_tm, g_ffn[0])
    act_s, st_a, st_b = _ffn_up(hn_s, w_up_b, conv_w[0], cb, to_tm(state_conv[0]), tm=n_s, step=DEC_BATCH,
                                seqs=DEC_BATCH, tiles_per_seq=1)
    y_s = from_tm(_ffn_down(h_tm, act_s, w_down_b), DEC_SEQ)
    conv_s = from_tm(jnp.concatenate([st_a, st_b], axis=-1), CONV_W - 1)

    win_sample = win_out_t.reshape(DEC_BATCH, 2, NSA_KV, HEAD_DIM, wb).transpose(0, 4, 1, 2, 3)
    return (
        y_p.reshape(BATCH, SEQ, D_MODEL),
        y_s,
        kv_p_t.reshape(BATCH, 4, NSA_KV, HEAD_DIM, SEQ).transpose(0, 4, 1, 2, 3)[None],
        kv_s.reshape(1, DEC_BATCH, DEC_SEQ, 4, NSA_KV, HEAD_DIM),
        win_p.reshape(BATCH, SEQ, 2 * KV_WIDTH)[:, SEQ - WINDOW:].reshape(1, BATCH, WINDOW, 2, NSA_KV, HEAD_DIM),
        win_sample[None],
        ret_state_p[None],
        ret_state_s[None],
        conv_p[None],
        conv_s[None],
    )
```

```python
import functools

import numpy as np
import jax
import jax.numpy as jnp
from jax import lax
from jax.experimental import pallas as pl
from jax.experimental.pallas import tpu as pltpu

D_MODEL = 2048
BATCH = 4
SEQ = 2048
DEC_BATCH = 128
DEC_SEQ = 8
PAST_LEN = 2048
PAGE_SIZE = 128
N_PAGES = PAST_LEN // PAGE_SIZE
RET_HEADS = 4
RET_DK = 256
RET_DV = 256
RET_WIDTH = RET_HEADS * RET_DV
RET_CHUNK = 128
NSA_HEADS = 16
NSA_KV = 4
NSA_REP = NSA_HEADS // NSA_KV
HEAD_DIM = 64
NSA_WIDTH = NSA_HEADS * HEAD_DIM
KV_WIDTH = NSA_KV * HEAD_DIM
CMP_LEN = 32
CMP_STRIDE = 16
CMP_HIDDEN = 256
SEL_BLK = 64
SEL_TOPK = 16
WINDOW = 512
SCALE = HEAD_DIM ** -0.5
D_FF = 5632
CONV_W = 3
EPS = 1e-6

N_HALF = PAST_LEN // CMP_STRIDE
NEG = -1e30
FORCED = 1e30
VMEM_LIMIT = 56 * 1024 * 1024

BF16 = jnp.bfloat16
F32 = jnp.float32


def _params(semantics, vmem=VMEM_LIMIT):
    return pltpu.CompilerParams(dimension_semantics=semantics, vmem_limit_bytes=vmem)


def _dot(a, b):
    return jnp.dot(a, b, preferred_element_type=F32)


def _dot_nt(a, b):
    return lax.dot_general(a, b, (((1,), (1,)), ((), ())), preferred_element_type=F32)


def _split_bf16(x):
    hi = x.astype(BF16)
    lo = (x - hi.astype(F32)).astype(BF16)
    return hi, lo


def _group_mask(width, g):
    lane = lax.broadcasted_iota(jnp.int32, (1, width), 1)
    return (lane >= g * HEAD_DIM) & (lane < (g + 1) * HEAD_DIM)


def _rms_groups64(z, gain):
    width = z.shape[-1]
    outs = []
    for k in range(width // 128):
        zk = z[:, k * 128:(k + 1) * 128]
        zz = zk * zk
        lane = lax.broadcasted_iota(jnp.int32, (1, 128), 1)
        lo_half = lane < HEAD_DIM
        s_lo = jnp.sum(jnp.where(lo_half, zz, 0.0), axis=-1, keepdims=True)
        s_hi = jnp.sum(jnp.where(lo_half, 0.0, zz), axis=-1, keepdims=True)
        ms = jnp.where(lo_half, s_lo, s_hi) * (1.0 / HEAD_DIM)
        outs.append(zk * lax.rsqrt(ms + EPS))
    y = outs[0] if len(outs) == 1 else jnp.concatenate(outs, axis=-1)
    return y * gain


def _rmsnorm_kernel(x_ref, g_ref, o_ref):
    x = x_ref[...]
    ms = jnp.mean(x * x, axis=-1, keepdims=True)
    o_ref[...] = (x * lax.rsqrt(ms + EPS) * g_ref[...]).astype(o_ref.dtype)


def _rmsnorm(x, g, tm=1024):
    n, d = x.shape
    return pl.pallas_call(
        _rmsnorm_kernel,
        grid=(n // tm,),
        in_specs=[pl.BlockSpec((tm, d), lambda i: (i, 0)), pl.BlockSpec((1, d), lambda i: (0, 0))],
        out_specs=pl.BlockSpec((tm, d), lambda i: (i, 0)),
        out_shape=jax.ShapeDtypeStruct((n, d), BF16),
        compiler_params=_params(("parallel",)),
        name="rmsnorm",
    )(x, g.reshape(1, d))


def _bf16_weights(w_ref, wb_ref, row_axis):
    @pl.when(pl.program_id(row_axis) == 0)
    def _():
        wb_ref[...] = w_ref[...].astype(BF16)
    return wb_ref


def _proj_plain_kernel(x_ref, w_ref, o_ref, wb_ref):
    o_ref[...] = _dot_nt(x_ref[...], _bf16_weights(w_ref, wb_ref, 1)[...]).astype(o_ref.dtype)


def _proj_sigmoid_kernel(x_ref, w_ref, o_ref, wb_ref):
    o_ref[...] = jax.nn.sigmoid(_dot_nt(x_ref[...], _bf16_weights(w_ref, wb_ref, 1)[...]))


def _sub_dots(x_ref, wt_ref, width):
    x = x_ref[...]
    return [_dot_nt(x, wt_ref[k * width:(k + 1) * width, :]) for k in range(wt_ref.shape[0] // width)]


def _proj_norm_kernel(x_ref, w_ref, gain_ref, nmask_ref, o_ref, *rest):
    wb_ref = rest[-1]
    for k, z in enumerate(_sub_dots(x_ref, _bf16_weights(w_ref, wb_ref, 1), KV_WIDTH)):
        cols = slice(k * KV_WIDTH, (k + 1) * KV_WIDTH)
        y = jnp.where(nmask_ref[:, cols] > 0.5, _rms_groups64(z, gain_ref[:, cols]), z)
        o_ref[:, cols] = y
        if len(rest) == 2:
            rest[0][0, cols, :] = y.T


def _proj_rot_kernel(x_ref, w_ref, cos_ref, sin_ref, scale_ref, o_ref, wb_ref):
    c = cos_ref[...]
    s = sin_ref[...]
    for hh, z in enumerate(_sub_dots(x_ref, _bf16_weights(w_ref, wb_ref, 1), RET_DK)):
        lo, mid, hi = hh * RET_DK, hh * RET_DK + 128, (hh + 1) * RET_DK
        x1 = z[:, :128]
        x2 = z[:, 128:]
        o_ref[:, lo:mid] = (x1 * c - x2 * s) * scale_ref[:, lo:mid]
        o_ref[:, mid:hi] = (x2 * c + x1 * s) * scale_ref[:, mid:hi]


PROJ_TN = 1024


def _proj(xn, wt, row0, c, kernel, row_extras=(), col_extras=(), out_dtype=F32, tm=1024, name="proj",
          transposed_seq=None):
    n, d = xn.shape
    tn = min(PROJ_TN, c)
    off = row0 // tn
    assert row0 == off * tn and c % tn == 0
    extra_specs = ([pl.BlockSpec((tm, 128), functools.partial(lambda j, i, period: (i % period, 0),
                                                              period=a.shape[0] // tm)) for a in row_extras]
                   + [pl.BlockSpec((1, tn), lambda j, i: (0, j)) for _ in col_extras])
    out_specs = pl.BlockSpec((tm, tn), lambda j, i: (i, j))
    out_shape = jax.ShapeDtypeStruct((n, c), out_dtype)
    if transposed_seq is not None:
        per_seq = transposed_seq // tm
        out_specs = [out_specs, pl.BlockSpec((1, tn, tm), lambda j, i: (i // per_seq, j, i % per_seq))]
        out_shape = [out_shape, jax.ShapeDtypeStruct((n // transposed_seq, c, transposed_seq), out_dtype)]
    return pl.pallas_call(
        kernel,
        grid=(c // tn, n // tm),
        in_specs=[pl.BlockSpec((tm, d), lambda j, i: (i, 0)), pl.BlockSpec((tn, d), lambda j, i: (j + off, 0))]
        + extra_specs,
        out_specs=out_specs,
        out_shape=out_shape,
        scratch_shapes=[pltpu.VMEM((tn, d), BF16)],
        compiler_params=_params(("arbitrary", "arbitrary")),
        name=name,
    )(xn, wt, *row_extras, *col_extras)


def _project_all(x2d, pos, wts, tm=1024, kv_transposed_seq=None):
    xn = _rmsnorm(x2d, wts["g_attn"])
    half = RET_DK // 2
    inv = 1.0 / (10000.0 ** jnp.linspace(0.0, 1.0, half, dtype=F32))
    ang = pos.astype(F32)[:, None] * inv[None, :]
    cos = jnp.cos(ang)
    sin = jnp.sin(ang)
    w_in_t = wts["w_in_t"]
    rw = 2 * RET_WIDTH
    qk = _proj(xn, w_in_t, 0, rw, _proj_rot_kernel, (cos, sin), (wts["rot_scale"],), tm=tm, name="proj_rot")
    vg = _proj(xn, w_in_t, rw, rw, _proj_plain_kernel, tm=tm, name="proj_vg")
    nq = _proj(xn, wts["w_nq_t"], 0, NSA_WIDTH, _proj_norm_kernel, (), (wts["nq_gain"], wts["nq_mask"]), tm=tm,
               name="proj_nq")
    kv0 = 2 * rw + NSA_WIDTH
    kv = _proj(xn, w_in_t, kv0, 4 * KV_WIDTH, _proj_norm_kernel, (), (wts["kv_gain"], wts["kv_mask"]), tm=tm,
               name="proj_kv", transposed_seq=kv_transposed_seq)
    win = _proj(xn, w_in_t, kv0 + 4 * KV_WIDTH, 2 * KV_WIDTH, _proj_norm_kernel, (),
                (wts["win_gain"], wts["win_mask"]), tm=tm, name="proj_win")
    gates = _proj(xn, wts["w_ng_t"], 0, 128, _proj_sigmoid_kernel, tm=tm, name="proj_gate")
    return qk, vg, nq, kv, win, gates


def _retention_step(q, k, v, state, dmask, xi, zeta, gch):
    qb = q.astype(BF16)
    kb = k.astype(BF16)
    vb = v.astype(BF16)
    s = _dot_nt(qb, kb) * dmask
    o = _dot(s.astype(BF16), vb) + _dot(qb, state.astype(BF16)) * xi
    kz = (k * zeta).astype(BF16)
    new_state = state * gch + _dot(kz.T, vb)
    return o, new_state


def _ret_mix(o, gn, gate):
    ms = jnp.mean(o * o, axis=-1, keepdims=True)
    return o * lax.rsqrt(ms + EPS) * gn * jax.nn.silu(gate)


def _ret_prompt_kernel(qk_ref, vg_ref, dmask_ref, xi_ref, zeta_ref, gch_ref, gn_ref, mix_ref, state_ref):
    @pl.when(pl.program_id(1) == 0)
    def _():
        state_ref[...] = jnp.zeros_like(state_ref)

    for h in range(RET_HEADS):
        lo, hi = h * RET_DK, (h + 1) * RET_DK
        o, new_state = _retention_step(qk_ref[0, :, lo:hi], qk_ref[0, :, RET_WIDTH + lo:RET_WIDTH + hi],
                                       vg_ref[0, :, lo:hi], state_ref[0, h], dmask_ref[h], xi_ref[h],
                                       zeta_ref[h], gch_ref[h])
        state_ref[0, h] = new_state
        mix_ref[0, :, lo:hi] = _ret_mix(o, gn_ref[h], vg_ref[0, :, RET_WIDTH + lo:RET_WIDTH + hi]).astype(mix_ref.dtype)


def _ret_tables(chunk):
    h = jnp.arange(RET_HEADS, dtype=F32)
    lg = jnp.log(1.0 - jnp.exp2(-5.0 - h))
    i = jnp.arange(chunk, dtype=F32)
    diff = i[:, None] - i[None, :]
    dmask = jnp.where(diff >= 0, jnp.exp(lg[:, None, None] * jnp.maximum(diff, 0.0)), 0.0)
    xi = jnp.exp(lg[:, None] * (i[None, :] + 1.0))
    zeta = jnp.exp(lg[:, None] * (chunk - 1.0 - i[None, :]))
    gch = jnp.exp(lg * chunk)
    bc = lambda a: jnp.broadcast_to(a[:, :, None], (RET_HEADS, chunk, RET_DV))
    return dmask, bc(xi), bc(zeta), jnp.broadcast_to(gch[:, None, None], (RET_HEADS, 1, RET_DV))


def _retention_prompt(qk, vg, gn):
    b, t = BATCH, SEQ
    c = RET_CHUNK
    qk3 = qk.reshape(b, t, 2 * RET_WIDTH)
    vg3 = vg.reshape(b, t, 2 * RET_WIDTH)
    dmask, xi, zeta, gch = _ret_tables(c)
    rows = pl.BlockSpec((1, c, 2 * RET_WIDTH), lambda bi, ci: (bi, ci, 0))
    full = lambda shape: pl.BlockSpec(shape, lambda bi, ci: (0,) * len(shape))
    return pl.pallas_call(
        _ret_prompt_kernel,
        grid=(b, t // c),
        in_specs=[rows, rows, full((RET_HEADS, c, c)), full((RET_HEADS, c, RET_DV)), full((RET_HEADS, c, RET_DV)),
                  full((RET_HEADS, 1, RET_DV)), full((RET_HEADS, 1, RET_DV))],
        out_specs=[pl.BlockSpec((1, c, RET_WIDTH), lambda bi, ci: (bi, ci, 0)),
                   pl.BlockSpec((1, RET_HEADS, RET_DK, RET_DV), lambda bi, ci: (bi, 0, 0, 0))],
        out_shape=[jax.ShapeDtypeStruct((b, t, RET_WIDTH), BF16),
                   jax.ShapeDtypeStruct((b, RET_HEADS, RET_DK, RET_DV), F32)],
        compiler_params=_params(("parallel", "arbitrary")),
        name="retention_prompt",
    )(qk3, vg3, dmask, xi, zeta, gch, gn.reshape(RET_HEADS, 1, RET_DV))


def _ret_sample_kernel(qk_ref, vg_ref, s0_ref, dmask_ref, xi_ref, zeta_ref, gch_ref, gn_ref, mix_ref, state_ref):
    t = qk_ref.shape[1]
    pad = lambda a: jnp.concatenate([a, jnp.zeros((RET_PAD - t, a.shape[-1]), F32)], axis=0)
    for s in range(qk_ref.shape[0]):
        for h in range(RET_HEADS):
            lo, hi = h * RET_DK, (h + 1) * RET_DK
            gate = vg_ref[s, :, RET_WIDTH + lo:RET_WIDTH + hi]
            o, new_state = _retention_step(pad(qk_ref[s, :, lo:hi]), pad(qk_ref[s, :, RET_WIDTH + lo:RET_WIDTH + hi]),
                                           pad(vg_ref[s, :, lo:hi]), s0_ref[s, h], dmask_ref[h], xi_ref[h],
                                           zeta_ref[h], gch_ref[h])
            state_ref[s, h] = new_state
            mix_ref[s, :, lo:hi] = _ret_mix(o[0:t], gn_ref[h], gate).astype(mix_ref.dtype)


RET_PAD = 128
RET_SEQS_PER_STEP = 4


def _retention_sample(qk, vg, state0, gn):
    b, t = DEC_BATCH, DEC_SEQ
    dmask, xi, zeta, gch = _ret_tables(t)
    dmask = jnp.pad(dmask, ((0, 0), (0, RET_PAD - t), (0, RET_PAD - t)))
    xi = jnp.pad(xi, ((0, 0), (0, RET_PAD - t), (0, 0)))
    zeta = jnp.pad(zeta, ((0, 0), (0, RET_PAD - t), (0, 0)))
    full = lambda shape: pl.BlockSpec(shape, lambda bi: (0,) * len(shape))
    return pl.pallas_call(
        _ret_sample_kernel,
        grid=(b // RET_SEQS_PER_STEP,),
        in_specs=[pl.BlockSpec((RET_SEQS_PER_STEP, t, 2 * RET_WIDTH), lambda bi: (bi, 0, 0)),
                  pl.BlockSpec((RET_SEQS_PER_STEP, t, 2 * RET_WIDTH), lambda bi: (bi, 0, 0)),
                  pl.BlockSpec((RET_SEQS_PER_STEP, RET_HEADS, RET_DK, RET_DV), lambda bi: (bi, 0, 0, 0)),
                  full((RET_HEADS, RET_PAD, RET_PAD)), full((RET_HEADS, RET_PAD, RET_DV)), full((RET_HEADS, RET_PAD, RET_DV)),
                  full((RET_HEADS, 1, RET_DV)), full((RET_HEADS, 1, RET_DV))],
        out_specs=[pl.BlockSpec((RET_SEQS_PER_STEP, t, RET_WIDTH), lambda bi: (bi, 0, 0)),
                   pl.BlockSpec((RET_SEQS_PER_STEP, RET_HEADS, RET_DK, RET_DV), lambda bi: (bi, 0, 0, 0))],
        out_shape=[jax.ShapeDtypeStruct((b, t, RET_WIDTH), BF16),
                   jax.ShapeDtypeStruct((b, RET_HEADS, RET_DK, RET_DV), F32)],
        compiler_params=_params(("parallel",)),
        name="retention_sample",
    )(qk.reshape(b, t, -1), vg.reshape(b, t, -1), state0, dmask, xi, zeta, gch, gn.reshape(RET_HEADS, 1, RET_DV))


def _cmp_bias_kernel(pe_ref, w_ref, b1_ref, o_ref):
    acc = b1_ref[0]
    for j in range(2):
        acc = acc + _dot(pe_ref[0, j].astype(BF16), w_ref[0, j])[0:1]
    o_ref[0] = acc


def _cmp_bias(pe8, w1flat, b1):
    return pl.pallas_call(
        _cmp_bias_kernel,
        grid=(2,),
        in_specs=[pl.BlockSpec((1, 2, 16, CMP_STRIDE * HEAD_DIM), lambda c: (c, 0, 0, 0)),
                  pl.BlockSpec((1, 2, CMP_STRIDE * HEAD_DIM, CMP_HIDDEN), lambda c: (c, 0, 0, 0)),
                  pl.BlockSpec((1, 1, CMP_HIDDEN), lambda c: (c, 0, 0))],
        out_specs=pl.BlockSpec((1, 1, CMP_HIDDEN), lambda c: (c, 0, 0)),
        out_shape=jax.ShapeDtypeStruct((2, 1, CMP_HIDDEN), F32),
        compiler_params=_params(("parallel",)),
        name="cmp_bias",
    )(pe8, w1flat, b1)


def _compress_body(load_rows, w1_ref, bias_ref, w2_ref, b2_ref, gain_ref, o_ref, xm_ref):
    for l in range(CMP_STRIDE):
        xl = load_rows(l)
        for g in range(NSA_KV):
            xm_ref[g * N_HALF:(g + 1) * N_HALF, l * KV_WIDTH:(l + 1) * KV_WIDTH] = jnp.where(
                _group_mask(KV_WIDTH, g), xl, 0.0).astype(BF16)
    acc = _dot(xm_ref[...], w1_ref[0])
    h0 = acc[:, :CMP_HIDDEN]
    h1 = pltpu.roll(acc[:, CMP_HIDDEN:], 4 * N_HALF - 1, 0)
    hid = jax.nn.silu(h0 + h1 + bias_ref[0])
    out = _dot(hid.astype(BF16), w2_ref[0]) + b2_ref[0]
    res = jnp.zeros((N_HALF, KV_WIDTH), F32)
    for g in range(NSA_KV):
        res = res + jnp.where(_group_mask(KV_WIDTH, g), out[g * N_HALF:(g + 1) * N_HALF], 0.0)
    normed = _rms_groups64(res, gain_ref[0])
    o_ref[0, 0] = jnp.where(pl.program_id(0) == 0, normed, res)


def _compress_prompt_kernel(lo_ref, hi_ref, w1_ref, bias_ref, w2_ref, b2_ref, gain_ref, o_ref, xm_ref):
    load = lambda l: jnp.concatenate(
        [r[0, pl.ds(l, N_HALF, stride=CMP_STRIDE), :] for r in (lo_ref, hi_ref)], axis=-1)
    _compress_body(load, w1_ref, bias_ref, w2_ref, b2_ref, gain_ref, o_ref, xm_ref)


def _compress_sample_kernel(pt_ref, *refs):
    pages = refs[:N_PAGES]
    w1_ref, bias_ref, w2t_ref, b2_ref, gain_ref, o_ref, t_ref, xm_ref = refs[N_PAGES:]
    lane = lax.broadcasted_iota(jnp.int32, (1, 128), 1)
    for c in range(2):
        for p in range(N_PAGES):
            xt = pages[p][0, c]
            for pair in range(2):
                t_ref[c, pair, p * PAGE_SIZE:(p + 1) * PAGE_SIZE, :] = xt[pair * 128:(pair + 1) * 128, :].T
        for l in range(CMP_STRIDE):
            for pair in range(2):
                rows = t_ref[c, pair, pl.ds(l, N_HALF, stride=CMP_STRIDE), :]
                for member in range(2):
                    g = 2 * pair + member
                    keep = (lane >= member * HEAD_DIM) & (lane < (member + 1) * HEAD_DIM)
                    xm_ref[c, g * N_HALF:(g + 1) * N_HALF, l * 128:(l + 1) * 128] = (
                        jnp.where(keep, rows, 0.0).astype(BF16))
        acc = _dot(xm_ref[c], w1_ref[c])
        h0 = acc[:, :CMP_HIDDEN]
        h1 = pltpu.roll(acc[:, CMP_HIDDEN:], 4 * N_HALF - 1, 0)
        hid = jax.nn.silu(h0 + h1 + bias_ref[c]).astype(BF16)
        outs = []
        for g in range(NSA_KV):
            og = _dot_nt(w2t_ref[c], hid[g * N_HALF:(g + 1) * N_HALF]) + b2_ref[c]
            if c == 0:
                og = og * lax.rsqrt(jnp.mean(og * og, axis=0, keepdims=True) + EPS) * gain_ref[...]
            outs.append(og)
        o_ref[c, 0] = jnp.concatenate(outs, axis=0)


def _cmp_weight_specs(nargs):
    cmap = (lambda c, b: (c, 0, 0)) if nargs == 2 else (lambda c, b, pt: (c, 0, 0))
    return [pl.BlockSpec((1, CMP_STRIDE * KV_WIDTH, 2 * CMP_HIDDEN), cmap),
            pl.BlockSpec((1, 1, CMP_HIDDEN), cmap),
            pl.BlockSpec((1, CMP_HIDDEN, KV_WIDTH), cmap),
            pl.BlockSpec((1, 1, KV_WIDTH), cmap),
            pl.BlockSpec((1, 1, KV_WIDTH), cmap)]


def _compress_prompt(kv3, cw):
    b = kv3.shape[0]
    return pl.pallas_call(
        _compress_prompt_kernel,
        grid=(2, b),
        in_specs=[pl.BlockSpec((1, SEQ, 128), lambda c, bi: (bi, 0, 2 * c)),
                  pl.BlockSpec((1, SEQ, 128), lambda c, bi: (bi, 0, 2 * c + 1))] + _cmp_weight_specs(2),
        out_specs=pl.BlockSpec((1, 1, N_HALF, KV_WIDTH), lambda c, bi: (c, bi, 0, 0)),
        out_shape=jax.ShapeDtypeStruct((2, b, N_HALF, KV_WIDTH), F32),
        scratch_shapes=[pltpu.VMEM((NSA_KV * N_HALF, CMP_STRIDE * KV_WIDTH), BF16)],
        compiler_params=_params(("arbitrary", "arbitrary")),
        name="compress_prompt",
    )(kv3, kv3, cw["w1rep"], cw["bias"], cw["w2rep"], cw["b2rep"], cw["gain"])


def _compress_sample(cache_t, page_flat, cw):
    b = DEC_BATCH
    page_specs = [pl.BlockSpec((1, 2, KV_WIDTH, PAGE_SIZE),
                               functools.partial(lambda bi, pt, p: (pt[bi * N_PAGES + p], 0, 0, 0), p=p))
                  for p in range(N_PAGES)]
    full = lambda shape: pl.BlockSpec(shape, lambda bi, pt: (0,) * len(shape))
    grid_spec = pltpu.PrefetchScalarGridSpec(
        num_scalar_prefetch=1,
        grid=(b,),
        in_specs=page_specs + [full((2, CMP_STRIDE * 128, 2 * CMP_HIDDEN)), full((2, 1, CMP_HIDDEN)),
                               full((2, HEAD_DIM, CMP_HIDDEN)), full((2, HEAD_DIM, N_HALF)),
                               full((HEAD_DIM, N_HALF))],
        out_specs=pl.BlockSpec((2, 1, KV_WIDTH, N_HALF), lambda bi, pt: (0, bi, 0, 0)),
        scratch_shapes=[pltpu.VMEM((2, 2, PAST_LEN, 128), F32),
                        pltpu.VMEM((2, NSA_KV * N_HALF, CMP_STRIDE * 128), BF16)],
    )
    return pl.pallas_call(
        _compress_sample_kernel,
        grid_spec=grid_spec,
        out_shape=jax.ShapeDtypeStruct((2, b, KV_WIDTH, N_HALF), F32),
        compiler_params=_params(("arbitrary",)),
        name="compress_sample",
    )(page_flat, *([cache_t] * N_PAGES), cw["w1pair"], cw["bias"], cw["w2t"], cw["b2col"], cw["gaincol"])


def _select_blocks(imp, cur, n_blocks):
    rows = imp.shape[0]
    j = lax.broadcasted_iota(jnp.int32, (rows, 1), 0)
    valid = j <= cur
    forced = (j == 0) | (j == cur) | (j == cur - 1)
    score = jnp.where(valid, jnp.where(forced, FORCED, imp), NEG)
    rank = jnp.zeros(score.shape, F32)
    for i in range(n_blocks):
        row = score[i:i + 1, :]
        ahead = (row > score) | ((row == score) & (j > i))
        rank = rank + jnp.where(ahead, 1.0, 0.0)
    return jnp.where(valid & (rank < float(SEL_TOPK)) & (j < n_blocks), 1.0, 0.0)


def _softmax_rows(s, valid):
    s = jnp.where(valid, s, NEG)
    m = jnp.max(s, axis=-1, keepdims=True)
    e = jnp.where(valid, jnp.exp(s - m), 0.0)
    return e, jnp.maximum(jnp.sum(e, axis=-1, keepdims=True), 1e-30)


def _softmax_cols(s, valid):
    s = jnp.where(valid, s, NEG)
    m = jnp.max(s, axis=0, keepdims=True)
    e = jnp.where(valid, jnp.exp(s - m), 0.0)
    return e, jnp.maximum(jnp.sum(e, axis=0, keepdims=True), 1e-30)


def _slope(g, r):
    return float(2.0 ** (-8.0 * (g * NSA_REP + r + 1) / NSA_HEADS))


NSA_TQ = 128
SEL_CHUNK_LOG2 = 9
SEL_CHUNK = 1 << SEL_CHUNK_LOG2
WIN_KEYS = WINDOW + NSA_TQ


def _nsa_prompt_kernel(q_ref, gate_ref, kc_ref, vc_ref, ks_ref, vs_ref, kw_ref, vw_ref, cover_ref, expand_ref,
                       slope_ref, o_ref, ksb, vsb, kwb, vwb):
    i = pl.program_id(1)
    tq = NSA_TQ
    rows = NSA_REP * tq

    @pl.when(i == 0)
    def _():
        pos = lax.broadcasted_iota(jnp.int32, (ks_ref.shape[1], 1), 0)
        lane = lax.broadcasted_iota(jnp.int32, (1, KV_WIDTH), 1)
        hi_part = lax.shift_right_logical(pos, 6).astype(F32)
        lo_part = (pos & (SEL_BLK - 1)).astype(F32)
        ks = ks_ref[0]
        kw = kw_ref[0]
        for g in range(NSA_KV):
            off = lane - ((g + 1) % NSA_KV) * HEAD_DIM
            feat = jnp.where(off < 3, hi_part, jnp.where(off < 6, lo_part, 0.0))
            spare = (off >= 0) & (off < HEAD_DIM)
            ksb[g] = jnp.where(spare, feat, ks).astype(BF16)
            kwb[g] = jnp.where(spare, feat, kw).astype(BF16)
        vsb[...] = vs_ref[0].astype(BF16)
        vwb[...] = vw_ref[0].astype(BF16)

    q = q_ref[0] * SCALE
    gates = gate_ref[0]
    kcb = kc_ref[0, 0].astype(BF16)
    vcb = vc_ref[0, 0].astype(BF16)
    q0 = i * tq
    qpos_col = q0 + lax.broadcasted_iota(jnp.int32, (tq, 1), 0)
    qpos4 = jnp.concatenate([qpos_col] * NSA_REP, axis=0)
    qpos_row = q0 + lax.broadcasted_iota(jnp.int32, (1, tq), 1)
    cur_row = lax.shift_right_logical(qpos_row, 6)
    win_start = pl.multiple_of(jnp.maximum(q0 - WINDOW, 0), NSA_TQ)
    n_chunks = lax.shift_right_logical(q0 + tq + SEL_CHUNK - 1, SEL_CHUNK_LOG2)

    groups = range(NSA_KV)
    qgs = [jnp.concatenate(
        [jnp.where(_group_mask(KV_WIDTH, g), q[:, r * KV_WIDTH:(r + 1) * KV_WIDTH], 0.0) for r in range(NSA_REP)],
        axis=0).astype(BF16) for g in groups]

    def gate_col(g, branch):
        cols = [gates[:, g * 12 + r * 3 + branch:g * 12 + r * 3 + branch + 1] for r in range(NSA_REP)]
        return jnp.concatenate(cols, axis=0)

    q_pos = [qgs[g] + slope_ref[g] for g in groups]
    all_reps = lambda a: jnp.concatenate([a] * NSA_REP, axis=0)

    n_idx = lax.broadcasted_iota(jnp.int32, (1, N_HALF), 1)
    dist = (qpos4 - (n_idx * CMP_STRIDE + (CMP_LEN - 1))).astype(F32)
    valid = dist >= 0.0
    o_cmp, sel_qs = [], []
    for g in groups:
        slope = jnp.concatenate([jnp.full((tq, 1), _slope(g, r), F32) for r in range(NSA_REP)], axis=0)
        e, den = _softmax_rows(_dot_nt(qgs[g], kcb) - slope * dist, valid)
        pc = e / den
        o_cmp.append(_dot(pc.astype(BF16), vcb))
        pcsum = pc[0:tq]
        for r in range(1, NSA_REP):
            pcsum = pcsum + pc[r * tq:(r + 1) * tq]
        hi, lo = _split_bf16(pcsum)
        imp = _dot_nt(cover_ref[...], hi) + _dot_nt(cover_ref[...], lo)
        sel = _select_blocks(imp, cur_row, SEQ // SEL_BLK)
        sel = jnp.concatenate([sel, jnp.zeros((128 - sel.shape[0], tq), F32)], axis=0)
        sel_qs.append(sel.T.astype(BF16))

    def sel_step(kk, carry):
        ks = pl.multiple_of(kk * SEL_CHUNK, SEL_CHUNK)
        kpos = ks + lax.broadcasted_iota(jnp.int32, (1, SEL_CHUNK), 1)
        causal = (qpos_col - kpos) >= 0
        vblk = vsb[pl.ds(ks, SEL_CHUNK), :]
        new = []
        for g in groups:
            m, l, acc = carry[g]
            chosen = _dot(sel_qs[g], expand_ref[kk])
            mask_bias = jnp.where((chosen > 0.5) & causal, 0.0, NEG)
            s = _dot_nt(q_pos[g], ksb[g, pl.ds(ks, SEL_CHUNK), :]) + all_reps(mask_bias)
            m_new = jnp.maximum(m, jnp.max(s, axis=-1, keepdims=True))
            p = jnp.exp(s - m_new)
            alpha = jnp.exp(m - m_new)
            l = alpha * l + jnp.sum(p, axis=-1, keepdims=True)
            acc = alpha * acc + _dot(p.astype(BF16), vblk)
            new.append((m_new, l, acc))
        return tuple(new)

    init = tuple((jnp.full((rows, 1), NEG, F32), jnp.zeros((rows, 1), F32), jnp.zeros((rows, KV_WIDTH), F32))
                 for _ in groups)
    sel_state = lax.fori_loop(0, n_chunks, sel_step, init)

    kpos = win_start + lax.broadcasted_iota(jnp.int32, (1, WIN_KEYS), 1)
    di = qpos_col - kpos
    win_bias = all_reps(jnp.where((di >= 0) & (di < WINDOW), 0.0, NEG))
    vwin = vwb[pl.ds(win_start, WIN_KEYS), :]

    out = [jnp.zeros((tq, KV_WIDTH), F32) for _ in range(NSA_REP)]
    for g in groups:
        s = _dot_nt(q_pos[g], kwb[g, pl.ds(win_start, WIN_KEYS), :]) + win_bias
        e = jnp.exp(s - jnp.max(s, axis=-1, keepdims=True))
        o_win = _dot(e.astype(BF16), vwin) / jnp.sum(e, axis=-1, keepdims=True)
        _, l_sel, acc_sel = sel_state[g]
        o_sel = acc_sel / jnp.maximum(l_sel, 1e-30)
        mixed = gate_col(g, 0) * o_cmp[g] + gate_col(g, 1) * o_sel + gate_col(g, 2) * o_win
        gm = _group_mask(KV_WIDTH, g)
        for r in range(NSA_REP):
            out[r] = out[r] + jnp.where(gm, mixed[r * tq:(r + 1) * tq], 0.0)

    for r in range(NSA_REP):
        o_ref[0, :, r * KV_WIDTH:(r + 1) * KV_WIDTH] = out[r].astype(o_ref.dtype)


def _nsa_prompt(nq, gates, cmp_tok, kv, win, cover_t, expand):
    b, t, tq = BATCH, SEQ, NSA_TQ
    nq3 = nq.reshape(b, t, NSA_WIDTH)
    g3 = gates.reshape(b, t, 128)
    kv3 = kv.reshape(b, t, 4 * KV_WIDTH)
    win3 = win.reshape(b, t, 2 * KV_WIDTH)
    col = lambda c: pl.BlockSpec((1, t, KV_WIDTH), lambda bi, i: (bi, 0, c))
    tok = lambda c: pl.BlockSpec((1, 1, N_HALF, KV_WIDTH), lambda bi, i: (c, bi, 0, 0))
    slope = jnp.asarray([[_slope(g, r) for r in range(NSA_REP)] for g in range(NSA_KV)], F32)
    p1 = slope.astype(BF16)
    p2 = (slope - p1.astype(F32)).astype(BF16)
    p3 = (slope - p1.astype(F32) - p2.astype(F32)).astype(BF16)
    pieces = jnp.stack([p1, p2, p3], axis=-1).astype(F32)
    six = jnp.concatenate([pieces * float(SEL_BLK), pieces], axis=-1)
    lanes = jnp.zeros((NSA_KV, NSA_REP, KV_WIDTH), F32)
    for g in range(NSA_KV):
        start = ((g + 1) % NSA_KV) * HEAD_DIM
        lanes = lanes.at[g, :, start:start + 6].set(six[g])
    slope_tab = jnp.broadcast_to(lanes[:, :, None, :], (NSA_KV, NSA_REP, tq, KV_WIDTH))
    slope_tab = slope_tab.reshape(NSA_KV, NSA_REP * tq, KV_WIDTH).astype(BF16)
    return pl.pallas_call(
        _nsa_prompt_kernel,
        grid=(b, t // tq),
        in_specs=[pl.BlockSpec((1, tq, NSA_WIDTH), lambda bi, i: (bi, i, 0)),
                  pl.BlockSpec((1, tq, 128), lambda bi, i: (bi, i, 0)),
                  tok(0), tok(1), col(2), col(3), col(0), col(1),
                  pl.BlockSpec(cover_t.shape, lambda bi, i: (0, 0)),
                  pl.BlockSpec(expand.shape, lambda bi, i: (0, 0, 0)),
                  pl.BlockSpec(slope_tab.shape, lambda bi, i: (0, 0, 0))],
        out_specs=pl.BlockSpec((1, tq, NSA_WIDTH), lambda bi, i: (bi, i, 0)),
        out_shape=jax.ShapeDtypeStruct((b, t, NSA_WIDTH), BF16),
        scratch_shapes=[pltpu.VMEM((NSA_KV, t, KV_WIDTH), BF16), pltpu.VMEM((t, KV_WIDTH), BF16),
                        pltpu.VMEM((NSA_KV, t, KV_WIDTH), BF16), pltpu.VMEM((t, KV_WIDTH), BF16)],
        compiler_params=_params(("parallel", "arbitrary")),
        name="nsa_prompt",
    )(nq3, g3, cmp_tok, cmp_tok, kv3, kv3, win3, win3, cover_t, expand, slope_tab)


SAMPLE_SEL_BLOCKS = -(-(PAST_LEN + DEC_SEQ) // SEL_BLK)
SEL_ROWS = 48


def _nsa_sample_kernel(pt_ref, *refs):
    pages = refs[:N_PAGES]
    (q_ref, gate_ref, kc_ref, vc_ref, new_ref, cw_ref, wnew_ref, cover_ref, perm_ref, o_ref, wout_ref) = refs[N_PAGES:]
    t = DEC_SEQ
    nrow = NSA_REP * NSA_KV * t
    wb = cw_ref.shape[-1]

    q = q_ref[0] * SCALE
    pieces = []
    for r in range(NSA_REP):
        qr = q[:, r * KV_WIDTH:(r + 1) * KV_WIDTH]
        for g in range(NSA_KV):
            pieces.append(jnp.where(_group_mask(KV_WIDTH, g), qr, 0.0))
    qm = jnp.concatenate(pieces, axis=0).astype(BF16)

    rowi = lax.broadcasted_iota(jnp.int32, (nrow, 1), 0)
    g_col = lax.shift_right_logical(rowi, 3) & (NSA_KV - 1)
    r_col = lax.shift_right_logical(rowi, 5)
    slope = jnp.exp2(-8.0 * (g_col * NSA_REP + r_col + 1).astype(F32) / NSA_HEADS)
    qpos = PAST_LEN + (rowi & (t - 1))
    lane = lax.broadcasted_iota(jnp.int32, (1, 128), 1)
    pad_rows = lambda a: jnp.concatenate([a, jnp.zeros((128 - t, a.shape[-1]), F32)], axis=0)

    def bias_and_mask(s, kpos_row, extra_ok=None, window=False):
        d = (qpos - kpos_row).astype(F32)
        ok = d >= 0.0
        if window:
            ok = ok & (d < float(WINDOW))
        if extra_ok is not None:
            ok = ok & extra_ok
        return s - slope * d, ok

    s, ok = bias_and_mask(_dot(qm, kc_ref[0, 0].astype(BF16)), lane * CMP_STRIDE + (CMP_LEN - 1))
    e, den = _softmax_rows(s, ok)
    pc = e / den
    o_cmp = _dot_nt(pc.astype(BF16), vc_ref[0, 0].astype(BF16))
    hi, lo = _split_bf16(pc)
    imp = _dot(hi, cover_ref[...]) + _dot(lo, cover_ref[...])
    per_rep = NSA_KV * t
    imp = imp[0:per_rep] + imp[per_rep:2 * per_rep] + imp[2 * per_rep:3 * per_rep] + imp[3 * per_rep:]
    imp_t = jnp.concatenate([imp, jnp.zeros((nrow - per_rep, 128), F32)], axis=0).T
    cur = lax.shift_right_logical(PAST_LEN + (lane & (t - 1)), 6)
    sel_t = _select_blocks(imp_t[0:SEL_ROWS], cur, SAMPLE_SEL_BLOCKS)
    sel = jnp.concatenate([sel_t, jnp.zeros((128 - SEL_ROWS, 128), F32)], axis=0).T[0:per_rep]
    sel = jnp.concatenate([sel] * NSA_REP, axis=0)

    scores, oks = [], []
    for p in range(N_PAGES):
        chosen = jnp.where(lane < SEL_BLK, sel[:, 2 * p:2 * p + 1], sel[:, 2 * p + 1:2 * p + 2])
        s, ok = bias_and_mask(_dot(qm, pages[p][0, 0].astype(BF16)), p * PAGE_SIZE + lane, chosen > 0.5)
        scores.append(s)
        oks.append(ok)
    knew = pad_rows(new_ref[0, :, 0:KV_WIDTH]).astype(BF16)
    vnew = pad_rows(new_ref[0, :, KV_WIDTH:2 * KV_WIDTH]).astype(BF16)
    last = SAMPLE_SEL_BLOCKS - 1
    s, ok = bias_and_mask(_dot_nt(qm, knew), PAST_LEN + lane, (sel[:, last:last + 1] > 0.5) & (lane < t))
    scores.append(s)
    oks.append(ok)
    e, den = _softmax_rows(jnp.concatenate(scores, axis=-1), jnp.concatenate(oks, axis=-1))
    prob = (e / den).astype(BF16)
    o_sel = _dot(prob[:, PAST_LEN:], vnew)
    for p in range(N_PAGES):
        o_sel = o_sel + _dot_nt(prob[:, p * PAGE_SIZE:(p + 1) * PAGE_SIZE], pages[p][0, 1].astype(BF16))

    lane_w = lax.broadcasted_iota(jnp.int32, (1, wb), 1)
    s_old, ok_old = bias_and_mask(_dot(qm, cw_ref[0, 0].astype(BF16)), PAST_LEN - wb + lane_w, window=True)
    kwn = pad_rows(wnew_ref[0, :, 0:KV_WIDTH]).astype(BF16)
    vwn = pad_rows(wnew_ref[0, :, KV_WIDTH:2 * KV_WIDTH]).astype(BF16)
    s_new, ok_new = bias_and_mask(_dot_nt(qm, kwn), PAST_LEN + lane, lane < t, window=True)
    e, den = _softmax_rows(jnp.concatenate([s_old, s_new], axis=-1), jnp.concatenate([ok_old, ok_new], axis=-1))
    prob = (e / den).astype(BF16)
    o_win = _dot_nt(prob[:, 0:wb], cw_ref[0, 1].astype(BF16)) + _dot(prob[:, wb:], vwn)

    ghi, glo = _split_bf16(pad_rows(gate_ref[0]))
    grow = _dot(perm_ref[...], ghi) + _dot(perm_ref[...], glo)
    base = g_col * 12 + r_col * 3

    def gate_col(branch):
        return jnp.sum(jnp.where(lane == base + branch, grow, 0.0), axis=-1, keepdims=True)

    mixed = gate_col(0) * o_cmp + gate_col(1) * o_sel + gate_col(2) * o_win
    for r in range(NSA_REP):
        acc = jnp.zeros((t, KV_WIDTH), F32)
        for g in range(NSA_KV):
            lo_row = (r * NSA_KV + g) * t
            acc = acc + jnp.where(_group_mask(KV_WIDTH, g), mixed[lo_row:lo_row + t], 0.0)
        o_ref[0, :, r * KV_WIDTH:(r + 1) * KV_WIDTH] = acc.astype(o_ref.dtype)

    for c in range(2):
        new_t = pad_rows(wnew_ref[0, :, c * KV_WIDTH:(c + 1) * KV_WIDTH]).T
        tail = jnp.concatenate([jnp.zeros((KV_WIDTH, wb - 128), F32), pltpu.roll(new_t, 128 - t, 1)], axis=-1)
        wout_ref[0, c] = jnp.where(lane_w >= wb - t, tail, pltpu.roll(cw_ref[0, c], wb - t, 1))


def _nsa_sample(nq, gates, cmp_tok, cache_t, page_flat, kv_new, win_t, win_new, cover, perm):
    b, t = DEC_BATCH, DEC_SEQ
    wb = win_t.shape[-1]
    per_b = lambda shape: pl.BlockSpec(shape, lambda bi, pt: (bi,) + (0,) * (len(shape) - 1))
    page_specs = [pl.BlockSpec((1, 2, KV_WIDTH, PAGE_SIZE),
                               functools.partial(lambda bi, pt, p: (pt[bi * N_PAGES + p], 1, 0, 0), p=p))
                  for p in range(N_PAGES)]
    tok = lambda c: pl.BlockSpec((1, 1, KV_WIDTH, N_HALF), lambda bi, pt: (c, bi, 0, 0))
    grid_spec = pltpu.PrefetchScalarGridSpec(
        num_scalar_prefetch=1,
        grid=(b,),
        in_specs=page_specs + [per_b((1, t, NSA_WIDTH)), per_b((1, t, 128)), tok(0), tok(1),
                               pl.BlockSpec((1, t, 2 * KV_WIDTH), lambda bi, pt: (bi, 0, 1)),
                               per_b((1, 2, KV_WIDTH, wb)), per_b((1, t, 2 * KV_WIDTH)),
                               pl.BlockSpec(cover.shape, lambda bi, pt: (0, 0)),
                               pl.BlockSpec(perm.shape, lambda bi, pt: (0, 0))],
        out_specs=[per_b((1, t, NSA_WIDTH)), per_b((1, 2, KV_WIDTH, wb))],
    )
    return pl.pallas_call(
        _nsa_sample_kernel,
        grid_spec=grid_spec,
        out_shape=[jax.ShapeDtypeStruct((b, t, NSA_WIDTH), BF16), jax.ShapeDtypeStruct(win_t.shape, F32)],
        compiler_params=_params(("arbitrary",)),
        name="nsa_sample",
    )(page_flat, *([cache_t] * N_PAGES), nq.reshape(b, t, NSA_WIDTH), gates.reshape(b, t, 128), cmp_tok, cmp_tok,
      kv_new.reshape(b, t, 4 * KV_WIDTH), win_t, win_new.reshape(b, t, 2 * KV_WIDTH), cover, perm)


def _out_kernel(x_ref, ret_ref, nsa_ref, wr_ref, wn_ref, o_ref, wr16, wn16):
    wr = _bf16_weights(wr_ref, wr16, 1)
    wn = _bf16_weights(wn_ref, wn16, 1)
    o_ref[...] = x_ref[...] + _dot(ret_ref[...], wr[...]) + _dot(nsa_ref[...], wn[...])


def _mixer_out(x2d, ret, nsa, w_out, w_nsa, tm=1024, tn=1024):
    n, d = x2d.shape
    return pl.pallas_call(
        _out_kernel,
        grid=(d // tn, n // tm),
        in_specs=[pl.BlockSpec((tm, tn), lambda j, i: (i, j)),
                  pl.BlockSpec((tm, RET_WIDTH), lambda j, i: (i, 0)),
                  pl.BlockSpec((tm, NSA_WIDTH), lambda j, i: (i, 0)),
                  pl.BlockSpec((RET_WIDTH, tn), lambda j, i: (0, j)),
                  pl.BlockSpec((NSA_WIDTH, tn), lambda j, i: (0, j))],
        out_specs=pl.BlockSpec((tm, tn), lambda j, i: (i, j)),
        out_shape=jax.ShapeDtypeStruct((n, d), F32),
        scratch_shapes=[pltpu.VMEM((RET_WIDTH, tn), BF16), pltpu.VMEM((NSA_WIDTH, tn), BF16)],
        compiler_params=_params(("arbitrary", "arbitrary")),
        name="mixer_out",
    )(x2d, ret, nsa, w_out, w_nsa)


FFN_TN = 512
FFN_COLS = D_FF // FFN_TN
FFN_SUB = 512


def _ffn_up_kernel(*refs, hist, step, tiles_per_seq, has_prev):
    if has_prev:
        x_ref, wa_ref, wb_ref, cwa_ref, cwb_ref, cba_ref, cbb_ref, pa_ref, pb_ref = refs[:9]
        act_ref, sta_ref, stb_ref, exta, extb, wa16, wb16 = refs[9:]
    else:
        x_ref, wa_ref, wb_ref, cwa_ref, cwb_ref, cba_ref, cbb_ref = refs[:7]
        act_ref, sta_ref, stb_ref, exta, extb, wa16, wb16 = refs[7:]
        pa_ref = pb_ref = None
    tm = x_ref.shape[0]
    first = (pl.program_id(1) % tiles_per_seq) == 0
    x = x_ref[...]
    wa_ref = _bf16_weights(wa_ref, wa16, 1)
    wb_ref = _bf16_weights(wb_ref, wb16, 1)
    halves = ((wa_ref, cwa_ref, cba_ref, pa_ref, sta_ref, exta), (wb_ref, cwb_ref, cbb_ref, pb_ref, stb_ref, extb))
    if not has_prev:
        @pl.when(first)
        def _():
            for half in halves:
                half[5][hist - 2 * step:hist, :] = jnp.zeros((2 * step, half[5].shape[-1]), F32)
    pieces = act_ref.shape[-1] // FFN_SUB
    ups = [[_dot(x, half[0][:, k * FFN_SUB:(k + 1) * FFN_SUB]) for half in halves] for k in range(pieces)]
    for k in range(pieces):
        cols = slice(k * FFN_SUB, (k + 1) * FFN_SUB)
        outs = []
        for (w_ref, cw_ref, cb_ref, p_ref, st_ref, ext), u in zip(halves, ups[k]):
            if has_prev:
                ext[hist - 2 * step:hist, cols] = p_ref[:, cols]
            ext[hist:hist + tm, cols] = u
            cw = cw_ref[:, cols]
            c = (cb_ref[:, cols] + cw[2:3] * u + cw[1:2] * ext[hist - step:hist - step + tm, cols]
                 + cw[0:1] * ext[hist - 2 * step:hist - 2 * step + tm, cols])
            outs.append(c)
            tail = ext[hist + tm - 2 * step:hist + tm, cols]
            if has_prev:
                st_ref[:, cols] = tail
            else:
                st_ref[0, :, cols] = tail
                ext[hist - 2 * step:hist, cols] = tail
        act_ref[:, cols] = (jax.nn.silu(outs[0]) * outs[1]).astype(act_ref.dtype)


def _ffn_up(hn, w_up, conv_w, conv_b, prev, *, tm, step, seqs, tiles_per_seq):
    n, d = hn.shape
    tn = FFN_TN
    hist = 8 if step == 1 else 2 * step
    has_prev = prev is not None
    half = lambda off: (lambda j, i: (0, j + off))
    in_specs = [pl.BlockSpec((tm, d), lambda j, i: (i, 0)),
                pl.BlockSpec((d, tn), half(0)), pl.BlockSpec((d, tn), half(FFN_COLS)),
                pl.BlockSpec((CONV_W, tn), half(0)), pl.BlockSpec((CONV_W, tn), half(FFN_COLS)),
                pl.BlockSpec((1, tn), half(0)), pl.BlockSpec((1, tn), half(FFN_COLS))]
    args = [hn, w_up, w_up, conv_w, conv_w, conv_b, conv_b]
    if has_prev:
        in_specs += [pl.BlockSpec((2 * step, tn), half(0)), pl.BlockSpec((2 * step, tn), half(FFN_COLS))]
        args += [prev, prev]
        st_spec = pl.BlockSpec((2 * step, tn), lambda j, i: (0, j))
        st_shape = jax.ShapeDtypeStruct((2 * step, D_FF), F32)
    else:
        st_spec = pl.BlockSpec((1, 2, tn), lambda j, i: (i // tiles_per_seq, 0, j))
        st_shape = jax.ShapeDtypeStruct((seqs, 2, D_FF), F32)
    kernel = functools.partial(_ffn_up_kernel, hist=hist, step=step, tiles_per_seq=tiles_per_seq, has_prev=has_prev)
    return pl.pallas_call(
        kernel,
        grid=(FFN_COLS, n // tm),
        in_specs=in_specs,
        out_specs=[pl.BlockSpec((tm, tn), lambda j, i: (i, j)), st_spec, st_spec],
        out_shape=[jax.ShapeDtypeStruct((n, D_FF), BF16), st_shape, st_shape],
        scratch_shapes=[pltpu.VMEM((hist + tm, tn), F32), pltpu.VMEM((hist + tm, tn), F32),
                        pltpu.VMEM((d, tn), BF16), pltpu.VMEM((d, tn), BF16)],
        compiler_params=_params(("arbitrary", "arbitrary")),
        name="ffn_up",
    )(*args)


def _ffn_down_kernel(h_ref, a_ref, w_ref, o_ref):
    o_ref[...] = h_ref[...] + _dot(a_ref[...], w_ref[...])


def _ffn_down(h, act, w_down, tm=1024, tn=512):
    n, d = h.shape
    return pl.pallas_call(
        _ffn_down_kernel,
        grid=(n // tm, d // tn),
        in_specs=[pl.BlockSpec((tm, tn), lambda i, j: (i, j)),
                  pl.BlockSpec((tm, D_FF), lambda i, j: (i, 0)),
                  pl.BlockSpec((D_FF, tn), lambda i, j: (0, j))],
        out_specs=pl.BlockSpec((tm, tn), lambda i, j: (i, j)),
        out_shape=jax.ShapeDtypeStruct((n, d), F32),
        compiler_params=_params(("parallel", "arbitrary")),
        name="ffn_down",
    )(h, act, w_down)


def _block_cover_t(n_sel, rows):
    cs = np.arange(N_HALF - 1)[None, :] * CMP_STRIDE
    js = np.arange(n_sel)[:, None] * SEL_BLK
    cov = np.clip(np.minimum(cs + CMP_LEN, js + SEL_BLK) - np.maximum(cs, js), 0, None) / CMP_LEN
    out = np.zeros((rows, N_HALF), np.float32)
    out[:n_sel, :N_HALF - 1] = cov
    return jnp.asarray(out, dtype=BF16)


def _prepare_weights(g_attn, w_in, q_norm, k_norm_cmp, k_norm_slc, k_norm_win, cmp_pe, cmp_w1, cmp_b1, cmp_w2,
                     cmp_b2, w_out):
    d = D_MODEL
    w_in_t = w_in.T
    q0 = 4 * RET_WIDTH
    w_nq_t = w_in_t[q0:q0 + NSA_WIDTH].reshape(NSA_KV, NSA_REP, HEAD_DIM, d).transpose(1, 0, 2, 3).reshape(NSA_WIDTH, d)
    w_ng_t = w_in_t[q0 + NSA_WIDTH + 6 * KV_WIDTH:]
    w_ng_t = jnp.pad(w_ng_t, ((0, 128 - w_ng_t.shape[0]), (0, 0)))
    tile4 = lambda v: jnp.tile(v, NSA_KV)
    zeros = jnp.zeros((KV_WIDTH,), F32)
    ones = jnp.ones((KV_WIDTH,), F32)
    wts = dict(
        g_attn=g_attn, w_in_t=w_in_t, w_nq_t=w_nq_t, w_ng_t=w_ng_t,
        rot_scale=jnp.concatenate([jnp.ones((RET_WIDTH,), F32),
                                   jnp.full((RET_WIDTH,), RET_DK ** -0.5, F32)]).reshape(1, -1),
        nq_gain=jnp.tile(q_norm, NSA_HEADS).reshape(1, -1), nq_mask=jnp.ones((1, NSA_WIDTH), F32),
        kv_gain=jnp.concatenate([zeros, zeros, tile4(k_norm_slc), zeros]).reshape(1, -1),
        kv_mask=jnp.concatenate([zeros, zeros, ones, zeros]).reshape(1, -1),
        win_gain=jnp.concatenate([tile4(k_norm_win), zeros]).reshape(1, -1),
        win_mask=jnp.concatenate([ones, zeros]).reshape(1, -1),
    )
    w1 = cmp_w1.reshape(2, 2, CMP_STRIDE, HEAD_DIM, CMP_HIDDEN)
    w1rep = jnp.broadcast_to(w1.transpose(0, 2, 3, 1, 4)[:, :, None],
                             (2, CMP_STRIDE, NSA_KV, HEAD_DIM, 2, CMP_HIDDEN))
    w1rep = w1rep.reshape(2, CMP_STRIDE * KV_WIDTH, 2 * CMP_HIDDEN).astype(BF16)
    w1flat = w1.reshape(2, 2, CMP_STRIDE * HEAD_DIM, CMP_HIDDEN).astype(BF16)
    pe8 = jnp.pad(cmp_pe.reshape(2, 2, 1, CMP_STRIDE * HEAD_DIM), ((0, 0), (0, 0), (0, 15), (0, 0)))
    bias = _cmp_bias(pe8, w1flat, cmp_b1.reshape(2, 1, CMP_HIDDEN))
    w1pair = jnp.broadcast_to(w1.transpose(0, 2, 3, 1, 4)[:, :, None], (2, CMP_STRIDE, 2, HEAD_DIM, 2, CMP_HIDDEN))
    col = lambda v: jnp.broadcast_to(v[:, :, None], (2, HEAD_DIM, N_HALF))
    cw = dict(
        w1pair=w1pair.reshape(2, CMP_STRIDE * 128, 2 * CMP_HIDDEN).astype(BF16),
        w2t=cmp_w2.transpose(0, 2, 1).astype(BF16), b2col=col(cmp_b2),
        gaincol=jnp.broadcast_to(k_norm_cmp[:, None], (HEAD_DIM, N_HALF)),
        w1rep=w1rep, bias=bias,
        w2rep=jnp.tile(cmp_w2, (1, 1, NSA_KV)).astype(BF16),
        b2rep=jnp.tile(cmp_b2, (1, NSA_KV)).reshape(2, 1, KV_WIDTH),
        gain=jnp.stack([tile4(k_norm_cmp), ones]).reshape(2, 1, KV_WIDTH),
    )
    w_nsa = w_out[RET_WIDTH:].reshape(NSA_KV, NSA_REP, HEAD_DIM, d).transpose(1, 0, 2, 3).reshape(NSA_WIDTH, d)
    return wts, cw, w_out, w_nsa


def kernel(x_prompt, x_sample, cache_kv, cache_win, state_ret, state_conv, page_table, g_attn, w_in, q_norm,
           k_norm_cmp, k_norm_slc, k_norm_win, cmp_pe, cmp_w1, cmp_b1, cmp_w2, cmp_b2, ret_gn, w_out, g_ffn, w_up,
           conv_w, conv_b, w_down):
    assert x_prompt.shape == (BATCH, SEQ, D_MODEL) and x_sample.shape == (DEC_BATCH, DEC_SEQ, D_MODEL)
    assert g_attn.shape[0] == 1, "single layer"
    wts, cw, w_ret, w_nsa = _prepare_weights(g_attn[0], w_in[0], q_norm[0], k_norm_cmp[0], k_norm_slc[0],
                                              k_norm_win[0], cmp_pe[0], cmp_w1[0], cmp_b1[0], cmp_w2[0],
                                              cmp_b2[0], w_out[0])
    w_up_b = w_up[0]
    w_down_b = w_down[0].astype(BF16)
    cb = conv_b[0].reshape(1, -1)
    n_p, n_s = BATCH * SEQ, DEC_BATCH * DEC_SEQ
    xp = x_prompt.reshape(n_p, D_MODEL)
    xs = x_sample.reshape(n_s, D_MODEL)

    pos_p = jnp.arange(SEQ, dtype=jnp.int32)
    qk, vg, nq, (kv_p, kv_p_t), win_p, gates = _project_all(xp, pos_p, wts, kv_transposed_seq=SEQ)
    ret_mix, ret_state_p = _retention_prompt(qk, vg, ret_gn[0])
    kv3 = kv_p.reshape(BATCH, SEQ, 4 * KV_WIDTH)
    cmp_tok = _compress_prompt(kv3, cw)
    cover_p = _block_cover_t(SEQ // SEL_BLK, SEQ // SEL_BLK)
    key = np.arange(SEQ).reshape(SEQ // SEL_CHUNK, 1, SEL_CHUNK)
    expand = jnp.asarray((key // SEL_BLK) == np.arange(128)[None, :, None], dtype=BF16)
    nsa = _nsa_prompt(nq, gates, cmp_tok, kv_p, win_p, cover_p, expand)
    h_p = _mixer_out(xp, ret_mix.reshape(n_p, RET_WIDTH), nsa.reshape(n_p, NSA_WIDTH), w_ret, w_nsa)
    hn_p = _rmsnorm(h_p, g_ffn[0])
    act_p, st_a, st_b = _ffn_up(hn_p, w_up_b, conv_w[0], cb, None, tm=1024, step=1, seqs=BATCH,
                                tiles_per_seq=SEQ // 1024)
    y_p = _ffn_down(h_p, act_p, w_down_b)
    conv_p = jnp.concatenate([st_a, st_b], axis=-1)

    pos_s = jnp.tile(PAST_LEN + jnp.arange(DEC_SEQ, dtype=jnp.int32), DEC_BATCH)
    qk, vg, nq, kv_s, win_s, gates = _project_all(xs, pos_s, wts)
    ret_mix, ret_state_s = _retention_sample(qk, vg, state_ret[0], ret_gn[0])
    cache_t = cache_kv[0].transpose(0, 2, 3, 4, 1).reshape(cache_kv.shape[1], 4, KV_WIDTH, PAGE_SIZE)
    page_flat = page_table.reshape(-1).astype(jnp.int32)
    cmp_tok = _compress_sample(cache_t, page_flat, cw)
    wb = cache_win.shape[2]
    win_t = cache_win[0].transpose(0, 2, 3, 4, 1).reshape(DEC_BATCH, 2, KV_WIDTH, wb)
    cover_s = _block_cover_t(SAMPLE_SEL_BLOCKS, 128).T
    lanes = np.arange(128)
    perm = jnp.asarray((lanes[:, None] % DEC_SEQ) == lanes[None, :], dtype=BF16)
    nsa, win_out_t = _nsa_sample(nq, gates, cmp_tok, cache_t, page_flat, kv_s, win_t, win_s, cover_s, perm)
    h_s = _mixer_out(xs, ret_mix.reshape(n_s, RET_WIDTH), nsa.reshape(n_s, NSA_WIDTH), w_ret, w_nsa)
    to_tm = lambda a: a.reshape(DEC_BATCH, -1, a.shape[-1]).transpose(1, 0, 2).reshape(-1, a.shape[-1])
    from_tm = lambda a, t: a.reshape(t, DEC_BATCH, a.shape[-1]).transpose(1, 0, 2)
    h_tm = to_tm(h_s)
    hn_s = _rmsnorm(h_tm, g_ffn[0])
    act_s, st_a, st_b = _ffn_up(hn_s, w_up_b, conv_w[0], cb, to_tm(state_conv[0]), tm=n_s, step=DEC_BATCH,
                                seqs=DEC_BATCH, tiles_per_seq=1)
    y_s = from_tm(_ffn_down(h_tm, act_s, w_down_b), DEC_SEQ)
    conv_s = from_tm(jnp.concatenate([st_a, st_b], axis=-1), CONV_W - 1)

    win_sample = win_out_t.reshape(DEC_BATCH, 2, NSA_KV, HEAD_DIM, wb).transpose(0, 4, 1, 2, 3)
    return (
        y_p.reshape(BATCH, SEQ, D_MODEL),
        y_s,
        kv_p_t.reshape(BATCH, 4, NSA_KV, HEAD_DIM, SEQ).transpose(0, 4, 1, 2, 3)[None],
        kv_s.reshape(1, DEC_BATCH, DEC_SEQ, 4, NSA_KV, HEAD_DIM),
        win_p.reshape(BATCH, SEQ, 2 * KV_WIDTH)[:, SEQ - WINDOW:].reshape(1, BATCH, WINDOW, 2, NSA_KV, HEAD_DIM),
        win_sample[None],
        ret_state_p[None],
        ret_state_s[None],
        conv_p[None],
        conv_s[None],
    )
```

```python
import functools

import numpy as np
import jax
import jax.numpy as jnp
from jax import lax
from jax.experimental import pallas as pl
from jax.experimental.pallas import tpu as pltpu

D_MODEL = 2048
BATCH = 4
SEQ = 2048
DEC_BATCH = 128
DEC_SEQ = 8
PAST_LEN = 2048
PAGE_SIZE = 128
N_PAGES = PAST_LEN // PAGE_SIZE
RET_HEADS = 4
RET_DK = 256
RET_DV = 256
RET_WIDTH = RET_HEADS * RET_DV
RET_CHUNK = 128
NSA_HEADS = 16
NSA_KV = 4
NSA_REP = NSA_HEADS // NSA_KV
HEAD_DIM = 64
NSA_WIDTH = NSA_HEADS * HEAD_DIM
KV_WIDTH = NSA_KV * HEAD_DIM
CMP_LEN = 32
CMP_STRIDE = 16
CMP_HIDDEN = 256
SEL_BLK = 64
SEL_TOPK = 16
WINDOW = 512
SCALE = HEAD_DIM ** -0.5
D_FF = 5632
CONV_W = 3
EPS = 1e-6

N_HALF = PAST_LEN // CMP_STRIDE
NEG = -1e30
FORCED = 1e30
VMEM_LIMIT = 56 * 1024 * 1024

BF16 = jnp.bfloat16
F32 = jnp.float32


def _params(semantics, vmem=VMEM_LIMIT):
    return pltpu.CompilerParams(dimension_semantics=semantics, vmem_limit_bytes=vmem)


def _dot(a, b):
    return jnp.dot(a, b, preferred_element_type=F32)


def _dot_nt(a, b):
    return lax.dot_general(a, b, (((1,), (1,)), ((), ())), preferred_element_type=F32)


def _split_bf16(x):
    hi = x.astype(BF16)
    lo = (x - hi.astype(F32)).astype(BF16)
    return hi, lo


def _group_mask(width, g):
    lane = lax.broadcasted_iota(jnp.int32, (1, width), 1)
    return (lane >= g * HEAD_DIM) & (lane < (g + 1) * HEAD_DIM)


def _rms_groups64(z, gain):
    width = z.shape[-1]
    outs = []
    for k in range(width // 128):
        zk = z[:, k * 128:(k + 1) * 128]
        zz = zk * zk
        lane = lax.broadcasted_iota(jnp.int32, (1, 128), 1)
        lo_half = lane < HEAD_DIM
        s_lo = jnp.sum(jnp.where(lo_half, zz, 0.0), axis=-1, keepdims=True)
        s_hi = jnp.sum(jnp.where(lo_half, 0.0, zz), axis=-1, keepdims=True)
        ms = jnp.where(lo_half, s_lo, s_hi) * (1.0 / HEAD_DIM)
        outs.append(zk * lax.rsqrt(ms + EPS))
    y = outs[0] if len(outs) == 1 else jnp.concatenate(outs, axis=-1)
    return y * gain


def _rmsnorm_kernel(x_ref, g_ref, o_ref):
    x = x_ref[...]
    ms = jnp.mean(x * x, axis=-1, keepdims=True)
    o_ref[...] = (x * lax.rsqrt(ms + EPS) * g_ref[...]).astype(o_ref.dtype)


def _rmsnorm(x, g, tm=1024):
    n, d = x.shape
    return pl.pallas_call(
        _rmsnorm_kernel,
        grid=(n // tm,),
        in_specs=[pl.BlockSpec((tm, d), lambda i: (i, 0)), pl.BlockSpec((1, d), lambda i: (0, 0))],
        out_specs=pl.BlockSpec((tm, d), lambda i: (i, 0)),
        out_shape=jax.ShapeDtypeStruct((n, d), BF16),
        compiler_params=_params(("parallel",)),
        name="rmsnorm",
    )(x, g.reshape(1, d))


def _bf16_weights(w_ref, wb_ref, row_axis):
    @pl.when(pl.program_id(row_axis) == 0)
    def _():
        wb_ref[...] = w_ref[...].astype(BF16)
    return wb_ref


def _proj_plain_kernel(x_ref, w_ref, o_ref, wb_ref):
    o_ref[...] = _dot_nt(x_ref[...], _bf16_weights(w_ref, wb_ref, 1)[...]).astype(o_ref.dtype)


def _proj_sigmoid_kernel(x_ref, w_ref, o_ref, wb_ref):
    o_ref[...] = jax.nn.sigmoid(_dot_nt(x_ref[...], _bf16_weights(w_ref, wb_ref, 1)[...]))


def _sub_dots(x_ref, wt_ref, width):
    x = x_ref[...]
    return [_dot_nt(x, wt_ref[k * width:(k + 1) * width, :]) for k in range(wt_ref.shape[0] // width)]


def _proj_norm_kernel(x_ref, w_ref, gain_ref, nmask_ref, o_ref, *rest):
    wb_ref = rest[-1]
    for k, z in enumerate(_sub_dots(x_ref, _bf16_weights(w_ref, wb_ref, 1), KV_WIDTH)):
        cols = slice(k * KV_WIDTH, (k + 1) * KV_WIDTH)
        y = jnp.where(nmask_ref[:, cols] > 0.5, _rms_groups64(z, gain_ref[:, cols]), z)
        o_ref[:, cols] = y
        if len(rest) == 2:
            rest[0][0, cols, :] = y.T


def _proj_rot_kernel(x_ref, w_ref, cos_ref, sin_ref, scale_ref, o_ref, wb_ref):
    c = cos_ref[...]
    s = sin_ref[...]
    for hh, z in enumerate(_sub_dots(x_ref, _bf16_weights(w_ref, wb_ref, 1), RET_DK)):
        lo, mid, hi = hh * RET_DK, hh * RET_DK + 128, (hh + 1) * RET_DK
        x1 = z[:, :128]
        x2 = z[:, 128:]
        o_ref[:, lo:mid] = (x1 * c - x2 * s) * scale_ref[:, lo:mid]
        o_ref[:, mid:hi] = (x2 * c + x1 * s) * scale_ref[:, mid:hi]


PROJ_TN = 1024


def _proj(xn, wt, row0, c, kernel, row_extras=(), col_extras=(), out_dtype=F32, tm=1024, name="proj",
          transposed_seq=None):
    n, d = xn.shape
    tn = min(PROJ_TN, c)
    off = row0 // tn
    assert row0 == off * tn and c % tn == 0
    extra_specs = ([pl.BlockSpec((tm, 128), functools.partial(lambda j, i, period: (i % period, 0),
                                                              period=a.shape[0] // tm)) for a in row_extras]
                   + [pl.BlockSpec((1, tn), lambda j, i: (0, j)) for _ in col_extras])
    out_specs = pl.BlockSpec((tm, tn), lambda j, i: (i, j))
    out_shape = jax.ShapeDtypeStruct((n, c), out_dtype)
    if transposed_seq is not None:
        per_seq = transposed_seq // tm
        out_specs = [out_specs, pl.BlockSpec((1, tn, tm), lambda j, i: (i // per_seq, j, i % per_seq))]
        out_shape = [out_shape, jax.ShapeDtypeStruct((n // transposed_seq, c, transposed_seq), out_dtype)]
    return pl.pallas_call(
        kernel,
        grid=(c // tn, n // tm),
        in_specs=[pl.BlockSpec((tm, d), lambda j, i: (i, 0)), pl.BlockSpec((tn, d), lambda j, i: (j + off, 0))]
        + extra_specs,
        out_specs=out_specs,
        out_shape=out_shape,
        scratch_shapes=[pltpu.VMEM((tn, d), BF16)],
        compiler_params=_params(("arbitrary", "arbitrary")),
        name=name,
    )(xn, wt, *row_extras, *col_extras)


def _project_all(x2d, pos, wts, tm=1024, kv_transposed_seq=None):
    xn = _rmsnorm(x2d, wts["g_attn"])
    half = RET_DK // 2
    inv = 1.0 / (10000.0 ** jnp.linspace(0.0, 1.0, half, dtype=F32))
    ang = pos.astype(F32)[:, None] * inv[None, :]
    cos = jnp.cos(ang)
    sin = jnp.sin(ang)
    w_in_t = wts["w_in_t"]
    rw = 2 * RET_WIDTH
    qk = _proj(xn, w_in_t, 0, rw, _proj_rot_kernel, (cos, sin), (wts["rot_scale"],), tm=tm, name="proj_rot")
    vg = _proj(xn, w_in_t, rw, rw, _proj_plain_kernel, tm=tm, name="proj_vg")
    nq = _proj(xn, wts["w_nq_t"], 0, NSA_WIDTH, _proj_norm_kernel, (), (wts["nq_gain"], wts["nq_mask"]), tm=tm,
               name="proj_nq")
    kv0 = 2 * rw + NSA_WIDTH
    kv = _proj(xn, w_in_t, kv0, 4 * KV_WIDTH, _proj_norm_kernel, (), (wts["kv_gain"], wts["kv_mask"]), tm=tm,
               name="proj_kv", transposed_seq=kv_transposed_seq)
    win = _proj(xn, w_in_t, kv0 + 4 * KV_WIDTH, 2 * KV_WIDTH, _proj_norm_kernel, (),
                (wts["win_gain"], wts["win_mask"]), tm=tm, name="proj_win")
    gates = _proj(xn, wts["w_ng_t"], 0, 128, _proj_sigmoid_kernel, tm=tm, name="proj_gate")
    return qk, vg, nq, kv, win, gates


def _retention_step(q, k, v, state, dmask, xi, zeta, gch):
    qb = q.astype(BF16)
    kb = k.astype(BF16)
    vb = v.astype(BF16)
    s = _dot_nt(qb, kb) * dmask
    o = _dot(s.astype(BF16), vb) + _dot(qb, state.astype(BF16)) * xi
    kz = (k * zeta).astype(BF16)
    new_state = state * gch + _dot(kz.T, vb)
    return o, new_state


def _ret_mix(o, gn, gate):
    ms = jnp.mean(o * o, axis=-1, keepdims=True)
    return o * lax.rsqrt(ms + EPS) * gn * jax.nn.silu(gate)


def _ret_prompt_kernel(qk_ref, vg_ref, dmask_ref, xi_ref, zeta_ref, gch_ref, gn_ref, mix_ref, state_ref):
    @pl.when(pl.program_id(1) == 0)
    def _():
        state_ref[...] = jnp.zeros_like(state_ref)

    for h in range(RET_HEADS):
        lo, hi = h * RET_DK, (h + 1) * RET_DK
        o, new_state = _retention_step(qk_ref[0, :, lo:hi], qk_ref[0, :, RET_WIDTH + lo:RET_WIDTH + hi],
                                       vg_ref[0, :, lo:hi], state_ref[0, h], dmask_ref[h], xi_ref[h],
                                       zeta_ref[h], gch_ref[h])
        state_ref[0, h] = new_state
        mix_ref[0, :, lo:hi] = _ret_mix(o, gn_ref[h], vg_ref[0, :, RET_WIDTH + lo:RET_WIDTH + hi]).astype(mix_ref.dtype)


def _ret_tables(chunk):
    h = jnp.arange(RET_HEADS, dtype=F32)
    lg = jnp.log(1.0 - jnp.exp2(-5.0 - h))
    i = jnp.arange(chunk, dtype=F32)
    diff = i[:, None] - i[None, :]
    dmask = jnp.where(diff >= 0, jnp.exp(lg[:, None, None] * jnp.maximum(diff, 0.0)), 0.0)
    xi = jnp.exp(lg[:, None] * (i[None, :] + 1.0))
    zeta = jnp.exp(lg[:, None] * (chunk - 1.0 - i[None, :]))
    gch = jnp.exp(lg * chunk)
    bc = lambda a: jnp.broadcast_to(a[:, :, None], (RET_HEADS, chunk, RET_DV))
    return dmask, bc(xi), bc(zeta), jnp.broadcast_to(gch[:, None, None], (RET_HEADS, 1, RET_DV))


def _retention_prompt(qk, vg, gn):
    b, t = BATCH, SEQ
    c = RET_CHUNK
    qk3 = qk.reshape(b, t, 2 * RET_WIDTH)
    vg3 = vg.reshape(b, t, 2 * RET_WIDTH)
    dmask, xi, zeta, gch = _ret_tables(c)
    rows = pl.BlockSpec((1, c, 2 * RET_WIDTH), lambda bi, ci: (bi, ci, 0))
    full = lambda shape: pl.BlockSpec(shape, lambda bi, ci: (0,) * len(shape))
    return pl.pallas_call(
        _ret_prompt_kernel,
        grid=(b, t // c),
        in_specs=[rows, rows, full((RET_HEADS, c, c)), full((RET_HEADS, c, RET_DV)), full((RET_HEADS, c, RET_DV)),
                  full((RET_HEADS, 1, RET_DV)), full((RET_HEADS, 1, RET_DV))],
        out_specs=[pl.BlockSpec((1, c, RET_WIDTH), lambda bi, ci: (bi, ci, 0)),
                   pl.BlockSpec((1, RET_HEADS, RET_DK, RET_DV), lambda bi, ci: (bi, 0, 0, 0))],
        out_shape=[jax.ShapeDtypeStruct((b, t, RET_WIDTH), BF16),
                   jax.ShapeDtypeStruct((b, RET_HEADS, RET_DK, RET_DV), F32)],
        compiler_params=_params(("parallel", "arbitrary")),
        name="retention_prompt",
    )(qk3, vg3, dmask, xi, zeta, gch, gn.reshape(RET_HEADS, 1, RET_DV))


def _ret_sample_kernel(qk_ref, vg_ref, s0_ref, dmask_ref, xi_ref, zeta_ref, gch_ref, gn_ref, mix_ref, state_ref):
    t = qk_ref.shape[1]
    pad = lambda a: jnp.concatenate([a, jnp.zeros((RET_PAD - t, a.shape[-1]), F32)], axis=0)
    for s in range(qk_ref.shape[0]):
        for h in range(RET_HEADS):
            lo, hi = h * RET_DK, (h + 1) * RET_DK
            gate = vg_ref[s, :, RET_WIDTH + lo:RET_WIDTH + hi]
            o, new_state = _retention_step(pad(qk_ref[s, :, lo:hi]), pad(qk_ref[s, :, RET_WIDTH + lo:RET_WIDTH + hi]),
                                           pad(vg_ref[s, :, lo:hi]), s0_ref[s, h], dmask_ref[h], xi_ref[h],
                                           zeta_ref[h], gch_ref[h])
            state_ref[s, h] = new_state
            mix_ref[s, :, lo:hi] = _ret_mix(o[0:t], gn_ref[h], gate).astype(mix_ref.dtype)


RET_PAD = 128
RET_SEQS_PER_STEP = 4


def _retention_sample(qk, vg, state0, gn):
    b, t = DEC_BATCH, DEC_SEQ
    dmask, xi, zeta, gch = _ret_tables(t)
    dmask = jnp.pad(dmask, ((0, 0), (0, RET_PAD - t), (0, RET_PAD - t)))
    xi = jnp.pad(xi, ((0, 0), (0, RET_PAD - t), (0, 0)))
    zeta = jnp.pad(zeta, ((0, 0), (0, RET_PAD - t), (0, 0)))
    full = lambda shape: pl.BlockSpec(shape, lambda bi: (0,) * len(shape))
    return pl.pallas_call(
        _ret_sample_kernel,
        grid=(b // RET_SEQS_PER_STEP,),
        in_specs=[pl.BlockSpec((RET_SEQS_PER_STEP, t, 2 * RET_WIDTH), lambda bi: (bi, 0, 0)),
                  pl.BlockSpec((RET_SEQS_PER_STEP, t, 2 * RET_WIDTH), lambda bi: (bi, 0, 0)),
                  pl.BlockSpec((RET_SEQS_PER_STEP, RET_HEADS, RET_DK, RET_DV), lambda bi: (bi, 0, 0, 0)),
                  full((RET_HEADS, RET_PAD, RET_PAD)), full((RET_HEADS, RET_PAD, RET_DV)), full((RET_HEADS, RET_PAD, RET_DV)),
                  full((RET_HEADS, 1, RET_DV)), full((RET_HEADS, 1, RET_DV))],
        out_specs=[pl.BlockSpec((RET_SEQS_PER_STEP, t, RET_WIDTH), lambda bi: (bi, 0, 0)),
                   pl.BlockSpec((RET_SEQS_PER_STEP, RET_HEADS, RET_DK, RET_DV), lambda bi: (bi, 0, 0, 0))],
        out_shape=[jax.ShapeDtypeStruct((b, t, RET_WIDTH), BF16),
                   jax.ShapeDtypeStruct((b, RET_HEADS, RET_DK, RET_DV), F32)],
        compiler_params=_params(("parallel",)),
        name="retention_sample",
    )(qk.reshape(b, t, -1), vg.reshape(b, t, -1), state0, dmask, xi, zeta, gch, gn.reshape(RET_HEADS, 1, RET_DV))


def _cmp_bias_kernel(pe_ref, w_ref, b1_ref, o_ref):
    acc = b1_ref[0]
    for j in range(2):
        acc = acc + _dot(pe_ref[0, j].astype(BF16), w_ref[0, j])[0:1]
    o_ref[0] = acc


def _cmp_bias(pe8, w1flat, b1):
    return pl.pallas_call(
        _cmp_bias_kernel,
        grid=(2,),
        in_specs=[pl.BlockSpec((1, 2, 16, CMP_STRIDE * HEAD_DIM), lambda c: (c, 0, 0, 0)),
                  pl.BlockSpec((1, 2, CMP_STRIDE * HEAD_DIM, CMP_HIDDEN), lambda c: (c, 0, 0, 0)),
                  pl.BlockSpec((1, 1, CMP_HIDDEN), lambda c: (c, 0, 0))],
        out_specs=pl.BlockSpec((1, 1, CMP_HIDDEN), lambda c: (c, 0, 0)),
        out_shape=jax.ShapeDtypeStruct((2, 1, CMP_HIDDEN), F32),
        compiler_params=_params(("parallel",)),
        name="cmp_bias",
    )(pe8, w1flat, b1)


def _compress_body(load_rows, w1_ref, bias_ref, w2_ref, b2_ref, gain_ref, o_ref, xm_ref):
    for l in range(CMP_STRIDE):
        xl = load_rows(l)
        for g in range(NSA_KV):
            xm_ref[g * N_HALF:(g + 1) * N_HALF, l * KV_WIDTH:(l + 1) * KV_WIDTH] = jnp.where(
                _group_mask(KV_WIDTH, g), xl, 0.0).astype(BF16)
    acc = _dot(xm_ref[...], w1_ref[0])
    h0 = acc[:, :CMP_HIDDEN]
    h1 = pltpu.roll(acc[:, CMP_HIDDEN:], 4 * N_HALF - 1, 0)
    hid = jax.nn.silu(h0 + h1 + bias_ref[0])
    out = _dot(hid.astype(BF16), w2_ref[0]) + b2_ref[0]
    res = jnp.zeros((N_HALF, KV_WIDTH), F32)
    for g in range(NSA_KV):
        res = res + jnp.where(_group_mask(KV_WIDTH, g), out[g * N_HALF:(g + 1) * N_HALF], 0.0)
    normed = _rms_groups64(res, gain_ref[0])
    o_ref[0, 0] = jnp.where(pl.program_id(0) == 0, normed, res)


def _compress_prompt_kernel(lo_ref, hi_ref, w1_ref, bias_ref, w2_ref, b2_ref, gain_ref, o_ref, xm_ref):
    load = lambda l: jnp.concatenate(
        [r[0, pl.ds(l, N_HALF, stride=CMP_STRIDE), :] for r in (lo_ref, hi_ref)], axis=-1)
    _compress_body(load, w1_ref, bias_ref, w2_ref, b2_ref, gain_ref, o_ref, xm_ref)


def _compress_sample_kernel(pt_ref, *refs):
    pages = refs[:N_PAGES]
    w1_ref, bias_ref, w2t_ref, b2_ref, gain_ref, o_ref, t_ref, xm_ref = refs[N_PAGES:]
    lane = lax.broadcasted_iota(jnp.int32, (1, 128), 1)
    for c in range(2):
        for p in range(N_PAGES):
            xt = pages[p][0, c]
            for pair in range(2):
                t_ref[c, pair, p * PAGE_SIZE:(p + 1) * PAGE_SIZE, :] = xt[pair * 128:(pair + 1) * 128, :].T
        for l in range(CMP_STRIDE):
            for pair in range(2):
                rows = t_ref[c, pair, pl.ds(l, N_HALF, stride=CMP_STRIDE), :]
                for member in range(2):
                    g = 2 * pair + member
                    keep = (lane >= member * HEAD_DIM) & (lane < (member + 1) * HEAD_DIM)
                    xm_ref[c, g * N_HALF:(g + 1) * N_HALF, l * 128:(l + 1) * 128] = (
                        jnp.where(keep, rows, 0.0).astype(BF16))
        acc = _dot(xm_ref[c], w1_ref[c])
        h0 = acc[:, :CMP_HIDDEN]
        h1 = pltpu.roll(acc[:, CMP_HIDDEN:], 4 * N_HALF - 1, 0)
        hid = jax.nn.silu(h0 + h1 + bias_ref[c]).astype(BF16)
        outs = []
        for g in range(NSA_KV):
            og = _dot_nt(w2t_ref[c], hid[g * N_HALF:(g + 1) * N_HALF]) + b2_ref[c]
            if c == 0:
                og = og * lax.rsqrt(jnp.mean(og * og, axis=0, keepdims=True) + EPS) * gain_ref[...]
            outs.append(og)
        o_ref[c, 0] = jnp.concatenate(outs, axis=0)


def _cmp_weight_specs(nargs):
    cmap = (lambda c, b: (c, 0, 0)) if nargs == 2 else (lambda c, b, pt: (c, 0, 0))
    return [pl.BlockSpec((1, CMP_STRIDE * KV_WIDTH, 2 * CMP_HIDDEN), cmap),
            pl.BlockSpec((1, 1, CMP_HIDDEN), cmap),
            pl.BlockSpec((1, CMP_HIDDEN, KV_WIDTH), cmap),
            pl.BlockSpec((1, 1, KV_WIDTH), cmap),
            pl.BlockSpec((1, 1, KV_WIDTH), cmap)]


def _compress_prompt(kv3, cw):
    b = kv3.shape[0]
    return pl.pallas_call(
        _compress_prompt_kernel,
        grid=(2, b),
        in_specs=[pl.BlockSpec((1, SEQ, 128), lambda c, bi: (bi, 0, 2 * c)),
                  pl.BlockSpec((1, SEQ, 128), lambda c, bi: (bi, 0, 2 * c + 1))] + _cmp_weight_specs(2),
        out_specs=pl.BlockSpec((1, 1, N_HALF, KV_WIDTH), lambda c, bi: (c, bi, 0, 0)),
        out_shape=jax.ShapeDtypeStruct((2, b, N_HALF, KV_WIDTH), F32),
        scratch_shapes=[pltpu.VMEM((NSA_KV * N_HALF, CMP_STRIDE * KV_WIDTH), BF16)],
        compiler_params=_params(("arbitrary", "arbitrary")),
        name="compress_prompt",
    )(kv3, kv3, cw["w1rep"], cw["bias"], cw["w2rep"], cw["b2rep"], cw["gain"])


def _compress_sample(cache_t, page_flat, cw):
    b = DEC_BATCH
    page_specs = [pl.BlockSpec((1, 2, KV_WIDTH, PAGE_SIZE),
                               functools.partial(lambda bi, pt, p: (pt[bi * N_PAGES + p], 0, 0, 0), p=p))
                  for p in range(N_PAGES)]
    full = lambda shape: pl.BlockSpec(shape, lambda bi, pt: (0,) * len(shape))
    grid_spec = pltpu.PrefetchScalarGridSpec(
        num_scalar_prefetch=1,
        grid=(b,),
        in_specs=page_specs + [full((2, CMP_STRIDE * 128, 2 * CMP_HIDDEN)), full((2, 1, CMP_HIDDEN)),
                               full((2, HEAD_DIM, CMP_HIDDEN)), full((2, HEAD_DIM, N_HALF)),
                               full((HEAD_DIM, N_HALF))],
        out_specs=pl.BlockSpec((2, 1, KV_WIDTH, N_HALF), lambda bi, pt: (0, bi, 0, 0)),
        scratch_shapes=[pltpu.VMEM((2, 2, PAST_LEN, 128), F32),
                        pltpu.VMEM((2, NSA_KV * N_HALF, CMP_STRIDE * 128), BF16)],
    )
    return pl.pallas_call(
        _compress_sample_kernel,
        grid_spec=grid_spec,
        out_shape=jax.ShapeDtypeStruct((2, b, KV_WIDTH, N_HALF), F32),
        compiler_params=_params(("arbitrary",)),
        name="compress_sample",
    )(page_flat, *([cache_t] * N_PAGES), cw["w1pair"], cw["bias"], cw["w2t"], cw["b2col"], cw["gaincol"])


def _select_blocks(imp, cur, n_blocks):
    rows = imp.shape[0]
    j = lax.broadcasted_iota(jnp.int32, (rows, 1), 0)
    valid = j <= cur
    forced = (j == 0) | (j == cur) | (j == cur - 1)
    score = jnp.where(valid, jnp.where(forced, FORCED, imp), NEG)
    rank = jnp.zeros(score.shape, F32)
    for i in range(n_blocks):
        row = score[i:i + 1, :]
        ahead = (row > score) | ((row == score) & (j > i))
        rank = rank + jnp.where(ahead, 1.0, 0.0)
    return jnp.where(valid & (rank < float(SEL_TOPK)) & (j < n_blocks), 1.0, 0.0)


def _softmax_rows(s, valid):
    s = jnp.where(valid, s, NEG)
    m = jnp.max(s, axis=-1, keepdims=True)
    e = jnp.where(valid, jnp.exp(s - m), 0.0)
    return e, jnp.maximum(jnp.sum(e, axis=-1, keepdims=True), 1e-30)


def _softmax_cols(s, valid):
    s = jnp.where(valid, s, NEG)
    m = jnp.max(s, axis=0, keepdims=True)
    e = jnp.where(valid, jnp.exp(s - m), 0.0)
    return e, jnp.maximum(jnp.sum(e, axis=0, keepdims=True), 1e-30)


def _slope(g, r):
    return float(2.0 ** (-8.0 * (g * NSA_REP + r + 1) / NSA_HEADS))


NSA_TQ = 128
SEL_CHUNK_LOG2 = 9
SEL_CHUNK = 1 << SEL_CHUNK_LOG2
WIN_KEYS = WINDOW + NSA_TQ


def _nsa_prompt_kernel(q_ref, gate_ref, kc_ref, vc_ref, ks_ref, vs_ref, kw_ref, vw_ref, cover_ref, expand_ref,
                       slope_ref, o_ref, ksb, vsb, kwb, vwb):
    i = pl.program_id(1)
    tq = NSA_TQ
    rows = NSA_REP * tq

    @pl.when(i == 0)
    def _():
        pos = lax.broadcasted_iota(jnp.int32, (ks_ref.shape[1], 1), 0)
        lane = lax.broadcasted_iota(jnp.int32, (1, KV_WIDTH), 1)
        hi_part = lax.shift_right_logical(pos, 6).astype(F32)
        lo_part = (pos & (SEL_BLK - 1)).astype(F32)
        ks = ks_ref[0]
        kw = kw_ref[0]
        for g in range(NSA_KV):
            off = lane - ((g + 1) % NSA_KV) * HEAD_DIM
            feat = jnp.where(off < 3, hi_part, jnp.where(off < 6, lo_part, 0.0))
            spare = (off >= 0) & (off < HEAD_DIM)
            ksb[g] = jnp.where(spare, feat, ks).astype(BF16)
            kwb[g] = jnp.where(spare, feat, kw).astype(BF16)
        vsb[...] = vs_ref[0].astype(BF16)
        vwb[...] = vw_ref[0].astype(BF16)

    q = q_ref[0] * SCALE
    gates = gate_ref[0]
    kcb = kc_ref[0, 0].astype(BF16)
    vcb = vc_ref[0, 0].astype(BF16)
    q0 = i * tq
    qpos_col = q0 + lax.broadcasted_iota(jnp.int32, (tq, 1), 0)
    qpos4 = jnp.concatenate([qpos_col] * NSA_REP, axis=0)
    qpos_row = q0 + lax.broadcasted_iota(jnp.int32, (1, tq), 1)
    cur_row = lax.shift_right_logical(qpos_row, 6)
    win_start = pl.multiple_of(jnp.maximum(q0 - WINDOW, 0), NSA_TQ)
    n_chunks = lax.shift_right_logical(q0 + tq + SEL_CHUNK - 1, SEL_CHUNK_LOG2)

    groups = range(NSA_KV)
    qgs = [jnp.concatenate(
        [jnp.where(_group_mask(KV_WIDTH, g), q[:, r * KV_WIDTH:(r + 1) * KV_WIDTH], 0.0) for r in range(NSA_REP)],
        axis=0).astype(BF16) for g in groups]

    def gate_col(g, branch):
        cols = [gates[:, g * 12 + r * 3 + branch:g * 12 + r * 3 + branch + 1] for r in range(NSA_REP)]
        return jnp.concatenate(cols, axis=0)

    q_pos = [qgs[g] + slope_ref[g] for g in groups]
    all_reps = lambda a: jnp.concatenate([a] * NSA_REP, axis=0)

    n_idx = lax.broadcasted_iota(jnp.int32, (1, N_HALF), 1)
    dist = (qpos4 - (n_idx * CMP_STRIDE + (CMP_LEN - 1))).astype(F32)
    valid = dist >= 0.0
    o_cmp, sel_qs = [], []
    for g in groups:
        slope = jnp.concatenate([jnp.full((tq, 1), _slope(g, r), F32) for r in range(NSA_REP)], axis=0)
        e, den = _softmax_rows(_dot_nt(qgs[g], kcb) - slope * dist, valid)
        pc = e / den
        o_cmp.append(_dot(pc.astype(BF16), vcb))
        pcsum = pc[0:tq]
        for r in range(1, NSA_REP):
            pcsum = pcsum + pc[r * tq:(r + 1) * tq]
        hi, lo = _split_bf16(pcsum)
        imp = _dot_nt(cover_ref[...], hi) + _dot_nt(cover_ref[...], lo)
        sel = _select_blocks(imp, cur_row, SEQ // SEL_BLK)
        sel = jnp.concatenate([sel, jnp.zeros((128 - sel.shape[0], tq), F32)], axis=0)
        sel_qs.append(sel.T.astype(BF16))

    def sel_step(kk, carry):
        ks = pl.multiple_of(kk * SEL_CHUNK, SEL_CHUNK)
        kpos = ks + lax.broadcasted_iota(jnp.int32, (1, SEL_CHUNK), 1)
        causal = (qpos_col - kpos) >= 0
        vblk = vsb[pl.ds(ks, SEL_CHUNK), :]
        new = []
        for g in groups:
            m, l, acc = carry[g]
            chosen = _dot(sel_qs[g], expand_ref[kk])
            mask_bias = jnp.where((chosen > 0.5) & causal, 0.0, NEG)
            s = _dot_nt(q_pos[g], ksb[g, pl.ds(ks, SEL_CHUNK), :]) + all_reps(mask_bias)
            m_new = jnp.maximum(m, jnp.max(s, axis=-1, keepdims=True))
            p = jnp.exp(s - m_new)
            alpha = jnp.exp(m - m_new)
            l = alpha * l + jnp.sum(p, axis=-1, keepdims=True)
            acc = alpha * acc + _dot(p.astype(BF16), vblk)
            new.append((m_new, l, acc))
        return tuple(new)

    init = tuple((jnp.full((rows, 1), NEG, F32), jnp.zeros((rows, 1), F32), jnp.zeros((rows, KV_WIDTH), F32))
                 for _ in groups)
    sel_state = lax.fori_loop(0, n_chunks, sel_step, init)

    kpos = win_start + lax.broadcasted_iota(jnp.int32, (1, WIN_KEYS), 1)
    di = qpos_col - kpos
    win_bias = all_reps(jnp.where((di >= 0) & (di < WINDOW), 0.0, NEG))
    vwin = vwb[pl.ds(win_start, WIN_KEYS), :]

    out = [jnp.zeros((tq, KV_WIDTH), F32) for _ in range(NSA_REP)]
    for g in groups:
        s = _dot_nt(q_pos[g], kwb[g, pl.ds(win_start, WIN_KEYS), :]) + win_bias
        e = jnp.exp(s - jnp.max(s, axis=-1, keepdims=True))
        o_win = _dot(e.astype(BF16), vwin) / jnp.sum(e, axis=-1, keepdims=True)
        _, l_sel, acc_sel = sel_state[g]
        o_sel = acc_sel / jnp.maximum(l_sel, 1e-30)
        mixed = gate_col(g, 0) * o_cmp[g] + gate_col(g, 1) * o_sel + gate_col(g, 2) * o_win
        gm = _group_mask(KV_WIDTH, g)
        for r in range(NSA_REP):
            out[r] = out[r] + jnp.where(gm, mixed[r * tq:(r + 1) * tq], 0.0)

    for r in range(NSA_REP):
        o_ref[0, :, r * KV_WIDTH:(r + 1) * KV_WIDTH] = out[r].astype(o_ref.dtype)


def _nsa_prompt(nq, gates, cmp_tok, kv, win, cover_t, expand):
    b, t, tq = BATCH, SEQ, NSA_TQ
    nq3 = nq.reshape(b, t, NSA_WIDTH)
    g3 = gates.reshape(b, t, 128)
    kv3 = kv.reshape(b, t, 4 * KV_WIDTH)
    win3 = win.reshape(b, t, 2 * KV_WIDTH)
    col = lambda c: pl.BlockSpec((1, t, KV_WIDTH), lambda bi, i: (bi, 0, c))
    tok = lambda c: pl.BlockSpec((1, 1, N_HALF, KV_WIDTH), lambda bi, i: (c, bi, 0, 0))
    slope = jnp.asarray([[_slope(g, r) for r in range(NSA_REP)] for g in range(NSA_KV)], F32)
    p1 = slope.astype(BF16)
    p2 = (slope - p1.astype(F32)).astype(BF16)
    p3 = (slope - p1.astype(F32) - p2.astype(F32)).astype(BF16)
    pieces = jnp.stack([p1, p2, p3], axis=-1).astype(F32)
    six = jnp.concatenate([pieces * float(SEL_BLK), pieces], axis=-1)
    lanes = jnp.zeros((NSA_KV, NSA_REP, KV_WIDTH), F32)
    for g in range(NSA_KV):
        start = ((g + 1) % NSA_KV) * HEAD_DIM
        lanes = lanes.at[g, :, start:start + 6].set(six[g])
    slope_tab = jnp.broadcast_to(lanes[:, :, None, :], (NSA_KV, NSA_REP, tq, KV_WIDTH))
    slope_tab = slope_tab.reshape(NSA_KV, NSA_REP * tq, KV_WIDTH).astype(BF16)
    return pl.pallas_call(
        _nsa_prompt_kernel,
        grid=(b, t // tq),
        in_specs=[pl.BlockSpec((1, tq, NSA_WIDTH), lambda bi, i: (bi, i, 0)),
                  pl.BlockSpec((1, tq, 128), lambda bi, i: (bi, i, 0)),
                  tok(0), tok(1), col(2), col(3), col(0), col(1),
                  pl.BlockSpec(cover_t.shape, lambda bi, i: (0, 0)),
                  pl.BlockSpec(expand.shape, lambda bi, i: (0, 0, 0)),
                  pl.BlockSpec(slope_tab.shape, lambda bi, i: (0, 0, 0))],
        out_specs=pl.BlockSpec((1, tq, NSA_WIDTH), lambda bi, i: (bi, i, 0)),
        out_shape=jax.ShapeDtypeStruct((b, t, NSA_WIDTH), BF16),
        scratch_shapes=[pltpu.VMEM((NSA_KV, t, KV_WIDTH), BF16), pltpu.VMEM((t, KV_WIDTH), BF16),
                        pltpu.VMEM((NSA_KV, t, KV_WIDTH), BF16), pltpu.VMEM((t, KV_WIDTH), BF16)],
        compiler_params=_params(("parallel", "arbitrary")),
        name="nsa_prompt",
    )(nq3, g3, cmp_tok, cmp_tok, kv3, kv3, win3, win3, cover_t, expand, slope_tab)


SAMPLE_SEL_BLOCKS = -(-(PAST_LEN + DEC_SEQ) // SEL_BLK)
SEL_ROWS = 48


def _nsa_sample_kernel(pt_ref, *refs):
    pages = refs[:N_PAGES]
    (q_ref, gate_ref, kc_ref, vc_ref, new_ref, cw_ref, wnew_ref, cover_ref, perm_ref, o_ref, wout_ref) = refs[N_PAGES:]
    t = DEC_SEQ
    nrow = NSA_REP * NSA_KV * t
    wb = cw_ref.shape[-1]

    q = q_ref[0] * SCALE
    pieces = []
    for r in range(NSA_REP):
        qr = q[:, r * KV_WIDTH:(r + 1) * KV_WIDTH]
        for g in range(NSA_KV):
            pieces.append(jnp.where(_group_mask(KV_WIDTH, g), qr, 0.0))
    qm = jnp.concatenate(pieces, axis=0).astype(BF16)

    rowi = lax.broadcasted_iota(jnp.int32, (nrow, 1), 0)
    g_col = lax.shift_right_logical(rowi, 3) & (NSA_KV - 1)
    r_col = lax.shift_right_logical(rowi, 5)
    slope = jnp.exp2(-8.0 * (g_col * NSA_REP + r_col + 1).astype(F32) / NSA_HEADS)
    qpos = PAST_LEN + (rowi & (t - 1))
    lane = lax.broadcasted_iota(jnp.int32, (1, 128), 1)
    pad_rows = lambda a: jnp.concatenate([a, jnp.zeros((128 - t, a.shape[-1]), F32)], axis=0)

    def bias_and_mask(s, kpos_row, extra_ok=None, window=False):
        d = (qpos - kpos_row).astype(F32)
        ok = d >= 0.0
        if window:
            ok = ok & (d < float(WINDOW))
        if extra_ok is not None:
            ok = ok & extra_ok
        return s - slope * d, ok

    s, ok = bias_and_mask(_dot(qm, kc_ref[0, 0].astype(BF16)), lane * CMP_STRIDE + (CMP_LEN - 1))
    e, den = _softmax_rows(s, ok)
    pc = e / den
    o_cmp = _dot_nt(pc.astype(BF16), vc_ref[0, 0].astype(BF16))
    hi, lo = _split_bf16(pc)
    imp = _dot(hi, cover_ref[...]) + _dot(lo, cover_ref[...])
    per_rep = NSA_KV * t
    imp = imp[0:per_rep] + imp[per_rep:2 * per_rep] + imp[2 * per_rep:3 * per_rep] + imp[3 * per_rep:]
    imp_t = jnp.concatenate([imp, jnp.zeros((nrow - per_rep, 128), F32)], axis=0).T
    cur = lax.shift_right_logical(PAST_LEN + (lane & (t - 1)), 6)
    sel_t = _select_blocks(imp_t[0:SEL_ROWS], cur, SAMPLE_SEL_BLOCKS)
    sel = jnp.concatenate([sel_t, jnp.zeros((128 - SEL_ROWS, 128), F32)], axis=0).T[0:per_rep]
    sel = jnp.concatenate([sel] * NSA_REP, axis=0)

    scores, oks = [], []
    for p in range(N_PAGES):
        chosen = jnp.where(lane < SEL_BLK, sel[:, 2 * p:2 * p + 1], sel[:, 2 * p + 1:2 * p + 2])
        s, ok = bias_and_mask(_dot(qm, pages[p][0, 0].astype(BF16)), p * PAGE_SIZE + lane, chosen > 0.5)
        scores.append(s)
        oks.append(ok)
    knew = pad_rows(new_ref[0, :, 0:KV_WIDTH]).astype(BF16)
    vnew = pad_rows(new_ref[0, :, KV_WIDTH:2 * KV_WIDTH]).astype(BF16)
    last = SAMPLE_SEL_BLOCKS - 1
    s, ok = bias_and_mask(_dot_nt(qm, knew), PAST_LEN + lane, (sel[:, last:last + 1] > 0.5) & (lane < t))
    scores.append(s)
    oks.append(ok)
    s_all = jnp.concatenate([jnp.where(ok, s, NEG) for s, ok in zip(scores, oks)], axis=-1)
    e = jnp.exp(s_all - jnp.max(s_all, axis=-1, keepdims=True))
    prob = (e / jnp.sum(e, axis=-1, keepdims=True)).astype(BF16)
    o_sel = _dot(prob[:, PAST_LEN:], vnew)
    for p in range(N_PAGES):
        o_sel = o_sel + _dot_nt(prob[:, p * PAGE_SIZE:(p + 1) * PAGE_SIZE], pages[p][0, 1].astype(BF16))

    lane_w = lax.broadcasted_iota(jnp.int32, (1, wb), 1)
    s_old, ok_old = bias_and_mask(_dot(qm, cw_ref[0, 0].astype(BF16)), PAST_LEN - wb + lane_w, window=True)
    kwn = pad_rows(wnew_ref[0, :, 0:KV_WIDTH]).astype(BF16)
    vwn = pad_rows(wnew_ref[0, :, KV_WIDTH:2 * KV_WIDTH]).astype(BF16)
    s_new, ok_new = bias_and_mask(_dot_nt(qm, kwn), PAST_LEN + lane, lane < t, window=True)
    s_all = jnp.concatenate([jnp.where(ok_old, s_old, NEG), jnp.where(ok_new, s_new, NEG)], axis=-1)
    e = jnp.exp(s_all - jnp.max(s_all, axis=-1, keepdims=True))
    prob = (e / jnp.sum(e, axis=-1, keepdims=True)).astype(BF16)
    o_win = _dot_nt(prob[:, 0:wb], cw_ref[0, 1].astype(BF16)) + _dot(prob[:, wb:], vwn)

    ghi, glo = _split_bf16(pad_rows(gate_ref[0]))
    grow = _dot(perm_ref[...], ghi) + _dot(perm_ref[...], glo)
    base = g_col * 12 + r_col * 3

    def gate_col(branch):
        return jnp.sum(jnp.where(lane == base + branch, grow, 0.0), axis=-1, keepdims=True)

    mixed = gate_col(0) * o_cmp + gate_col(1) * o_sel + gate_col(2) * o_win
    for r in range(NSA_REP):
        acc = jnp.zeros((t, KV_WIDTH), F32)
        for g in range(NSA_KV):
            lo_row = (r * NSA_KV + g) * t
            acc = acc + jnp.where(_group_mask(KV_WIDTH, g), mixed[lo_row:lo_row + t], 0.0)
        o_ref[0, :, r * KV_WIDTH:(r + 1) * KV_WIDTH] = acc.astype(o_ref.dtype)

    for c in range(2):
        new_t = pad_rows(wnew_ref[0, :, c * KV_WIDTH:(c + 1) * KV_WIDTH]).T
        tail = jnp.concatenate([jnp.zeros((KV_WIDTH, wb - 128), F32), pltpu.roll(new_t, 128 - t, 1)], axis=-1)
        wout_ref[0, c] = jnp.where(lane_w >= wb - t, tail, pltpu.roll(cw_ref[0, c], wb - t, 1))


def _nsa_sample(nq, gates, cmp_tok, cache_t, page_flat, kv_new, win_t, win_new, cover, perm):
    b, t = DEC_BATCH, DEC_SEQ
    wb = win_t.shape[-1]
    per_b = lambda shape: pl.BlockSpec(shape, lambda bi, pt: (bi,) + (0,) * (len(shape) - 1))
    page_specs = [pl.BlockSpec((1, 2, KV_WIDTH, PAGE_SIZE),
                               functools.partial(lambda bi, pt, p: (pt[bi * N_PAGES + p], 1, 0, 0), p=p))
                  for p in range(N_PAGES)]
    tok = lambda c: pl.BlockSpec((1, 1, KV_WIDTH, N_HALF), lambda bi, pt: (c, bi, 0, 0))
    grid_spec = pltpu.PrefetchScalarGridSpec(
        num_scalar_prefetch=1,
        grid=(b,),
        in_specs=page_specs + [per_b((1, t, NSA_WIDTH)), per_b((1, t, 128)), tok(0), tok(1),
                               pl.BlockSpec((1, t, 2 * KV_WIDTH), lambda bi, pt: (bi, 0, 1)),
                               per_b((1, 2, KV_WIDTH, wb)), per_b((1, t, 2 * KV_WIDTH)),
                               pl.BlockSpec(cover.shape, lambda bi, pt: (0, 0)),
                               pl.BlockSpec(perm.shape, lambda bi, pt: (0, 0))],
        out_specs=[per_b((1, t, NSA_WIDTH)), per_b((1, 2, KV_WIDTH, wb))],
    )
    return pl.pallas_call(
        _nsa_sample_kernel,
        grid_spec=grid_spec,
        out_shape=[jax.ShapeDtypeStruct((b, t, NSA_WIDTH), BF16), jax.ShapeDtypeStruct(win_t.shape, F32)],
        compiler_params=_params(("arbitrary",)),
        name="nsa_sample",
    )(page_flat, *([cache_t] * N_PAGES), nq.reshape(b, t, NSA_WIDTH), gates.reshape(b, t, 128), cmp_tok, cmp_tok,
      kv_new.reshape(b, t, 4 * KV_WIDTH), win_t, win_new.reshape(b, t, 2 * KV_WIDTH), cover, perm)


def _out_kernel(x_ref, ret_ref, nsa_ref, wr_ref, wn_ref, o_ref, wr16, wn16):
    wr = _bf16_weights(wr_ref, wr16, 1)
    wn = _bf16_weights(wn_ref, wn16, 1)
    o_ref[...] = x_ref[...] + _dot(ret_ref[...], wr[...]) + _dot(nsa_ref[...], wn[...])


def _mixer_out(x2d, ret, nsa, w_out, w_nsa, tm=1024, tn=1024):
    n, d = x2d.shape
    return pl.pallas_call(
        _out_kernel,
        grid=(d // tn, n // tm),
        in_specs=[pl.BlockSpec((tm, tn), lambda j, i: (i, j)),
                  pl.BlockSpec((tm, RET_WIDTH), lambda j, i: (i, 0)),
                  pl.BlockSpec((tm, NSA_WIDTH), lambda j, i: (i, 0)),
                  pl.BlockSpec((RET_WIDTH, tn), lambda j, i: (0, j)),
                  pl.BlockSpec((NSA_WIDTH, tn), lambda j, i: (0, j))],
        out_specs=pl.BlockSpec((tm, tn), lambda j, i: (i, j)),
        out_shape=jax.ShapeDtypeStruct((n, d), F32),
        scratch_shapes=[pltpu.VMEM((RET_WIDTH, tn), BF16), pltpu.VMEM((NSA_WIDTH, tn), BF16)],
        compiler_params=_params(("arbitrary", "arbitrary")),
        name="mixer_out",
    )(x2d, ret, nsa, w_out, w_nsa)


FFN_TN = 512
FFN_COLS = D_FF // FFN_TN
FFN_SUB = 512


def _ffn_up_kernel(*refs, hist, step, tiles_per_seq, has_prev):
    if has_prev:
        x_ref, wa_ref, wb_ref, cwa_ref, cwb_ref, cba_ref, cbb_ref, pa_ref, pb_ref = refs[:9]
        act_ref, sta_ref, stb_ref, exta, extb, wa16, wb16 = refs[9:]
    else:
        x_ref, wa_ref, wb_ref, cwa_ref, cwb_ref, cba_ref, cbb_ref = refs[:7]
        act_ref, sta_ref, stb_ref, exta, extb, wa16, wb16 = refs[7:]
        pa_ref = pb_ref = None
    tm = x_ref.shape[0]
    first = (pl.program_id(1) % tiles_per_seq) == 0
    x = x_ref[...]
    wa_ref = _bf16_weights(wa_ref, wa16, 1)
    wb_ref = _bf16_weights(wb_ref, wb16, 1)
    halves = ((wa_ref, cwa_ref, cba_ref, pa_ref, sta_ref, exta), (wb_ref, cwb_ref, cbb_ref, pb_ref, stb_ref, extb))
    if not has_prev:
        @pl.when(first)
        def _():
            for half in halves:
                half[5][hist - 2 * step:hist, :] = jnp.zeros((2 * step, half[5].shape[-1]), F32)
    pieces = act_ref.shape[-1] // FFN_SUB
    ups = [[_dot(x, half[0][:, k * FFN_SUB:(k + 1) * FFN_SUB]) for half in halves] for k in range(pieces)]
    for k in range(pieces):
        cols = slice(k * FFN_SUB, (k + 1) * FFN_SUB)
        outs = []
        for (w_ref, cw_ref, cb_ref, p_ref, st_ref, ext), u in zip(halves, ups[k]):
            if has_prev:
                ext[hist - 2 * step:hist, cols] = p_ref[:, cols]
            ext[hist:hist + tm, cols] = u
            cw = cw_ref[:, cols]
            c = (cb_ref[:, cols] + cw[2:3] * u + cw[1:2] * ext[hist - step:hist - step + tm, cols]
                 + cw[0:1] * ext[hist - 2 * step:hist - 2 * step + tm, cols])
            outs.append(c)
            tail = ext[hist + tm - 2 * step:hist + tm, cols]
            if has_prev:
                st_ref[:, cols] = tail
            else:
                st_ref[0, :, cols] = tail
                ext[hist - 2 * step:hist, cols] = tail
        act_ref[:, cols] = (jax.nn.silu(outs[0]) * outs[1]).astype(act_ref.dtype)


def _ffn_up(hn, w_up, conv_w, conv_b, prev, *, tm, step, seqs, tiles_per_seq):
    n, d = hn.shape
    tn = FFN_TN
    hist = 8 if step == 1 else 2 * step
    has_prev = prev is not None
    half = lambda off: (lambda j, i: (0, j + off))
    in_specs = [pl.BlockSpec((tm, d), lambda j, i: (i, 0)),
                pl.BlockSpec((d, tn), half(0)), pl.BlockSpec((d, tn), half(FFN_COLS)),
                pl.BlockSpec((CONV_W, tn), half(0)), pl.BlockSpec((CONV_W, tn), half(FFN_COLS)),
                pl.BlockSpec((1, tn), half(0)), pl.BlockSpec((1, tn), half(FFN_COLS))]
    args = [hn, w_up, w_up, conv_w, conv_w, conv_b, conv_b]
    if has_prev:
        in_specs += [pl.BlockSpec((2 * step, tn), half(0)), pl.BlockSpec((2 * step, tn), half(FFN_COLS))]
        args += [prev, prev]
        st_spec = pl.BlockSpec((2 * step, tn), lambda j, i: (0, j))
        st_shape = jax.ShapeDtypeStruct((2 * step, D_FF), F32)
    else:
        st_spec = pl.BlockSpec((1, 2, tn), lambda j, i: (i // tiles_per_seq, 0, j))
        st_shape = jax.ShapeDtypeStruct((seqs, 2, D_FF), F32)
    kernel = functools.partial(_ffn_up_kernel, hist=hist, step=step, tiles_per_seq=tiles_per_seq, has_prev=has_prev)
    return pl.pallas_call(
        kernel,
        grid=(FFN_COLS, n // tm),
        in_specs=in_specs,
        out_specs=[pl.BlockSpec((tm, tn), lambda j, i: (i, j)), st_spec, st_spec],
        out_shape=[jax.ShapeDtypeStruct((n, D_FF), BF16), st_shape, st_shape],
        scratch_shapes=[pltpu.VMEM((hist + tm, tn), F32), pltpu.VMEM((hist + tm, tn), F32),
                        pltpu.VMEM((d, tn), BF16), pltpu.VMEM((d, tn), BF16)],
        compiler_params=_params(("arbitrary", "arbitrary")),
        name="ffn_up",
    )(*args)


def _ffn_down_kernel(h_ref, a_ref, w_ref, o_ref):
    o_ref[...] = h_ref[...] + _dot(a_ref[...], w_ref[...])


def _ffn_down(h, act, w_down, tm=1024, tn=512):
    n, d = h.shape
    return pl.pallas_call(
        _ffn_down_kernel,
        grid=(n // tm, d // tn),
        in_specs=[pl.BlockSpec((tm, tn), lambda i, j: (i, j)),
                  pl.BlockSpec((tm, D_FF), lambda i, j: (i, 0)),
                  pl.BlockSpec((D_FF, tn), lambda i, j: (0, j))],
        out_specs=pl.BlockSpec((tm, tn), lambda i, j: (i, j)),
        out_shape=jax.ShapeDtypeStruct((n, d), F32),
        compiler_params=_params(("parallel", "arbitrary")),
        name="ffn_down",
    )(h, act, w_down)


def _block_cover_t(n_sel, rows):
    cs = np.arange(N_HALF - 1)[None, :] * CMP_STRIDE
    js = np.arange(n_sel)[:, None] * SEL_BLK
    cov = np.clip(np.minimum(cs + CMP_LEN, js + SEL_BLK) - np.maximum(cs, js), 0, None) / CMP_LEN
    out = np.zeros((rows, N_HALF), np.float32)
    out[:n_sel, :N_HALF - 1] = cov
    return jnp.asarray(out, dtype=BF16)


def _prepare_weights(g_attn, w_in, q_norm, k_norm_cmp, k_norm_slc, k_norm_win, cmp_pe, cmp_w1, cmp_b1, cmp_w2,
                     cmp_b2, w_out):
    d = D_MODEL
    w_in_t = w_in.T
    q0 = 4 * RET_WIDTH
    w_nq_t = w_in_t[q0:q0 + NSA_WIDTH].reshape(NSA_KV, NSA_REP, HEAD_DIM, d).transpose(1, 0, 2, 3).reshape(NSA_WIDTH, d)
    w_ng_t = w_in_t[q0 + NSA_WIDTH + 6 * KV_WIDTH:]
    w_ng_t = jnp.pad(w_ng_t, ((0, 128 - w_ng_t.shape[0]), (0, 0)))
    tile4 = lambda v: jnp.tile(v, NSA_KV)
    zeros = jnp.zeros((KV_WIDTH,), F32)
    ones = jnp.ones((KV_WIDTH,), F32)
    wts = dict(
        g_attn=g_attn, w_in_t=w_in_t, w_nq_t=w_nq_t, w_ng_t=w_ng_t,
        rot_scale=jnp.concatenate([jnp.ones((RET_WIDTH,), F32),
                                   jnp.full((RET_WIDTH,), RET_DK ** -0.5, F32)]).reshape(1, -1),
        nq_gain=jnp.tile(q_norm, NSA_HEADS).reshape(1, -1), nq_mask=jnp.ones((1, NSA_WIDTH), F32),
        kv_gain=jnp.concatenate([zeros, zeros, tile4(k_norm_slc), zeros]).reshape(1, -1),
        kv_mask=jnp.concatenate([zeros, zeros, ones, zeros]).reshape(1, -1),
        win_gain=jnp.concatenate([tile4(k_norm_win), zeros]).reshape(1, -1),
        win_mask=jnp.concatenate([ones, zeros]).reshape(1, -1),
    )
    w1 = cmp_w1.reshape(2, 2, CMP_STRIDE, HEAD_DIM, CMP_HIDDEN)
    w1rep = jnp.broadcast_to(w1.transpose(0, 2, 3, 1, 4)[:, :, None],
                             (2, CMP_STRIDE, NSA_KV, HEAD_DIM, 2, CMP_HIDDEN))
    w1rep = w1rep.reshape(2, CMP_STRIDE * KV_WIDTH, 2 * CMP_HIDDEN).astype(BF16)
    w1flat = w1.reshape(2, 2, CMP_STRIDE * HEAD_DIM, CMP_HIDDEN).astype(BF16)
    pe8 = jnp.pad(cmp_pe.reshape(2, 2, 1, CMP_STRIDE * HEAD_DIM), ((0, 0), (0, 0), (0, 15), (0, 0)))
    bias = _cmp_bias(pe8, w1flat, cmp_b1.reshape(2, 1, CMP_HIDDEN))
    w1pair = jnp.broadcast_to(w1.transpose(0, 2, 3, 1, 4)[:, :, None], (2, CMP_STRIDE, 2, HEAD_DIM, 2, CMP_HIDDEN))
    col = lambda v: jnp.broadcast_to(v[:, :, None], (2, HEAD_DIM, N_HALF))
    cw = dict(
        w1pair=w1pair.reshape(2, CMP_STRIDE * 128, 2 * CMP_HIDDEN).astype(BF16),
        w2t=cmp_w2.transpose(0, 2, 1).astype(BF16), b2col=col(cmp_b2),
        gaincol=jnp.broadcast_to(k_norm_cmp[:, None], (HEAD_DIM, N_HALF)),
        w1rep=w1rep, bias=bias,
        w2rep=jnp.tile(cmp_w2, (1, 1, NSA_KV)).astype(BF16),
        b2rep=jnp.tile(cmp_b2, (1, NSA_KV)).reshape(2, 1, KV_WIDTH),
        gain=jnp.stack([tile4(k_norm_cmp), ones]).reshape(2, 1, KV_WIDTH),
    )
    w_nsa = w_out[RET_WIDTH:].reshape(NSA_KV, NSA_REP, HEAD_DIM, d).transpose(1, 0, 2, 3).reshape(NSA_WIDTH, d)
    return wts, cw, w_out, w_nsa


def kernel(x_prompt, x_sample, cache_kv, cache_win, state_ret, state_conv, page_table, g_attn, w_in, q_norm,
           k_norm_cmp, k_norm_slc, k_norm_win, cmp_pe, cmp_w1, cmp_b1, cmp_w2, cmp_b2, ret_gn, w_out, g_ffn, w_up,
           conv_w, conv_b, w_down):
    assert x_prompt.shape == (BATCH, SEQ, D_MODEL) and x_sample.shape == (DEC_BATCH, DEC_SEQ, D_MODEL)
    assert g_attn.shape[0] == 1, "single layer"
    wts, cw, w_ret, w_nsa = _prepare_weights(g_attn[0], w_in[0], q_norm[0], k_norm_cmp[0], k_norm_slc[0],
                                              k_norm_win[0], cmp_pe[0], cmp_w1[0], cmp_b1[0], cmp_w2[0],
                                              cmp_b2[0], w_out[0])
    w_up_b = w_up[0]
    w_down_b = w_down[0].astype(BF16)
    cb = conv_b[0].reshape(1, -1)
    n_p, n_s = BATCH * SEQ, DEC_BATCH * DEC_SEQ
    xp = x_prompt.reshape(n_p, D_MODEL)
    xs = x_sample.reshape(n_s, D_MODEL)

    pos_p = jnp.arange(SEQ, dtype=jnp.int32)
    qk, vg, nq, (kv_p, kv_p_t), win_p, gates = _project_all(xp, pos_p, wts, kv_transposed_seq=SEQ)
    ret_mix, ret_state_p = _retention_prompt(qk, vg, ret_gn[0])
    kv3 = kv_p.reshape(BATCH, SEQ, 4 * KV_WIDTH)
    cmp_tok = _compress_prompt(kv3, cw)
    cover_p = _block_cover_t(SEQ // SEL_BLK, SEQ // SEL_BLK)
    key = np.arange(SEQ).reshape(SEQ // SEL_CHUNK, 1, SEL_CHUNK)
    expand = jnp.asarray((key // SEL_BLK) == np.arange(128)[None, :, None], dtype=BF16)
    nsa = _nsa_prompt(nq, gates, cmp_tok, kv_p, win_p, cover_p, expand)
    h_p = _mixer_out(xp, ret_mix.reshape(n_p, RET_WIDTH), nsa.reshape(n_p, NSA_WIDTH), w_ret, w_nsa)
    hn_p = _rmsnorm(h_p, g_ffn[0])
    act_p, st_a, st_b = _ffn_up(hn_p, w_up_b, conv_w[0], cb, None, tm=1024, step=1, seqs=BATCH,
                                tiles_per_seq=SEQ // 1024)
    y_p = _ffn_down(h_p, act_p, w_down_b)
    conv_p = jnp.concatenate([st_a, st_b], axis=-1)

    pos_s = jnp.tile(PAST_LEN + jnp.arange(DEC_SEQ, dtype=jnp.int32), DEC_BATCH)
    qk, vg, nq, kv_s, win_s, gates = _project_all(xs, pos_s, wts)
    ret_mix, ret_state_s = _retention_sample(qk, vg, state_ret[0], ret_gn[0])
    cache_t = cache_kv[0].transpose(0, 2, 3, 4, 1).reshape(cache_kv.shape[1], 4, KV_WIDTH, PAGE_SIZE)
    page_flat = page_table.reshape(-1).astype(jnp.int32)
    cmp_tok = _compress_sample(cache_t, page_flat, cw)
    wb = cache_win.shape[2]
    win_t = cache_win[0].transpose(0, 2, 3, 4, 1).reshape(DEC_BATCH, 2, KV_WIDTH, wb)
    cover_s = _block_cover_t(SAMPLE_SEL_BLOCKS, 128).T
    lanes = np.arange(128)
    perm = jnp.asarray((lanes[:, None] % DEC_SEQ) == lanes[None, :], dtype=BF16)
    nsa, win_out_t = _nsa_sample(nq, gates, cmp_tok, cache_t, page_flat, kv_s, win_t, win_s, cover_s, perm)
    h_s = _mixer_out(xs, ret_mix.reshape(n_s, RET_WIDTH), nsa.reshape(n_s, NSA_WIDTH), w_ret, w_nsa)
    to_tm = lambda a: a.reshape(DEC_BATCH, -1, a.shape[-1]).transpose(1, 0, 2).reshape(-1, a.shape[-1])
    from_tm = lambda a, t: a.reshape(t, DEC_BATCH, a.shape[-1]).transpose(1, 0, 2)
    h_tm = to_tm(h_s)
    hn_s = _rmsnorm(h_tm, g_ffn[0])
    act_s, st_a, st_b = _ffn_up(hn_s, w_up_b, conv_w[0], cb, to_tm(state_conv[0]), tm=n_s, step=DEC_BATCH,
                                seqs=DEC_BATCH, tiles_per_seq=1)
    y_s = from_tm(_ffn_down(h_tm, act_s, w_down_b), DEC_SEQ)
    conv_s = from_tm(jnp.concatenate([st_a, st_b], axis=-1), CONV_W - 1)

    win_sample = win_out_t.reshape(DEC_BATCH, 2, NSA_KV, HEAD_DIM, wb).transpose(0, 4, 1, 2, 3)
    return (
        y_p.reshape(BATCH, SEQ, D_MODEL),
        y_s,
        kv_p_t.reshape(BATCH, 4, NSA_KV, HEAD_DIM, SEQ).transpose(0, 4, 1, 2, 3)[None],
        kv_s.reshape(1, DEC_BATCH, DEC_SEQ, 4, NSA_KV, HEAD_DIM),
        win_p.reshape(BATCH, SEQ, 2 * KV_WIDTH)[:, SEQ - WINDOW:].reshape(1, BATCH, WINDOW, 2, NSA_KV, HEAD_DIM),
        win_sample[None],
        ret_state_p[None],
        ret_state_s[None],
        conv_p[None],
        conv_s[None],
    )
```
